```python
import math
import jax, jax.numpy as jnp
from jax import lax
import numpy as np

D_MODEL = 2048
BATCH = 8
SEQ = 2048
DEPTH = 1

SSM_WIDTH = D_MODEL // 2
SSM_GROUP = 16
SSM_GROUPS = SSM_WIDTH // SSM_GROUP
SSM_STATE = 64
DT_MIN = 0.001
DT_MAX = 0.1
ATTN_HEADS = 8
HEAD_DIM = 128
ATTN_WIDTH = ATTN_HEADS * HEAD_DIM
MOBA_BLOCK = 256
MOBA_TOPK = 3
Q_BLOCK = 128
REL_BUCKETS = 32
REL_MAX_DIST = 128
D_FF = 4 * D_MODEL
N_MOD = 6
EPS = 1e-6
NEG_INF = -1e30
IN_WIDTH = SSM_WIDTH + 3 * ATTN_WIDTH + 2 * D_MODEL

kernel_name = "hybrid_s5_moba_gated_block"


def rms_norm(x, g):
    x32 = x.astype(jnp.float32)
    y = x32 * lax.rsqrt(jnp.mean(x32 * x32, axis=-1, keepdims=True) + EPS)
    return (y * g.astype(jnp.float32)).astype(x.dtype)


def _ssm_combine(lhs, rhs):
    a_re1, a_im1, b_re1, b_im1 = lhs
    a_re2, a_im2, b_re2, b_im2 = rhs
    a_re = a_re2 * a_re1 - a_im2 * a_im1
    a_im = a_re2 * a_im1 + a_im2 * a_re1
    b_re = a_re2 * b_re1 - a_im2 * b_im1 + b_re2
    b_im = a_re2 * b_im1 + a_im2 * b_re1 + b_im2
    return (a_re, a_im, b_re, b_im)


def s5_branch(u, a_re, a_im, log_dt, b_re, b_im, c_re, c_im, d_skip, w_glu, b_glu):
    bsz, seq_len, _ = u.shape
    f32 = jnp.float32
    u32 = u.astype(f32).reshape(bsz, seq_len, SSM_GROUPS, SSM_GROUP)
    a_re = a_re.astype(f32)
    a_im = a_im.astype(f32)
    dt = jnp.exp(log_dt.astype(f32))[:, None]
    mag = jnp.exp(dt * a_re)
    abar_re = mag * jnp.cos(dt * a_im)
    abar_im = mag * jnp.sin(dt * a_im)
    den = a_re * a_re + a_im * a_im
    p_re = abar_re - 1.0
    f_re = (p_re * a_re + abar_im * a_im) / den
    f_im = (abar_im * a_re - p_re * a_im) / den
    b_re = b_re.astype(f32)
    b_im = b_im.astype(f32)
    bbar_re = f_re[..., None] * b_re - f_im[..., None] * b_im
    bbar_im = f_re[..., None] * b_im + f_im[..., None] * b_re
    bu_re = jnp.einsum('blgp,gnp->blgn', u32, bbar_re)
    bu_im = jnp.einsum('blgp,gnp->blgn', u32, bbar_im)
    a_seq_re = jnp.broadcast_to(abar_re, bu_re.shape)
    a_seq_im = jnp.broadcast_to(abar_im, bu_im.shape)
    _, _, s_re, s_im = lax.associative_scan(
        _ssm_combine, (a_seq_re, a_seq_im, bu_re, bu_im), axis=1)
    y = (jnp.einsum('blgn,gpn->blgp', s_re, c_re.astype(f32))
         - jnp.einsum('blgn,gpn->blgp', s_im, c_im.astype(f32)))
    y = y.reshape(bsz, seq_len, SSM_WIDTH) + d_skip.astype(f32) * u32.reshape(bsz, seq_len, SSM_WIDTH)
    y = jax.nn.gelu(y).astype(u.dtype)
    return y * jax.nn.sigmoid(y @ w_glu + b_glu)


def t5_bucket(rel):
    n = jnp.maximum(rel, 0)
    max_exact = REL_BUCKETS // 2
    nf = jnp.maximum(n, 1).astype(jnp.float32)
    large = max_exact + (jnp.log(nf / max_exact) / math.log(REL_MAX_DIST / max_exact)
                         * (REL_BUCKETS - max_exact)).astype(jnp.int32)
    large = jnp.minimum(large, REL_BUCKETS - 1)
    return jnp.where(n < max_exact, n, large)


def moba_attention(q, k, v, rel_bias):
    bsz, seq_len = q.shape[0], q.shape[1]
    n_blk = -(-seq_len // MOBA_BLOCK)
    pad = n_blk * MOBA_BLOCK - seq_len
    n_sel = min(MOBA_TOPK, n_blk - 1)
    q = q.transpose(0, 2, 1, 3)
    k = jnp.pad(k.transpose(0, 2, 1, 3), ((0, 0), (0, 0), (0, pad), (0, 0)))
    v = jnp.pad(v.transpose(0, 2, 1, 3), ((0, 0), (0, 0), (0, pad), (0, 0)))
    kb = k.reshape(bsz, ATTN_HEADS, n_blk, MOBA_BLOCK, HEAD_DIM)
    vb = v.reshape(bsz, ATTN_HEADS, n_blk, MOBA_BLOCK, HEAD_DIM)
    kmean = jnp.mean(kb.astype(jnp.float32), axis=3).astype(k.dtype)
    bias_table = rel_bias.T
    head_ix = jnp.arange(ATTN_HEADS)
    n_qblk = seq_len // Q_BLOCK
    scale = HEAD_DIM ** -0.5
    blk_off = jnp.arange(MOBA_BLOCK)

    def one_block(i):
        b = i // n_qblk
        q0 = (i % n_qblk) * Q_BLOCK
        own = q0 // MOBA_BLOCK
        qc = lax.dynamic_slice_in_dim(lax.dynamic_index_in_dim(q, b, 0, keepdims=False),
                                      q0, Q_BLOCK, axis=1)
        kb_b = lax.dynamic_index_in_dim(kb, b, 0, keepdims=False)
        vb_b = lax.dynamic_index_in_dim(vb, b, 0, keepdims=False)
        q_pos = q0 + jnp.arange(Q_BLOCK)
        k_own = lax.dynamic_index_in_dim(kb_b, own, 1, keepdims=False)
        v_own = lax.dynamic_index_in_dim(vb_b, own, 1, keepdims=False)
        rel_own = q_pos[:, None] - (own * MOBA_BLOCK + blk_off)[None, :]
        s_own = (jnp.einsum('hqd,hsd->hqs', qc, k_own).astype(jnp.float32) * scale
                 + bias_table[:, t5_bucket(rel_own)].astype(jnp.float32))
        s_own = jnp.where(rel_own[None] >= 0, s_own, NEG_INF)
        if n_sel == 0:
            p = jax.nn.softmax(s_own, axis=-1).astype(v.dtype)
            return jnp.einsum('hqs,hsd->hqd', p, v_own)
        gate = jnp.einsum('hqd,hnd->hqn', qc, kmean[b]).astype(jnp.float32)
        gate = jnp.where(jnp.arange(n_blk) < own, gate, NEG_INF)
        _, sel = lax.top_k(gate, n_sel)
        valid = jnp.arange(n_sel) < own
        kg = kb_b[head_ix[:, None, None], sel]
        vg = vb_b[head_ix[:, None, None], sel]
        rel_sel = q_pos[None, :, None, None] - (sel[..., None] * MOBA_BLOCK + blk_off)
        s_sel = (jnp.einsum('hqd,hqksd->hqks', qc, kg).astype(jnp.float32) * scale
                 + bias_table[head_ix[:, None, None, None], t5_bucket(rel_sel)].astype(jnp.float32))
        s_sel = jnp.where(valid[None, None, :, None], s_sel, NEG_INF)
        s = jnp.concatenate([s_sel.reshape(ATTN_HEADS, Q_BLOCK, n_sel * MOBA_BLOCK), s_own], axis=-1)
        p = jax.nn.softmax(s, axis=-1).astype(v.dtype)
        p_sel = p[..., :n_sel * MOBA_BLOCK].reshape(ATTN_HEADS, Q_BLOCK, n_sel, MOBA_BLOCK)
        return (jnp.einsum('hqks,hqksd->hqd', p_sel, vg)
                + jnp.einsum('hqs,hsd->hqd', p[..., n_sel * MOBA_BLOCK:], v_own))

    o = lax.map(one_block, jnp.arange(bsz * n_qblk))
    o = o.reshape(bsz, n_qblk, ATTN_HEADS, Q_BLOCK, HEAD_DIM).transpose(0, 1, 3, 2, 4)
    return o.reshape(bsz, seq_len, ATTN_WIDTH)


def setup_inputs(seed: int = 0) -> dict:
    key = jax.random.key(seed)
    ks = jax.random.split(key, 24)
    f32 = jnp.float32
    nrm = lambda k, shape, s: jax.random.normal(k, shape, f32) * s
    n_idx = jnp.arange(SSM_STATE, dtype=f32)
    a_re = -0.5 + nrm(ks[4], (DEPTH, SSM_GROUPS, SSM_STATE), 0.01)
    a_im = math.pi * n_idx + nrm(ks[5], (DEPTH, SSM_GROUPS, SSM_STATE), 0.01)
    log_dt = jax.random.uniform(ks[6], (DEPTH, SSM_GROUPS), f32,
                                math.log(DT_MIN), math.log(DT_MAX))
    return {
        "x": nrm(ks[0], (BATCH, SEQ, D_MODEL), 1.0),
        "c": nrm(ks[1], (BATCH, D_MODEL), 1.0),
        "rel_bias": nrm(ks[2], (REL_BUCKETS, ATTN_HEADS), 0.5),
        "w_ada": nrm(ks[3], (DEPTH, D_MODEL, N_MOD * D_MODEL), 0.5 * D_MODEL ** -0.5),
        "b_ada": nrm(ks[7], (DEPTH, N_MOD * D_MODEL), 0.02),
        "norm_mix_g": 1.0 + nrm(ks[8], (DEPTH, D_MODEL), 0.02),
        "w_in": nrm(ks[9], (DEPTH, D_MODEL, IN_WIDTH), D_MODEL ** -0.5),
        "ssm_a_re": a_re,
        "ssm_a_im": a_im,
        "ssm_log_dt": log_dt,
        "ssm_b_re": nrm(ks[10], (DEPTH, SSM_GROUPS, SSM_STATE, SSM_GROUP), (2 * SSM_GROUP) ** -0.5),
        "ssm_b_im": nrm(ks[11], (DEPTH, SSM_GROUPS, SSM_STATE, SSM_GROUP), (2 * SSM_GROUP) ** -0.5),
        "ssm_c_re": nrm(ks[12], (DEPTH, SSM_GROUPS, SSM_GROUP, SSM_STATE), (2 * SSM_STATE) ** -0.5),
        "ssm_c_im": nrm(ks[13], (DEPTH, SSM_GROUPS, SSM_GROUP, SSM_STATE), (2 * SSM_STATE) ** -0.5),
        "ssm_d": nrm(ks[14], (DEPTH, SSM_WIDTH), 1.0),
        "w_glu": nrm(ks[15], (DEPTH, SSM_WIDTH, SSM_WIDTH), SSM_WIDTH ** -0.5),
        "b_glu": nrm(ks[16], (DEPTH, SSM_WIDTH), 0.02),
        "w_proj_ssm": nrm(ks[17], (DEPTH, SSM_WIDTH, D_MODEL), SSM_WIDTH ** -0.5),
        "w_proj_attn": nrm(ks[18], (DEPTH, ATTN_WIDTH, D_MODEL), ATTN_WIDTH ** -0.5),
        "w_out": nrm(ks[19], (DEPTH, D_MODEL, D_MODEL), D_MODEL ** -0.5),
        "norm_mlp_g": 1.0 + nrm(ks[20], (DEPTH, D_MODEL), 0.02),
        "w_ff1": nrm(ks[21], (DEPTH, D_MODEL, D_FF), D_MODEL ** -0.5),
        "w_ff2": nrm(ks[22], (DEPTH, D_FF, D_MODEL), D_FF ** -0.5),
        "norm_final_g": 1.0 + nrm(ks[23], (D_MODEL,), 0.02),
    }


def reference(x, c, rel_bias, w_ada, b_ada, norm_mix_g, w_in, ssm_a_re, ssm_a_im, ssm_log_dt,
              ssm_b_re, ssm_b_im, ssm_c_re, ssm_c_im, ssm_d, w_glu, b_glu, w_proj_ssm,
              w_proj_attn, w_out, norm_mlp_g, w_ff1, w_ff2, norm_final_g):
    bsz, seq_len, _ = x.shape
    split_at = [SSM_WIDTH, SSM_WIDTH + ATTN_WIDTH, SSM_WIDTH + 2 * ATTN_WIDTH,
                SSM_WIDTH + 3 * ATTN_WIDTH, SSM_WIDTH + 3 * ATTN_WIDTH + D_MODEL]
    c_act = jax.nn.silu(c)
    for l in range(DEPTH):
        mod = c_act @ w_ada[l] + b_ada[l]
        sh1, sc1, g1, sh2, sc2, g2 = [m[:, None, :] for m in jnp.split(mod, N_MOD, axis=-1)]
        h = rms_norm(x, norm_mix_g[l]) * (1.0 + sc1) + sh1
        proj = h @ w_in[l]
        u, q, k, v, ga, gb = jnp.split(proj, split_at, axis=-1)
        y_ssm = s5_branch(u, ssm_a_re[l], ssm_a_im[l], ssm_log_dt[l], ssm_b_re[l], ssm_b_im[l],
                          ssm_c_re[l], ssm_c_im[l], ssm_d[l], w_glu[l], b_glu[l])
        shp = (bsz, seq_len, ATTN_HEADS, HEAD_DIM)
        y_att = moba_attention(q.reshape(shp), k.reshape(shp), v.reshape(shp), rel_bias)
        merged = (jax.nn.sigmoid(ga) * (y_ssm @ w_proj_ssm[l])
                  + jax.nn.sigmoid(gb) * (y_att @ w_proj_attn[l]))
        x = x + g1 * (merged @ w_out[l])
        h2 = rms_norm(x, norm_mlp_g[l]) * (1.0 + sc2) + sh2
        x = x + g2 * (jnp.square(jax.nn.relu(h2 @ w_ff1[l])) @ w_ff2[l])
    return rms_norm(x, norm_final_g)
```

```python
import functools
import math

import jax
import jax.numpy as jnp
from jax import lax
from jax.experimental import pallas as pl
from jax.experimental.pallas import tpu as pltpu

F32 = jnp.float32
BF16 = jnp.bfloat16

D_MODEL = 2048
SSM_WIDTH = 1024
SSM_GROUP = 16
SSM_GROUPS = 64
SSM_STATE = 64
ATTN_HEADS = 8
HEAD_DIM = 128
ATTN_WIDTH = 1024
MOBA_BLOCK = 256
MOBA_TOPK = 3
REL_BUCKETS = 32
REL_MAX_DIST = 128
D_FF = 4 * D_MODEL
N_MOD = 6
EPS = 1e-6
NEG_INF = -1e30
IN_WIDTH = SSM_WIDTH + 3 * ATTN_WIDTH + 2 * D_MODEL

VMEM_LIMIT_BYTES = 56 * 1024 * 1024

S5_GROUPS_PER_BLOCK = 8
S5_CH = S5_GROUPS_PER_BLOCK * SSM_GROUP
S5_ST = S5_GROUPS_PER_BLOCK * SSM_STATE
S5_TC = 128


def _cparams(sem):
    return pltpu.CompilerParams(dimension_semantics=sem,
                                vmem_limit_bytes=VMEM_LIMIT_BYTES)


def _mod_kernel(c_ref, w_ref, b_ref, o_ref):
    c = c_ref[...]
    ca = (c * jax.nn.sigmoid(c)).astype(BF16)
    o_ref[...] = jnp.dot(ca, w_ref[...].astype(BF16),
                         preferred_element_type=F32) + b_ref[...]


def _mod(c, w_ada, b_ada):
    bsz, d = c.shape
    n = w_ada.shape[1]
    tn = 1024
    return pl.pallas_call(
        _mod_kernel,
        out_shape=jax.ShapeDtypeStruct((bsz, n), F32),
        grid=(n // tn,),
        in_specs=[pl.BlockSpec((bsz, d), lambda j: (0, 0)),
                  pl.BlockSpec((d, tn), lambda j: (0, j)),
                  pl.BlockSpec((1, tn), lambda j: (0, j))],
        out_specs=pl.BlockSpec((bsz, tn), lambda j: (0, j)),
        compiler_params=_cparams(("parallel",)),
        name="mod",
    )(c, w_ada, b_ada.reshape(1, n))


def _norm_mm_kernel(x_ref, g_ref, sc_ref, sh_ref, w_ref, o_ref, h_ref, *, act, rows):
    @pl.when(pl.program_id(1) == 0)
    def _():
        g = g_ref[...]
        sc = 1.0 + sc_ref[0]
        sh = sh_ref[0]
        tm = x_ref.shape[0]
        for r in range(0, tm, rows):
            x = x_ref[r:r + rows, :]
            ms = jnp.mean(x * x, axis=-1, keepdims=True)
            y = x * lax.rsqrt(ms + EPS) * g
            h_ref[r:r + rows, :] = (y * sc + sh).astype(BF16)

    acc = jnp.dot(h_ref[...], w_ref[...], preferred_element_type=F32)
    if act == "relu2":
        acc = jnp.square(jnp.maximum(acc, 0.0))
    o_ref[...] = acc.astype(o_ref.dtype)


def _norm_mm(x2d, g, mod4, sc_idx, sh_idx, w, seq_len, *, act, tm=1024, tn=1024):
    t, d = x2d.shape
    n = w.shape[1]
    per_b = seq_len // tm
    return pl.pallas_call(
        functools.partial(_norm_mm_kernel, act=act, rows=256),
        out_shape=jax.ShapeDtypeStruct((t, n), BF16),
        grid=(t // tm, n // tn),
        in_specs=[pl.BlockSpec((tm, d), lambda i, j: (i, 0)),
                  pl.BlockSpec((1, d), lambda i, j: (0, 0)),
                  pl.BlockSpec((1, 1, d), lambda i, j: ((i // per_b) * N_MOD + sc_idx, 0, 0)),
                  pl.BlockSpec((1, 1, d), lambda i, j: ((i // per_b) * N_MOD + sh_idx, 0, 0)),
                  pl.BlockSpec((d, tn), lambda i, j: (0, j))],
        out_specs=pl.BlockSpec((tm, tn), lambda i, j: (i, j)),
        scratch_shapes=[pltpu.VMEM((tm, d), BF16)],
        compiler_params=_cparams(("parallel", "arbitrary")),
        name="norm_mm_" + act,
    )(x2d, g.reshape(1, d), mod4, mod4, w)


def _s5prep_kernel(are_ref, aim_ref, ldt_ref, bre_ref, bim_ref,
                   abre_ref, abim_ref, bbre_ref, bbim_ref):
    a_re = are_ref[...]
    a_im = aim_ref[...]
    dt = jnp.exp(ldt_ref[...])
    mag = jnp.exp(dt * a_re)
    abar_re = mag * jnp.cos(dt * a_im)
    abar_im = mag * jnp.sin(dt * a_im)
    den = a_re * a_re + a_im * a_im
    p_re = abar_re - 1.0
    f_re = (p_re * a_re + abar_im * a_im) / den
    f_im = (abar_im * a_re - p_re * a_im) / den
    abre_ref[...] = abar_re
    abim_ref[...] = abar_im
    b_re = bre_ref[...]
    b_im = bim_ref[...]
    bbre_ref[...] = f_re * b_re - f_im * b_im
    bbim_ref[...] = f_re * b_im + f_im * b_re


def _s5prep(a_re, a_im, log_dt, b_re, b_im):
    g, n = a_re.shape
    p = b_re.shape[2]
    b_re_t = jnp.transpose(b_re, (0, 2, 1))
    b_im_t = jnp.transpose(b_im, (0, 2, 1))
    return pl.pallas_call(
        _s5prep_kernel,
        out_shape=(jax.ShapeDtypeStruct((g, 1, n), F32),
                   jax.ShapeDtypeStruct((g, 1, n), F32),
                   jax.ShapeDtypeStruct((g, p, n), F32),
                   jax.ShapeDtypeStruct((g, p, n), F32)),
        name="s5prep",
    )(a_re.reshape(g, 1, n), a_im.reshape(g, 1, n), log_dt.reshape(g, 1, 1),
      b_re_t, b_im_t)


def _s5_kernel(u_ref, bm_ref, cm_ref, ab_ref, d_ref, y_ref, bu_ref, s_ref, st_ref, *, bsz):
    @pl.when(pl.program_id(1) == 0)
    def _():
        st_ref[...] = jnp.zeros_like(st_ref)

    u = u_ref[...]
    bu_ref[...] = jnp.dot(u, bm_ref[0], preferred_element_type=F32)
    a_re = ab_ref[0, :, :S5_ST]
    a_im = ab_ref[0, :, S5_ST:]
    n_pairs = u.shape[0] // (2 * bsz)

    def step(i, carry):
        s_re, s_im = carry
        rows = []
        for k in range(2):
            r = pl.multiple_of(i * (2 * bsz) + k * bsz, bsz)
            b = bu_ref[pl.ds(r, bsz), :]
            n_re = a_re * s_re - a_im * s_im + b[:, :S5_ST]
            n_im = a_re * s_im + a_im * s_re + b[:, S5_ST:]
            s_re, s_im = n_re, n_im
            rows.append(jnp.concatenate([s_re, s_im], axis=1))
        r0 = pl.multiple_of(i * (2 * bsz), 2 * bsz)
        s_ref[pl.ds(r0, 2 * bsz), :] = jnp.concatenate(rows, axis=0).astype(BF16)
        return s_re, s_im

    s_re, s_im = lax.fori_loop(0, n_pairs, step,
                               (st_ref[:, :S5_ST], st_ref[:, S5_ST:]), unroll=4)
    st_ref[:, :S5_ST] = s_re
    st_ref[:, S5_ST:] = s_im

    y = jnp.dot(s_ref[...], cm_ref[0], preferred_element_type=F32)
    y = y + d_ref[0] * u.astype(F32)
    y_ref[...] = jax.nn.gelu(y).astype(y_ref.dtype)


def _s5(u_tm, bmat, cmat, abar, d_skip, bsz, seq_len):
    rows = bsz * S5_TC
    n_gb = SSM_WIDTH // S5_CH
    return pl.pallas_call(
        functools.partial(_s5_kernel, bsz=bsz),
        out_shape=jax.ShapeDtypeStruct((seq_len * bsz, SSM_WIDTH), BF16),
        grid=(n_gb, seq_len // S5_TC),
        in_specs=[pl.BlockSpec((rows, S5_CH), lambda g, c: (c, g)),
                  pl.BlockSpec((1, S5_CH, 2 * S5_ST), lambda g, c: (g, 0, 0)),
                  pl.BlockSpec((1, 2 * S5_ST, S5_CH), lambda g, c: (g, 0, 0)),
                  pl.BlockSpec((1, bsz, 2 * S5_ST), lambda g, c: (g, 0, 0)),
                  pl.BlockSpec((1, 1, S5_CH), lambda g, c: (g, 0, 0))],
        out_specs=pl.BlockSpec((rows, S5_CH), lambda g, c: (c, g)),
        scratch_shapes=[pltpu.VMEM((rows, 2 * S5_ST), F32),
                        pltpu.VMEM((rows, 2 * S5_ST), BF16),
                        pltpu.VMEM((bsz, 2 * S5_ST), F32)],
        compiler_params=_cparams(("parallel", "arbitrary")),
        name="s5",
    )(u_tm, bmat, cmat, abar, d_skip.reshape(n_gb, 1, S5_CH))


def _block_diag(x):
    nb, ng, r, c = x.shape
    eye = jnp.eye(ng, dtype=x.dtype)
    return jnp.einsum("agrc,gh->agrhc", x, eye).reshape(nb, ng * r, ng * c)


def _glu_kernel(y_ref, w_ref, b_ref, o_ref):
    y = y_ref[...]
    z = jnp.dot(y, w_ref[...], preferred_element_type=F32) + b_ref[...]
    o_ref[...] = (y.astype(F32) * jax.nn.sigmoid(z)).astype(o_ref.dtype)


def _glu(y, w, b, tm=1024):
    t, n = y.shape
    return pl.pallas_call(
        _glu_kernel,
        out_shape=jax.ShapeDtypeStruct((t, n), BF16),
        grid=(t // tm,),
        in_specs=[pl.BlockSpec((tm, n), lambda i: (i, 0)),
                  pl.BlockSpec((n, n), lambda i: (0, 0)),
                  pl.BlockSpec((1, n), lambda i: (0, 0))],
        out_specs=pl.BlockSpec((tm, n), lambda i: (i, 0)),
        compiler_params=_cparams(("parallel",)),
        name="glu",
    )(y, w, b.reshape(1, n))


def _t5_bucket(rel):
    n = jnp.maximum(rel, 0)
    max_exact = REL_BUCKETS // 2
    nf = jnp.maximum(n, 1).astype(F32)
    large = max_exact + (jnp.log(nf / max_exact) / math.log(REL_MAX_DIST / max_exact)
                         * (REL_BUCKETS - max_exact)).astype(jnp.int32)
    large = jnp.minimum(large, REL_BUCKETS - 1)
    return jnp.where(n < max_exact, n, large)


def _bias_tiles(rel_bias):
    i = jnp.arange(MOBA_BLOCK)
    rel = (jnp.arange(3)[:, None, None] * MOBA_BLOCK + i[None, :, None] - i[None, None, :])
    tiles = rel_bias.astype(F32)[_t5_bucket(rel)]
    return jnp.transpose(tiles, (3, 0, 1, 2))


_NT = (((1,), (1,)), ((), ()))


def _moba_kernel(q_ref, k_ref, v_ref, bias_ref, o_ref,
                 km_ref, rank_ref, m_ref, l_ref, acc_ref, *, n_blk):
    bs = MOBA_BLOCK
    scale = HEAD_DIM ** -0.5
    own = pl.program_id(2)

    @pl.when(own == 0)
    def _():
        km_ref[...] = jnp.zeros_like(km_ref)
        for n in range(n_blk):
            kb = k_ref[n * bs:(n + 1) * bs, :].astype(F32)
            km_ref[n:n + 1, :] = jnp.mean(kb, axis=0, keepdims=True)

    q = q_ref[...]

    r0 = pl.multiple_of(own * bs, bs)
    k_own = k_ref[pl.ds(r0, bs), :]
    v_own = v_ref[pl.ds(r0, bs), :]
    s = lax.dot_general(q, k_own, _NT, preferred_element_type=F32) * scale + bias_ref[0, 0]
    row = lax.broadcasted_iota(jnp.int32, (bs, bs), 0)
    col = lax.broadcasted_iota(jnp.int32, (bs, bs), 1)
    s = jnp.where(row >= col, s, NEG_INF)
    m = jnp.max(s, axis=1, keepdims=True)
    p = jnp.exp(s - m)
    l = jnp.sum(p, axis=1, keepdims=True)
    acc_ref[...] = jnp.dot(p.astype(BF16), v_own, preferred_element_type=F32)
    m_ref[...] = jnp.broadcast_to(m, m_ref.shape)
    l_ref[...] = jnp.broadcast_to(l, l_ref.shape)

    @pl.when(own > 0)
    def _():
        gate = lax.dot_general(q, km_ref[...].astype(BF16), _NT, preferred_element_type=F32)
        lane = lax.broadcasted_iota(jnp.int32, gate.shape, 1)
        gm = jnp.where(lane < own, gate, NEG_INF)
        rank = jnp.zeros_like(gate)
        for n in range(n_blk - 1):
            g_n = gm[:, n:n + 1]
            ge = jnp.where(gm >= g_n, 1.0, 0.0)
            gt = jnp.where(gm > g_n, 1.0, 0.0)
            cnt = jnp.sum(jnp.where(lane < n, ge, gt), axis=1, keepdims=True)
            rank = jnp.where(lane == n, cnt, rank)
        rank_ref[...] = rank

    for j in range(n_blk - 1):
        @pl.when(j < own)
        def _(j=j):
            k_j = k_ref[j * bs:(j + 1) * bs, :]
            v_j = v_ref[j * bs:(j + 1) * bs, :]
            dist = jnp.minimum(own - j, 2)
            s = (lax.dot_general(q, k_j, _NT, preferred_element_type=F32) * scale
                 + bias_ref[0, dist])
            sel = rank_ref[:, j:j + 1] < float(MOBA_TOPK)
            s = jnp.where(sel, s, NEG_INF)
            m_prev = m_ref[...]
            m_next = jnp.maximum(m_prev, jnp.max(s, axis=1, keepdims=True))
            alpha = jnp.exp(m_prev - m_next)
            p = jnp.exp(s - jnp.concatenate([m_next, m_next], axis=1))
            l_ref[...] = alpha * l_ref[...] + jnp.sum(p, axis=1, keepdims=True)
            acc_ref[...] = alpha * acc_ref[...] + jnp.dot(p.astype(BF16), v_j,
                                                         preferred_element_type=F32)
            m_ref[...] = m_next

    o_ref[...] = (acc_ref[...] / l_ref[...]).astype(o_ref.dtype)


def _moba(proj, bias_tiles, bsz, seq_len):
    n_blk = seq_len // MOBA_BLOCK
    q_off = SSM_WIDTH // HEAD_DIM
    k_off = q_off + ATTN_HEADS
    v_off = k_off + ATTN_HEADS
    return pl.pallas_call(
        functools.partial(_moba_kernel, n_blk=n_blk),
        out_shape=jax.ShapeDtypeStruct((bsz * seq_len, ATTN_WIDTH), BF16),
        grid=(bsz, ATTN_HEADS, n_blk),
        in_specs=[pl.BlockSpec((MOBA_BLOCK, HEAD_DIM), lambda b, h, i: (b * n_blk + i, q_off + h)),
                  pl.BlockSpec((seq_len, HEAD_DIM), lambda b, h, i: (b, k_off + h)),
                  pl.BlockSpec((seq_len, HEAD_DIM), lambda b, h, i: (b, v_off + h)),
                  pl.BlockSpec((1, 3, MOBA_BLOCK, MOBA_BLOCK), lambda b, h, i: (h, 0, 0, 0))],
        out_specs=pl.BlockSpec((MOBA_BLOCK, HEAD_DIM), lambda b, h, i: (b * n_blk + i, h)),
        scratch_shapes=[pltpu.VMEM((HEAD_DIM, HEAD_DIM), F32),
                        pltpu.VMEM((MOBA_BLOCK, HEAD_DIM), F32),
                        pltpu.VMEM((MOBA_BLOCK, HEAD_DIM), F32),
                        pltpu.VMEM((MOBA_BLOCK, HEAD_DIM), F32),
                        pltpu.VMEM((MOBA_BLOCK, HEAD_DIM), F32)],
        compiler_params=_cparams(("parallel", "parallel", "arbitrary")),
        name="moba",
    )(proj, proj, proj, bias_tiles)


def _merge_kernel(ys_ref, ya_ref, ws_ref, wa_ref, ga_ref, gb_ref, o_ref):
    a = jnp.dot(ys_ref[...], ws_ref[...], preferred_element_type=F32)
    b = jnp.dot(ya_ref[...], wa_ref[...], preferred_element_type=F32)
    o_ref[...] = (jax.nn.sigmoid(ga_ref[...].astype(F32)) * a
                  + jax.nn.sigmoid(gb_ref[...].astype(F32)) * b).astype(o_ref.dtype)


def _merge(y_ssm, y_att, w_ps, w_pa, proj, tm=1024, tn=1024):
    t, k = y_ssm.shape
    n = w_ps.shape[1]
    ga_off = (SSM_WIDTH + 3 * ATTN_WIDTH) // tn
    gb_off = ga_off + D_MODEL // tn
    return pl.pallas_call(
        _merge_kernel,
        out_shape=jax.ShapeDtypeStruct((t, n), BF16),
        grid=(t // tm, n // tn),
        in_specs=[pl.BlockSpec((tm, k), lambda i, j: (i, 0)),
                  pl.BlockSpec((tm, k), lambda i, j: (i, 0)),
                  pl.BlockSpec((k, tn), lambda i, j: (0, j)),
                  pl.BlockSpec((k, tn), lambda i, j: (0, j)),
                  pl.BlockSpec((tm, tn), lambda i, j: (i, ga_off + j)),
                  pl.BlockSpec((tm, tn), lambda i, j: (i, gb_off + j))],
        out_specs=pl.BlockSpec((tm, tn), lambda i, j: (i, j)),
        compiler_params=_cparams(("parallel", "parallel")),
        name="merge",
    )(y_ssm, y_att, w_ps, w_pa, proj, proj)


def _resid_mm_kernel(a_ref, w_ref, x_ref, g_ref, o_ref):
    acc = jnp.dot(a_ref[...], w_ref[...], preferred_element_type=F32)
    o_ref[...] = x_ref[...] + g_ref[0] * acc


def _resid_mm(a, w, x2d, mod4, g_idx, seq_len, tm=1024, tn=1024):
    t, k = a.shape
    n = w.shape[1]
    per_b = seq_len // tm
    return pl.pallas_call(
        _resid_mm_kernel,
        out_shape=jax.ShapeDtypeStruct((t, n), F32),
        grid=(t // tm, n // tn),
        in_specs=[pl.BlockSpec((tm, k), lambda i, j: (i, 0)),
                  pl.BlockSpec((k, tn), lambda i, j: (0, j)),
                  pl.BlockSpec((tm, tn), lambda i, j: (i, j)),
                  pl.BlockSpec((1, 1, tn), lambda i, j: ((i // per_b) * N_MOD + g_idx, 0, j))],
        out_specs=pl.BlockSpec((tm, tn), lambda i, j: (i, j)),
        compiler_params=_cparams(("parallel", "parallel")),
        name="resid_mm",
    )(a, w, x2d, mod4)


def _ff2_kernel(h_ref, w_ref, x_ref, g_ref, gf_ref, o_ref, acc_ref):
    k = pl.program_id(1)

    @pl.when(k == 0)
    def _():
        acc_ref[...] = jnp.zeros_like(acc_ref)

    acc_ref[...] += jnp.dot(h_ref[...], w_ref[...], preferred_element_type=F32)

    @pl.when(k == pl.num_programs(1) - 1)
    def _():
        xo = x_ref[...] + g_ref[0] * acc_ref[...]
        ms = jnp.mean(xo * xo, axis=-1, keepdims=True)
        o_ref[...] = xo * lax.rsqrt(ms + EPS) * gf_ref[...]


def _ff2(hid, w, x2d, mod4, g_idx, gf, seq_len, tm=512, tk=1024):
    t, kdim = hid.shape
    n = w.shape[1]
    per_b = seq_len // tm
    return pl.pallas_call(
        _ff2_kernel,
        out_shape=jax.ShapeDtypeStruct((t, n), F32),
        grid=(t // tm, kdim // tk),
        in_specs=[pl.BlockSpec((tm, tk), lambda i, k: (i, k)),
                  pl.BlockSpec((tk, n), lambda i, k: (k, 0)),
                  pl.BlockSpec((tm, n), lambda i, k: (i, 0)),
                  pl.BlockSpec((1, 1, n), lambda i, k: ((i // per_b) * N_MOD + g_idx, 0, 0)),
                  pl.BlockSpec((1, n), lambda i, k: (0, 0))],
        out_specs=pl.BlockSpec((tm, n), lambda i, k: (i, 0)),
        scratch_shapes=[pltpu.VMEM((tm, n), F32)],
        compiler_params=_cparams(("parallel", "arbitrary")),
        name="ff2_final",
    )(hid, w, x2d, mod4, gf.reshape(1, n))


def kernel(x, c, rel_bias, w_ada, b_ada, norm_mix_g, w_in, ssm_a_re, ssm_a_im, ssm_log_dt,
           ssm_b_re, ssm_b_im, ssm_c_re, ssm_c_im, ssm_d, w_glu, b_glu, w_proj_ssm,
           w_proj_attn, w_out, norm_mlp_g, w_ff1, w_ff2, norm_final_g):
    bsz, seq_len, d = x.shape
    depth = w_in.shape[0]
    assert depth == 1, "the final rms_norm is fused into the single layer's ff2 kernel"
    t = bsz * seq_len
    x2d = x.reshape(t, d)
    bias_tiles = _bias_tiles(rel_bias)
    n_gb = SSM_WIDTH // S5_CH

    for l in range(depth):
        mod4 = _mod(c, w_ada[l], b_ada[l]).reshape(bsz * N_MOD, 1, d)

        proj = _norm_mm(x2d, norm_mix_g[l], mod4, 1, 0, w_in[l].astype(BF16), seq_len, act="none")

        abar_re, abar_im, bb_re, bb_im = _s5prep(ssm_a_re[l], ssm_a_im[l], ssm_log_dt[l],
                                                 ssm_b_re[l], ssm_b_im[l])
        shp = (n_gb, S5_GROUPS_PER_BLOCK, SSM_GROUP, SSM_STATE)
        bmat = jnp.concatenate([_block_diag(bb_re.reshape(shp)),
                                _block_diag(bb_im.reshape(shp))], axis=-1).astype(BF16)
        c_re_t = jnp.transpose(ssm_c_re[l].reshape(shp), (0, 1, 3, 2))
        c_im_t = jnp.transpose(ssm_c_im[l].reshape(shp), (0, 1, 3, 2))
        cmat = jnp.concatenate([_block_diag(c_re_t), _block_diag(-c_im_t)], axis=1).astype(BF16)
        abar = jnp.concatenate([abar_re.reshape(n_gb, 1, S5_ST),
                                abar_im.reshape(n_gb, 1, S5_ST)], axis=-1)
        abar = jnp.broadcast_to(abar, (n_gb, bsz, 2 * S5_ST))

        u_tm = jnp.transpose(proj[:, :SSM_WIDTH].reshape(bsz, seq_len, SSM_WIDTH),
                             (1, 0, 2)).reshape(t, SSM_WIDTH)
        y_tm = _s5(u_tm, bmat, cmat, abar, ssm_d[l], bsz, seq_len)
        y_glu_tm = _glu(y_tm, w_glu[l].astype(BF16), b_glu[l])
        y_ssm = jnp.transpose(y_glu_tm.reshape(seq_len, bsz, SSM_WIDTH),
                              (1, 0, 2)).reshape(t, SSM_WIDTH)

        y_att = _moba(proj, bias_tiles, bsz, seq_len)

        merged = _merge(y_ssm, y_att, w_proj_ssm[l].astype(BF16), w_proj_attn[l].astype(BF16), proj)
        x2d = _resid_mm(merged, w_out[l].astype(BF16), x2d, mod4, 2, seq_len)

        hid = _norm_mm(x2d, norm_mlp_g[l], mod4, 4, 3, w_ff1[l].astype(BF16), seq_len, act="relu2")
        x2d = _ff2(hid, w_ff2[l].astype(BF16), x2d, mod4, 5, norm_final_g, seq_len)
    return x2d.reshape(bsz, seq_len, d)
```

```python
import functools
import math

import jax
import jax.numpy as jnp
from jax import lax
from jax.experimental import pallas as pl
from jax.experimental.pallas import tpu as pltpu

F32 = jnp.float32
BF16 = jnp.bfloat16

D_MODEL = 2048
SSM_WIDTH = 1024
SSM_GROUP = 16
SSM_GROUPS = 64
SSM_STATE = 64
ATTN_HEADS = 8
HEAD_DIM = 128
ATTN_WIDTH = 1024
MOBA_BLOCK = 256
MOBA_TOPK = 3
REL_BUCKETS = 32
REL_MAX_DIST = 128
D_FF = 4 * D_MODEL
N_MOD = 6
EPS = 1e-6
NEG_INF = -1e30
IN_WIDTH = SSM_WIDTH + 3 * ATTN_WIDTH + 2 * D_MODEL

VMEM_LIMIT_BYTES = 56 * 1024 * 1024

S5_GROUPS_PER_BLOCK = 8
S5_CH = S5_GROUPS_PER_BLOCK * SSM_GROUP
S5_ST = S5_GROUPS_PER_BLOCK * SSM_STATE
S5_TC = 128


def _cparams(sem):
    return pltpu.CompilerParams(dimension_semantics=sem,
                                vmem_limit_bytes=VMEM_LIMIT_BYTES)


def _mod_kernel(c_ref, w_ref, b_ref, o_ref):
    c = c_ref[...]
    ca = (c * jax.nn.sigmoid(c)).astype(BF16)
    o_ref[...] = jnp.dot(ca, w_ref[...].astype(BF16),
                         preferred_element_type=F32) + b_ref[...]


def _mod(c, w_ada, b_ada):
    bsz, d = c.shape
    n = w_ada.shape[1]
    tn = 1024
    return pl.pallas_call(
        _mod_kernel,
        out_shape=jax.ShapeDtypeStruct((bsz, n), F32),
        grid=(n // tn,),
        in_specs=[pl.BlockSpec((bsz, d), lambda j: (0, 0)),
                  pl.BlockSpec((d, tn), lambda j: (0, j)),
                  pl.BlockSpec((1, tn), lambda j: (0, j))],
        out_specs=pl.BlockSpec((bsz, tn), lambda j: (0, j)),
        compiler_params=_cparams(("parallel",)),
        name="mod",
    )(c, w_ada, b_ada.reshape(1, n))


def _norm_mm_kernel(x_ref, g_ref, sc_ref, sh_ref, w_ref, o_ref, h_ref, *, act, rows):
    @pl.when(pl.program_id(1) == 0)
    def _():
        g = g_ref[...]
        sc = 1.0 + sc_ref[0]
        sh = sh_ref[0]
        tm = x_ref.shape[0]
        for r in range(0, tm, rows):
            x = x_ref[r:r + rows, :]
            ms = jnp.mean(x * x, axis=-1, keepdims=True)
            y = x * lax.rsqrt(ms + EPS) * g
            h_ref[r:r + rows, :] = (y * sc + sh).astype(BF16)

    acc = jnp.dot(h_ref[...], w_ref[...], preferred_element_type=F32)
    if act == "relu2":
        acc = jnp.square(jnp.maximum(acc, 0.0))
    o_ref[...] = acc.astype(o_ref.dtype)


def _norm_mm(x2d, g, mod4, sc_idx, sh_idx, w, seq_len, *, act, tm=1024, tn=1024):
    t, d = x2d.shape
    n = w.shape[1]
    per_b = seq_len // tm
    return pl.pallas_call(
        functools.partial(_norm_mm_kernel, act=act, rows=256),
        out_shape=jax.ShapeDtypeStruct((t, n), BF16),
        grid=(t // tm, n // tn),
        in_specs=[pl.BlockSpec((tm, d), lambda i, j: (i, 0)),
                  pl.BlockSpec((1, d), lambda i, j: (0, 0)),
                  pl.BlockSpec((1, 1, d), lambda i, j: ((i // per_b) * N_MOD + sc_idx, 0, 0)),
                  pl.BlockSpec((1, 1, d), lambda i, j: ((i // per_b) * N_MOD + sh_idx, 0, 0)),
                  pl.BlockSpec((d, tn), lambda i, j: (0, j))],
        out_specs=pl.BlockSpec((tm, tn), lambda i, j: (i, j)),
        scratch_shapes=[pltpu.VMEM((tm, d), BF16)],
        compiler_params=_cparams(("parallel", "arbitrary")),
        name="norm_mm_" + act,
    )(x2d, g.reshape(1, d), mod4, mod4, w)


def _s5prep_kernel(are_ref, aim_ref, ldt_ref, bre_ref, bim_ref,
                   abre_ref, abim_ref, bbre_ref, bbim_ref):
    a_re = are_ref[...]
    a_im = aim_ref[...]
    dt = jnp.exp(ldt_ref[...])
    mag = jnp.exp(dt * a_re)
    abar_re = mag * jnp.cos(dt * a_im)
    abar_im = mag * jnp.sin(dt * a_im)
    den = a_re * a_re + a_im * a_im
    p_re = abar_re - 1.0
    f_re = (p_re * a_re + abar_im * a_im) / den
    f_im = (abar_im * a_re - p_re * a_im) / den
    abre_ref[...] = abar_re
    abim_ref[...] = abar_im
    b_re = bre_ref[...]
    b_im = bim_ref[...]
    bbre_ref[...] = f_re * b_re - f_im * b_im
    bbim_ref[...] = f_re * b_im + f_im * b_re


def _s5prep(a_re, a_im, log_dt, b_re, b_im):
    g, n = a_re.shape
    p = b_re.shape[2]
    b_re_t = jnp.transpose(b_re, (0, 2, 1))
    b_im_t = jnp.transpose(b_im, (0, 2, 1))
    return pl.pallas_call(
        _s5prep_kernel,
        out_shape=(jax.ShapeDtypeStruct((g, 1, n), F32),
                   jax.ShapeDtypeStruct((g, 1, n), F32),
                   jax.ShapeDtypeStruct((g, p, n), F32),
                   jax.ShapeDtypeStruct((g, p, n), F32)),
        name="s5prep",
    )(a_re.reshape(g, 1, n), a_im.reshape(g, 1, n), log_dt.reshape(g, 1, 1),
      b_re_t, b_im_t)


def _s5_kernel(u_ref, bm_ref, cm_ref, ab_ref, d_ref, y_ref, bu_ref, s_ref, st_ref, *, bsz):
    @pl.when(pl.program_id(1) == 0)
    def _():
        st_ref[...] = jnp.zeros_like(st_ref)

    u = u_ref[...]
    bu_ref[...] = jnp.dot(u, bm_ref[0], preferred_element_type=F32)
    a_re = ab_ref[0, :, :S5_ST]
    a_im = ab_ref[0, :, S5_ST:]
    n_pairs = u.shape[0] // (2 * bsz)

    def step(i, carry):
        s_re, s_im = carry
        rows = []
        for k in range(2):
            r = pl.multiple_of(i * (2 * bsz) + k * bsz, bsz)
            b = bu_ref[pl.ds(r, bsz), :]
            n_re = a_re * s_re - a_im * s_im + b[:, :S5_ST]
            n_im = a_re * s_im + a_im * s_re + b[:, S5_ST:]
            s_re, s_im = n_re, n_im
            rows.append(jnp.concatenate([s_re, s_im], axis=1))
        r0 = pl.multiple_of(i * (2 * bsz), 2 * bsz)
        s_ref[pl.ds(r0, 2 * bsz), :] = jnp.concatenate(rows, axis=0).astype(BF16)
        return s_re, s_im

    s_re, s_im = lax.fori_loop(0, n_pairs, step,
                               (st_ref[:, :S5_ST], st_ref[:, S5_ST:]), unroll=4)
    st_ref[:, :S5_ST] = s_re
    st_ref[:, S5_ST:] = s_im

    y = jnp.dot(s_ref[...], cm_ref[0], preferred_element_type=F32)
    y = y + d_ref[0] * u.astype(F32)
    y_ref[...] = jax.nn.gelu(y).astype(y_ref.dtype)


def _s5(u_tm, bmat, cmat, abar, d_skip, bsz, seq_len):
    rows = bsz * S5_TC
    n_gb = SSM_WIDTH // S5_CH
    return pl.pallas_call(
        functools.partial(_s5_kernel, bsz=bsz),
        out_shape=jax.ShapeDtypeStruct((seq_len * bsz, SSM_WIDTH), BF16),
        grid=(n_gb, seq_len // S5_TC),
        in_specs=[pl.BlockSpec((rows, S5_CH), lambda g, c: (c, g)),
                  pl.BlockSpec((1, S5_CH, 2 * S5_ST), lambda g, c: (g, 0, 0)),
                  pl.BlockSpec((1, 2 * S5_ST, S5_CH), lambda g, c: (g, 0, 0)),
                  pl.BlockSpec((1, bsz, 2 * S5_ST), lambda g, c: (g, 0, 0)),
                  pl.BlockSpec((1, 1, S5_CH), lambda g, c: (g, 0, 0))],
        out_specs=pl.BlockSpec((rows, S5_CH), lambda g, c: (c, g)),
        scratch_shapes=[pltpu.VMEM((rows, 2 * S5_ST), F32),
                        pltpu.VMEM((rows, 2 * S5_ST), BF16),
                        pltpu.VMEM((bsz, 2 * S5_ST), F32)],
        compiler_params=_cparams(("parallel", "arbitrary")),
        name="s5",
    )(u_tm, bmat, cmat, abar, d_skip.reshape(n_gb, 1, S5_CH))


def _block_diag(x):
    nb, ng, r, c = x.shape
    eye = jnp.eye(ng, dtype=x.dtype)
    return jnp.einsum("agrc,gh->agrhc", x, eye).reshape(nb, ng * r, ng * c)


def _glu_kernel(y_ref, w_ref, b_ref, o_ref):
    y = y_ref[...]
    z = jnp.dot(y, w_ref[...], preferred_element_type=F32) + b_ref[...]
    o_ref[...] = (y.astype(F32) * jax.nn.sigmoid(z)).astype(o_ref.dtype)


def _glu(y, w, b, tm=1024):
    t, n = y.shape
    return pl.pallas_call(
        _glu_kernel,
        out_shape=jax.ShapeDtypeStruct((t, n), BF16),
        grid=(t // tm,),
        in_specs=[pl.BlockSpec((tm, n), lambda i: (i, 0)),
                  pl.BlockSpec((n, n), lambda i: (0, 0)),
                  pl.BlockSpec((1, n), lambda i: (0, 0))],
        out_specs=pl.BlockSpec((tm, n), lambda i: (i, 0)),
        compiler_params=_cparams(("parallel",)),
        name="glu",
    )(y, w, b.reshape(1, n))


def _t5_bucket(rel):
    n = jnp.maximum(rel, 0)
    max_exact = REL_BUCKETS // 2
    nf = jnp.maximum(n, 1).astype(F32)
    large = max_exact + (jnp.log(nf / max_exact) / math.log(REL_MAX_DIST / max_exact)
                         * (REL_BUCKETS - max_exact)).astype(jnp.int32)
    large = jnp.minimum(large, REL_BUCKETS - 1)
    return jnp.where(n < max_exact, n, large)


def _bias_kernel(relb_ref, bucket_ref, o_ref):
    h = pl.program_id(0)
    bucket = bucket_ref[0]
    out = jnp.zeros(bucket.shape, F32)
    for b in range(REL_BUCKETS):
        out = jnp.where(bucket == b, relb_ref[b * ATTN_HEADS + h], out)
    o_ref[0, 0] = out


def _bias_tiles(rel_bias):
    i = jnp.arange(MOBA_BLOCK)
    rel = (jnp.arange(3)[:, None, None] * MOBA_BLOCK + i[None, None, :] - i[None, :, None])
    bucket = _t5_bucket(rel)
    return pl.pallas_call(
        _bias_kernel,
        out_shape=jax.ShapeDtypeStruct((ATTN_HEADS, 3, MOBA_BLOCK, MOBA_BLOCK), F32),
        grid=(ATTN_HEADS, 3),
        in_specs=[pl.BlockSpec(memory_space=pltpu.SMEM),
                  pl.BlockSpec((1, MOBA_BLOCK, MOBA_BLOCK), lambda h, d: (d, 0, 0))],
        out_specs=pl.BlockSpec((1, 1, MOBA_BLOCK, MOBA_BLOCK), lambda h, d: (h, d, 0, 0)),
        compiler_params=_cparams(("parallel", "parallel")),
        name="bias_tiles",
    )(rel_bias.astype(F32).reshape(-1), bucket)


_NT = (((1,), (1,)), ((), ()))


def _moba_qtile(own, slot0, q_ref, k_ref, vt_ref, bias_ref, o_ref, km_ref, s_ref):
    bs = MOBA_BLOCK
    scale = HEAD_DIM ** -0.5
    q = q_ref[own * bs:(own + 1) * bs, :]

    ranks = []
    if own > 0:
        gate = lax.dot_general(km_ref[...].astype(BF16), q, _NT, preferred_element_type=F32)
        blk = lax.broadcasted_iota(jnp.int32, gate.shape, 0)
        gm = jnp.where(blk < own, gate, NEG_INF)
        for n in range(own):
            g_n = gm[n:n + 1, :]
            ge = jnp.where(gm >= g_n, 1.0, 0.0)
            gt = jnp.where(gm > g_n, 1.0, 0.0)
            ranks.append(jnp.sum(jnp.where(blk < n, ge, gt), axis=0, keepdims=True))

    blocks = [own] + list(range(own))
    m = None
    for idx, j in enumerate(blocks):
        k_j = k_ref[j * bs:(j + 1) * bs, :]
        s = (lax.dot_general(k_j, q, _NT, preferred_element_type=F32) * scale
             + bias_ref[0, min(own - j, 2)])
        if j == own:
            key = lax.broadcasted_iota(jnp.int32, (bs, bs), 0)
            qry = lax.broadcasted_iota(jnp.int32, (bs, bs), 1)
            s = jnp.where(key <= qry, s, NEG_INF)
        else:
            s = jnp.where(ranks[j] < float(MOBA_TOPK), s, NEG_INF)
        s_ref[slot0 + idx] = s
        m_j = jnp.max(s, axis=0, keepdims=True)
        m = m_j if m is None else jnp.maximum(m, m_j)

    l = None
    acc = None
    for idx, j in enumerate(blocks):
        p = jnp.exp(s_ref[slot0 + idx] - m)
        l_j = jnp.sum(p, axis=0, keepdims=True)
        a_j = jnp.dot(vt_ref[:, j * bs:(j + 1) * bs], p.astype(BF16),
                      preferred_element_type=F32)
        l = l_j if l is None else l + l_j
        acc = a_j if acc is None else acc + a_j
    o_ref[own * bs:(own + 1) * bs, :] = (acc / l).T.astype(o_ref.dtype)


def _moba_kernel(q_ref, k_ref, v_ref, bias_ref, o_ref, km_ref, vt_ref, s_ref, *, n_blk):
    bs = MOBA_BLOCK
    pair = pl.program_id(2)

    @pl.when(pair == 0)
    def _():
        for n in range(n_blk):
            kb = k_ref[n * bs:(n + 1) * bs, :].astype(F32)
            km_ref[n:n + 1, :] = jnp.mean(kb, axis=0, keepdims=True)
            vt_ref[:, n * bs:(n + 1) * bs] = v_ref[n * bs:(n + 1) * bs, :].astype(F32).T.astype(BF16)

    for c in range(n_blk // 2):
        @pl.when(pair == c)
        def _(c=c):
            _moba_qtile(c, 0, q_ref, k_ref, vt_ref, bias_ref, o_ref, km_ref, s_ref)
            _moba_qtile(n_blk - 1 - c, c + 1, q_ref, k_ref, vt_ref, bias_ref, o_ref, km_ref, s_ref)


def _moba(proj, bias_tiles, bsz, seq_len):
    n_blk = seq_len // MOBA_BLOCK
    assert n_blk % 2 == 0
    q_off = SSM_WIDTH // HEAD_DIM
    k_off = q_off + ATTN_HEADS
    v_off = k_off + ATTN_HEADS
    return pl.pallas_call(
        functools.partial(_moba_kernel, n_blk=n_blk),
        out_shape=jax.ShapeDtypeStruct((bsz * seq_len, ATTN_WIDTH), BF16),
        grid=(bsz, ATTN_HEADS, n_blk // 2),
        in_specs=[pl.BlockSpec((seq_len, HEAD_DIM), lambda b, h, i: (b, q_off + h)),
                  pl.BlockSpec((seq_len, HEAD_DIM), lambda b, h, i: (b, k_off + h)),
                  pl.BlockSpec((seq_len, HEAD_DIM), lambda b, h, i: (b, v_off + h)),
                  pl.BlockSpec((1, 3, MOBA_BLOCK, MOBA_BLOCK), lambda b, h, i: (h, 0, 0, 0))],
        out_specs=pl.BlockSpec((seq_len, HEAD_DIM), lambda b, h, i: (b, h)),
        scratch_shapes=[pltpu.VMEM((n_blk, HEAD_DIM), F32),
                        pltpu.VMEM((HEAD_DIM, seq_len), BF16),
                        pltpu.VMEM((n_blk + 1, MOBA_BLOCK, MOBA_BLOCK), F32)],
        compiler_params=_cparams(("parallel", "parallel", "arbitrary")),
        name="moba",
    )(proj, proj, proj, bias_tiles)


def _merge_kernel(ys_ref, ya_ref, ws_ref, wa_ref, ga_ref, gb_ref, o_ref):
    a = jnp.dot(ys_ref[...], ws_ref[...], preferred_element_type=F32)
    b = jnp.dot(ya_ref[...], wa_ref[...], preferred_element_type=F32)
    o_ref[...] = (jax.nn.sigmoid(ga_ref[...].astype(F32)) * a
                  + jax.nn.sigmoid(gb_ref[...].astype(F32)) * b).astype(o_ref.dtype)


def _merge(y_ssm, y_att, w_ps, w_pa, proj, tm=1024, tn=1024):
    t, k = y_ssm.shape
    n = w_ps.shape[1]
    ga_off = (SSM_WIDTH + 3 * ATTN_WIDTH) // tn
    gb_off = ga_off + D_MODEL // tn
    return pl.pallas_call(
        _merge_kernel,
        out_shape=jax.ShapeDtypeStruct((t, n), BF16),
        grid=(t // tm, n // tn),
        in_specs=[pl.BlockSpec((tm, k), lambda i, j: (i, 0)),
                  pl.BlockSpec((tm, k), lambda i, j: (i, 0)),
                  pl.BlockSpec((k, tn), lambda i, j: (0, j)),
                  pl.BlockSpec((k, tn), lambda i, j: (0, j)),
                  pl.BlockSpec((tm, tn), lambda i, j: (i, ga_off + j)),
                  pl.BlockSpec((tm, tn), lambda i, j: (i, gb_off + j))],
        out_specs=pl.BlockSpec((tm, tn), lambda i, j: (i, j)),
        compiler_params=_cparams(("parallel", "parallel")),
        name="merge",
    )(y_ssm, y_att, w_ps, w_pa, proj, proj)


def _resid_mm_kernel(a_ref, w_ref, x_ref, g_ref, o_ref):
    acc = jnp.dot(a_ref[...], w_ref[...], preferred_element_type=F32)
    o_ref[...] = x_ref[...] + g_ref[0] * acc


def _resid_mm(a, w, x2d, mod4, g_idx, seq_len, tm=1024, tn=1024):
    t, k = a.shape
    n = w.shape[1]
    per_b = seq_len // tm
    return pl.pallas_call(
        _resid_mm_kernel,
        out_shape=jax.ShapeDtypeStruct((t, n), F32),
        grid=(t // tm, n // tn),
        in_specs=[pl.BlockSpec((tm, k), lambda i, j: (i, 0)),
                  pl.BlockSpec((k, tn), lambda i, j: (0, j)),
                  pl.BlockSpec((tm, tn), lambda i, j: (i, j)),
                  pl.BlockSpec((1, 1, tn), lambda i, j: ((i // per_b) * N_MOD + g_idx, 0, j))],
        out_specs=pl.BlockSpec((tm, tn), lambda i, j: (i, j)),
        compiler_params=_cparams(("parallel", "parallel")),
        name="resid_mm",
    )(a, w, x2d, mod4)


def _ff2_kernel(h_ref, w_ref, x_ref, g_ref, gf_ref, o_ref, acc_ref):
    k = pl.program_id(1)

    @pl.when(k == 0)
    def _():
        acc_ref[...] = jnp.zeros_like(acc_ref)

    acc_ref[...] += jnp.dot(h_ref[...], w_ref[...], preferred_element_type=F32)

    @pl.when(k == pl.num_programs(1) - 1)
    def _():
        xo = x_ref[...] + g_ref[0] * acc_ref[...]
        ms = jnp.mean(xo * xo, axis=-1, keepdims=True)
        o_ref[...] = xo * lax.rsqrt(ms + EPS) * gf_ref[...]


def _ff2(hid, w, x2d, mod4, g_idx, gf, seq_len, tm=512, tk=1024):
    t, kdim = hid.shape
    n = w.shape[1]
    per_b = seq_len // tm
    return pl.pallas_call(
        _ff2_kernel,
        out_shape=jax.ShapeDtypeStruct((t, n), F32),
        grid=(t // tm, kdim // tk),
        in_specs=[pl.BlockSpec((tm, tk), lambda i, k: (i, k)),
                  pl.BlockSpec((tk, n), lambda i, k: (k, 0)),
                  pl.BlockSpec((tm, n), lambda i, k: (i, 0)),
                  pl.BlockSpec((1, 1, n), lambda i, k: ((i // per_b) * N_MOD + g_idx, 0, 0)),
                  pl.BlockSpec((1, n), lambda i, k: (0, 0))],
        out_specs=pl.BlockSpec((tm, n), lambda i, k: (i, 0)),
        scratch_shapes=[pltpu.VMEM((tm, n), F32)],
        compiler_params=_cparams(("parallel", "arbitrary")),
        name="ff2_final",
    )(hid, w, x2d, mod4, gf.reshape(1, n))


def kernel(x, c, rel_bias, w_ada, b_ada, norm_mix_g, w_in, ssm_a_re, ssm_a_im, ssm_log_dt,
           ssm_b_re, ssm_b_im, ssm_c_re, ssm_c_im, ssm_d, w_glu, b_glu, w_proj_ssm,
           w_proj_attn, w_out, norm_mlp_g, w_ff1, w_ff2, norm_final_g):
    bsz, seq_len, d = x.shape
    depth = w_in.shape[0]
    assert depth == 1, "the final rms_norm is fused into the single layer's ff2 kernel"
    t = bsz * seq_len
    x2d = x.reshape(t, d)
    bias_tiles = _bias_tiles(rel_bias)
    n_gb = SSM_WIDTH // S5_CH

    for l in range(depth):
        mod4 = _mod(c, w_ada[l], b_ada[l]).reshape(bsz * N_MOD, 1, d)

        proj = _norm_mm(x2d, norm_mix_g[l], mod4, 1, 0, w_in[l].astype(BF16), seq_len, act="none")

        abar_re, abar_im, bb_re, bb_im = _s5prep(ssm_a_re[l], ssm_a_im[l], ssm_log_dt[l],
                                                 ssm_b_re[l], ssm_b_im[l])
        shp = (n_gb, S5_GROUPS_PER_BLOCK, SSM_GROUP, SSM_STATE)
        bmat = jnp.concatenate([_block_diag(bb_re.reshape(shp)),
                                _block_diag(bb_im.reshape(shp))], axis=-1).astype(BF16)
        c_re_t = jnp.transpose(ssm_c_re[l].reshape(shp), (0, 1, 3, 2))
        c_im_t = jnp.transpose(ssm_c_im[l].reshape(shp), (0, 1, 3, 2))
        cmat = jnp.concatenate([_block_diag(c_re_t), _block_diag(-c_im_t)], axis=1).astype(BF16)
        abar = jnp.concatenate([abar_re.reshape(n_gb, 1, S5_ST),
                                abar_im.reshape(n_gb, 1, S5_ST)], axis=-1)
        abar = jnp.broadcast_to(abar, (n_gb, bsz, 2 * S5_ST))

        u_tm = jnp.transpose(proj[:, :SSM_WIDTH].reshape(bsz, seq_len, SSM_WIDTH),
                             (1, 0, 2)).reshape(t, SSM_WIDTH)
        y_tm = _s5(u_tm, bmat, cmat, abar, ssm_d[l], bsz, seq_len)
        y_glu_tm = _glu(y_tm, w_glu[l].astype(BF16), b_glu[l])
        y_ssm = jnp.transpose(y_glu_tm.reshape(seq_len, bsz, SSM_WIDTH),
                              (1, 0, 2)).reshape(t, SSM_WIDTH)

        y_att = _moba(proj, bias_tiles, bsz, seq_len)

        merged = _merge(y_ssm, y_att, w_proj_ssm[l].astype(BF16), w_proj_attn[l].astype(BF16), proj)
        x2d = _resid_mm(merged, w_out[l].astype(BF16), x2d, mod4, 2, seq_len)

        hid = _norm_mm(x2d, norm_mlp_g[l], mod4, 4, 3, w_ff1[l].astype(BF16), seq_len, act="relu2")
        x2d = _ff2(hid, w_ff2[l].astype(BF16), x2d, mod4, 5, norm_final_g, seq_len)
    return x2d.reshape(bsz, seq_len, d)
```

```python
import functools
import math

import jax
import jax.numpy as jnp
from jax import lax
from jax.experimental import pallas as pl
from jax.experimental.pallas import tpu as pltpu

F32 = jnp.float32
BF16 = jnp.bfloat16

D_MODEL = 2048
SSM_WIDTH = 1024
SSM_GROUP = 16
SSM_GROUPS = 64
SSM_STATE = 64
ATTN_HEADS = 8
HEAD_DIM = 128
ATTN_WIDTH = 1024
MOBA_BLOCK = 256
MOBA_TOPK = 3
REL_BUCKETS = 32
REL_MAX_DIST = 128
D_FF = 4 * D_MODEL
N_MOD = 6
EPS = 1e-6
NEG_INF = -1e30
IN_WIDTH = SSM_WIDTH + 3 * ATTN_WIDTH + 2 * D_MODEL

VMEM_LIMIT_BYTES = 56 * 1024 * 1024

S5_GROUPS_PER_BLOCK = 8
S5_CH = S5_GROUPS_PER_BLOCK * SSM_GROUP
S5_ST = S5_GROUPS_PER_BLOCK * SSM_STATE
S5_TC = 128


def _cparams(sem):
    return pltpu.CompilerParams(dimension_semantics=sem,
                                vmem_limit_bytes=VMEM_LIMIT_BYTES)


def _mod_kernel(c_ref, w_ref, b_ref, o_ref):
    c = c_ref[...]
    ca = (c * jax.nn.sigmoid(c)).astype(BF16)
    o_ref[...] = jnp.dot(ca, w_ref[...].astype(BF16),
                         preferred_element_type=F32) + b_ref[...]


def _mod(c, w_ada, b_ada):
    bsz, d = c.shape
    n = w_ada.shape[1]
    tn = 1024
    return pl.pallas_call(
        _mod_kernel,
        out_shape=jax.ShapeDtypeStruct((bsz, n), F32),
        grid=(n // tn,),
        in_specs=[pl.BlockSpec((bsz, d), lambda j: (0, 0)),
                  pl.BlockSpec((d, tn), lambda j: (0, j)),
                  pl.BlockSpec((1, tn), lambda j: (0, j))],
        out_specs=pl.BlockSpec((bsz, tn), lambda j: (0, j)),
        compiler_params=_cparams(("parallel",)),
        name="mod",
    )(c, w_ada, b_ada.reshape(1, n))


def _norm_mm_kernel(x_ref, g_ref, sc_ref, sh_ref, w_ref, o_ref, h_ref, *, act, rows):
    def mm(h):
        acc = jnp.dot(h, w_ref[...], preferred_element_type=F32)
        if act == "relu2":
            acc = jnp.square(jnp.maximum(acc, 0.0))
        return acc.astype(o_ref.dtype)

    @pl.when(pl.program_id(1) == 0)
    def _():
        g = g_ref[...]
        sc = 1.0 + sc_ref[0]
        sh = sh_ref[0]
        tm = x_ref.shape[0]
        for r in range(0, tm, rows):
            x = x_ref[r:r + rows, :]
            ms = jnp.mean(x * x, axis=-1, keepdims=True)
            y = x * lax.rsqrt(ms + EPS) * g
            h = (y * sc + sh).astype(BF16)
            h_ref[r:r + rows, :] = h
            o_ref[r:r + rows, :] = mm(h)

    @pl.when(pl.program_id(1) != 0)
    def _():
        o_ref[...] = mm(h_ref[...])


def _norm_mm(x2d, g, mod4, sc_idx, sh_idx, w, seq_len, *, act, tm=1024, tn=1024):
    t, d = x2d.shape
    n = w.shape[1]
    per_b = seq_len // tm
    return pl.pallas_call(
        functools.partial(_norm_mm_kernel, act=act, rows=256),
        out_shape=jax.ShapeDtypeStruct((t, n), BF16),
        grid=(t // tm, n // tn),
        in_specs=[pl.BlockSpec((tm, d), lambda i, j: (i, 0)),
                  pl.BlockSpec((1, d), lambda i, j: (0, 0)),
                  pl.BlockSpec((1, 1, d), lambda i, j: ((i // per_b) * N_MOD + sc_idx, 0, 0)),
                  pl.BlockSpec((1, 1, d), lambda i, j: ((i // per_b) * N_MOD + sh_idx, 0, 0)),
                  pl.BlockSpec((d, tn), lambda i, j: (0, j))],
        out_specs=pl.BlockSpec((tm, tn), lambda i, j: (i, j)),
        scratch_shapes=[pltpu.VMEM((tm, d), BF16)],
        compiler_params=_cparams(("parallel", "arbitrary")),
        name="norm_mm_" + act,
    )(x2d, g.reshape(1, d), mod4, mod4, w)


def _s5prep_kernel(are_ref, aim_ref, ldt_ref, bre_ref, bim_ref,
                   abre_ref, abim_ref, bbre_ref, bbim_ref):
    a_re = are_ref[...]
    a_im = aim_ref[...]
    dt = jnp.exp(ldt_ref[...])
    mag = jnp.exp(dt * a_re)
    abar_re = mag * jnp.cos(dt * a_im)
    abar_im = mag * jnp.sin(dt * a_im)
    den = a_re * a_re + a_im * a_im
    p_re = abar_re - 1.0
    f_re = (p_re * a_re + abar_im * a_im) / den
    f_im = (abar_im * a_re - p_re * a_im) / den
    abre_ref[...] = abar_re
    abim_ref[...] = abar_im
    b_re = bre_ref[...]
    b_im = bim_ref[...]
    bbre_ref[...] = f_re * b_re - f_im * b_im
    bbim_ref[...] = f_re * b_im + f_im * b_re


def _s5prep(a_re, a_im, log_dt, b_re, b_im):
    g, n = a_re.shape
    p = b_re.shape[2]
    b_re_t = jnp.transpose(b_re, (0, 2, 1))
    b_im_t = jnp.transpose(b_im, (0, 2, 1))
    return pl.pallas_call(
        _s5prep_kernel,
        out_shape=(jax.ShapeDtypeStruct((g, 1, n), F32),
                   jax.ShapeDtypeStruct((g, 1, n), F32),
                   jax.ShapeDtypeStruct((g, p, n), F32),
                   jax.ShapeDtypeStruct((g, p, n), F32)),
        name="s5prep",
    )(a_re.reshape(g, 1, n), a_im.reshape(g, 1, n), log_dt.reshape(g, 1, 1),
      b_re_t, b_im_t)


def _s5_kernel(u_ref, bm_ref, cm_ref, ab_ref, d_ref, y_ref, bu_ref, s_ref, st_ref, *, bsz):
    @pl.when(pl.program_id(1) == 0)
    def _():
        st_ref[...] = jnp.zeros_like(st_ref)

    u = u_ref[...]
    bu_ref[...] = jnp.dot(u, bm_ref[0], preferred_element_type=F32)
    a_re = ab_ref[0, :, :S5_ST]
    a_im = ab_ref[0, :, S5_ST:]
    n_pairs = u.shape[0] // (2 * bsz)

    def step(i, carry):
        s_re, s_im = carry
        rows = []
        for k in range(2):
            r = pl.multiple_of(i * (2 * bsz) + k * bsz, bsz)
            b = bu_ref[pl.ds(r, bsz), :]
            n_re = a_re * s_re - a_im * s_im + b[:, :S5_ST]
            n_im = a_re * s_im + a_im * s_re + b[:, S5_ST:]
            s_re, s_im = n_re, n_im
            rows.append(jnp.concatenate([s_re, s_im], axis=1))
        r0 = pl.multiple_of(i * (2 * bsz), 2 * bsz)
        s_ref[pl.ds(r0, 2 * bsz), :] = jnp.concatenate(rows, axis=0).astype(BF16)
        return s_re, s_im

    s_re, s_im = lax.fori_loop(0, n_pairs, step,
                               (st_ref[:, :S5_ST], st_ref[:, S5_ST:]), unroll=4)
    st_ref[:, :S5_ST] = s_re
    st_ref[:, S5_ST:] = s_im

    y = jnp.dot(s_ref[...], cm_ref[0], preferred_element_type=F32)
    y = y + d_ref[0] * u.astype(F32)
    y_ref[...] = jax.nn.gelu(y).astype(y_ref.dtype)


def _s5(u_tm, bmat, cmat, abar, d_skip, bsz, seq_len):
    rows = bsz * S5_TC
    n_gb = SSM_WIDTH // S5_CH
    return pl.pallas_call(
        functools.partial(_s5_kernel, bsz=bsz),
        out_shape=jax.ShapeDtypeStruct((seq_len * bsz, SSM_WIDTH), BF16),
        grid=(n_gb, seq_len // S5_TC),
        in_specs=[pl.BlockSpec((rows, S5_CH), lambda g, c: (c, g)),
                  pl.BlockSpec((1, S5_CH, 2 * S5_ST), lambda g, c: (g, 0, 0)),
                  pl.BlockSpec((1, 2 * S5_ST, S5_CH), lambda g, c: (g, 0, 0)),
                  pl.BlockSpec((1, bsz, 2 * S5_ST), lambda g, c: (g, 0, 0)),
                  pl.BlockSpec((1, 1, S5_CH), lambda g, c: (g, 0, 0))],
        out_specs=pl.BlockSpec((rows, S5_CH), lambda g, c: (c, g)),
        scratch_shapes=[pltpu.VMEM((rows, 2 * S5_ST), F32),
                        pltpu.VMEM((rows, 2 * S5_ST), BF16),
                        pltpu.VMEM((bsz, 2 * S5_ST), F32)],
        compiler_params=_cparams(("parallel", "arbitrary")),
        name="s5",
    )(u_tm, bmat, cmat, abar, d_skip.reshape(n_gb, 1, S5_CH))


def _block_diag(x):
    nb, ng, r, c = x.shape
    eye = jnp.eye(ng, dtype=x.dtype)
    return jnp.einsum("agrc,gh->agrhc", x, eye).reshape(nb, ng * r, ng * c)


def _glu_kernel(y_ref, w_ref, b_ref, o_ref):
    y = y_ref[...]
    z = jnp.dot(y, w_ref[...], preferred_element_type=F32) + b_ref[...]
    o_ref[...] = (y.astype(F32) * jax.nn.sigmoid(z)).astype(o_ref.dtype)


def _glu(y, w, b, tm=1024):
    t, n = y.shape
    return pl.pallas_call(
        _glu_kernel,
        out_shape=jax.ShapeDtypeStruct((t, n), BF16),
        grid=(t // tm,),
        in_specs=[pl.BlockSpec((tm, n), lambda i: (i, 0)),
                  pl.BlockSpec((n, n), lambda i: (0, 0)),
                  pl.BlockSpec((1, n), lambda i: (0, 0))],
        out_specs=pl.BlockSpec((tm, n), lambda i: (i, 0)),
        compiler_params=_cparams(("parallel",)),
        name="glu",
    )(y, w, b.reshape(1, n))


def _t5_bucket(rel):
    n = jnp.maximum(rel, 0)
    max_exact = REL_BUCKETS // 2
    nf = jnp.maximum(n, 1).astype(F32)
    large = max_exact + (jnp.log(nf / max_exact) / math.log(REL_MAX_DIST / max_exact)
                         * (REL_BUCKETS - max_exact)).astype(jnp.int32)
    large = jnp.minimum(large, REL_BUCKETS - 1)
    return jnp.where(n < max_exact, n, large)


def _bias_kernel(relb_ref, bucket_ref, o_ref):
    h = pl.program_id(0)
    bucket = bucket_ref[0]
    out = jnp.zeros(bucket.shape, F32)
    for b in range(REL_BUCKETS):
        out = jnp.where(bucket == b, relb_ref[b * ATTN_HEADS + h], out)
    o_ref[0, 0] = out


def _bias_tiles(rel_bias):
    i = jnp.arange(MOBA_BLOCK)
    rel = (jnp.arange(3)[:, None, None] * MOBA_BLOCK + i[None, None, :] - i[None, :, None])
    bucket = _t5_bucket(rel)
    return pl.pallas_call(
        _bias_kernel,
        out_shape=jax.ShapeDtypeStruct((ATTN_HEADS, 3, MOBA_BLOCK, MOBA_BLOCK), F32),
        grid=(ATTN_HEADS, 3),
        in_specs=[pl.BlockSpec(memory_space=pltpu.SMEM),
                  pl.BlockSpec((1, MOBA_BLOCK, MOBA_BLOCK), lambda h, d: (d, 0, 0))],
        out_specs=pl.BlockSpec((1, 1, MOBA_BLOCK, MOBA_BLOCK), lambda h, d: (h, d, 0, 0)),
        compiler_params=_cparams(("parallel", "parallel")),
        name="bias_tiles",
    )(rel_bias.astype(F32).reshape(-1), bucket)


_NT = (((1,), (1,)), ((), ()))


def _moba_qtile(own, slot0, q_ref, k_ref, vt_ref, bias_ref, o_ref, km_ref, s_ref):
    bs = MOBA_BLOCK
    scale = HEAD_DIM ** -0.5
    q = q_ref[own * bs:(own + 1) * bs, :]

    ranks = []
    if own > 0:
        gate = lax.dot_general(km_ref[...].astype(BF16), q, _NT, preferred_element_type=F32)
        blk = lax.broadcasted_iota(jnp.int32, gate.shape, 0)
        gm = jnp.where(blk < own, gate, NEG_INF)
        for n in range(own):
            g_n = gm[n:n + 1, :]
            ge = jnp.where(gm >= g_n, 1.0, 0.0)
            gt = jnp.where(gm > g_n, 1.0, 0.0)
            ranks.append(jnp.sum(jnp.where(blk < n, ge, gt), axis=0, keepdims=True))

    blocks = [own] + list(range(own))
    m = None
    for idx, j in enumerate(blocks):
        k_j = k_ref[j * bs:(j + 1) * bs, :]
        s = (lax.dot_general(k_j, q, _NT, preferred_element_type=F32) * scale
             + bias_ref[0, min(own - j, 2)])
        if j == own:
            key = lax.broadcasted_iota(jnp.int32, (bs, bs), 0)
            qry = lax.broadcasted_iota(jnp.int32, (bs, bs), 1)
            s = jnp.where(key <= qry, s, NEG_INF)
        else:
            s = jnp.where(ranks[j] < float(MOBA_TOPK), s, NEG_INF)
        s_ref[slot0 + idx] = s
        m_j = jnp.max(s, axis=0, keepdims=True)
        m = m_j if m is None else jnp.maximum(m, m_j)

    l = None
    acc = None
    for idx, j in enumerate(blocks):
        p = jnp.exp(s_ref[slot0 + idx] - m)
        l_j = jnp.sum(p, axis=0, keepdims=True)
        a_j = jnp.dot(vt_ref[:, j * bs:(j + 1) * bs], p.astype(BF16),
                      preferred_element_type=F32)
        l = l_j if l is None else l + l_j
        acc = a_j if acc is None else acc + a_j
    o_ref[own * bs:(own + 1) * bs, :] = (acc / l).T.astype(o_ref.dtype)


def _moba_kernel(q_ref, k_ref, v_ref, bias_ref, o_ref, km_ref, vt_ref, s_ref, *, n_blk):
    bs = MOBA_BLOCK
    pair = pl.program_id(2)

    @pl.when(pair == 0)
    def _():
        for n in range(n_blk):
            kb = k_ref[n * bs:(n + 1) * bs, :].astype(F32)
            km_ref[n:n + 1, :] = jnp.mean(kb, axis=0, keepdims=True)
            vt_ref[:, n * bs:(n + 1) * bs] = v_ref[n * bs:(n + 1) * bs, :].astype(F32).T.astype(BF16)

    for c in range(n_blk // 2):
        @pl.when(pair == c)
        def _(c=c):
            _moba_qtile(c, 0, q_ref, k_ref, vt_ref, bias_ref, o_ref, km_ref, s_ref)
            _moba_qtile(n_blk - 1 - c, c + 1, q_ref, k_ref, vt_ref, bias_ref, o_ref, km_ref, s_ref)


def _moba(proj, bias_tiles, bsz, seq_len):
    n_blk = seq_len // MOBA_BLOCK
    assert n_blk % 2 == 0
    q_off = SSM_WIDTH // HEAD_DIM
    k_off = q_off + ATTN_HEADS
    v_off = k_off + ATTN_HEADS
    return pl.pallas_call(
        functools.partial(_moba_kernel, n_blk=n_blk),
        out_shape=jax.ShapeDtypeStruct((bsz * seq_len, ATTN_WIDTH), BF16),
        grid=(bsz, ATTN_HEADS, n_blk // 2),
        in_specs=[pl.BlockSpec((seq_len, HEAD_DIM), lambda b, h, i: (b, q_off + h)),
                  pl.BlockSpec((seq_len, HEAD_DIM), lambda b, h, i: (b, k_off + h)),
                  pl.BlockSpec((seq_len, HEAD_DIM), lambda b, h, i: (b, v_off + h)),
                  pl.BlockSpec((1, 3, MOBA_BLOCK, MOBA_BLOCK), lambda b, h, i: (h, 0, 0, 0))],
        out_specs=pl.BlockSpec((seq_len, HEAD_DIM), lambda b, h, i: (b, h)),
        scratch_shapes=[pltpu.VMEM((n_blk, HEAD_DIM), F32),
                        pltpu.VMEM((HEAD_DIM, seq_len), BF16),
                        pltpu.VMEM((n_blk + 1, MOBA_BLOCK, MOBA_BLOCK), F32)],
        compiler_params=_cparams(("parallel", "parallel", "arbitrary")),
        name="moba",
    )(proj, proj, proj, bias_tiles)


def _merge_kernel(ys_ref, ya_ref, ws_ref, wa_ref, ga_ref, gb_ref, o_ref):
    a = jnp.dot(ys_ref[...], ws_ref[...], preferred_element_type=F32)
    b = jnp.dot(ya_ref[...], wa_ref[...], preferred_element_type=F32)
    o_ref[...] = (jax.nn.sigmoid(ga_ref[...].astype(F32)) * a
                  + jax.nn.sigmoid(gb_ref[...].astype(F32)) * b).astype(o_ref.dtype)


def _merge(y_ssm, y_att, w_ps, w_pa, proj, tm=512, tn=2048):
    t, k = y_ssm.shape
    n = w_ps.shape[1]
    ga_off = (SSM_WIDTH + 3 * ATTN_WIDTH) // tn
    gb_off = ga_off + D_MODEL // tn
    return pl.pallas_call(
        _merge_kernel,
        out_shape=jax.ShapeDtypeStruct((t, n), BF16),
        grid=(t // tm, n // tn),
        in_specs=[pl.BlockSpec((tm, k), lambda i, j: (i, 0)),
                  pl.BlockSpec((tm, k), lambda i, j: (i, 0)),
                  pl.BlockSpec((k, tn), lambda i, j: (0, j)),
                  pl.BlockSpec((k, tn), lambda i, j: (0, j)),
                  pl.BlockSpec((tm, tn), lambda i, j: (i, ga_off + j)),
                  pl.BlockSpec((tm, tn), lambda i, j: (i, gb_off + j))],
        out_specs=pl.BlockSpec((tm, tn), lambda i, j: (i, j)),
        compiler_params=_cparams(("parallel", "parallel")),
        name="merge",
    )(y_ssm, y_att, w_ps, w_pa, proj, proj)


def _resid_mm_kernel(a_ref, w_ref, x_ref, g_ref, o_ref):
    acc = jnp.dot(a_ref[...], w_ref[...], preferred_element_type=F32)
    o_ref[...] = x_ref[...] + g_ref[0] * acc


def _resid_mm(a, w, x2d, mod4, g_idx, seq_len, tm=512, tn=2048):
    t, k = a.shape
    n = w.shape[1]
    per_b = seq_len // tm
    return pl.pallas_call(
        _resid_mm_kernel,
        out_shape=jax.ShapeDtypeStruct((t, n), F32),
        grid=(t // tm, n // tn),
        in_specs=[pl.BlockSpec((tm, k), lambda i, j: (i, 0)),
                  pl.BlockSpec((k, tn), lambda i, j: (0, j)),
                  pl.BlockSpec((tm, tn), lambda i, j: (i, j)),
                  pl.BlockSpec((1, 1, tn), lambda i, j: ((i // per_b) * N_MOD + g_idx, 0, j))],
        out_specs=pl.BlockSpec((tm, tn), lambda i, j: (i, j)),
        compiler_params=_cparams(("parallel", "parallel")),
        name="resid_mm",
    )(a, w, x2d, mod4)


def _ff2_kernel(h_ref, w_ref, x_ref, g_ref, gf_ref, o_ref, *, rows):
    k = pl.program_id(1)

    @pl.when(k == 0)
    def _():
        o_ref[...] = jnp.dot(h_ref[...], w_ref[...], preferred_element_type=F32)

    @pl.when(k != 0)
    def _():
        o_ref[...] += jnp.dot(h_ref[...], w_ref[...], preferred_element_type=F32)

    @pl.when(k == pl.num_programs(1) - 1)
    def _():
        for r in range(0, o_ref.shape[0], rows):
            xo = x_ref[r:r + rows, :] + g_ref[0] * o_ref[r:r + rows, :]
            ms = jnp.mean(xo * xo, axis=-1, keepdims=True)
            o_ref[r:r + rows, :] = xo * lax.rsqrt(ms + EPS) * gf_ref[...]


def _ff2(hid, w, x2d, mod4, g_idx, gf, seq_len, tm=1024, tk=1024):
    t, kdim = hid.shape
    n = w.shape[1]
    per_b = seq_len // tm
    return pl.pallas_call(
        functools.partial(_ff2_kernel, rows=256),
        out_shape=jax.ShapeDtypeStruct((t, n), F32),
        grid=(t // tm, kdim // tk),
        in_specs=[pl.BlockSpec((tm, tk), lambda i, k: (i, k)),
                  pl.BlockSpec((tk, n), lambda i, k: (k, 0)),
                  pl.BlockSpec((tm, n), lambda i, k: (i, 0)),
                  pl.BlockSpec((1, 1, n), lambda i, k: ((i // per_b) * N_MOD + g_idx, 0, 0)),
                  pl.BlockSpec((1, n), lambda i, k: (0, 0))],
        out_specs=pl.BlockSpec((tm, n), lambda i, k: (i, 0)),
        compiler_params=_cparams(("parallel", "arbitrary")),
        name="ff2_final",
    )(hid, w, x2d, mod4, gf.reshape(1, n))


def kernel(x, c, rel_bias, w_ada, b_ada, norm_mix_g, w_in, ssm_a_re, ssm_a_im, ssm_log_dt,
           ssm_b_re, ssm_b_im, ssm_c_re, ssm_c_im, ssm_d, w_glu, b_glu, w_proj_ssm,
           w_proj_attn, w_out, norm_mlp_g, w_ff1, w_ff2, norm_final_g):
    bsz, seq_len, d = x.shape
    depth = w_in.shape[0]
    assert depth == 1, "the final rms_norm is fused into the single layer's ff2 kernel"
    t = bsz * seq_len
    x2d = x.reshape(t, d)
    bias_tiles = _bias_tiles(rel_bias)
    n_gb = SSM_WIDTH // S5_CH

    for l in range(depth):
        mod4 = _mod(c, w_ada[l], b_ada[l]).reshape(bsz * N_MOD, 1, d)

        proj = _norm_mm(x2d, norm_mix_g[l], mod4, 1, 0, w_in[l].astype(BF16), seq_len, act="none")

        abar_re, abar_im, bb_re, bb_im = _s5prep(ssm_a_re[l], ssm_a_im[l], ssm_log_dt[l],
                                                 ssm_b_re[l], ssm_b_im[l])
        shp = (n_gb, S5_GROUPS_PER_BLOCK, SSM_GROUP, SSM_STATE)
        bmat = jnp.concatenate([_block_diag(bb_re.reshape(shp)),
                                _block_diag(bb_im.reshape(shp))], axis=-1).astype(BF16)
        c_re_t = jnp.transpose(ssm_c_re[l].reshape(shp), (0, 1, 3, 2))
        c_im_t = jnp.transpose(ssm_c_im[l].reshape(shp), (0, 1, 3, 2))
        cmat = jnp.concatenate([_block_diag(c_re_t), _block_diag(-c_im_t)], axis=1).astype(BF16)
        abar = jnp.concatenate([abar_re.reshape(n_gb, 1, S5_ST),
                                abar_im.reshape(n_gb, 1, S5_ST)], axis=-1)
        abar = jnp.broadcast_to(abar, (n_gb, bsz, 2 * S5_ST))

        u_tm = jnp.transpose(proj[:, :SSM_WIDTH].reshape(bsz, seq_len, SSM_WIDTH),
                             (1, 0, 2)).reshape(t, SSM_WIDTH)
        y_tm = _s5(u_tm, bmat, cmat, abar, ssm_d[l], bsz, seq_len)
        y_glu_tm = _glu(y_tm, w_glu[l].astype(BF16), b_glu[l])
        y_ssm = jnp.transpose(y_glu_tm.reshape(seq_len, bsz, SSM_WIDTH),
                              (1, 0, 2)).reshape(t, SSM_WIDTH)

        y_att = _moba(proj, bias_tiles, bsz, seq_len)

        merged = _merge(y_ssm, y_att, w_proj_ssm[l].astype(BF16), w_proj_attn[l].astype(BF16), proj)
        x2d = _resid_mm(merged, w_out[l].astype(BF16), x2d, mod4, 2, seq_len)

        hid = _norm_mm(x2d, norm_mlp_g[l], mod4, 4, 3, w_ff1[l].astype(BF16), seq_len, act="relu2")
        x2d = _ff2(hid, w_ff2[l].astype(BF16), x2d, mod4, 5, norm_final_g, seq_len)
    return x2d.reshape(bsz, seq_len, d)
```

```python
import functools
import math

import jax
import jax.numpy as jnp
from jax import lax
from jax.experimental import pallas as pl
from jax.experimental.pallas import tpu as pltpu

F32 = jnp.float32
BF16 = jnp.bfloat16

D_MODEL = 2048
SSM_WIDTH = 1024
SSM_GROUP = 16
SSM_GROUPS = 64
SSM_STATE = 64
ATTN_HEADS = 8
HEAD_DIM = 128
ATTN_WIDTH = 1024
MOBA_BLOCK = 256
MOBA_TOPK = 3
REL_BUCKETS = 32
REL_MAX_DIST = 128
D_FF = 4 * D_MODEL
N_MOD = 6
EPS = 1e-6
NEG_INF = -1e30
LOG2E = math.log2(math.e)
IN_WIDTH = SSM_WIDTH + 3 * ATTN_WIDTH + 2 * D_MODEL

VMEM_LIMIT_BYTES = 56 * 1024 * 1024

S5_GROUPS_PER_BLOCK = 8
S5_CH = S5_GROUPS_PER_BLOCK * SSM_GROUP
S5_ST = S5_GROUPS_PER_BLOCK * SSM_STATE
S5_TC = 128


def _cparams(sem):
    return pltpu.CompilerParams(dimension_semantics=sem,
                                vmem_limit_bytes=VMEM_LIMIT_BYTES)


def _mod_kernel(c_ref, w_ref, b_ref, o_ref):
    c = c_ref[...]
    ca = (c * jax.nn.sigmoid(c)).astype(BF16)
    o_ref[...] = jnp.dot(ca, w_ref[...].astype(BF16),
                         preferred_element_type=F32) + b_ref[...]


def _mod(c, w_ada, b_ada):
    bsz, d = c.shape
    n = w_ada.shape[1]
    tn = 1024
    return pl.pallas_call(
        _mod_kernel,
        out_shape=jax.ShapeDtypeStruct((bsz, n), F32),
        grid=(n // tn,),
        in_specs=[pl.BlockSpec((bsz, d), lambda j: (0, 0)),
                  pl.BlockSpec((d, tn), lambda j: (0, j)),
                  pl.BlockSpec((1, tn), lambda j: (0, j))],
        out_specs=pl.BlockSpec((bsz, tn), lambda j: (0, j)),
        compiler_params=_cparams(("parallel",)),
        name="mod",
    )(c, w_ada, b_ada.reshape(1, n))


def _norm_mm_kernel(x_ref, g_ref, sc_ref, sh_ref, w_ref, o_ref, h_ref, *, act, rows):
    def mm(h):
        acc = jnp.dot(h, w_ref[...], preferred_element_type=F32)
        if act == "relu2":
            acc = jnp.square(jnp.maximum(acc, 0.0))
        return acc.astype(o_ref.dtype)

    @pl.when(pl.program_id(1) == 0)
    def _():
        g = g_ref[...]
        sc = 1.0 + sc_ref[0]
        sh = sh_ref[0]
        tm = x_ref.shape[0]
        for r in range(0, tm, rows):
            x = x_ref[r:r + rows, :]
            ms = jnp.mean(x * x, axis=-1, keepdims=True)
            y = x * lax.rsqrt(ms + EPS) * g
            h = (y * sc + sh).astype(BF16)
            h_ref[r:r + rows, :] = h
            o_ref[r:r + rows, :] = mm(h)

    @pl.when(pl.program_id(1) != 0)
    def _():
        o_ref[...] = mm(h_ref[...])


def _norm_mm(x2d, g, mod4, sc_idx, sh_idx, w, seq_len, *, act, tm=1024, tn=1024):
    t, d = x2d.shape
    n = w.shape[1]
    per_b = seq_len // tm
    return pl.pallas_call(
        functools.partial(_norm_mm_kernel, act=act, rows=256),
        out_shape=jax.ShapeDtypeStruct((t, n), BF16),
        grid=(t // tm, n // tn),
        in_specs=[pl.BlockSpec((tm, d), lambda i, j: (i, 0)),
                  pl.BlockSpec((1, d), lambda i, j: (0, 0)),
                  pl.BlockSpec((1, 1, d), lambda i, j: ((i // per_b) * N_MOD + sc_idx, 0, 0)),
                  pl.BlockSpec((1, 1, d), lambda i, j: ((i // per_b) * N_MOD + sh_idx, 0, 0)),
                  pl.BlockSpec((d, tn), lambda i, j: (0, j))],
        out_specs=pl.BlockSpec((tm, tn), lambda i, j: (i, j)),
        scratch_shapes=[pltpu.VMEM((tm, d), BF16)],
        compiler_params=_cparams(("parallel", "arbitrary")),
        name="norm_mm_" + act,
    )(x2d, g.reshape(1, d), mod4, mod4, w)


def _s5prep_kernel(are_ref, aim_ref, ldt_ref, bre_ref, bim_ref,
                   abre_ref, abim_ref, bbre_ref, bbim_ref):
    a_re = are_ref[...]
    a_im = aim_ref[...]
    dt = jnp.exp(ldt_ref[...])
    mag = jnp.exp(dt * a_re)
    abar_re = mag * jnp.cos(dt * a_im)
    abar_im = mag * jnp.sin(dt * a_im)
    den = a_re * a_re + a_im * a_im
    p_re = abar_re - 1.0
    f_re = (p_re * a_re + abar_im * a_im) / den
    f_im = (abar_im * a_re - p_re * a_im) / den
    abre_ref[...] = abar_re
    abim_ref[...] = abar_im
    b_re = bre_ref[...]
    b_im = bim_ref[...]
    bbre_ref[...] = f_re * b_re - f_im * b_im
    bbim_ref[...] = f_re * b_im + f_im * b_re


def _s5prep(a_re, a_im, log_dt, b_re, b_im):
    g, n = a_re.shape
    p = b_re.shape[2]
    b_re_t = jnp.transpose(b_re, (0, 2, 1))
    b_im_t = jnp.transpose(b_im, (0, 2, 1))
    return pl.pallas_call(
        _s5prep_kernel,
        out_shape=(jax.ShapeDtypeStruct((g, 1, n), F32),
                   jax.ShapeDtypeStruct((g, 1, n), F32),
                   jax.ShapeDtypeStruct((g, p, n), F32),
                   jax.ShapeDtypeStruct((g, p, n), F32)),
        name="s5prep",
    )(a_re.reshape(g, 1, n), a_im.reshape(g, 1, n), log_dt.reshape(g, 1, 1),
      b_re_t, b_im_t)


def _s5_kernel(u_ref, bm_ref, cm_ref, ab_ref, d_ref, y_ref, bu_ref, s_ref, st_ref, *, bsz):
    @pl.when(pl.program_id(1) == 0)
    def _():
        st_ref[...] = jnp.zeros_like(st_ref)

    u = u_ref[...]
    bu_ref[...] = jnp.dot(u, bm_ref[0], preferred_element_type=F32)
    a_re = ab_ref[0, :, :S5_ST]
    a_im = ab_ref[0, :, S5_ST:]
    n_pairs = u.shape[0] // (2 * bsz)

    def step(i, carry):
        s_re, s_im = carry
        rows = []
        for k in range(2):
            r = pl.multiple_of(i * (2 * bsz) + k * bsz, bsz)
            b = bu_ref[pl.ds(r, bsz), :]
            n_re = a_re * s_re - a_im * s_im + b[:, :S5_ST]
            n_im = a_re * s_im + a_im * s_re + b[:, S5_ST:]
            s_re, s_im = n_re, n_im
            rows.append(jnp.concatenate([s_re, s_im], axis=1))
        r0 = pl.multiple_of(i * (2 * bsz), 2 * bsz)
        s_ref[pl.ds(r0, 2 * bsz), :] = jnp.concatenate(rows, axis=0).astype(BF16)
        return s_re, s_im

    s_re, s_im = lax.fori_loop(0, n_pairs, step,
                               (st_ref[:, :S5_ST], st_ref[:, S5_ST:]), unroll=4)
    st_ref[:, :S5_ST] = s_re
    st_ref[:, S5_ST:] = s_im

    y = jnp.dot(s_ref[...], cm_ref[0], preferred_element_type=F32)
    y = y + d_ref[0] * u.astype(F32)
    y_ref[...] = jax.nn.gelu(y).astype(y_ref.dtype)


def _s5(u_tm, bmat, cmat, abar, d_skip, bsz, seq_len):
    rows = bsz * S5_TC
    n_gb = SSM_WIDTH // S5_CH
    return pl.pallas_call(
        functools.partial(_s5_kernel, bsz=bsz),
        out_shape=jax.ShapeDtypeStruct((seq_len * bsz, SSM_WIDTH), BF16),
        grid=(n_gb, seq_len // S5_TC),
        in_specs=[pl.BlockSpec((rows, S5_CH), lambda g, c: (c, g)),
                  pl.BlockSpec((1, S5_CH, 2 * S5_ST), lambda g, c: (g, 0, 0)),
                  pl.BlockSpec((1, 2 * S5_ST, S5_CH), lambda g, c: (g, 0, 0)),
                  pl.BlockSpec((1, bsz, 2 * S5_ST), lambda g, c: (g, 0, 0)),
                  pl.BlockSpec((1, 1, S5_CH), lambda g, c: (g, 0, 0))],
        out_specs=pl.BlockSpec((rows, S5_CH), lambda g, c: (c, g)),
        scratch_shapes=[pltpu.VMEM((rows, 2 * S5_ST), F32),
                        pltpu.VMEM((rows, 2 * S5_ST), BF16),
                        pltpu.VMEM((bsz, 2 * S5_ST), F32)],
        compiler_params=_cparams(("parallel", "arbitrary")),
        name="s5",
    )(u_tm, bmat, cmat, abar, d_skip.reshape(n_gb, 1, S5_CH))


def _block_diag(x):
    nb, ng, r, c = x.shape
    eye = jnp.eye(ng, dtype=x.dtype)
    return jnp.einsum("agrc,gh->agrhc", x, eye).reshape(nb, ng * r, ng * c)


def _glu_kernel(y_ref, w_ref, b_ref, o_ref):
    y = y_ref[...]
    z = jnp.dot(y, w_ref[...], preferred_element_type=F32) + b_ref[...]
    o_ref[...] = (y.astype(F32) * jax.nn.sigmoid(z)).astype(o_ref.dtype)


def _glu(y, w, b, tm=1024):
    t, n = y.shape
    return pl.pallas_call(
        _glu_kernel,
        out_shape=jax.ShapeDtypeStruct((t, n), BF16),
        grid=(t // tm,),
        in_specs=[pl.BlockSpec((tm, n), lambda i: (i, 0)),
                  pl.BlockSpec((n, n), lambda i: (0, 0)),
                  pl.BlockSpec((1, n), lambda i: (0, 0))],
        out_specs=pl.BlockSpec((tm, n), lambda i: (i, 0)),
        compiler_params=_cparams(("parallel",)),
        name="glu",
    )(y, w, b.reshape(1, n))


def _t5_bucket(rel):
    n = jnp.maximum(rel, 0)
    max_exact = REL_BUCKETS // 2
    nf = jnp.maximum(n, 1).astype(F32)
    large = max_exact + (jnp.log(nf / max_exact) / math.log(REL_MAX_DIST / max_exact)
                         * (REL_BUCKETS - max_exact)).astype(jnp.int32)
    large = jnp.minimum(large, REL_BUCKETS - 1)
    return jnp.where(n < max_exact, n, large)


def _bias_kernel(relb_ref, bucket_ref, o_ref):
    h = pl.program_id(0)
    bucket = bucket_ref[0]
    out = jnp.zeros(bucket.shape, F32)
    for b in range(REL_BUCKETS):
        out = jnp.where(bucket == b, relb_ref[b * ATTN_HEADS + h], out)
    o_ref[0, 0] = out * LOG2E


def _bias_tiles(rel_bias):
    i = jnp.arange(MOBA_BLOCK)
    rel = (jnp.arange(3)[:, None, None] * MOBA_BLOCK + i[None, None, :] - i[None, :, None])
    bucket = _t5_bucket(rel)
    return pl.pallas_call(
        _bias_kernel,
        out_shape=jax.ShapeDtypeStruct((ATTN_HEADS, 3, MOBA_BLOCK, MOBA_BLOCK), F32),
        grid=(ATTN_HEADS, 3),
        in_specs=[pl.BlockSpec(memory_space=pltpu.SMEM),
                  pl.BlockSpec((1, MOBA_BLOCK, MOBA_BLOCK), lambda h, d: (d, 0, 0))],
        out_specs=pl.BlockSpec((1, 1, MOBA_BLOCK, MOBA_BLOCK), lambda h, d: (h, d, 0, 0)),
        compiler_params=_cparams(("parallel", "parallel")),
        name="bias_tiles",
    )(rel_bias.astype(F32).reshape(-1), bucket)


_NT = (((1,), (1,)), ((), ()))


def _moba_qtile(own, slot0, q_ref, k_ref, vt_ref, bias_ref, o_ref, km_ref, s_ref):
    bs = MOBA_BLOCK
    scale = HEAD_DIM ** -0.5
    q = q_ref[own * bs:(own + 1) * bs, :]

    ranks = []
    if own > 0:
        gate = lax.dot_general(km_ref[...].astype(BF16), q, _NT, preferred_element_type=F32)
        blk = lax.broadcasted_iota(jnp.int32, gate.shape, 0)
        gm = jnp.where(blk < own, gate, NEG_INF)
        for n in range(own):
            g_n = gm[n:n + 1, :]
            ge = jnp.where(gm >= g_n, 1.0, 0.0)
            gt = jnp.where(gm > g_n, 1.0, 0.0)
            ranks.append(jnp.sum(jnp.where(blk < n, ge, gt), axis=0, keepdims=True))

    c1 = scale * LOG2E
    blocks = [own] + list(range(own))
    m = None
    for idx, j in enumerate(blocks):
        k_j = k_ref[j * bs:(j + 1) * bs, :]
        raw = lax.dot_general(k_j, q, _NT, preferred_element_type=F32)
        if j == own:
            key = lax.broadcasted_iota(jnp.int32, (bs, bs), 0)
            qry = lax.broadcasted_iota(jnp.int32, (bs, bs), 1)
            s = jnp.where(key <= qry, raw * c1 + bias_ref[0, 0], NEG_INF)
        elif own - j == 1:
            s = jnp.where(ranks[j] < float(MOBA_TOPK), raw * c1 + bias_ref[0, 1], NEG_INF)
        else:
            row = jnp.where(ranks[j] < float(MOBA_TOPK), bias_ref[0, 2, 0:1, :], NEG_INF)
            s = raw * c1 + row
        s_ref[slot0 + idx] = s
        m_j = jnp.max(s, axis=0, keepdims=True)
        m = m_j if m is None else jnp.maximum(m, m_j)

    acc = None
    for idx, j in enumerate(blocks):
        p = jnp.exp2((s_ref[slot0 + idx] - m).astype(BF16))
        a_j = jnp.dot(vt_ref[:, j * bs:(j + 1) * bs], p, preferred_element_type=F32)
        acc = a_j if acc is None else acc + a_j
    o = acc[:HEAD_DIM, :] / acc[HEAD_DIM:HEAD_DIM + 1, :]
    o_ref[own * bs:(own + 1) * bs, :] = o.T.astype(o_ref.dtype)


def _moba_kernel(q_ref, k_ref, v_ref, bias_ref, o_ref, km_ref, vt_ref, s_ref, *, n_blk):
    bs = MOBA_BLOCK
    vt_ref[HEAD_DIM:, :] = jnp.ones((vt_ref.shape[0] - HEAD_DIM, vt_ref.shape[1]), BF16)
    for n in range(n_blk):
        kb = k_ref[n * bs:(n + 1) * bs, :].astype(F32)
        km_ref[n:n + 1, :] = jnp.mean(kb, axis=0, keepdims=True)
        vt_ref[:HEAD_DIM, n * bs:(n + 1) * bs] = (
            v_ref[n * bs:(n + 1) * bs, :].astype(F32).T.astype(BF16))

    slot = 0
    for own in range(n_blk):
        _moba_qtile(own, slot, q_ref, k_ref, vt_ref, bias_ref, o_ref, km_ref, s_ref)
        slot += own + 1


def _moba(proj, bias_tiles, bsz, seq_len):
    n_blk = seq_len // MOBA_BLOCK
    q_off = SSM_WIDTH // HEAD_DIM
    k_off = q_off + ATTN_HEADS
    v_off = k_off + ATTN_HEADS
    n_slots = n_blk * (n_blk + 1) // 2
    return pl.pallas_call(
        functools.partial(_moba_kernel, n_blk=n_blk),
        out_shape=jax.ShapeDtypeStruct((bsz * seq_len, ATTN_WIDTH), BF16),
        grid=(bsz, ATTN_HEADS),
        in_specs=[pl.BlockSpec((seq_len, HEAD_DIM), lambda b, h: (b, q_off + h)),
                  pl.BlockSpec((seq_len, HEAD_DIM), lambda b, h: (b, k_off + h)),
                  pl.BlockSpec((seq_len, HEAD_DIM), lambda b, h: (b, v_off + h)),
                  pl.BlockSpec((1, 3, MOBA_BLOCK, MOBA_BLOCK), lambda b, h: (h, 0, 0, 0))],
        out_specs=pl.BlockSpec((seq_len, HEAD_DIM), lambda b, h: (b, h)),
        scratch_shapes=[pltpu.VMEM((n_blk, HEAD_DIM), F32),
                        pltpu.VMEM((HEAD_DIM + 16, seq_len), BF16),
                        pltpu.VMEM((n_slots, MOBA_BLOCK, MOBA_BLOCK), F32)],
        compiler_params=_cparams(("parallel", "parallel")),
        name="moba",
    )(proj, proj, proj, bias_tiles)


def _merge_kernel(ys_ref, ya_ref, ws_ref, wa_ref, ga_ref, gb_ref, o_ref):
    a = jnp.dot(ys_ref[...], ws_ref[...], preferred_element_type=F32)
    b = jnp.dot(ya_ref[...], wa_ref[...], preferred_element_type=F32)
    o_ref[...] = (jax.nn.sigmoid(ga_ref[...].astype(F32)) * a
                  + jax.nn.sigmoid(gb_ref[...].astype(F32)) * b).astype(o_ref.dtype)


def _merge(y_ssm, y_att, w_ps, w_pa, proj, tm=512, tn=2048):
    t, k = y_ssm.shape
    n = w_ps.shape[1]
    ga_off = (SSM_WIDTH + 3 * ATTN_WIDTH) // tn
    gb_off = ga_off + D_MODEL // tn
    return pl.pallas_call(
        _merge_kernel,
        out_shape=jax.ShapeDtypeStruct((t, n), BF16),
        grid=(t // tm, n // tn),
        in_specs=[pl.BlockSpec((tm, k), lambda i, j: (i, 0)),
                  pl.BlockSpec((tm, k), lambda i, j: (i, 0)),
                  pl.BlockSpec((k, tn), lambda i, j: (0, j)),
                  pl.BlockSpec((k, tn), lambda i, j: (0, j)),
                  pl.BlockSpec((tm, tn), lambda i, j: (i, ga_off + j)),
                  pl.BlockSpec((tm, tn), lambda i, j: (i, gb_off + j))],
        out_specs=pl.BlockSpec((tm, tn), lambda i, j: (i, j)),
        compiler_params=_cparams(("parallel", "parallel")),
        name="merge",
    )(y_ssm, y_att, w_ps, w_pa, proj, proj)


def _resid_mm_kernel(a_ref, w_ref, x_ref, g_ref, o_ref):
    acc = jnp.dot(a_ref[...], w_ref[...], preferred_element_type=F32)
    o_ref[...] = x_ref[...] + g_ref[0] * acc


def _resid_mm(a, w, x2d, mod4, g_idx, seq_len, tm=512, tn=2048):
    t, k = a.shape
    n = w.shape[1]
    per_b = seq_len // tm
    return pl.pallas_call(
        _resid_mm_kernel,
        out_shape=jax.ShapeDtypeStruct((t, n), F32),
        grid=(t // tm, n // tn),
        in_specs=[pl.BlockSpec((tm, k), lambda i, j: (i, 0)),
                  pl.BlockSpec((k, tn), lambda i, j: (0, j)),
                  pl.BlockSpec((tm, tn), lambda i, j: (i, j)),
                  pl.BlockSpec((1, 1, tn), lambda i, j: ((i // per_b) * N_MOD + g_idx, 0, j))],
        out_specs=pl.BlockSpec((tm, tn), lambda i, j: (i, j)),
        compiler_params=_cparams(("parallel", "parallel")),
        name="resid_mm",
    )(a, w, x2d, mod4)


def _ff2_kernel(h_ref, w_ref, x_ref, g_ref, gf_ref, o_ref, *, rows):
    k = pl.program_id(1)

    @pl.when(k == 0)
    def _():
        o_ref[...] = jnp.dot(h_ref[...], w_ref[...], preferred_element_type=F32)

    @pl.when(k != 0)
    def _():
        o_ref[...] += jnp.dot(h_ref[...], w_ref[...], preferred_element_type=F32)

    @pl.when(k == pl.num_programs(1) - 1)
    def _():
        for r in range(0, o_ref.shape[0], rows):
            xo = x_ref[r:r + rows, :] + g_ref[0] * o_ref[r:r + rows, :]
            ms = jnp.mean(xo * xo, axis=-1, keepdims=True)
            o_ref[r:r + rows, :] = xo * lax.rsqrt(ms + EPS) * gf_ref[...]


def _ff2(hid, w, x2d, mod4, g_idx, gf, seq_len, tm=1024, tk=1024):
    t, kdim = hid.shape
    n = w.shape[1]
    per_b = seq_len // tm
    return pl.pallas_call(
        functools.partial(_ff2_kernel, rows=256),
        out_shape=jax.ShapeDtypeStruct((t, n), F32),
        grid=(t // tm, kdim // tk),
        in_specs=[pl.BlockSpec((tm, tk), lambda i, k: (i, k)),
                  pl.BlockSpec((tk, n), lambda i, k: (k, 0)),
                  pl.BlockSpec((tm, n), lambda i, k: (i, 0)),
                  pl.BlockSpec((1, 1, n), lambda i, k: ((i // per_b) * N_MOD + g_idx, 0, 0)),
                  pl.BlockSpec((1, n), lambda i, k: (0, 0))],
        out_specs=pl.BlockSpec((tm, n), lambda i, k: (i, 0)),
        compiler_params=_cparams(("parallel", "arbitrary")),
        name="ff2_final",
    )(hid, w, x2d, mod4, gf.reshape(1, n))


def kernel(x, c, rel_bias, w_ada, b_ada, norm_mix_g, w_in, ssm_a_re, ssm_a_im, ssm_log_dt,
           ssm_b_re, ssm_b_im, ssm_c_re, ssm_c_im, ssm_d, w_glu, b_glu, w_proj_ssm,
           w_proj_attn, w_out, norm_mlp_g, w_ff1, w_ff2, norm_final_g):
    bsz, seq_len, d = x.shape
    depth = w_in.shape[0]
    assert depth == 1, "the final rms_norm is fused into the single layer's ff2 kernel"
    t = bsz * seq_len
    x2d = x.reshape(t, d)
    bias_tiles = _bias_tiles(rel_bias)
    n_gb = SSM_WIDTH // S5_CH

    for l in range(depth):
        mod4 = _mod(c, w_ada[l], b_ada[l]).reshape(bsz * N_MOD, 1, d)

        proj = _norm_mm(x2d, norm_mix_g[l], mod4, 1, 0, w_in[l].astype(BF16), seq_len, act="none")

        abar_re, abar_im, bb_re, bb_im = _s5prep(ssm_a_re[l], ssm_a_im[l], ssm_log_dt[l],
                                                 ssm_b_re[l], ssm_b_im[l])
        shp = (n_gb, S5_GROUPS_PER_BLOCK, SSM_GROUP, SSM_STATE)
        bmat = jnp.concatenate([_block_diag(bb_re.reshape(shp)),
                                _block_diag(bb_im.reshape(shp))], axis=-1).astype(BF16)
        c_re_t = jnp.transpose(ssm_c_re[l].reshape(shp), (0, 1, 3, 2))
        c_im_t = jnp.transpose(ssm_c_im[l].reshape(shp), (0, 1, 3, 2))
        cmat = jnp.concatenate([_block_diag(c_re_t), _block_diag(-c_im_t)], axis=1).astype(BF16)
        abar = jnp.concatenate([abar_re.reshape(n_gb, 1, S5_ST),
                                abar_im.reshape(n_gb, 1, S5_ST)], axis=-1)
        abar = jnp.broadcast_to(abar, (n_gb, bsz, 2 * S5_ST))

        u_tm = jnp.transpose(proj[:, :SSM_WIDTH].reshape(bsz, seq_len, SSM_WIDTH),
                             (1, 0, 2)).reshape(t, SSM_WIDTH)
        y_tm = _s5(u_tm, bmat, cmat, abar, ssm_d[l], bsz, seq_len)
        y_glu_tm = _glu(y_tm, w_glu[l].astype(BF16), b_glu[l])
        y_ssm = jnp.transpose(y_glu_tm.reshape(seq_len, bsz, SSM_WIDTH),
                              (1, 0, 2)).reshape(t, SSM_WIDTH)

        y_att = _moba(proj, bias_tiles, bsz, seq_len)

        merged = _merge(y_ssm, y_att, w_proj_ssm[l].astype(BF16), w_proj_attn[l].astype(BF16), proj)
        x2d = _resid_mm(merged, w_out[l].astype(BF16), x2d, mod4, 2, seq_len)

        hid = _norm_mm(x2d, norm_mlp_g[l], mod4, 4, 3, w_ff1[l].astype(BF16), seq_len, act="relu2")
        x2d = _ff2(hid, w_ff2[l].astype(BF16), x2d, mod4, 5, norm_final_g, seq_len)
    return x2d.reshape(bsz, seq_len, d)
```

```python
import functools
import math

import jax
import jax.numpy as jnp
from jax import lax
from jax.experimental import pallas as pl
from jax.experimental.pallas import tpu as pltpu

F32 = jnp.float32
BF16 = jnp.bfloat16

D_MODEL = 2048
SSM_WIDTH = 1024
SSM_GROUP = 16
SSM_GROUPS = 64
SSM_STATE = 64
ATTN_HEADS = 8
HEAD_DIM = 128
ATTN_WIDTH = 1024
MOBA_BLOCK = 256
MOBA_TOPK = 3
REL_BUCKETS = 32
REL_MAX_DIST = 128
D_FF = 4 * D_MODEL
N_MOD = 6
EPS = 1e-6
NEG_INF = -1e30
LOG2E = math.log2(math.e)
IN_WIDTH = SSM_WIDTH + 3 * ATTN_WIDTH + 2 * D_MODEL

VMEM_LIMIT_BYTES = 56 * 1024 * 1024

S5_GROUPS_PER_BLOCK = 8
S5_CH = S5_GROUPS_PER_BLOCK * SSM_GROUP
S5_ST = S5_GROUPS_PER_BLOCK * SSM_STATE
S5_TC = 256


def _cparams(sem):
    return pltpu.CompilerParams(dimension_semantics=sem,
                                vmem_limit_bytes=VMEM_LIMIT_BYTES)


def _mod_kernel(c_ref, w_ref, b_ref, o_ref):
    c = c_ref[...]
    ca = (c * jax.nn.sigmoid(c)).astype(BF16)
    o_ref[...] = jnp.dot(ca, w_ref[...].astype(BF16),
                         preferred_element_type=F32) + b_ref[...]


def _mod(c, w_ada, b_ada):
    bsz, d = c.shape
    n = w_ada.shape[1]
    tn = 1024
    return pl.pallas_call(
        _mod_kernel,
        out_shape=jax.ShapeDtypeStruct((bsz, n), F32),
        grid=(n // tn,),
        in_specs=[pl.BlockSpec((bsz, d), lambda j: (0, 0)),
                  pl.BlockSpec((d, tn), lambda j: (0, j)),
                  pl.BlockSpec((1, tn), lambda j: (0, j))],
        out_specs=pl.BlockSpec((bsz, tn), lambda j: (0, j)),
        compiler_params=_cparams(("parallel",)),
        name="mod",
    )(c, w_ada, b_ada.reshape(1, n))


def _norm_mm_kernel(x_ref, g_ref, sc_ref, sh_ref, w_ref, o_ref, h_ref, *, act, rows):
    def mm(h):
        acc = jnp.dot(h, w_ref[...], preferred_element_type=F32)
        if act == "relu2":
            acc = jnp.square(jnp.maximum(acc, 0.0))
        return acc.astype(o_ref.dtype)

    @pl.when(pl.program_id(1) == 0)
    def _():
        g = g_ref[...]
        sc = 1.0 + sc_ref[0]
        sh = sh_ref[0]
        tm = x_ref.shape[0]
        for r in range(0, tm, rows):
            x = x_ref[r:r + rows, :]
            ms = jnp.mean(x * x, axis=-1, keepdims=True)
            y = x * lax.rsqrt(ms + EPS) * g
            h = (y * sc + sh).astype(BF16)
            h_ref[r:r + rows, :] = h
            o_ref[r:r + rows, :] = mm(h)

    @pl.when(pl.program_id(1) != 0)
    def _():
        o_ref[...] = mm(h_ref[...])


def _norm_mm(x2d, g, mod4, sc_idx, sh_idx, w, seq_len, *, act, tm=1024, tn=1024):
    t, d = x2d.shape
    n = w.shape[1]
    per_b = seq_len // tm
    return pl.pallas_call(
        functools.partial(_norm_mm_kernel, act=act, rows=256),
        out_shape=jax.ShapeDtypeStruct((t, n), BF16),
        grid=(t // tm, n // tn),
        in_specs=[pl.BlockSpec((tm, d), lambda i, j: (i, 0)),
                  pl.BlockSpec((1, d), lambda i, j: (0, 0)),
                  pl.BlockSpec((1, 1, d), lambda i, j: ((i // per_b) * N_MOD + sc_idx, 0, 0)),
                  pl.BlockSpec((1, 1, d), lambda i, j: ((i // per_b) * N_MOD + sh_idx, 0, 0)),
                  pl.BlockSpec((d, tn), lambda i, j: (0, j))],
        out_specs=pl.BlockSpec((tm, tn), lambda i, j: (i, j)),
        scratch_shapes=[pltpu.VMEM((tm, d), BF16)],
        compiler_params=_cparams(("parallel", "arbitrary")),
        name="norm_mm_" + act,
    )(x2d, g.reshape(1, d), mod4, mod4, w)


def _s5prep_kernel(are_ref, aim_ref, ldt_ref, bre_ref, bim_ref, cre_ref, cim_ref,
                   a2_ref, bw_ref, cw_ref, kw_ref):
    a_re = are_ref[...]
    a_im = aim_ref[...]
    dt = jnp.exp(ldt_ref[...])
    mag = jnp.exp(dt * a_re)
    abar_re = mag * jnp.cos(dt * a_im)
    abar_im = mag * jnp.sin(dt * a_im)
    den = a_re * a_re + a_im * a_im
    p_re = abar_re - 1.0
    f_re = (p_re * a_re + abar_im * a_im) / den
    f_im = (abar_im * a_re - p_re * a_im) / den
    b_re = bre_ref[...]
    b_im = bim_ref[...]

    def cmul(x_re, x_im, y_re, y_im):
        return x_re * y_re - x_im * y_im, x_re * y_im + x_im * y_re

    def cat(re, im):
        return jnp.concatenate([re, im], axis=-1)

    bb_re, bb_im = cmul(f_re, f_im, b_re, b_im)
    a2_re, a2_im = cmul(abar_re, abar_im, abar_re, abar_im)
    abb_re, abb_im = cmul(abar_re, abar_im, bb_re, bb_im)
    c_re = cre_ref[...]
    c_im = cim_ref[...]
    ca_re, ca_im = cmul(c_re, c_im, abar_re, abar_im)
    ca2_re, ca2_im = cmul(c_re, c_im, a2_re, a2_im)

    a2_ref[...] = cat(a2_re, a2_im)
    y0 = cat(bb_re, bb_im)
    bw_ref[0] = cat(abb_re, abb_im)
    bw_ref[1] = y0
    x0 = cat(c_re, -c_im)
    x1 = cat(ca_re, -ca_im)
    cw_ref[0] = x1
    cw_ref[1] = cat(ca2_re, -ca2_im)
    kw_ref[0] = jnp.einsum("gpn,gqn->gpq", x0, y0, precision=lax.Precision.HIGHEST,
                           preferred_element_type=F32)
    kw_ref[1] = jnp.einsum("gpn,gqn->gpq", x1, y0, precision=lax.Precision.HIGHEST,
                           preferred_element_type=F32)


def _s5prep(a_re, a_im, log_dt, b_re, b_im, c_re, c_im):
    g, n = a_re.shape
    p = b_re.shape[2]
    b_re_t = jnp.transpose(b_re, (0, 2, 1))
    b_im_t = jnp.transpose(b_im, (0, 2, 1))
    return pl.pallas_call(
        _s5prep_kernel,
        out_shape=(jax.ShapeDtypeStruct((g, 1, 2 * n), F32),
                   jax.ShapeDtypeStruct((2, g, p, 2 * n), F32),
                   jax.ShapeDtypeStruct((2, g, p, 2 * n), F32),
                   jax.ShapeDtypeStruct((2, g, p, p), F32)),
        name="s5prep",
    )(a_re.reshape(g, 1, n), a_im.reshape(g, 1, n), log_dt.reshape(g, 1, 1),
      b_re_t, b_im_t, c_re, c_im)


def _block_diag(x):
    nb, ng, r, c = x.shape
    eye = jnp.eye(ng, dtype=x.dtype)
    return jnp.einsum("agrc,gh->agrhc", x, eye).reshape(nb, ng * r, ng * c)


def _s5_weights(a2, bw, cw, kw, d_skip, bsz):
    n_gb = SSM_GROUPS // S5_GROUPS_PER_BLOCK
    n = SSM_STATE
    blk = (n_gb, S5_GROUPS_PER_BLOCK)

    def bd(x):
        return _block_diag(x.reshape(blk + x.shape[1:]))

    def bd_t(x):
        return bd(jnp.transpose(x, (0, 2, 1)))

    bmat = jnp.concatenate(
        [jnp.concatenate([bd(bw[i, :, :, :n]), bd(bw[i, :, :, n:])], axis=-1) for i in range(2)],
        axis=1).astype(BF16)
    cmat = jnp.concatenate(
        [jnp.concatenate([bd_t(cw[i, :, :, :n]), bd_t(cw[i, :, :, n:])], axis=1) for i in range(2)],
        axis=-1).astype(BF16)
    k0 = bd_t(kw[0])
    k1 = bd_t(kw[1])
    kmat = jnp.concatenate([jnp.concatenate([k0, k1], axis=-1),
                            jnp.concatenate([jnp.zeros_like(k0), k0], axis=-1)],
                           axis=1).astype(BF16)
    a2v = jnp.concatenate([a2[:, 0, :n].reshape(n_gb, 1, S5_ST),
                           a2[:, 0, n:].reshape(n_gb, 1, S5_ST)], axis=-1)
    a2v = jnp.broadcast_to(a2v, (n_gb, bsz, 2 * S5_ST))
    d = d_skip.reshape(n_gb, 1, S5_CH)
    return bmat, cmat, kmat, a2v, jnp.concatenate([d, d], axis=-1)


def _s5_kernel(u_ref, bw_ref, cw_ref, kw_ref, a2_ref, d_ref, y_ref, w_ref, s_ref, st_ref,
               *, bsz, n_sub):
    @pl.when(pl.program_id(1) == 0)
    def _():
        st_ref[...] = jnp.zeros_like(st_ref)

    a_re = a2_ref[0, :, :S5_ST]
    a_im = a2_ref[0, :, S5_ST:]
    s_re = st_ref[:, :S5_ST]
    s_im = st_ref[:, S5_ST:]
    d2 = d_ref[0]
    sub = u_ref.shape[0] // n_sub
    for c in range(n_sub):
        r0 = c * sub
        u = u_ref[r0:r0 + sub, :]
        w_ref[r0:r0 + sub, :] = jnp.dot(u, bw_ref[0], preferred_element_type=F32)
        for k in range(0, sub // bsz, 2):
            before = []
            for kk in range(2):
                r = r0 + (k + kk) * bsz
                before.append(jnp.concatenate([s_re, s_im], axis=1))
                w = w_ref[r:r + bsz, :]
                s_re, s_im = (a_re * s_re - a_im * s_im + w[:, :S5_ST],
                              a_re * s_im + a_im * s_re + w[:, S5_ST:])
            s_ref[r0 + k * bsz:r0 + (k + 2) * bsz, :] = (
                jnp.concatenate(before, axis=0).astype(BF16))
        y = (jnp.dot(s_ref[r0:r0 + sub, :], cw_ref[0], preferred_element_type=F32)
             + jnp.dot(u, kw_ref[0], preferred_element_type=F32)
             + d2 * u.astype(F32))
        y_ref[r0:r0 + sub, :] = jax.nn.gelu(y).astype(y_ref.dtype)
    st_ref[:, :S5_ST] = s_re
    st_ref[:, S5_ST:] = s_im


def _s5(u_p, bmat, cmat, kmat, a2v, d2, bsz, seq_len):
    n_gb = SSM_WIDTH // S5_CH
    rows = bsz * S5_TC // 2
    return pl.pallas_call(
        functools.partial(_s5_kernel, bsz=bsz, n_sub=4),
        out_shape=jax.ShapeDtypeStruct((seq_len // 2 * bsz, 2 * SSM_WIDTH), BF16),
        grid=(n_gb, seq_len // S5_TC),
        in_specs=[pl.BlockSpec((rows, 2 * S5_CH), lambda g, c: (c, g)),
                  pl.BlockSpec((1, 2 * S5_CH, 2 * S5_ST), lambda g, c: (g, 0, 0)),
                  pl.BlockSpec((1, 2 * S5_ST, 2 * S5_CH), lambda g, c: (g, 0, 0)),
                  pl.BlockSpec((1, 2 * S5_CH, 2 * S5_CH), lambda g, c: (g, 0, 0)),
                  pl.BlockSpec((1, bsz, 2 * S5_ST), lambda g, c: (g, 0, 0)),
                  pl.BlockSpec((1, 1, 2 * S5_CH), lambda g, c: (g, 0, 0))],
        out_specs=pl.BlockSpec((rows, 2 * S5_CH), lambda g, c: (c, g)),
        scratch_shapes=[pltpu.VMEM((rows, 2 * S5_ST), F32),
                        pltpu.VMEM((rows, 2 * S5_ST), BF16),
                        pltpu.VMEM((bsz, 2 * S5_ST), F32)],
        compiler_params=_cparams(("parallel", "arbitrary")),
        name="s5",
    )(u_p, bmat, cmat, kmat, a2v, d2)


def _pair_layout(x, bsz, seq_len):
    n_gb = SSM_WIDTH // S5_CH
    return jnp.transpose(x.reshape(bsz, seq_len // 2, 2, n_gb, S5_CH),
                         (1, 0, 3, 2, 4)).reshape(seq_len // 2 * bsz, 2 * SSM_WIDTH)


def _token_layout(y, bsz, seq_len):
    n_gb = SSM_WIDTH // S5_CH
    return jnp.transpose(y.reshape(seq_len // 2, bsz, n_gb, 2, S5_CH),
                         (1, 0, 3, 2, 4)).reshape(bsz * seq_len, SSM_WIDTH)


def _glu_kernel(y_ref, w_ref, b_ref, o_ref):
    y = y_ref[...]
    z = jnp.dot(y, w_ref[...], preferred_element_type=F32) + b_ref[...]
    o_ref[...] = (y.astype(F32) * jax.nn.sigmoid(z)).astype(o_ref.dtype)


def _glu(y, w, b, tm=1024):
    t, n = y.shape
    return pl.pallas_call(
        _glu_kernel,
        out_shape=jax.ShapeDtypeStruct((t, n), BF16),
        grid=(t // tm,),
        in_specs=[pl.BlockSpec((tm, n), lambda i: (i, 0)),
                  pl.BlockSpec((n, n), lambda i: (0, 0)),
                  pl.BlockSpec((1, n), lambda i: (0, 0))],
        out_specs=pl.BlockSpec((tm, n), lambda i: (i, 0)),
        compiler_params=_cparams(("parallel",)),
        name="glu",
    )(y, w, b.reshape(1, n))


def _t5_bucket(rel):
    n = jnp.maximum(rel, 0)
    max_exact = REL_BUCKETS // 2
    nf = jnp.maximum(n, 1).astype(F32)
    large = max_exact + (jnp.log(nf / max_exact) / math.log(REL_MAX_DIST / max_exact)
                         * (REL_BUCKETS - max_exact)).astype(jnp.int32)
    large = jnp.minimum(large, REL_BUCKETS - 1)
    return jnp.where(n < max_exact, n, large)


def _bias_kernel(relb_ref, bucket_ref, o_ref):
    h = pl.program_id(0)
    bucket = bucket_ref[0]
    out = jnp.zeros(bucket.shape, F32)
    for b in range(REL_BUCKETS):
        out = jnp.where(bucket == b, relb_ref[b * ATTN_HEADS + h], out)
    o_ref[0, 0] = out * LOG2E


def _bias_tiles(rel_bias):
    i = jnp.arange(MOBA_BLOCK)
    rel = (jnp.arange(3)[:, None, None] * MOBA_BLOCK + i[None, None, :] - i[None, :, None])
    bucket = _t5_bucket(rel)
    return pl.pallas_call(
        _bias_kernel,
        out_shape=jax.ShapeDtypeStruct((ATTN_HEADS, 3, MOBA_BLOCK, MOBA_BLOCK), F32),
        grid=(ATTN_HEADS, 3),
        in_specs=[pl.BlockSpec(memory_space=pltpu.SMEM),
                  pl.BlockSpec((1, MOBA_BLOCK, MOBA_BLOCK), lambda h, d: (d, 0, 0))],
        out_specs=pl.BlockSpec((1, 1, MOBA_BLOCK, MOBA_BLOCK), lambda h, d: (h, d, 0, 0)),
        compiler_params=_cparams(("parallel", "parallel")),
        name="bias_tiles",
    )(rel_bias.astype(F32).reshape(-1), bucket)


_NT = (((1,), (1,)), ((), ()))


def _moba_qtile(own, slot0, q_ref, k_ref, vt_ref, bias_ref, o_ref, km_ref, s_ref):
    bs = MOBA_BLOCK
    scale = HEAD_DIM ** -0.5
    q = q_ref[own * bs:(own + 1) * bs, :]

    ranks = []
    if own > 0:
        gate = lax.dot_general(km_ref[...].astype(BF16), q, _NT, preferred_element_type=F32)
        blk = lax.broadcasted_iota(jnp.int32, gate.shape, 0)
        gm = jnp.where(blk < own, gate, NEG_INF)
        for n in range(own):
            g_n = gm[n:n + 1, :]
            ge = jnp.where(gm >= g_n, 1.0, 0.0)
            gt = jnp.where(gm > g_n, 1.0, 0.0)
            ranks.append(jnp.sum(jnp.where(blk < n, ge, gt), axis=0, keepdims=True))

    c1 = scale * LOG2E
    blocks = [own] + list(range(own))
    m = None
    for idx, j in enumerate(blocks):
        k_j = k_ref[j * bs:(j + 1) * bs, :]
        raw = lax.dot_general(k_j, q, _NT, preferred_element_type=F32)
        if j == own:
            key = lax.broadcasted_iota(jnp.int32, (bs, bs), 0)
            qry = lax.broadcasted_iota(jnp.int32, (bs, bs), 1)
            s = jnp.where(key <= qry, raw * c1 + bias_ref[0, 0], NEG_INF)
        elif own - j == 1:
            s = jnp.where(ranks[j] < float(MOBA_TOPK), raw * c1 + bias_ref[0, 1], NEG_INF)
        else:
            row = jnp.where(ranks[j] < float(MOBA_TOPK), bias_ref[0, 2, 0:1, :], NEG_INF)
            s = raw * c1 + row
        s_ref[slot0 + idx] = s
        m_j = jnp.max(s, axis=0, keepdims=True)
        m = m_j if m is None else jnp.maximum(m, m_j)

    acc = None
    for idx, j in enumerate(blocks):
        p = jnp.exp2((s_ref[slot0 + idx] - m).astype(BF16))
        a_j = jnp.dot(vt_ref[:, j * bs:(j + 1) * bs], p, preferred_element_type=F32)
        acc = a_j if acc is None else acc + a_j
    o = acc[:HEAD_DIM, :] / acc[HEAD_DIM:HEAD_DIM + 1, :]
    o_ref[own * bs:(own + 1) * bs, :] = o.T.astype(o_ref.dtype)


def _moba_kernel(q_ref, k_ref, v_ref, bias_ref, o_ref, km_ref, vt_ref, s_ref, *, n_blk):
    bs = MOBA_BLOCK
    vt_ref[HEAD_DIM:, :] = jnp.ones((vt_ref.shape[0] - HEAD_DIM, vt_ref.shape[1]), BF16)
    for n in range(n_blk):
        kb = k_ref[n * bs:(n + 1) * bs, :].astype(F32)
        km_ref[n:n + 1, :] = jnp.mean(kb, axis=0, keepdims=True)
        vt_ref[:HEAD_DIM, n * bs:(n + 1) * bs] = (
            v_ref[n * bs:(n + 1) * bs, :].astype(F32).T.astype(BF16))

    slot = 0
    for own in range(n_blk):
        _moba_qtile(own, slot, q_ref, k_ref, vt_ref, bias_ref, o_ref, km_ref, s_ref)
        slot += own + 1


def _moba(proj, bias_tiles, bsz, seq_len):
    n_blk = seq_len // MOBA_BLOCK
    q_off = SSM_WIDTH // HEAD_DIM
    k_off = q_off + ATTN_HEADS
    v_off = k_off + ATTN_HEADS
    n_slots = n_blk * (n_blk + 1) // 2
    return pl.pallas_call(
        functools.partial(_moba_kernel, n_blk=n_blk),
        out_shape=jax.ShapeDtypeStruct((bsz * seq_len, ATTN_WIDTH), BF16),
        grid=(bsz, ATTN_HEADS),
        in_specs=[pl.BlockSpec((seq_len, HEAD_DIM), lambda b, h: (b, q_off + h)),
                  pl.BlockSpec((seq_len, HEAD_DIM), lambda b, h: (b, k_off + h)),
                  pl.BlockSpec((seq_len, HEAD_DIM), lambda b, h: (b, v_off + h)),
                  pl.BlockSpec((1, 3, MOBA_BLOCK, MOBA_BLOCK), lambda b, h: (h, 0, 0, 0))],
        out_specs=pl.BlockSpec((seq_len, HEAD_DIM), lambda b, h: (b, h)),
        scratch_shapes=[pltpu.VMEM((n_blk, HEAD_DIM), F32),
                        pltpu.VMEM((HEAD_DIM + 16, seq_len), BF16),
                        pltpu.VMEM((n_slots, MOBA_BLOCK, MOBA_BLOCK), F32)],
        compiler_params=_cparams(("parallel", "parallel")),
        name="moba",
    )(proj, proj, proj, bias_tiles)


def _merge_kernel(ys_ref, ya_ref, ws_ref, wa_ref, ga_ref, gb_ref, o_ref):
    a = jnp.dot(ys_ref[...], ws_ref[...], preferred_element_type=F32)
    b = jnp.dot(ya_ref[...], wa_ref[...], preferred_element_type=F32)
    o_ref[...] = (jax.nn.sigmoid(ga_ref[...].astype(F32)) * a
                  + jax.nn.sigmoid(gb_ref[...].astype(F32)) * b).astype(o_ref.dtype)


def _merge(y_ssm, y_att, w_ps, w_pa, proj, tm=512, tn=2048):
    t, k = y_ssm.shape
    n = w_ps.shape[1]
    ga_off = (SSM_WIDTH + 3 * ATTN_WIDTH) // tn
    gb_off = ga_off + D_MODEL // tn
    return pl.pallas_call(
        _merge_kernel,
        out_shape=jax.ShapeDtypeStruct((t, n), BF16),
        grid=(t // tm, n // tn),
        in_specs=[pl.BlockSpec((tm, k), lambda i, j: (i, 0)),
                  pl.BlockSpec((tm, k), lambda i, j: (i, 0)),
                  pl.BlockSpec((k, tn), lambda i, j: (0, j)),
                  pl.BlockSpec((k, tn), lambda i, j: (0, j)),
                  pl.BlockSpec((tm, tn), lambda i, j: (i, ga_off + j)),
                  pl.BlockSpec((tm, tn), lambda i, j: (i, gb_off + j))],
        out_specs=pl.BlockSpec((tm, tn), lambda i, j: (i, j)),
        compiler_params=_cparams(("parallel", "parallel")),
        name="merge",
    )(y_ssm, y_att, w_ps, w_pa, proj, proj)


def _resid_mm_kernel(a_ref, w_ref, x_ref, g_ref, o_ref):
    acc = jnp.dot(a_ref[...], w_ref[...], preferred_element_type=F32)
    o_ref[...] = x_ref[...] + g_ref[0] * acc


def _resid_mm(a, w, x2d, mod4, g_idx, seq_len, tm=512, tn=2048):
    t, k = a.shape
    n = w.shape[1]
    per_b = seq_len // tm
    return pl.pallas_call(
        _resid_mm_kernel,
        out_shape=jax.ShapeDtypeStruct((t, n), F32),
        grid=(t // tm, n // tn),
        in_specs=[pl.BlockSpec((tm, k), lambda i, j: (i, 0)),
                  pl.BlockSpec((k, tn), lambda i, j: (0, j)),
                  pl.BlockSpec((tm, tn), lambda i, j: (i, j)),
                  pl.BlockSpec((1, 1, tn), lambda i, j: ((i // per_b) * N_MOD + g_idx, 0, j))],
        out_specs=pl.BlockSpec((tm, tn), lambda i, j: (i, j)),
        compiler_params=_cparams(("parallel", "parallel")),
        name="resid_mm",
    )(a, w, x2d, mod4)


def _ff2_kernel(h_ref, w_ref, x_ref, g_ref, gf_ref, o_ref, *, rows):
    k = pl.program_id(1)

    @pl.when(k == 0)
    def _():
        o_ref[...] = jnp.dot(h_ref[...], w_ref[...], preferred_element_type=F32)

    @pl.when(k != 0)
    def _():
        o_ref[...] += jnp.dot(h_ref[...], w_ref[...], preferred_element_type=F32)

    @pl.when(k == pl.num_programs(1) - 1)
    def _():
        for r in range(0, o_ref.shape[0], rows):
            xo = x_ref[r:r + rows, :] + g_ref[0] * o_ref[r:r + rows, :]
            ms = jnp.mean(xo * xo, axis=-1, keepdims=True)
            o_ref[r:r + rows, :] = xo * lax.rsqrt(ms + EPS) * gf_ref[...]


def _ff2(hid, w, x2d, mod4, g_idx, gf, seq_len, tm=1024, tk=1024):
    t, kdim = hid.shape
    n = w.shape[1]
    per_b = seq_len // tm
    return pl.pallas_call(
        functools.partial(_ff2_kernel, rows=256),
        out_shape=jax.ShapeDtypeStruct((t, n), F32),
        grid=(t // tm, kdim // tk),
        in_specs=[pl.BlockSpec((tm, tk), lambda i, k: (i, k)),
                  pl.BlockSpec((tk, n), lambda i, k: (k, 0)),
                  pl.BlockSpec((tm, n), lambda i, k: (i, 0)),
                  pl.BlockSpec((1, 1, n), lambda i, k: ((i // per_b) * N_MOD + g_idx, 0, 0)),
                  pl.BlockSpec((1, n), lambda i, k: (0, 0))],
        out_specs=pl.BlockSpec((tm, n), lambda i, k: (i, 0)),
        compiler_params=_cparams(("parallel", "arbitrary")),
        name="ff2_final",
    )(hid, w, x2d, mod4, gf.reshape(1, n))


def kernel(x, c, rel_bias, w_ada, b_ada, norm_mix_g, w_in, ssm_a_re, ssm_a_im, ssm_log_dt,
           ssm_b_re, ssm_b_im, ssm_c_re, ssm_c_im, ssm_d, w_glu, b_glu, w_proj_ssm,
           w_proj_attn, w_out, norm_mlp_g, w_ff1, w_ff2, norm_final_g):
    bsz, seq_len, d = x.shape
    depth = w_in.shape[0]
    assert depth == 1, "the final rms_norm is fused into the single layer's ff2 kernel"
    t = bsz * seq_len
    x2d = x.reshape(t, d)
    bias_tiles = _bias_tiles(rel_bias)

    for l in range(depth):
        mod4 = _mod(c, w_ada[l], b_ada[l]).reshape(bsz * N_MOD, 1, d)

        proj = _norm_mm(x2d, norm_mix_g[l], mod4, 1, 0, w_in[l].astype(BF16), seq_len, act="none")

        a2, bw, cw, kw = _s5prep(ssm_a_re[l], ssm_a_im[l], ssm_log_dt[l],
                                 ssm_b_re[l], ssm_b_im[l], ssm_c_re[l], ssm_c_im[l])
        bmat, cmat, kmat, a2v, d2 = _s5_weights(a2, bw, cw, kw, ssm_d[l], bsz)
        u_p = _pair_layout(proj[:, :SSM_WIDTH], bsz, seq_len)
        y_p = _s5(u_p, bmat, cmat, kmat, a2v, d2, bsz, seq_len)
        y_ssm = _glu(_token_layout(y_p, bsz, seq_len), w_glu[l].astype(BF16), b_glu[l])

        y_att = _moba(proj, bias_tiles, bsz, seq_len)

        merged = _merge(y_ssm, y_att, w_proj_ssm[l].astype(BF16), w_proj_attn[l].astype(BF16), proj)
        x2d = _resid_mm(merged, w_out[l].astype(BF16), x2d, mod4, 2, seq_len)

        hid = _norm_mm(x2d, norm_mlp_g[l], mod4, 4, 3, w_ff1[l].astype(BF16), seq_len, act="relu2")
        x2d = _ff2(hid, w_ff2[l].astype(BF16), x2d, mod4, 5, norm_final_g, seq_len)
    return x2d.reshape(bsz, seq_len, d)
```

```python
import functools
import itertools
import math

import jax
import jax.numpy as jnp
from jax import lax
from jax.experimental import pallas as pl
from jax.experimental.pallas import tpu as pltpu

F32 = jnp.float32
BF16 = jnp.bfloat16

D_MODEL = 2048
SSM_WIDTH = 1024
SSM_GROUP = 16
SSM_GROUPS = 64
SSM_STATE = 64
ATTN_HEADS = 8
HEAD_DIM = 128
ATTN_WIDTH = 1024
MOBA_BLOCK = 256
MOBA_TOPK = 3
REL_BUCKETS = 32
REL_MAX_DIST = 128
D_FF = 4 * D_MODEL
N_MOD = 6
EPS = 1e-6
NEG_INF = -1e30
LOG2E = math.log2(math.e)
IN_WIDTH = SSM_WIDTH + 3 * ATTN_WIDTH + 2 * D_MODEL

VMEM_LIMIT_BYTES = 56 * 1024 * 1024

S5_GROUPS_PER_BLOCK = 8
S5_CH = S5_GROUPS_PER_BLOCK * SSM_GROUP
S5_ST = S5_GROUPS_PER_BLOCK * SSM_STATE
S5_TC = 256


def _cparams(sem):
    return pltpu.CompilerParams(dimension_semantics=sem,
                                vmem_limit_bytes=VMEM_LIMIT_BYTES)


def _mod_kernel(c_ref, w_ref, b_ref, o_ref):
    c = c_ref[...]
    ca = (c * jax.nn.sigmoid(c)).astype(BF16)
    o_ref[...] = jnp.dot(ca, w_ref[...].astype(BF16),
                         preferred_element_type=F32) + b_ref[...]


def _mod(c, w_ada, b_ada):
    bsz, d = c.shape
    n = w_ada.shape[1]
    tn = 1024
    return pl.pallas_call(
        _mod_kernel,
        out_shape=jax.ShapeDtypeStruct((bsz, n), F32),
        grid=(n // tn,),
        in_specs=[pl.BlockSpec((bsz, d), lambda j: (0, 0)),
                  pl.BlockSpec((d, tn), lambda j: (0, j)),
                  pl.BlockSpec((1, tn), lambda j: (0, j))],
        out_specs=pl.BlockSpec((bsz, tn), lambda j: (0, j)),
        compiler_params=_cparams(("parallel",)),
        name="mod",
    )(c, w_ada, b_ada.reshape(1, n))


def _norm_mm_kernel(x_ref, g_ref, sc_ref, sh_ref, w_ref, o_ref, h_ref, *, act, rows):
    def mm(h):
        acc = jnp.dot(h, w_ref[...], preferred_element_type=F32)
        if act == "relu2":
            acc = jnp.square(jnp.maximum(acc, 0.0))
        return acc.astype(o_ref.dtype)

    @pl.when(pl.program_id(1) == 0)
    def _():
        g = g_ref[...]
        sc = 1.0 + sc_ref[0]
        sh = sh_ref[0]
        tm = x_ref.shape[0]
        for r in range(0, tm, rows):
            x = x_ref[r:r + rows, :]
            ms = jnp.mean(x * x, axis=-1, keepdims=True)
            y = x * lax.rsqrt(ms + EPS) * g
            h = (y * sc + sh).astype(BF16)
            h_ref[r:r + rows, :] = h
            o_ref[r:r + rows, :] = mm(h)

    @pl.when(pl.program_id(1) != 0)
    def _():
        o_ref[...] = mm(h_ref[...])


def _norm_mm(x2d, g, mod4, sc_idx, sh_idx, w, seq_len, *, act, tm=1024, tn=1024):
    t, d = x2d.shape
    n = w.shape[1]
    per_b = seq_len // tm
    return pl.pallas_call(
        functools.partial(_norm_mm_kernel, act=act, rows=256),
        out_shape=jax.ShapeDtypeStruct((t, n), BF16),
        grid=(t // tm, n // tn),
        in_specs=[pl.BlockSpec((tm, d), lambda i, j: (i, 0)),
                  pl.BlockSpec((1, d), lambda i, j: (0, 0)),
                  pl.BlockSpec((1, 1, d), lambda i, j: ((i // per_b) * N_MOD + sc_idx, 0, 0)),
                  pl.BlockSpec((1, 1, d), lambda i, j: ((i // per_b) * N_MOD + sh_idx, 0, 0)),
                  pl.BlockSpec((d, tn), lambda i, j: (0, j))],
        out_specs=pl.BlockSpec((tm, tn), lambda i, j: (i, j)),
        scratch_shapes=[pltpu.VMEM((tm, d), BF16)],
        compiler_params=_cparams(("parallel", "arbitrary")),
        name="norm_mm_" + act,
    )(x2d, g.reshape(1, d), mod4, mod4, w)


def _s5prep_kernel(are_ref, aim_ref, ldt_ref, bre_ref, bim_ref, cre_ref, cim_ref,
                   a2_ref, bw_ref, cw_ref, kw_ref):
    a_re = are_ref[...]
    a_im = aim_ref[...]
    dt = jnp.exp(ldt_ref[...])
    mag = jnp.exp(dt * a_re)
    abar_re = mag * jnp.cos(dt * a_im)
    abar_im = mag * jnp.sin(dt * a_im)
    den = a_re * a_re + a_im * a_im
    p_re = abar_re - 1.0
    f_re = (p_re * a_re + abar_im * a_im) / den
    f_im = (abar_im * a_re - p_re * a_im) / den
    b_re = bre_ref[...]
    b_im = bim_ref[...]

    def cmul(x_re, x_im, y_re, y_im):
        return x_re * y_re - x_im * y_im, x_re * y_im + x_im * y_re

    def cat(re, im):
        return jnp.concatenate([re, im], axis=-1)

    bb_re, bb_im = cmul(f_re, f_im, b_re, b_im)
    a2_re, a2_im = cmul(abar_re, abar_im, abar_re, abar_im)
    abb_re, abb_im = cmul(abar_re, abar_im, bb_re, bb_im)
    c_re = cre_ref[...]
    c_im = cim_ref[...]
    ca_re, ca_im = cmul(c_re, c_im, abar_re, abar_im)
    ca2_re, ca2_im = cmul(c_re, c_im, a2_re, a2_im)

    a2_ref[...] = cat(a2_re, a2_im)
    y0 = cat(bb_re, bb_im)
    bw_ref[0] = cat(abb_re, abb_im)
    bw_ref[1] = y0
    x0 = cat(c_re, -c_im)
    x1 = cat(ca_re, -ca_im)
    cw_ref[0] = x1
    cw_ref[1] = cat(ca2_re, -ca2_im)
    kw_ref[0] = jnp.einsum("gpn,gqn->gpq", x0, y0, precision=lax.Precision.HIGHEST,
                           preferred_element_type=F32)
    kw_ref[1] = jnp.einsum("gpn,gqn->gpq", x1, y0, precision=lax.Precision.HIGHEST,
                           preferred_element_type=F32)


def _s5prep(a_re, a_im, log_dt, b_re, b_im, c_re, c_im):
    g, n = a_re.shape
    p = b_re.shape[2]
    b_re_t = jnp.transpose(b_re, (0, 2, 1))
    b_im_t = jnp.transpose(b_im, (0, 2, 1))
    return pl.pallas_call(
        _s5prep_kernel,
        out_shape=(jax.ShapeDtypeStruct((g, 1, 2 * n), F32),
                   jax.ShapeDtypeStruct((2, g, p, 2 * n), F32),
                   jax.ShapeDtypeStruct((2, g, p, 2 * n), F32),
                   jax.ShapeDtypeStruct((2, g, p, p), F32)),
        name="s5prep",
    )(a_re.reshape(g, 1, n), a_im.reshape(g, 1, n), log_dt.reshape(g, 1, 1),
      b_re_t, b_im_t, c_re, c_im)


def _block_diag(x):
    nb, ng, r, c = x.shape
    eye = jnp.eye(ng, dtype=x.dtype)
    return jnp.einsum("agrc,gh->agrhc", x, eye).reshape(nb, ng * r, ng * c)


def _s5_weights(a2, bw, cw, kw, d_skip, bsz):
    n_gb = SSM_GROUPS // S5_GROUPS_PER_BLOCK
    n = SSM_STATE
    blk = (n_gb, S5_GROUPS_PER_BLOCK)

    def bd(x):
        return _block_diag(x.reshape(blk + x.shape[1:]))

    def bd_t(x):
        return bd(jnp.transpose(x, (0, 2, 1)))

    bmat = jnp.concatenate(
        [jnp.concatenate([bd(bw[i, :, :, :n]), bd(bw[i, :, :, n:])], axis=-1) for i in range(2)],
        axis=1).astype(BF16)
    cmat = jnp.concatenate(
        [jnp.concatenate([bd_t(cw[i, :, :, :n]), bd_t(cw[i, :, :, n:])], axis=1) for i in range(2)],
        axis=-1).astype(BF16)
    k0 = bd_t(kw[0])
    k1 = bd_t(kw[1])
    kmat = jnp.concatenate([jnp.concatenate([k0, k1], axis=-1),
                            jnp.concatenate([jnp.zeros_like(k0), k0], axis=-1)],
                           axis=1).astype(BF16)
    a2v = jnp.concatenate([a2[:, 0, :n].reshape(n_gb, 1, S5_ST),
                           a2[:, 0, n:].reshape(n_gb, 1, S5_ST)], axis=-1)
    a2v = jnp.broadcast_to(a2v, (n_gb, bsz, 2 * S5_ST))
    d = d_skip.reshape(n_gb, 1, S5_CH)
    return bmat, cmat, kmat, a2v, jnp.concatenate([d, d], axis=-1)


def _s5_kernel(u_ref, bw_ref, cw_ref, kw_ref, a2_ref, d_ref, y_ref, w_ref, s_ref, st_ref,
               *, bsz, n_sub):
    @pl.when(pl.program_id(1) == 0)
    def _():
        st_ref[...] = jnp.zeros_like(st_ref)

    a_re = a2_ref[0, :, :S5_ST]
    a_im = a2_ref[0, :, S5_ST:]
    s_re = st_ref[:, :S5_ST]
    s_im = st_ref[:, S5_ST:]
    d2 = d_ref[0]
    sub = u_ref.shape[0] // n_sub

    def project_in(c):
        w_ref[c * sub:(c + 1) * sub, :] = jnp.dot(u_ref[c * sub:(c + 1) * sub, :], bw_ref[0],
                                                  preferred_element_type=F32)

    project_in(0)
    for c in range(n_sub):
        r0 = c * sub
        u = u_ref[r0:r0 + sub, :]
        if c + 1 < n_sub:
            project_in(c + 1)
        for k in range(0, sub // bsz, 2):
            before = []
            for kk in range(2):
                r = r0 + (k + kk) * bsz
                before.append(jnp.concatenate([s_re, s_im], axis=1))
                w = w_ref[r:r + bsz, :]
                s_re, s_im = (a_re * s_re - a_im * s_im + w[:, :S5_ST],
                              a_re * s_im + a_im * s_re + w[:, S5_ST:])
            s_ref[r0 + k * bsz:r0 + (k + 2) * bsz, :] = (
                jnp.concatenate(before, axis=0).astype(BF16))
        y = (jnp.dot(s_ref[r0:r0 + sub, :], cw_ref[0], preferred_element_type=F32)
             + jnp.dot(u, kw_ref[0], preferred_element_type=F32)
             + d2 * u.astype(F32))
        y_ref[r0:r0 + sub, :] = jax.nn.gelu(y).astype(y_ref.dtype)
    st_ref[:, :S5_ST] = s_re
    st_ref[:, S5_ST:] = s_im


def _s5(u_p, bmat, cmat, kmat, a2v, d2, bsz, seq_len):
    n_gb = SSM_WIDTH // S5_CH
    rows = bsz * S5_TC // 2
    return pl.pallas_call(
        functools.partial(_s5_kernel, bsz=bsz, n_sub=4),
        out_shape=jax.ShapeDtypeStruct((seq_len // 2 * bsz, 2 * SSM_WIDTH), BF16),
        grid=(n_gb, seq_len // S5_TC),
        in_specs=[pl.BlockSpec((rows, 2 * S5_CH), lambda g, c: (c, g)),
                  pl.BlockSpec((1, 2 * S5_CH, 2 * S5_ST), lambda g, c: (g, 0, 0)),
                  pl.BlockSpec((1, 2 * S5_ST, 2 * S5_CH), lambda g, c: (g, 0, 0)),
                  pl.BlockSpec((1, 2 * S5_CH, 2 * S5_CH), lambda g, c: (g, 0, 0)),
                  pl.BlockSpec((1, bsz, 2 * S5_ST), lambda g, c: (g, 0, 0)),
                  pl.BlockSpec((1, 1, 2 * S5_CH), lambda g, c: (g, 0, 0))],
        out_specs=pl.BlockSpec((rows, 2 * S5_CH), lambda g, c: (c, g)),
        scratch_shapes=[pltpu.VMEM((rows, 2 * S5_ST), F32),
                        pltpu.VMEM((rows, 2 * S5_ST), BF16),
                        pltpu.VMEM((bsz, 2 * S5_ST), F32)],
        compiler_params=_cparams(("parallel", "arbitrary")),
        name="s5",
    )(u_p, bmat, cmat, kmat, a2v, d2)


def _pair_layout(x, bsz, seq_len):
    n_gb = SSM_WIDTH // S5_CH
    return jnp.transpose(x.reshape(bsz, seq_len // 2, 2, n_gb, S5_CH),
                         (1, 0, 3, 2, 4)).reshape(seq_len // 2 * bsz, 2 * SSM_WIDTH)


def _token_layout(y, bsz, seq_len):
    n_gb = SSM_WIDTH // S5_CH
    return jnp.transpose(y.reshape(seq_len // 2, bsz, n_gb, 2, S5_CH),
                         (1, 0, 3, 2, 4)).reshape(bsz * seq_len, SSM_WIDTH)


def _glu_kernel(y_ref, w_ref, b_ref, o_ref):
    y = y_ref[...]
    z = jnp.dot(y, w_ref[...], preferred_element_type=F32) + b_ref[...]
    o_ref[...] = (y.astype(F32) * jax.nn.sigmoid(z)).astype(o_ref.dtype)


def _glu(y, w, b, tm=1024):
    t, n = y.shape
    return pl.pallas_call(
        _glu_kernel,
        out_shape=jax.ShapeDtypeStruct((t, n), BF16),
        grid=(t // tm,),
        in_specs=[pl.BlockSpec((tm, n), lambda i: (i, 0)),
                  pl.BlockSpec((n, n), lambda i: (0, 0)),
                  pl.BlockSpec((1, n), lambda i: (0, 0))],
        out_specs=pl.BlockSpec((tm, n), lambda i: (i, 0)),
        compiler_params=_cparams(("parallel",)),
        name="glu",
    )(y, w, b.reshape(1, n))


def _t5_bucket(rel):
    n = jnp.maximum(rel, 0)
    max_exact = REL_BUCKETS // 2
    nf = jnp.maximum(n, 1).astype(F32)
    large = max_exact + (jnp.log(nf / max_exact) / math.log(REL_MAX_DIST / max_exact)
                         * (REL_BUCKETS - max_exact)).astype(jnp.int32)
    large = jnp.minimum(large, REL_BUCKETS - 1)
    return jnp.where(n < max_exact, n, large)


def _bias_kernel(relb_ref, bucket_ref, o_ref):
    h = pl.program_id(0)
    bucket = bucket_ref[0]
    out = jnp.zeros(bucket.shape, F32)
    for b in range(REL_BUCKETS):
        out = jnp.where(bucket == b, relb_ref[b * ATTN_HEADS + h], out)
    o_ref[0, 0] = out * LOG2E


def _bias_tiles(rel_bias):
    i = jnp.arange(MOBA_BLOCK)
    rel = (jnp.arange(3)[:, None, None] * MOBA_BLOCK + i[None, None, :] - i[None, :, None])
    bucket = _t5_bucket(rel)
    return pl.pallas_call(
        _bias_kernel,
        out_shape=jax.ShapeDtypeStruct((ATTN_HEADS, 3, MOBA_BLOCK, MOBA_BLOCK), F32),
        grid=(ATTN_HEADS, 3),
        in_specs=[pl.BlockSpec(memory_space=pltpu.SMEM),
                  pl.BlockSpec((1, MOBA_BLOCK, MOBA_BLOCK), lambda h, d: (d, 0, 0))],
        out_specs=pl.BlockSpec((1, 1, MOBA_BLOCK, MOBA_BLOCK), lambda h, d: (h, d, 0, 0)),
        compiler_params=_cparams(("parallel", "parallel")),
        name="bias_tiles",
    )(rel_bias.astype(F32).reshape(-1), bucket)


_NT = (((1,), (1,)), ((), ()))


def _moba_scores(own, slot0, state, q_ref, k_ref, bias_ref, km_ref, s_ref):
    bs = MOBA_BLOCK
    q = q_ref[own * bs:(own + 1) * bs, :]

    ranks = []
    if own > 0:
        gate = lax.dot_general(km_ref[...].astype(BF16), q, _NT, preferred_element_type=F32)
        blk = lax.broadcasted_iota(jnp.int32, gate.shape, 0)
        gm = jnp.where(blk < own, gate, NEG_INF)
        for n in range(own):
            g_n = gm[n:n + 1, :]
            ge = jnp.where(gm >= g_n, 1.0, 0.0)
            gt = jnp.where(gm > g_n, 1.0, 0.0)
            ranks.append(jnp.sum(jnp.where(blk < n, ge, gt), axis=0, keepdims=True))

    c1 = HEAD_DIM ** -0.5 * LOG2E
    m = None
    for idx, j in enumerate([own] + list(range(own))):
        k_j = k_ref[j * bs:(j + 1) * bs, :]
        raw = lax.dot_general(k_j, q, _NT, preferred_element_type=F32)
        if j == own:
            key = lax.broadcasted_iota(jnp.int32, (bs, bs), 0)
            qry = lax.broadcasted_iota(jnp.int32, (bs, bs), 1)
            s = jnp.where(key <= qry, raw * c1 + bias_ref[0, 0], NEG_INF)
        elif own - j == 1:
            s = jnp.where(ranks[j] < float(MOBA_TOPK), raw * c1 + bias_ref[0, 1], NEG_INF)
        else:
            row = jnp.where(ranks[j] < float(MOBA_TOPK), bias_ref[0, 2, 0:1, :], NEG_INF)
            s = raw * c1 + row
        s_ref[slot0 + idx] = s
        m_j = jnp.max(s, axis=0, keepdims=True)
        m = m_j if m is None else jnp.maximum(m, m_j)
        state["m"] = m
        yield


def _moba_values(own, slot0, state, vt_ref, o_ref, s_ref):
    bs = MOBA_BLOCK
    m = state["m"]
    acc = None
    for idx, j in enumerate([own] + list(range(own))):
        p = jnp.exp2((s_ref[slot0 + idx] - m).astype(BF16))
        a_j = jnp.dot(vt_ref[:, j * bs:(j + 1) * bs], p, preferred_element_type=F32)
        acc = a_j if acc is None else acc + a_j
        if idx == own:
            o = acc[:HEAD_DIM, :] / acc[HEAD_DIM:HEAD_DIM + 1, :]
            o_ref[own * bs:(own + 1) * bs, :] = o.T.astype(o_ref.dtype)
        yield


def _moba_kernel(q_ref, k_ref, v_ref, bias_ref, o_ref, km_ref, vt_ref, s_ref, *, n_blk):
    bs = MOBA_BLOCK
    vt_ref[HEAD_DIM:, :] = jnp.ones((vt_ref.shape[0] - HEAD_DIM, vt_ref.shape[1]), BF16)
    for n in range(n_blk):
        kb = k_ref[n * bs:(n + 1) * bs, :].astype(F32)
        km_ref[n:n + 1, :] = jnp.mean(kb, axis=0, keepdims=True)
        vt_ref[:HEAD_DIM, n * bs:(n + 1) * bs] = (
            v_ref[n * bs:(n + 1) * bs, :].astype(F32).T.astype(BF16))

    slots = [own * (own + 1) // 2 for own in range(n_blk)]
    states = [dict() for _ in range(n_blk)]
    pending = iter(())
    for own in range(n_blk):
        scores = _moba_scores(own, slots[own], states[own], q_ref, k_ref, bias_ref, km_ref, s_ref)
        for _ in itertools.zip_longest(scores, pending):
            pass
        pending = _moba_values(own, slots[own], states[own], vt_ref, o_ref, s_ref)
    for _ in pending:
        pass


def _moba(proj, bias_tiles, bsz, seq_len):
    n_blk = seq_len // MOBA_BLOCK
    q_off = SSM_WIDTH // HEAD_DIM
    k_off = q_off + ATTN_HEADS
    v_off = k_off + ATTN_HEADS
    n_slots = n_blk * (n_blk + 1) // 2
    return pl.pallas_call(
        functools.partial(_moba_kernel, n_blk=n_blk),
        out_shape=jax.ShapeDtypeStruct((bsz * seq_len, ATTN_WIDTH), BF16),
        grid=(bsz, ATTN_HEADS),
        in_specs=[pl.BlockSpec((seq_len, HEAD_DIM), lambda b, h: (b, q_off + h)),
                  pl.BlockSpec((seq_len, HEAD_DIM), lambda b, h: (b, k_off + h)),
                  pl.BlockSpec((seq_len, HEAD_DIM), lambda b, h: (b, v_off + h)),
                  pl.BlockSpec((1, 3, MOBA_BLOCK, MOBA_BLOCK), lambda b, h: (h, 0, 0, 0))],
        out_specs=pl.BlockSpec((seq_len, HEAD_DIM), lambda b, h: (b, h)),
        scratch_shapes=[pltpu.VMEM((n_blk, HEAD_DIM), F32),
                        pltpu.VMEM((HEAD_DIM + 16, seq_len), BF16),
                        pltpu.VMEM((n_slots, MOBA_BLOCK, MOBA_BLOCK), F32)],
        compiler_params=_cparams(("parallel", "parallel")),
        name="moba",
    )(proj, proj, proj, bias_tiles)


def _merge_kernel(ys_ref, ya_ref, ws_ref, wa_ref, ga_ref, gb_ref, o_ref):
    a = jnp.dot(ys_ref[...], ws_ref[...], preferred_element_type=F32)
    b = jnp.dot(ya_ref[...], wa_ref[...], preferred_element_type=F32)
    o_ref[...] = (jax.nn.sigmoid(ga_ref[...].astype(F32)) * a
                  + jax.nn.sigmoid(gb_ref[...].astype(F32)) * b).astype(o_ref.dtype)


def _merge(y_ssm, y_att, w_ps, w_pa, proj, tm=512, tn=2048):
    t, k = y_ssm.shape
    n = w_ps.shape[1]
    ga_off = (SSM_WIDTH + 3 * ATTN_WIDTH) // tn
    gb_off = ga_off + D_MODEL // tn
    return pl.pallas_call(
        _merge_kernel,
        out_shape=jax.ShapeDtypeStruct((t, n), BF16),
        grid=(t // tm, n // tn),
        in_specs=[pl.BlockSpec((tm, k), lambda i, j: (i, 0)),
                  pl.BlockSpec((tm, k), lambda i, j: (i, 0)),
                  pl.BlockSpec((k, tn), lambda i, j: (0, j)),
                  pl.BlockSpec((k, tn), lambda i, j: (0, j)),
                  pl.BlockSpec((tm, tn), lambda i, j: (i, ga_off + j)),
                  pl.BlockSpec((tm, tn), lambda i, j: (i, gb_off + j))],
        out_specs=pl.BlockSpec((tm, tn), lambda i, j: (i, j)),
        compiler_params=_cparams(("parallel", "parallel")),
        name="merge",
    )(y_ssm, y_att, w_ps, w_pa, proj, proj)


def _resid_mm_kernel(a_ref, w_ref, x_ref, g_ref, o_ref):
    acc = jnp.dot(a_ref[...], w_ref[...], preferred_element_type=F32)
    o_ref[...] = x_ref[...] + g_ref[0] * acc


def _resid_mm(a, w, x2d, mod4, g_idx, seq_len, tm=512, tn=2048):
    t, k = a.shape
    n = w.shape[1]
    per_b = seq_len // tm
    return pl.pallas_call(
        _resid_mm_kernel,
        out_shape=jax.ShapeDtypeStruct((t, n), F32),
        grid=(t // tm, n // tn),
        in_specs=[pl.BlockSpec((tm, k), lambda i, j: (i, 0)),
                  pl.BlockSpec((k, tn), lambda i, j: (0, j)),
                  pl.BlockSpec((tm, tn), lambda i, j: (i, j)),
                  pl.BlockSpec((1, 1, tn), lambda i, j: ((i // per_b) * N_MOD + g_idx, 0, j))],
        out_specs=pl.BlockSpec((tm, tn), lambda i, j: (i, j)),
        compiler_params=_cparams(("parallel", "parallel")),
        name="resid_mm",
    )(a, w, x2d, mod4)


def _ff2_kernel(h_ref, w_ref, x_ref, g_ref, gf_ref, o_ref, *, rows):
    k = pl.program_id(1)

    @pl.when(k == 0)
    def _():
        o_ref[...] = jnp.dot(h_ref[...], w_ref[...], preferred_element_type=F32)

    @pl.when(k != 0)
    def _():
        o_ref[...] += jnp.dot(h_ref[...], w_ref[...], preferred_element_type=F32)

    @pl.when(k == pl.num_programs(1) - 1)
    def _():
        for r in range(0, o_ref.shape[0], rows):
            xo = x_ref[r:r + rows, :] + g_ref[0] * o_ref[r:r + rows, :]
            ms = jnp.mean(xo * xo, axis=-1, keepdims=True)
            o_ref[r:r + rows, :] = xo * lax.rsqrt(ms + EPS) * gf_ref[...]


def _ff2(hid, w, x2d, mod4, g_idx, gf, seq_len, tm=1024, tk=1024):
    t, kdim = hid.shape
    n = w.shape[1]
    per_b = seq_len // tm
    return pl.pallas_call(
        functools.partial(_ff2_kernel, rows=256),
        out_shape=jax.ShapeDtypeStruct((t, n), F32),
        grid=(t // tm, kdim // tk),
        in_specs=[pl.BlockSpec((tm, tk), lambda i, k: (i, k)),
                  pl.BlockSpec((tk, n), lambda i, k: (k, 0)),
                  pl.BlockSpec((tm, n), lambda i, k: (i, 0)),
                  pl.BlockSpec((1, 1, n), lambda i, k: ((i // per_b) * N_MOD + g_idx, 0, 0)),
                  pl.BlockSpec((1, n), lambda i, k: (0, 0))],
        out_specs=pl.BlockSpec((tm, n), lambda i, k: (i, 0)),
        compiler_params=_cparams(("parallel", "arbitrary")),
        name="ff2_final",
    )(hid, w, x2d, mod4, gf.reshape(1, n))


def kernel(x, c, rel_bias, w_ada, b_ada, norm_mix_g, w_in, ssm_a_re, ssm_a_im, ssm_log_dt,
           ssm_b_re, ssm_b_im, ssm_c_re, ssm_c_im, ssm_d, w_glu, b_glu, w_proj_ssm,
           w_proj_attn, w_out, norm_mlp_g, w_ff1, w_ff2, norm_final_g):
    bsz, seq_len, d = x.shape
    depth = w_in.shape[0]
    assert depth == 1, "the final rms_norm is fused into the single layer's ff2 kernel"
    t = bsz * seq_len
    x2d = x.reshape(t, d)
    bias_tiles = _bias_tiles(rel_bias)

    for l in range(depth):
        mod4 = _mod(c, w_ada[l], b_ada[l]).reshape(bsz * N_MOD, 1, d)

        proj = _norm_mm(x2d, norm_mix_g[l], mod4, 1, 0, w_in[l].astype(BF16), seq_len, act="none")

        a2, bw, cw, kw = _s5prep(ssm_a_re[l], ssm_a_im[l], ssm_log_dt[l],
                                 ssm_b_re[l], ssm_b_im[l], ssm_c_re[l], ssm_c_im[l])
        bmat, cmat, kmat, a2v, d2 = _s5_weights(a2, bw, cw, kw, ssm_d[l], bsz)
        u_p = _pair_layout(proj[:, :SSM_WIDTH], bsz, seq_len)
        y_p = _s5(u_p, bmat, cmat, kmat, a2v, d2, bsz, seq_len)
        y_ssm = _glu(_token_layout(y_p, bsz, seq_len), w_glu[l].astype(BF16), b_glu[l])

        y_att = _moba(proj, bias_tiles, bsz, seq_len)

        merged = _merge(y_ssm, y_att, w_proj_ssm[l].astype(BF16), w_proj_attn[l].astype(BF16), proj)
        x2d = _resid_mm(merged, w_out[l].astype(BF16), x2d, mod4, 2, seq_len)

        hid = _norm_mm(x2d, norm_mlp_g[l], mod4, 4, 3, w_ff1[l].astype(BF16), seq_len, act="relu2")
        x2d = _ff2(hid, w_ff2[l].astype(BF16), x2d, mod4, 5, norm_final_g, seq_len)
    return x2d.reshape(bsz, seq_len, d)
```

```python
import functools
import itertools
import math

import jax
import jax.numpy as jnp
from jax import lax
from jax.experimental import pallas as pl
from jax.experimental.pallas import tpu as pltpu

F32 = jnp.float32
BF16 = jnp.bfloat16

D_MODEL = 2048
SSM_WIDTH = 1024
SSM_GROUP = 16
SSM_GROUPS = 64
SSM_STATE = 64
ATTN_HEADS = 8
HEAD_DIM = 128
ATTN_WIDTH = 1024
MOBA_BLOCK = 256
MOBA_TOPK = 3
REL_BUCKETS = 32
REL_MAX_DIST = 128
D_FF = 4 * D_MODEL
N_MOD = 6
EPS = 1e-6
NEG_INF = -1e30
LOG2E = math.log2(math.e)
IN_WIDTH = SSM_WIDTH + 3 * ATTN_WIDTH + 2 * D_MODEL

VMEM_LIMIT_BYTES = 56 * 1024 * 1024

S5_TAU = 4
S5_GROUPS = 4
S5_NB = 2
S5_CH = S5_GROUPS * SSM_GROUP
S5_UC = S5_TAU * S5_CH
S5_SC = 2 * S5_GROUPS * SSM_STATE
S5_NBLK = SSM_GROUPS // S5_GROUPS
S5_ROWS = 1024


def _cparams(sem):
    return pltpu.CompilerParams(dimension_semantics=sem,
                                vmem_limit_bytes=VMEM_LIMIT_BYTES)


def _mod_kernel(c_ref, w_ref, b_ref, o_ref):
    c = c_ref[...]
    ca = (c * jax.nn.sigmoid(c)).astype(BF16)
    o_ref[...] = jnp.dot(ca, w_ref[...].astype(BF16),
                         preferred_element_type=F32) + b_ref[...]


def _mod(c, w_ada, b_ada):
    bsz, d = c.shape
    n = w_ada.shape[1]
    tn = 1024
    return pl.pallas_call(
        _mod_kernel,
        out_shape=jax.ShapeDtypeStruct((bsz, n), F32),
        grid=(n // tn,),
        in_specs=[pl.BlockSpec((bsz, d), lambda j: (0, 0)),
                  pl.BlockSpec((d, tn), lambda j: (0, j)),
                  pl.BlockSpec((1, tn), lambda j: (0, j))],
        out_specs=pl.BlockSpec((bsz, tn), lambda j: (0, j)),
        compiler_params=_cparams(("parallel",)),
        name="mod",
    )(c, w_ada, b_ada.reshape(1, n))


def _norm_mm_kernel(x_ref, g_ref, sc_ref, sh_ref, w_ref, o_ref, h_ref, *, act, rows):
    def mm(h):
        acc = jnp.dot(h, w_ref[...], preferred_element_type=F32)
        if act == "relu2":
            acc = jnp.square(jnp.maximum(acc, 0.0))
        return acc.astype(o_ref.dtype)

    @pl.when(pl.program_id(1) == 0)
    def _():
        g = g_ref[...]
        sc = 1.0 + sc_ref[0]
        sh = sh_ref[0]
        tm = x_ref.shape[0]
        for r in range(0, tm, rows):
            x = x_ref[r:r + rows, :]
            ms = jnp.mean(x * x, axis=-1, keepdims=True)
            y = x * lax.rsqrt(ms + EPS) * g
            h = (y * sc + sh).astype(BF16)
            h_ref[r:r + rows, :] = h
            o_ref[r:r + rows, :] = mm(h)

    @pl.when(pl.program_id(1) != 0)
    def _():
        o_ref[...] = mm(h_ref[...])


def _norm_mm(x2d, g, mod4, sc_idx, sh_idx, w, seq_len, *, act, tm=1024, tn=1024):
    t, d = x2d.shape
    n = w.shape[1]
    per_b = seq_len // tm
    return pl.pallas_call(
        functools.partial(_norm_mm_kernel, act=act, rows=256),
        out_shape=jax.ShapeDtypeStruct((t, n), BF16),
        grid=(t // tm, n // tn),
        in_specs=[pl.BlockSpec((tm, d), lambda i, j: (i, 0)),
                  pl.BlockSpec((1, d), lambda i, j: (0, 0)),
                  pl.BlockSpec((1, 1, d), lambda i, j: ((i // per_b) * N_MOD + sc_idx, 0, 0)),
                  pl.BlockSpec((1, 1, d), lambda i, j: ((i // per_b) * N_MOD + sh_idx, 0, 0)),
                  pl.BlockSpec((d, tn), lambda i, j: (0, j))],
        out_specs=pl.BlockSpec((tm, tn), lambda i, j: (i, j)),
        scratch_shapes=[pltpu.VMEM((tm, d), BF16)],
        compiler_params=_cparams(("parallel", "arbitrary")),
        name="norm_mm_" + act,
    )(x2d, g.reshape(1, d), mod4, mod4, w)


def _s5prep_kernel(are_ref, aim_ref, ldt_ref, bre_ref, bim_ref, cre_ref, cim_ref,
                   at_ref, bw_ref, cw_ref, kw_ref):
    tau = bw_ref.shape[0]
    a_re = are_ref[...]
    a_im = aim_ref[...]
    dt = jnp.exp(ldt_ref[...])
    mag = jnp.exp(dt * a_re)
    abar_re = mag * jnp.cos(dt * a_im)
    abar_im = mag * jnp.sin(dt * a_im)
    den = a_re * a_re + a_im * a_im
    p_re = abar_re - 1.0
    f_re = (p_re * a_re + abar_im * a_im) / den
    f_im = (abar_im * a_re - p_re * a_im) / den

    def cmul(x, y):
        return x[0] * y[0] - x[1] * y[1], x[0] * y[1] + x[1] * y[0]

    def cat(x):
        return jnp.concatenate([x[0], x[1]], axis=-1)

    def re_dot(x):
        return jnp.concatenate([x[0], -x[1]], axis=-1)

    abar = (abar_re, abar_im)
    bb = cmul((f_re, f_im), (bre_ref[...], bim_ref[...]))
    cc = (cre_ref[...], cim_ref[...])
    apow = [None, abar]
    for _ in range(tau - 1):
        apow.append(cmul(apow[-1], abar))
    at_ref[...] = cat(apow[tau])
    y0 = cat(bb)
    for i in range(tau):
        j = tau - 1 - i
        bw_ref[i] = y0 if j == 0 else cat(cmul(apow[j], bb))
        cw_ref[i] = re_dot(cmul(cc, apow[i + 1]))
        x = re_dot(cc if i == 0 else cmul(cc, apow[i]))
        kw_ref[i] = jnp.einsum("gpn,gqn->gpq", x, y0, precision=lax.Precision.HIGHEST,
                               preferred_element_type=F32)


def _s5prep(a_re, a_im, log_dt, b_re, b_im, c_re, c_im):
    g, n = a_re.shape
    p = b_re.shape[2]
    b_re_t = jnp.transpose(b_re, (0, 2, 1))
    b_im_t = jnp.transpose(b_im, (0, 2, 1))
    return pl.pallas_call(
        _s5prep_kernel,
        out_shape=(jax.ShapeDtypeStruct((g, 1, 2 * n), F32),
                   jax.ShapeDtypeStruct((S5_TAU, g, p, 2 * n), F32),
                   jax.ShapeDtypeStruct((S5_TAU, g, p, 2 * n), F32),
                   jax.ShapeDtypeStruct((S5_TAU, g, p, p), F32)),
        name="s5prep",
    )(a_re.reshape(g, 1, n), a_im.reshape(g, 1, n), log_dt.reshape(g, 1, 1),
      b_re_t, b_im_t, c_re, c_im)


def _block_diag(x):
    nb, ng, r, c = x.shape
    eye = jnp.eye(ng, dtype=x.dtype)
    return jnp.einsum("agrc,gh->agrhc", x, eye).reshape(nb, ng * r, ng * c)


def _s5_weights(at, bw, cw, kw, d_skip, bsz):
    n = SSM_STATE
    blk = (S5_NBLK, S5_GROUPS)

    def bd(x):
        return _block_diag(x.reshape(blk + x.shape[1:]))

    def bd_t(x):
        return bd(jnp.transpose(x, (0, 2, 1)))

    bmat = jnp.concatenate(
        [jnp.concatenate([bd(bw[i, :, :, :n]), bd(bw[i, :, :, n:])], axis=-1)
         for i in range(S5_TAU)], axis=1).astype(BF16)
    cmat = jnp.concatenate(
        [jnp.concatenate([bd_t(cw[i, :, :, :n]), bd_t(cw[i, :, :, n:])], axis=1)
         for i in range(S5_TAU)], axis=-1).astype(BF16)
    lag = [bd_t(kw[i]) for i in range(S5_TAU)]
    zero = jnp.zeros_like(lag[0])
    kmat = jnp.concatenate(
        [jnp.concatenate([lag[i - ip] if ip <= i else zero for i in range(S5_TAU)], axis=-1)
         for ip in range(S5_TAU)], axis=1).astype(BF16)
    half = S5_SC // 2
    atv = jnp.concatenate([at[:, 0, :n].reshape(S5_NBLK, half),
                           at[:, 0, n:].reshape(S5_NBLK, half)], axis=-1)
    atv = jnp.broadcast_to(atv.reshape(S5_NBLK // S5_NB, 1, S5_NB * S5_SC),
                           (S5_NBLK // S5_NB, bsz, S5_NB * S5_SC))
    d = jnp.broadcast_to(d_skip.reshape(S5_NBLK, 1, S5_CH), (S5_NBLK, S5_TAU, S5_CH))
    return bmat, cmat, kmat, atv, d.reshape(S5_NBLK // S5_NB, 1, S5_NB * S5_UC)


def _s5_kernel(u_ref, bw_ref, cw_ref, kw_ref, a_ref, d_ref, y_ref, w_ref, s_ref, st_ref,
               *, bsz, n_sub):
    @pl.when(pl.program_id(1) == 0)
    def _():
        st_ref[...] = jnp.zeros_like(st_ref)

    half = S5_SC // 2
    re_cols = [slice(j * S5_SC, j * S5_SC + half) for j in range(S5_NB)]
    im_cols = [slice(j * S5_SC + half, (j + 1) * S5_SC) for j in range(S5_NB)]
    a_re = [a_ref[0, :, c] for c in re_cols]
    a_im = [a_ref[0, :, c] for c in im_cols]
    s_re = [st_ref[:, c] for c in re_cols]
    s_im = [st_ref[:, c] for c in im_cols]
    sub = u_ref.shape[0] // n_sub

    def project_in(c):
        for j in range(S5_NB):
            w_ref[c * sub:(c + 1) * sub, j * S5_SC:(j + 1) * S5_SC] = jnp.dot(
                u_ref[c * sub:(c + 1) * sub, j * S5_UC:(j + 1) * S5_UC], bw_ref[j],
                preferred_element_type=F32)

    project_in(0)
    for c in range(n_sub):
        r0 = c * sub
        if c + 1 < n_sub:
            project_in(c + 1)
        for k in range(0, sub // bsz, 2):
            before = []
            for kk in range(2):
                r = r0 + (k + kk) * bsz
                before.append(jnp.concatenate(
                    [x for j in range(S5_NB) for x in (s_re[j], s_im[j])], axis=1))
                for j in range(S5_NB):
                    w_re = w_ref[r:r + bsz, re_cols[j]]
                    w_im = w_ref[r:r + bsz, im_cols[j]]
                    s_re[j], s_im[j] = (a_re[j] * s_re[j] - a_im[j] * s_im[j] + w_re,
                                        a_re[j] * s_im[j] + a_im[j] * s_re[j] + w_im)
            s_ref[r0 + k * bsz:r0 + (k + 2) * bsz, :] = (
                jnp.concatenate(before, axis=0).astype(BF16))
        for j in range(S5_NB):
            cols = slice(j * S5_UC, (j + 1) * S5_UC)
            u = u_ref[r0:r0 + sub, cols]
            y = (jnp.dot(s_ref[r0:r0 + sub, j * S5_SC:(j + 1) * S5_SC], cw_ref[j],
                         preferred_element_type=F32)
                 + jnp.dot(u, kw_ref[j], preferred_element_type=F32)
                 + d_ref[0][:, cols] * u.astype(F32))
            y_ref[r0:r0 + sub, cols] = jax.nn.gelu(y).astype(y_ref.dtype)
    for j in range(S5_NB):
        st_ref[:, re_cols[j]] = s_re[j]
        st_ref[:, im_cols[j]] = s_im[j]


def _s5(u_p, bmat, cmat, kmat, atv, d2, bsz):
    return pl.pallas_call(
        functools.partial(_s5_kernel, bsz=bsz, n_sub=4),
        out_shape=jax.ShapeDtypeStruct(u_p.shape, BF16),
        grid=(S5_NBLK // S5_NB, u_p.shape[0] // S5_ROWS),
        in_specs=[pl.BlockSpec((S5_ROWS, S5_NB * S5_UC), lambda g, c: (c, g)),
                  pl.BlockSpec((S5_NB, S5_UC, S5_SC), lambda g, c: (g, 0, 0)),
                  pl.BlockSpec((S5_NB, S5_SC, S5_UC), lambda g, c: (g, 0, 0)),
                  pl.BlockSpec((S5_NB, S5_UC, S5_UC), lambda g, c: (g, 0, 0)),
                  pl.BlockSpec((1, bsz, S5_NB * S5_SC), lambda g, c: (g, 0, 0)),
                  pl.BlockSpec((1, 1, S5_NB * S5_UC), lambda g, c: (g, 0, 0))],
        out_specs=pl.BlockSpec((S5_ROWS, S5_NB * S5_UC), lambda g, c: (c, g)),
        scratch_shapes=[pltpu.VMEM((S5_ROWS, S5_NB * S5_SC), F32),
                        pltpu.VMEM((S5_ROWS, S5_NB * S5_SC), BF16),
                        pltpu.VMEM((bsz, S5_NB * S5_SC), F32)],
        compiler_params=_cparams(("parallel", "arbitrary")),
        name="s5",
    )(u_p, bmat, cmat, kmat, atv, d2)


def _group_layout(x, bsz, seq_len):
    return jnp.transpose(x.reshape(bsz, seq_len // S5_TAU, S5_TAU, S5_NBLK, S5_CH),
                         (1, 0, 3, 2, 4)).reshape(seq_len // S5_TAU * bsz, S5_TAU * SSM_WIDTH)


def _token_layout(y, bsz, seq_len):
    return jnp.transpose(y.reshape(seq_len // S5_TAU, bsz, S5_NBLK, S5_TAU, S5_CH),
                         (1, 0, 3, 2, 4)).reshape(bsz * seq_len, SSM_WIDTH)


def _glu_kernel(y_ref, w_ref, b_ref, o_ref):
    y = y_ref[...]
    z = jnp.dot(y, w_ref[...], preferred_element_type=F32) + b_ref[...]
    o_ref[...] = (y.astype(F32) * jax.nn.sigmoid(z)).astype(o_ref.dtype)


def _glu(y, w, b, tm=1024):
    t, n = y.shape
    return pl.pallas_call(
        _glu_kernel,
        out_shape=jax.ShapeDtypeStruct((t, n), BF16),
        grid=(t // tm,),
        in_specs=[pl.BlockSpec((tm, n), lambda i: (i, 0)),
                  pl.BlockSpec((n, n), lambda i: (0, 0)),
                  pl.BlockSpec((1, n), lambda i: (0, 0))],
        out_specs=pl.BlockSpec((tm, n), lambda i: (i, 0)),
        compiler_params=_cparams(("parallel",)),
        name="glu",
    )(y, w, b.reshape(1, n))


def _t5_bucket(rel):
    n = jnp.maximum(rel, 0)
    max_exact = REL_BUCKETS // 2
    nf = jnp.maximum(n, 1).astype(F32)
    large = max_exact + (jnp.log(nf / max_exact) / math.log(REL_MAX_DIST / max_exact)
                         * (REL_BUCKETS - max_exact)).astype(jnp.int32)
    large = jnp.minimum(large, REL_BUCKETS - 1)
    return jnp.where(n < max_exact, n, large)


def _bias_kernel(relb_ref, bucket_ref, o_ref):
    h = pl.program_id(0)
    bucket = bucket_ref[0]
    out = jnp.zeros(bucket.shape, F32)
    for b in range(REL_BUCKETS):
        out = jnp.where(bucket == b, relb_ref[b * ATTN_HEADS + h], out)
    o_ref[0, 0] = out * LOG2E


def _bias_tiles(rel_bias):
    i = jnp.arange(MOBA_BLOCK)
    rel = (jnp.arange(3)[:, None, None] * MOBA_BLOCK + i[None, None, :] - i[None, :, None])
    bucket = _t5_bucket(rel)
    return pl.pallas_call(
        _bias_kernel,
        out_shape=jax.ShapeDtypeStruct((ATTN_HEADS, 3, MOBA_BLOCK, MOBA_BLOCK), F32),
        grid=(ATTN_HEADS, 3),
        in_specs=[pl.BlockSpec(memory_space=pltpu.SMEM),
                  pl.BlockSpec((1, MOBA_BLOCK, MOBA_BLOCK), lambda h, d: (d, 0, 0))],
        out_specs=pl.BlockSpec((1, 1, MOBA_BLOCK, MOBA_BLOCK), lambda h, d: (h, d, 0, 0)),
        compiler_params=_cparams(("parallel", "parallel")),
        name="bias_tiles",
    )(rel_bias.astype(F32).reshape(-1), bucket)


_NT = (((1,), (1,)), ((), ()))


def _moba_scores(own, slot0, state, q_ref, k_ref, bias_ref, km_ref, s_ref):
    bs = MOBA_BLOCK
    q = q_ref[own * bs:(own + 1) * bs, :]

    ranks = []
    if own > 0:
        gate = lax.dot_general(km_ref[...].astype(BF16), q, _NT, preferred_element_type=F32)
        blk = lax.broadcasted_iota(jnp.int32, gate.shape, 0)
        gm = jnp.where(blk < own, gate, NEG_INF)
        for n in range(own):
            g_n = gm[n:n + 1, :]
            ge = jnp.where(gm >= g_n, 1.0, 0.0)
            gt = jnp.where(gm > g_n, 1.0, 0.0)
            ranks.append(jnp.sum(jnp.where(blk < n, ge, gt), axis=0, keepdims=True))

    c1 = HEAD_DIM ** -0.5 * LOG2E
    m = None
    for idx, j in enumerate([own] + list(range(own))):
        k_j = k_ref[j * bs:(j + 1) * bs, :]
        raw = lax.dot_general(k_j, q, _NT, preferred_element_type=F32)
        if j == own:
            key = lax.broadcasted_iota(jnp.int32, (bs, bs), 0)
            qry = lax.broadcasted_iota(jnp.int32, (bs, bs), 1)
            s = jnp.where(key <= qry, raw * c1 + bias_ref[0, 0], NEG_INF)
        elif own - j == 1:
            s = jnp.where(ranks[j] < float(MOBA_TOPK), raw * c1 + bias_ref[0, 1], NEG_INF)
        else:
            row = jnp.where(ranks[j] < float(MOBA_TOPK), bias_ref[0, 2, 0:1, :], NEG_INF)
            s = raw * c1 + row
        s_ref[slot0 + idx] = s
        m_j = jnp.max(s, axis=0, keepdims=True)
        m = m_j if m is None else jnp.maximum(m, m_j)
        state["m"] = m
        yield


def _moba_values(own, slot0, state, vt_ref, o_ref, s_ref):
    bs = MOBA_BLOCK
    m = state["m"]
    acc = None
    for idx, j in enumerate([own] + list(range(own))):
        p = jnp.exp2((s_ref[slot0 + idx] - m).astype(BF16))
        a_j = jnp.dot(vt_ref[:, j * bs:(j + 1) * bs], p, preferred_element_type=F32)
        acc = a_j if acc is None else acc + a_j
        if idx == own:
            o = acc[:HEAD_DIM, :] / acc[HEAD_DIM:HEAD_DIM + 1, :]
            o_ref[own * bs:(own + 1) * bs, :] = o.T.astype(o_ref.dtype)
        yield


def _moba_kernel(q_ref, k_ref, v_ref, bias_ref, o_ref, km_ref, vt_ref, s_ref, *, n_blk):
    bs = MOBA_BLOCK
    vt_ref[HEAD_DIM:, :] = jnp.ones((vt_ref.shape[0] - HEAD_DIM, vt_ref.shape[1]), BF16)
    for n in range(n_blk):
        kb = k_ref[n * bs:(n + 1) * bs, :].astype(F32)
        km_ref[n:n + 1, :] = jnp.mean(kb, axis=0, keepdims=True)
        vt_ref[:HEAD_DIM, n * bs:(n + 1) * bs] = (
            v_ref[n * bs:(n + 1) * bs, :].astype(F32).T.astype(BF16))

    slots = [own * (own + 1) // 2 for own in range(n_blk)]
    states = [dict() for _ in range(n_blk)]
    pending = iter(())
    for own in range(n_blk):
        scores = _moba_scores(own, slots[own], states[own], q_ref, k_ref, bias_ref, km_ref, s_ref)
        for _ in itertools.zip_longest(scores, pending):
            pass
        pending = _moba_values(own, slots[own], states[own], vt_ref, o_ref, s_ref)
    for _ in pending:
        pass


def _moba(proj, bias_tiles, bsz, seq_len):
    n_blk = seq_len // MOBA_BLOCK
    q_off = SSM_WIDTH // HEAD_DIM
    k_off = q_off + ATTN_HEADS
    v_off = k_off + ATTN_HEADS
    n_slots = n_blk * (n_blk + 1) // 2
    return pl.pallas_call(
        functools.partial(_moba_kernel, n_blk=n_blk),
        out_shape=jax.ShapeDtypeStruct((bsz * seq_len, ATTN_WIDTH), BF16),
        grid=(bsz, ATTN_HEADS),
        in_specs=[pl.BlockSpec((seq_len, HEAD_DIM), lambda b, h: (b, q_off + h)),
                  pl.BlockSpec((seq_len, HEAD_DIM), lambda b, h: (b, k_off + h)),
                  pl.BlockSpec((seq_len, HEAD_DIM), lambda b, h: (b, v_off + h)),
                  pl.BlockSpec((1, 3, MOBA_BLOCK, MOBA_BLOCK), lambda b, h: (h, 0, 0, 0))],
        out_specs=pl.BlockSpec((seq_len, HEAD_DIM), lambda b, h: (b, h)),
        scratch_shapes=[pltpu.VMEM((n_blk, HEAD_DIM), F32),
                        pltpu.VMEM((HEAD_DIM + 16, seq_len), BF16),
                        pltpu.VMEM((n_slots, MOBA_BLOCK, MOBA_BLOCK), F32)],
        compiler_params=_cparams(("parallel", "parallel")),
        name="moba",
    )(proj, proj, proj, bias_tiles)


def _merge_kernel(ys_ref, ya_ref, ws_ref, wa_ref, ga_ref, gb_ref, o_ref):
    a = jnp.dot(ys_ref[...], ws_ref[...], preferred_element_type=F32)
    b = jnp.dot(ya_ref[...], wa_ref[...], preferred_element_type=F32)
    o_ref[...] = (jax.nn.sigmoid(ga_ref[...].astype(F32)) * a
                  + jax.nn.sigmoid(gb_ref[...].astype(F32)) * b).astype(o_ref.dtype)


def _merge(y_ssm, y_att, w_ps, w_pa, proj, tm=512, tn=2048):
    t, k = y_ssm.shape
    n = w_ps.shape[1]
    ga_off = (SSM_WIDTH + 3 * ATTN_WIDTH) // tn
    gb_off = ga_off + D_MODEL // tn
    return pl.pallas_call(
        _merge_kernel,
        out_shape=jax.ShapeDtypeStruct((t, n), BF16),
        grid=(t // tm, n // tn),
        in_specs=[pl.BlockSpec((tm, k), lambda i, j: (i, 0)),
                  pl.BlockSpec((tm, k), lambda i, j: (i, 0)),
                  pl.BlockSpec((k, tn), lambda i, j: (0, j)),
                  pl.BlockSpec((k, tn), lambda i, j: (0, j)),
                  pl.BlockSpec((tm, tn), lambda i, j: (i, ga_off + j)),
                  pl.BlockSpec((tm, tn), lambda i, j: (i, gb_off + j))],
        out_specs=pl.BlockSpec((tm, tn), lambda i, j: (i, j)),
        compiler_params=_cparams(("parallel", "parallel")),
        name="merge",
    )(y_ssm, y_att, w_ps, w_pa, proj, proj)


def _resid_mm_kernel(a_ref, w_ref, x_ref, g_ref, o_ref):
    acc = jnp.dot(a_ref[...], w_ref[...], preferred_element_type=F32)
    o_ref[...] = x_ref[...] + g_ref[0] * acc


def _resid_mm(a, w, x2d, mod4, g_idx, seq_len, tm=512, tn=2048):
    t, k = a.shape
    n = w.shape[1]
    per_b = seq_len // tm
    return pl.pallas_call(
        _resid_mm_kernel,
        out_shape=jax.ShapeDtypeStruct((t, n), F32),
        grid=(t // tm, n // tn),
        in_specs=[pl.BlockSpec((tm, k), lambda i, j: (i, 0)),
                  pl.BlockSpec((k, tn), lambda i, j: (0, j)),
                  pl.BlockSpec((tm, tn), lambda i, j: (i, j)),
                  pl.BlockSpec((1, 1, tn), lambda i, j: ((i // per_b) * N_MOD + g_idx, 0, j))],
        out_specs=pl.BlockSpec((tm, tn), lambda i, j: (i, j)),
        compiler_params=_cparams(("parallel", "parallel")),
        name="resid_mm",
    )(a, w, x2d, mod4)


def _ff2_kernel(h_ref, w_ref, x_ref, g_ref, gf_ref, o_ref, *, rows):
    k = pl.program_id(1)

    @pl.when(k == 0)
    def _():
        o_ref[...] = jnp.dot(h_ref[...], w_ref[...], preferred_element_type=F32)

    @pl.when(k != 0)
    def _():
        o_ref[...] += jnp.dot(h_ref[...], w_ref[...], preferred_element_type=F32)

    @pl.when(k == pl.num_programs(1) - 1)
    def _():
        for r in range(0, o_ref.shape[0], rows):
            xo = x_ref[r:r + rows, :] + g_ref[0] * o_ref[r:r + rows, :]
            ms = jnp.mean(xo * xo, axis=-1, keepdims=True)
            o_ref[r:r + rows, :] = xo * lax.rsqrt(ms + EPS) * gf_ref[...]


def _ff2(hid, w, x2d, mod4, g_idx, gf, seq_len, tm=1024, tk=1024):
    t, kdim = hid.shape
    n = w.shape[1]
    per_b = seq_len // tm
    return pl.pallas_call(
        functools.partial(_ff2_kernel, rows=256),
        out_shape=jax.ShapeDtypeStruct((t, n), F32),
        grid=(t // tm, kdim // tk),
        in_specs=[pl.BlockSpec((tm, tk), lambda i, k: (i, k)),
                  pl.BlockSpec((tk, n), lambda i, k: (k, 0)),
                  pl.BlockSpec((tm, n), lambda i, k: (i, 0)),
                  pl.BlockSpec((1, 1, n), lambda i, k: ((i // per_b) * N_MOD + g_idx, 0, 0)),
                  pl.BlockSpec((1, n), lambda i, k: (0, 0))],
        out_specs=pl.BlockSpec((tm, n), lambda i, k: (i, 0)),
        compiler_params=_cparams(("parallel", "arbitrary")),
        name="ff2_final",
    )(hid, w, x2d, mod4, gf.reshape(1, n))


def kernel(x, c, rel_bias, w_ada, b_ada, norm_mix_g, w_in, ssm_a_re, ssm_a_im, ssm_log_dt,
           ssm_b_re, ssm_b_im, ssm_c_re, ssm_c_im, ssm_d, w_glu, b_glu, w_proj_ssm,
           w_proj_attn, w_out, norm_mlp_g, w_ff1, w_ff2, norm_final_g):
    bsz, seq_len, d = x.shape
    depth = w_in.shape[0]
    assert depth == 1, "the final rms_norm is fused into the single layer's ff2 kernel"
    t = bsz * seq_len
    x2d = x.reshape(t, d)
    bias_tiles = _bias_tiles(rel_bias)

    for l in range(depth):
        mod4 = _mod(c, w_ada[l], b_ada[l]).reshape(bsz * N_MOD, 1, d)

        proj = _norm_mm(x2d, norm_mix_g[l], mod4, 1, 0, w_in[l].astype(BF16), seq_len, act="none")

        at, bw, cw, kw = _s5prep(ssm_a_re[l], ssm_a_im[l], ssm_log_dt[l],
                                 ssm_b_re[l], ssm_b_im[l], ssm_c_re[l], ssm_c_im[l])
        bmat, cmat, kmat, atv, d2 = _s5_weights(at, bw, cw, kw, ssm_d[l], bsz)
        u_p = _group_layout(proj[:, :SSM_WIDTH], bsz, seq_len)
        y_p = _s5(u_p, bmat, cmat, kmat, atv, d2, bsz)
        y_ssm = _glu(_token_layout(y_p, bsz, seq_len), w_glu[l].astype(BF16), b_glu[l])

        y_att = _moba(proj, bias_tiles, bsz, seq_len)

        merged = _merge(y_ssm, y_att, w_proj_ssm[l].astype(BF16), w_proj_attn[l].astype(BF16), proj)
        x2d = _resid_mm(merged, w_out[l].astype(BF16), x2d, mod4, 2, seq_len)

        hid = _norm_mm(x2d, norm_mlp_g[l], mod4, 4, 3, w_ff1[l].astype(BF16), seq_len, act="relu2")
        x2d = _ff2(hid, w_ff2[l].astype(BF16), x2d, mod4, 5, norm_final_g, seq_len)
    return x2d.reshape(bsz, seq_len, d)
```

```python
import functools
import itertools
import math

import jax
import jax.numpy as jnp
from jax import lax
from jax.experimental import pallas as pl
from jax.experimental.pallas import tpu as pltpu

F32 = jnp.float32
BF16 = jnp.bfloat16

D_MODEL = 2048
SSM_WIDTH = 1024
SSM_GROUP = 16
SSM_GROUPS = 64
SSM_STATE = 64
ATTN_HEADS = 8
HEAD_DIM = 128
ATTN_WIDTH = 1024
MOBA_BLOCK = 256
MOBA_TOPK = 3
REL_BUCKETS = 32
REL_MAX_DIST = 128
D_FF = 4 * D_MODEL
N_MOD = 6
EPS = 1e-6
NEG_INF = -1e30
LOG2E = math.log2(math.e)
IN_WIDTH = SSM_WIDTH + 3 * ATTN_WIDTH + 2 * D_MODEL

VMEM_LIMIT_BYTES = 56 * 1024 * 1024

S5_TAU = 2
S5_GROUPS = 8
S5_CH = S5_GROUPS * SSM_GROUP
S5_UC = S5_TAU * S5_CH
S5_SC = 2 * S5_GROUPS * SSM_STATE
S5_NBLK = SSM_GROUPS // S5_GROUPS
S5_ROWS = 1024


def _cparams(sem):
    return pltpu.CompilerParams(dimension_semantics=sem,
                                vmem_limit_bytes=VMEM_LIMIT_BYTES)


def _mod_kernel(c_ref, w_ref, b_ref, o_ref):
    c = c_ref[...]
    ca = (c * jax.nn.sigmoid(c)).astype(BF16)
    o_ref[...] = jnp.dot(ca, w_ref[...].astype(BF16),
                         preferred_element_type=F32) + b_ref[...]


def _mod(c, w_ada, b_ada):
    bsz, d = c.shape
    n = w_ada.shape[1]
    tn = 1024
    return pl.pallas_call(
        _mod_kernel,
        out_shape=jax.ShapeDtypeStruct((bsz, n), F32),
        grid=(n // tn,),
        in_specs=[pl.BlockSpec((bsz, d), lambda j: (0, 0)),
                  pl.BlockSpec((d, tn), lambda j: (0, j)),
                  pl.BlockSpec((1, tn), lambda j: (0, j))],
        out_specs=pl.BlockSpec((bsz, tn), lambda j: (0, j)),
        compiler_params=_cparams(("parallel",)),
        name="mod",
    )(c, w_ada, b_ada.reshape(1, n))


def _norm_mm_kernel(x_ref, g_ref, sc_ref, sh_ref, w_ref, o_ref, h_ref, *, act, rows):
    def mm(h):
        acc = jnp.dot(h, w_ref[...], preferred_element_type=F32)
        if act == "relu2":
            acc = jnp.square(jnp.maximum(acc, 0.0))
        return acc.astype(o_ref.dtype)

    @pl.when(pl.program_id(1) == 0)
    def _():
        g = g_ref[...]
        sc = 1.0 + sc_ref[0]
        sh = sh_ref[0]
        tm = x_ref.shape[0]
        for r in range(0, tm, rows):
            x = x_ref[r:r + rows, :]
            ms = jnp.mean(x * x, axis=-1, keepdims=True)
            y = x * lax.rsqrt(ms + EPS) * g
            h = (y * sc + sh).astype(BF16)
            h_ref[r:r + rows, :] = h
            o_ref[r:r + rows, :] = mm(h)

    @pl.when(pl.program_id(1) != 0)
    def _():
        o_ref[...] = mm(h_ref[...])


def _norm_mm(x2d, g, mod4, sc_idx, sh_idx, w, seq_len, *, act, tm=1024, tn=1024):
    t, d = x2d.shape
    n = w.shape[1]
    per_b = seq_len // tm
    return pl.pallas_call(
        functools.partial(_norm_mm_kernel, act=act, rows=256),
        out_shape=jax.ShapeDtypeStruct((t, n), BF16),
        grid=(t // tm, n // tn),
        in_specs=[pl.BlockSpec((tm, d), lambda i, j: (i, 0)),
                  pl.BlockSpec((1, d), lambda i, j: (0, 0)),
                  pl.BlockSpec((1, 1, d), lambda i, j: ((i // per_b) * N_MOD + sc_idx, 0, 0)),
                  pl.BlockSpec((1, 1, d), lambda i, j: ((i // per_b) * N_MOD + sh_idx, 0, 0)),
                  pl.BlockSpec((d, tn), lambda i, j: (0, j))],
        out_specs=pl.BlockSpec((tm, tn), lambda i, j: (i, j)),
        scratch_shapes=[pltpu.VMEM((tm, d), BF16)],
        compiler_params=_cparams(("parallel", "arbitrary")),
        name="norm_mm_" + act,
    )(x2d, g.reshape(1, d), mod4, mod4, w)


def _s5prep_kernel(are_ref, aim_ref, ldt_ref, bre_ref, bim_ref, cre_ref, cim_ref,
                   at_ref, bw_ref, cw_ref, kw_ref):
    tau = bw_ref.shape[0]
    a_re = are_ref[...]
    a_im = aim_ref[...]
    dt = jnp.exp(ldt_ref[...])
    mag = jnp.exp(dt * a_re)
    abar_re = mag * jnp.cos(dt * a_im)
    abar_im = mag * jnp.sin(dt * a_im)
    den = a_re * a_re + a_im * a_im
    p_re = abar_re - 1.0
    f_re = (p_re * a_re + abar_im * a_im) / den
    f_im = (abar_im * a_re - p_re * a_im) / den

    def cmul(x, y):
        return x[0] * y[0] - x[1] * y[1], x[0] * y[1] + x[1] * y[0]

    def cat(x):
        return jnp.concatenate([x[0], x[1]], axis=-1)

    def re_dot(x):
        return jnp.concatenate([x[0], -x[1]], axis=-1)

    abar = (abar_re, abar_im)
    bb = cmul((f_re, f_im), (bre_ref[...], bim_ref[...]))
    cc = (cre_ref[...], cim_ref[...])
    apow = [None, abar]
    for _ in range(tau - 1):
        apow.append(cmul(apow[-1], abar))
    at_ref[...] = cat(apow[tau])
    y0 = cat(bb)
    for i in range(tau):
        j = tau - 1 - i
        bw_ref[i] = y0 if j == 0 else cat(cmul(apow[j], bb))
        cw_ref[i] = re_dot(cmul(cc, apow[i + 1]))
        x = re_dot(cc if i == 0 else cmul(cc, apow[i]))
        kw_ref[i] = jnp.einsum("gpn,gqn->gpq", x, y0, precision=lax.Precision.HIGHEST,
                               preferred_element_type=F32)


def _s5prep(a_re, a_im, log_dt, b_re, b_im, c_re, c_im):
    g, n = a_re.shape
    p = b_re.shape[2]
    b_re_t = jnp.transpose(b_re, (0, 2, 1))
    b_im_t = jnp.transpose(b_im, (0, 2, 1))
    return pl.pallas_call(
        _s5prep_kernel,
        out_shape=(jax.ShapeDtypeStruct((g, 1, 2 * n), F32),
                   jax.ShapeDtypeStruct((S5_TAU, g, p, 2 * n), F32),
                   jax.ShapeDtypeStruct((S5_TAU, g, p, 2 * n), F32),
                   jax.ShapeDtypeStruct((S5_TAU, g, p, p), F32)),
        name="s5prep",
    )(a_re.reshape(g, 1, n), a_im.reshape(g, 1, n), log_dt.reshape(g, 1, 1),
      b_re_t, b_im_t, c_re, c_im)


def _block_diag(x):
    nb, ng, r, c = x.shape
    eye = jnp.eye(ng, dtype=x.dtype)
    return jnp.einsum("agrc,gh->agrhc", x, eye).reshape(nb, ng * r, ng * c)


def _s5_weights(at, bw, cw, kw, d_skip, bsz):
    n = SSM_STATE
    blk = (S5_NBLK, S5_GROUPS)

    def bd(x):
        return _block_diag(x.reshape(blk + x.shape[1:]))

    def bd_t(x):
        return bd(jnp.transpose(x, (0, 2, 1)))

    bmat = jnp.concatenate(
        [jnp.concatenate([bd(bw[i, :, :, :n]), bd(bw[i, :, :, n:])], axis=-1)
         for i in range(S5_TAU)], axis=1).astype(BF16)
    cmat = jnp.concatenate(
        [jnp.concatenate([bd_t(cw[i, :, :, :n]), bd_t(cw[i, :, :, n:])], axis=1)
         for i in range(S5_TAU)], axis=-1).astype(BF16)
    lag = [bd_t(kw[i]) for i in range(S5_TAU)]
    zero = jnp.zeros_like(lag[0])
    kmat = jnp.concatenate(
        [jnp.concatenate([lag[i - ip] if ip <= i else zero for i in range(S5_TAU)], axis=-1)
         for ip in range(S5_TAU)], axis=1).astype(BF16)
    half = S5_SC // 2
    atv = jnp.concatenate([at[:, 0, :n].reshape(S5_NBLK, half),
                           at[:, 0, n:].reshape(S5_NBLK, half)], axis=-1)
    atv = jnp.broadcast_to(atv.reshape(S5_NBLK, 1, S5_SC), (S5_NBLK, bsz, S5_SC))
    d = jnp.broadcast_to(d_skip.reshape(S5_NBLK, 1, S5_CH), (S5_NBLK, S5_TAU, S5_CH))
    return bmat, cmat, kmat, atv, d.reshape(S5_NBLK, 1, S5_UC)


def _s5_kernel(*refs, bsz, n_sub):
    slot_refs = refs[:S5_TAU]
    bw_ref, cw_ref, kw_ref, a_ref, d_ref, y_ref, u_ref, w_ref, s_ref, st_ref = refs[S5_TAU:]

    @pl.when(pl.program_id(1) == 0)
    def _():
        st_ref[...] = jnp.zeros_like(st_ref)

    half = S5_SC // 2
    a_re = a_ref[0, :, :half]
    a_im = a_ref[0, :, half:]
    s_re = st_ref[:, :half]
    s_im = st_ref[:, half:]
    d2 = d_ref[0]
    sub = u_ref.shape[0] // n_sub
    ksub = sub // bsz

    def project_in(c):
        for i, src in enumerate(slot_refs):
            x = src[:, c * ksub:(c + 1) * ksub, :].astype(F32)
            x = pltpu.einshape("bkc->kbc", x).reshape(sub, S5_CH)
            u_ref[c * sub:(c + 1) * sub, i * S5_CH:(i + 1) * S5_CH] = x.astype(BF16)
        w_ref[c * sub:(c + 1) * sub, :] = jnp.dot(u_ref[c * sub:(c + 1) * sub, :], bw_ref[0],
                                                  preferred_element_type=F32)

    project_in(0)
    for c in range(n_sub):
        r0 = c * sub
        if c + 1 < n_sub:
            project_in(c + 1)
        u = u_ref[r0:r0 + sub, :]
        for k in range(0, ksub, 2):
            before = []
            for kk in range(2):
                r = r0 + (k + kk) * bsz
                before.append(jnp.concatenate([s_re, s_im], axis=1))
                w = w_ref[r:r + bsz, :]
                s_re, s_im = (a_re * s_re - a_im * s_im + w[:, :half],
                              a_re * s_im + a_im * s_re + w[:, half:])
            s_ref[r0 + k * bsz:r0 + (k + 2) * bsz, :] = (
                jnp.concatenate(before, axis=0).astype(BF16))
        y = (jnp.dot(s_ref[r0:r0 + sub, :], cw_ref[0], preferred_element_type=F32)
             + jnp.dot(u, kw_ref[0], preferred_element_type=F32)
             + d2 * u.astype(F32))
        y = jax.nn.gelu(y)
        for i in range(S5_TAU):
            y_i = pltpu.einshape(
                "kbc->bkc", y[:, i * S5_CH:(i + 1) * S5_CH].reshape(ksub, bsz, S5_CH))
            y_ref[:, c * ksub:(c + 1) * ksub, i * S5_CH:(i + 1) * S5_CH] = y_i.astype(y_ref.dtype)
    st_ref[:, :half] = s_re
    st_ref[:, half:] = s_im


def _s5(proj, bmat, cmat, kmat, atv, d2, bsz, seq_len):
    assert S5_CH == 128 and S5_UC == 256, "a block's channels must be one lane tile"
    kpc = S5_ROWS // bsz
    n_k = seq_len // S5_TAU
    groups = proj.reshape(bsz, n_k, S5_TAU * IN_WIDTH)
    slot_specs = [pl.BlockSpec((bsz, kpc, S5_CH),
                               lambda g, c, i=i: (0, c, i * (IN_WIDTH // S5_CH) + g))
                  for i in range(S5_TAU)]
    return pl.pallas_call(
        functools.partial(_s5_kernel, bsz=bsz, n_sub=4),
        out_shape=jax.ShapeDtypeStruct((bsz, n_k, S5_TAU * SSM_WIDTH), BF16),
        grid=(S5_NBLK, n_k // kpc),
        in_specs=slot_specs + [
            pl.BlockSpec((1, S5_UC, S5_SC), lambda g, c: (g, 0, 0)),
            pl.BlockSpec((1, S5_SC, S5_UC), lambda g, c: (g, 0, 0)),
            pl.BlockSpec((1, S5_UC, S5_UC), lambda g, c: (g, 0, 0)),
            pl.BlockSpec((1, bsz, S5_SC), lambda g, c: (g, 0, 0)),
            pl.BlockSpec((1, 1, S5_UC), lambda g, c: (g, 0, 0))],
        out_specs=pl.BlockSpec((bsz, kpc, S5_UC), lambda g, c: (0, c, g)),
        scratch_shapes=[pltpu.VMEM((S5_ROWS, S5_UC), BF16),
                        pltpu.VMEM((S5_ROWS, S5_SC), F32),
                        pltpu.VMEM((S5_ROWS, S5_SC), BF16),
                        pltpu.VMEM((bsz, S5_SC), F32)],
        compiler_params=_cparams(("parallel", "arbitrary")),
        name="s5",
    )(*([groups] * S5_TAU), bmat, cmat, kmat, atv, d2)


def _glu_kernel(y_ref, w_ref, b_ref, o_ref):
    n = w_ref.shape[0]
    for i in range(S5_TAU):
        y = jnp.concatenate(
            [y_ref[:, (S5_TAU * g + i) * S5_CH:(S5_TAU * g + i + 1) * S5_CH]
             for g in range(S5_NBLK)], axis=1)
        z = jnp.dot(y, w_ref[...], preferred_element_type=F32) + b_ref[...]
        o_ref[:, i * n:(i + 1) * n] = (y.astype(F32) * jax.nn.sigmoid(z)).astype(o_ref.dtype)


def _glu(y_groups, w, b, tm=512):
    rows, cols = y_groups.shape
    n = cols // S5_TAU
    return pl.pallas_call(
        _glu_kernel,
        out_shape=jax.ShapeDtypeStruct((rows, cols), BF16),
        grid=(rows // tm,),
        in_specs=[pl.BlockSpec((tm, cols), lambda i: (i, 0)),
                  pl.BlockSpec((n, n), lambda i: (0, 0)),
                  pl.BlockSpec((1, n), lambda i: (0, 0))],
        out_specs=pl.BlockSpec((tm, cols), lambda i: (i, 0)),
        compiler_params=_cparams(("parallel",)),
        name="glu",
    )(y_groups, w, b.reshape(1, n))


def _t5_bucket(rel):
    n = jnp.maximum(rel, 0)
    max_exact = REL_BUCKETS // 2
    nf = jnp.maximum(n, 1).astype(F32)
    large = max_exact + (jnp.log(nf / max_exact) / math.log(REL_MAX_DIST / max_exact)
                         * (REL_BUCKETS - max_exact)).astype(jnp.int32)
    large = jnp.minimum(large, REL_BUCKETS - 1)
    return jnp.where(n < max_exact, n, large)


def _bias_kernel(relb_ref, bucket_ref, o_ref):
    h = pl.program_id(0)
    bucket = bucket_ref[0]
    out = jnp.zeros(bucket.shape, F32)
    for b in range(REL_BUCKETS):
        out = jnp.where(bucket == b, relb_ref[b * ATTN_HEADS + h], out)
    o_ref[0, 0] = out * LOG2E


def _bias_tiles(rel_bias):
    i = jnp.arange(MOBA_BLOCK)
    rel = (jnp.arange(3)[:, None, None] * MOBA_BLOCK + i[None, None, :] - i[None, :, None])
    bucket = _t5_bucket(rel)
    return pl.pallas_call(
        _bias_kernel,
        out_shape=jax.ShapeDtypeStruct((ATTN_HEADS, 3, MOBA_BLOCK, MOBA_BLOCK), F32),
        grid=(ATTN_HEADS, 3),
        in_specs=[pl.BlockSpec(memory_space=pltpu.SMEM),
                  pl.BlockSpec((1, MOBA_BLOCK, MOBA_BLOCK), lambda h, d: (d, 0, 0))],
        out_specs=pl.BlockSpec((1, 1, MOBA_BLOCK, MOBA_BLOCK), lambda h, d: (h, d, 0, 0)),
        compiler_params=_cparams(("parallel", "parallel")),
        name="bias_tiles",
    )(rel_bias.astype(F32).reshape(-1), bucket)


_NT = (((1,), (1,)), ((), ()))


def _moba_scores(own, slot0, state, q_ref, k_ref, bias_ref, km_ref, s_ref):
    bs = MOBA_BLOCK
    q = q_ref[own * bs:(own + 1) * bs, :]

    ranks = []
    if own > 0:
        gate = lax.dot_general(km_ref[...].astype(BF16), q, _NT, preferred_element_type=F32)
        blk = lax.broadcasted_iota(jnp.int32, gate.shape, 0)
        gm = jnp.where(blk < own, gate, NEG_INF)
        for n in range(own):
            g_n = gm[n:n + 1, :]
            ge = jnp.where(gm >= g_n, 1.0, 0.0)
            gt = jnp.where(gm > g_n, 1.0, 0.0)
            ranks.append(jnp.sum(jnp.where(blk < n, ge, gt), axis=0, keepdims=True))

    c1 = HEAD_DIM ** -0.5 * LOG2E
    m = None
    for idx, j in enumerate([own] + list(range(own))):
        k_j = k_ref[j * bs:(j + 1) * bs, :]
        raw = lax.dot_general(k_j, q, _NT, preferred_element_type=F32)
        if j == own:
            key = lax.broadcasted_iota(jnp.int32, (bs, bs), 0)
            qry = lax.broadcasted_iota(jnp.int32, (bs, bs), 1)
            s = jnp.where(key <= qry, raw * c1 + bias_ref[0, 0], NEG_INF)
        elif own - j == 1:
            s = jnp.where(ranks[j] < float(MOBA_TOPK), raw * c1 + bias_ref[0, 1], NEG_INF)
        else:
            row = jnp.where(ranks[j] < float(MOBA_TOPK), bias_ref[0, 2, 0:1, :], NEG_INF)
            s = raw * c1 + row
        s_ref[slot0 + idx] = s
        m_j = jnp.max(s, axis=0, keepdims=True)
        m = m_j if m is None else jnp.maximum(m, m_j)
        state["m"] = m
        yield


def _moba_values(own, slot0, state, vt_ref, o_ref, s_ref):
    bs = MOBA_BLOCK
    m = state["m"]
    acc = None
    for idx, j in enumerate([own] + list(range(own))):
        p = jnp.exp2((s_ref[slot0 + idx] - m).astype(BF16))
        a_j = jnp.dot(vt_ref[:, j * bs:(j + 1) * bs], p, preferred_element_type=F32)
        acc = a_j if acc is None else acc + a_j
        if idx == own:
            o = acc[:HEAD_DIM, :] / acc[HEAD_DIM:HEAD_DIM + 1, :]
            o_ref[own * bs:(own + 1) * bs, :] = o.T.astype(o_ref.dtype)
        yield


def _moba_kernel(q_ref, k_ref, v_ref, bias_ref, o_ref, km_ref, vt_ref, s_ref, *, n_blk):
    bs = MOBA_BLOCK
    vt_ref[HEAD_DIM:, :] = jnp.ones((vt_ref.shape[0] - HEAD_DIM, vt_ref.shape[1]), BF16)
    for n in range(n_blk):
        kb = k_ref[n * bs:(n + 1) * bs, :].astype(F32)
        km_ref[n:n + 1, :] = jnp.mean(kb, axis=0, keepdims=True)
        vt_ref[:HEAD_DIM, n * bs:(n + 1) * bs] = (
            v_ref[n * bs:(n + 1) * bs, :].astype(F32).T.astype(BF16))

    slots = [own * (own + 1) // 2 for own in range(n_blk)]
    states = [dict() for _ in range(n_blk)]
    pending = iter(())
    for own in range(n_blk):
        scores = _moba_scores(own, slots[own], states[own], q_ref, k_ref, bias_ref, km_ref, s_ref)
        for _ in itertools.zip_longest(scores, pending):
            pass
        pending = _moba_values(own, slots[own], states[own], vt_ref, o_ref, s_ref)
    for _ in pending:
        pass


def _moba(proj, bias_tiles, bsz, seq_len):
    n_blk = seq_len // MOBA_BLOCK
    q_off = SSM_WIDTH // HEAD_DIM
    k_off = q_off + ATTN_HEADS
    v_off = k_off + ATTN_HEADS
    n_slots = n_blk * (n_blk + 1) // 2
    return pl.pallas_call(
        functools.partial(_moba_kernel, n_blk=n_blk),
        out_shape=jax.ShapeDtypeStruct((bsz * seq_len, ATTN_WIDTH), BF16),
        grid=(bsz, ATTN_HEADS),
        in_specs=[pl.BlockSpec((seq_len, HEAD_DIM), lambda b, h: (b, q_off + h)),
                  pl.BlockSpec((seq_len, HEAD_DIM), lambda b, h: (b, k_off + h)),
                  pl.BlockSpec((seq_len, HEAD_DIM), lambda b, h: (b, v_off + h)),
                  pl.BlockSpec((1, 3, MOBA_BLOCK, MOBA_BLOCK), lambda b, h: (h, 0, 0, 0))],
        out_specs=pl.BlockSpec((seq_len, HEAD_DIM), lambda b, h: (b, h)),
        scratch_shapes=[pltpu.VMEM((n_blk, HEAD_DIM), F32),
                        pltpu.VMEM((HEAD_DIM + 16, seq_len), BF16),
                        pltpu.VMEM((n_slots, MOBA_BLOCK, MOBA_BLOCK), F32)],
        compiler_params=_cparams(("parallel", "parallel")),
        name="moba",
    )(proj, proj, proj, bias_tiles)


def _merge_kernel(ys_ref, ya_ref, ws_ref, wa_ref, ga_ref, gb_ref, o_ref):
    a = jnp.dot(ys_ref[...], ws_ref[...], preferred_element_type=F32)
    b = jnp.dot(ya_ref[...], wa_ref[...], preferred_element_type=F32)
    o_ref[...] = (jax.nn.sigmoid(ga_ref[...].astype(F32)) * a
                  + jax.nn.sigmoid(gb_ref[...].astype(F32)) * b).astype(o_ref.dtype)


def _merge(y_ssm, y_att, w_ps, w_pa, proj, tm=512, tn=2048):
    t, k = y_ssm.shape
    n = w_ps.shape[1]
    ga_off = (SSM_WIDTH + 3 * ATTN_WIDTH) // tn
    gb_off = ga_off + D_MODEL // tn
    return pl.pallas_call(
        _merge_kernel,
        out_shape=jax.ShapeDtypeStruct((t, n), BF16),
        grid=(t // tm, n // tn),
        in_specs=[pl.BlockSpec((tm, k), lambda i, j: (i, 0)),
                  pl.BlockSpec((tm, k), lambda i, j: (i, 0)),
                  pl.BlockSpec((k, tn), lambda i, j: (0, j)),
                  pl.BlockSpec((k, tn), lambda i, j: (0, j)),
                  pl.BlockSpec((tm, tn), lambda i, j: (i, ga_off + j)),
                  pl.BlockSpec((tm, tn), lambda i, j: (i, gb_off + j))],
        out_specs=pl.BlockSpec((tm, tn), lambda i, j: (i, j)),
        compiler_params=_cparams(("parallel", "parallel")),
        name="merge",
    )(y_ssm, y_att, w_ps, w_pa, proj, proj)


def _resid_mm_kernel(a_ref, w_ref, x_ref, g_ref, o_ref):
    acc = jnp.dot(a_ref[...], w_ref[...], preferred_element_type=F32)
    o_ref[...] = x_ref[...] + g_ref[0] * acc


def _resid_mm(a, w, x2d, mod4, g_idx, seq_len, tm=512, tn=2048):
    t, k = a.shape
    n = w.shape[1]
    per_b = seq_len // tm
    return pl.pallas_call(
        _resid_mm_kernel,
        out_shape=jax.ShapeDtypeStruct((t, n), F32),
        grid=(t // tm, n // tn),
        in_specs=[pl.BlockSpec((tm, k), lambda i, j: (i, 0)),
                  pl.BlockSpec((k, tn), lambda i, j: (0, j)),
                  pl.BlockSpec((tm, tn), lambda i, j: (i, j)),
                  pl.BlockSpec((1, 1, tn), lambda i, j: ((i // per_b) * N_MOD + g_idx, 0, j))],
        out_specs=pl.BlockSpec((tm, tn), lambda i, j: (i, j)),
        compiler_params=_cparams(("parallel", "parallel")),
        name="resid_mm",
    )(a, w, x2d, mod4)


def _ff2_kernel(h_ref, w_ref, x_ref, g_ref, gf_ref, o_ref, *, rows):
    k = pl.program_id(1)

    @pl.when(k == 0)
    def _():
        o_ref[...] = jnp.dot(h_ref[...], w_ref[...], preferred_element_type=F32)

    @pl.when(k != 0)
    def _():
        o_ref[...] += jnp.dot(h_ref[...], w_ref[...], preferred_element_type=F32)

    @pl.when(k == pl.num_programs(1) - 1)
    def _():
        for r in range(0, o_ref.shape[0], rows):
            xo = x_ref[r:r + rows, :] + g_ref[0] * o_ref[r:r + rows, :]
            ms = jnp.mean(xo * xo, axis=-1, keepdims=True)
            o_ref[r:r + rows, :] = xo * lax.rsqrt(ms + EPS) * gf_ref[...]


def _ff2(hid, w, x2d, mod4, g_idx, gf, seq_len, tm=1024, tk=1024):
    t, kdim = hid.shape
    n = w.shape[1]
    per_b = seq_len // tm
    return pl.pallas_call(
        functools.partial(_ff2_kernel, rows=256),
        out_shape=jax.ShapeDtypeStruct((t, n), F32),
        grid=(t // tm, kdim // tk),
        in_specs=[pl.BlockSpec((tm, tk), lambda i, k: (i, k)),
                  pl.BlockSpec((tk, n), lambda i, k: (k, 0)),
                  pl.BlockSpec((tm, n), lambda i, k: (i, 0)),
                  pl.BlockSpec((1, 1, n), lambda i, k: ((i // per_b) * N_MOD + g_idx, 0, 0)),
                  pl.BlockSpec((1, n), lambda i, k: (0, 0))],
        out_specs=pl.BlockSpec((tm, n), lambda i, k: (i, 0)),
        compiler_params=_cparams(("parallel", "arbitrary")),
        name="ff2_final",
    )(hid, w, x2d, mod4, gf.reshape(1, n))


def kernel(x, c, rel_bias, w_ada, b_ada, norm_mix_g, w_in, ssm_a_re, ssm_a_im, ssm_log_dt,
           ssm_b_re, ssm_b_im, ssm_c_re, ssm_c_im, ssm_d, w_glu, b_glu, w_proj_ssm,
           w_proj_attn, w_out, norm_mlp_g, w_ff1, w_ff2, norm_final_g):
    bsz, seq_len, d = x.shape
    depth = w_in.shape[0]
    assert depth == 1, "the final rms_norm is fused into the single layer's ff2 kernel"
    t = bsz * seq_len
    x2d = x.reshape(t, d)
    bias_tiles = _bias_tiles(rel_bias)

    for l in range(depth):
        mod4 = _mod(c, w_ada[l], b_ada[l]).reshape(bsz * N_MOD, 1, d)

        proj = _norm_mm(x2d, norm_mix_g[l], mod4, 1, 0, w_in[l].astype(BF16), seq_len, act="none")

        at, bw, cw, kw = _s5prep(ssm_a_re[l], ssm_a_im[l], ssm_log_dt[l],
                                 ssm_b_re[l], ssm_b_im[l], ssm_c_re[l], ssm_c_im[l])
        bmat, cmat, kmat, atv, d2 = _s5_weights(at, bw, cw, kw, ssm_d[l], bsz)
        y_p = _s5(proj, bmat, cmat, kmat, atv, d2, bsz, seq_len)
        y_ssm = _glu(y_p.reshape(t // S5_TAU, S5_TAU * SSM_WIDTH), w_glu[l].astype(BF16),
                     b_glu[l]).reshape(t, SSM_WIDTH)

        y_att = _moba(proj, bias_tiles, bsz, seq_len)

        merged = _merge(y_ssm, y_att, w_proj_ssm[l].astype(BF16), w_proj_attn[l].astype(BF16), proj)
        x2d = _resid_mm(merged, w_out[l].astype(BF16), x2d, mod4, 2, seq_len)

        hid = _norm_mm(x2d, norm_mlp_g[l], mod4, 4, 3, w_ff1[l].astype(BF16), seq_len, act="relu2")
        x2d = _ff2(hid, w_ff2[l].astype(BF16), x2d, mod4, 5, norm_final_g, seq_len)
    return x2d.reshape(bsz, seq_len, d)
```

```python
import functools
import itertools
import math

import jax
import jax.numpy as jnp
from jax import lax
from jax.experimental import pallas as pl
from jax.experimental.pallas import tpu as pltpu

F32 = jnp.float32
BF16 = jnp.bfloat16

D_MODEL = 2048
SSM_WIDTH = 1024
SSM_GROUP = 16
SSM_GROUPS = 64
SSM_STATE = 64
ATTN_HEADS = 8
HEAD_DIM = 128
ATTN_WIDTH = 1024
MOBA_BLOCK = 256
MOBA_TOPK = 3
REL_BUCKETS = 32
REL_MAX_DIST = 128
D_FF = 4 * D_MODEL
N_MOD = 6
EPS = 1e-6
NEG_INF = -1e30
LOG2E = math.log2(math.e)
IN_WIDTH = SSM_WIDTH + 3 * ATTN_WIDTH + 2 * D_MODEL

VMEM_LIMIT_BYTES = 56 * 1024 * 1024

S5_TAU = 2
S5_GROUPS = 8
S5_CH = S5_GROUPS * SSM_GROUP
S5_UC = S5_TAU * S5_CH
S5_SC = 2 * S5_GROUPS * SSM_STATE
S5_NBLK = SSM_GROUPS // S5_GROUPS
S5_ROWS = 1024


def _cparams(sem):
    return pltpu.CompilerParams(dimension_semantics=sem,
                                vmem_limit_bytes=VMEM_LIMIT_BYTES)


def _mod_kernel(c_ref, w_ref, b_ref, o_ref):
    c = c_ref[...]
    ca = (c * jax.nn.sigmoid(c)).astype(BF16)
    o_ref[...] = jnp.dot(ca, w_ref[...].astype(BF16),
                         preferred_element_type=F32) + b_ref[...]


def _mod(c, w_ada, b_ada):
    bsz, d = c.shape
    n = w_ada.shape[1]
    tn = 1024
    return pl.pallas_call(
        _mod_kernel,
        out_shape=jax.ShapeDtypeStruct((bsz, n), F32),
        grid=(n // tn,),
        in_specs=[pl.BlockSpec((bsz, d), lambda j: (0, 0)),
                  pl.BlockSpec((d, tn), lambda j: (0, j)),
                  pl.BlockSpec((1, tn), lambda j: (0, j))],
        out_specs=pl.BlockSpec((bsz, tn), lambda j: (0, j)),
        compiler_params=_cparams(("parallel",)),
        name="mod",
    )(c, w_ada, b_ada.reshape(1, n))


def _norm_mm_kernel(x_ref, g_ref, sc_ref, sh_ref, w_ref, o_ref, *rest, act, rows):
    h_ref = rest[-1]
    first_f32_ref = rest[0] if len(rest) == 2 else None

    def mm(h):
        acc = jnp.dot(h, w_ref[...], preferred_element_type=F32)
        if act == "relu2":
            acc = jnp.square(jnp.maximum(acc, 0.0))
        return acc

    @pl.when(pl.program_id(1) == 0)
    def _():
        g = g_ref[...]
        sc = 1.0 + sc_ref[0]
        sh = sh_ref[0]
        tm = x_ref.shape[0]
        for r in range(0, tm, rows):
            x = x_ref[r:r + rows, :]
            ms = jnp.mean(x * x, axis=-1, keepdims=True)
            y = x * lax.rsqrt(ms + EPS) * g
            h = (y * sc + sh).astype(BF16)
            h_ref[r:r + rows, :] = h
            acc = mm(h)
            o_ref[r:r + rows, :] = acc.astype(o_ref.dtype)
            if first_f32_ref is not None:
                first_f32_ref[r:r + rows, :] = acc

    @pl.when(pl.program_id(1) != 0)
    def _():
        o_ref[...] = mm(h_ref[...]).astype(o_ref.dtype)


def _norm_mm(x2d, g, mod4, sc_idx, sh_idx, w, seq_len, *, act, first_tile_f32=False,
             tm=1024, tn=1024):
    t, d = x2d.shape
    n = w.shape[1]
    per_b = seq_len // tm
    out_shape = jax.ShapeDtypeStruct((t, n), BF16)
    out_specs = pl.BlockSpec((tm, tn), lambda i, j: (i, j))
    if first_tile_f32:
        out_shape = (out_shape, jax.ShapeDtypeStruct((t, tn), F32))
        out_specs = (out_specs, pl.BlockSpec((tm, tn), lambda i, j: (i, 0)))
    return pl.pallas_call(
        functools.partial(_norm_mm_kernel, act=act, rows=256),
        out_shape=out_shape,
        grid=(t // tm, n // tn),
        in_specs=[pl.BlockSpec((tm, d), lambda i, j: (i, 0)),
                  pl.BlockSpec((1, d), lambda i, j: (0, 0)),
                  pl.BlockSpec((1, 1, d), lambda i, j: ((i // per_b) * N_MOD + sc_idx, 0, 0)),
                  pl.BlockSpec((1, 1, d), lambda i, j: ((i // per_b) * N_MOD + sh_idx, 0, 0)),
                  pl.BlockSpec((d, tn), lambda i, j: (0, j))],
        out_specs=out_specs,
        scratch_shapes=[pltpu.VMEM((tm, d), BF16)],
        compiler_params=_cparams(("parallel", "arbitrary")),
        name="norm_mm_" + act,
    )(x2d, g.reshape(1, d), mod4, mod4, w)


def _s5prep_kernel(are_ref, aim_ref, ldt_ref, bre_ref, bim_ref, cre_ref, cim_ref,
                   at_ref, bw_ref, cw_ref, kw_ref):
    tau = bw_ref.shape[0]
    a_re = are_ref[...]
    a_im = aim_ref[...]
    dt = jnp.exp(ldt_ref[...])
    mag = jnp.exp(dt * a_re)
    abar_re = mag * jnp.cos(dt * a_im)
    abar_im = mag * jnp.sin(dt * a_im)
    den = a_re * a_re + a_im * a_im
    p_re = abar_re - 1.0
    f_re = (p_re * a_re + abar_im * a_im) / den
    f_im = (abar_im * a_re - p_re * a_im) / den

    def cmul(x, y):
        return x[0] * y[0] - x[1] * y[1], x[0] * y[1] + x[1] * y[0]

    def cat(x):
        return jnp.concatenate([x[0], x[1]], axis=-1)

    def re_dot(x):
        return jnp.concatenate([x[0], -x[1]], axis=-1)

    abar = (abar_re, abar_im)
    bb = cmul((f_re, f_im), (bre_ref[...], bim_ref[...]))
    cc = (cre_ref[...], cim_ref[...])
    apow = [None, abar]
    for _ in range(tau - 1):
        apow.append(cmul(apow[-1], abar))
    at_ref[...] = cat(apow[tau])
    y0 = cat(bb)
    for i in range(tau):
        j = tau - 1 - i
        bw_ref[i] = y0 if j == 0 else cat(cmul(apow[j], bb))
        cw_ref[i] = re_dot(cmul(cc, apow[i + 1]))
        x = re_dot(cc if i == 0 else cmul(cc, apow[i]))
        kw_ref[i] = jnp.einsum("gpn,gqn->gpq", x, y0, precision=lax.Precision.HIGHEST,
                               preferred_element_type=F32)


def _s5prep(a_re, a_im, log_dt, b_re, b_im, c_re, c_im):
    g, n = a_re.shape
    p = b_re.shape[2]
    b_re_t = jnp.transpose(b_re, (0, 2, 1))
    b_im_t = jnp.transpose(b_im, (0, 2, 1))
    return pl.pallas_call(
        _s5prep_kernel,
        out_shape=(jax.ShapeDtypeStruct((g, 1, 2 * n), F32),
                   jax.ShapeDtypeStruct((S5_TAU, g, p, 2 * n), F32),
                   jax.ShapeDtypeStruct((S5_TAU, g, p, 2 * n), F32),
                   jax.ShapeDtypeStruct((S5_TAU, g, p, p), F32)),
        name="s5prep",
    )(a_re.reshape(g, 1, n), a_im.reshape(g, 1, n), log_dt.reshape(g, 1, 1),
      b_re_t, b_im_t, c_re, c_im)


def _block_diag(x):
    nb, ng, r, c = x.shape
    eye = jnp.eye(ng, dtype=x.dtype)
    return jnp.einsum("agrc,gh->agrhc", x, eye).reshape(nb, ng * r, ng * c)


def _s5_weights(at, bw, cw, kw, d_skip, bsz):
    n = SSM_STATE
    blk = (S5_NBLK, S5_GROUPS)

    def bd(x):
        return _block_diag(x.reshape(blk + x.shape[1:]))

    def bd_t(x):
        return bd(jnp.transpose(x, (0, 2, 1)))

    bmat = jnp.concatenate(
        [jnp.concatenate([bd(bw[i, :, :, :n]), bd(bw[i, :, :, n:])], axis=-1)
         for i in range(S5_TAU)], axis=1).astype(BF16)
    cmat = jnp.concatenate(
        [jnp.concatenate([bd_t(cw[i, :, :, :n]), bd_t(cw[i, :, :, n:])], axis=1)
         for i in range(S5_TAU)], axis=-1).astype(BF16)
    lag = [bd_t(kw[i]) for i in range(S5_TAU)]
    zero = jnp.zeros_like(lag[0])
    kmat = jnp.concatenate(
        [jnp.concatenate([lag[i - ip] if ip <= i else zero for i in range(S5_TAU)], axis=-1)
         for ip in range(S5_TAU)], axis=1).astype(BF16)
    half = S5_SC // 2
    atv = jnp.concatenate([at[:, 0, :n].reshape(S5_NBLK, half),
                           at[:, 0, n:].reshape(S5_NBLK, half)], axis=-1)
    atv = jnp.broadcast_to(atv.reshape(S5_NBLK, 1, S5_SC), (S5_NBLK, bsz, S5_SC))
    d = jnp.broadcast_to(d_skip.reshape(S5_NBLK, 1, S5_CH), (S5_NBLK, S5_TAU, S5_CH))
    return bmat, cmat, kmat, atv, d.reshape(S5_NBLK, 1, S5_UC)


def _s5_kernel(x_ref, bw_ref, cw_ref, kw_ref, a_ref, d_ref, y_ref, u_ref, w_ref, s_ref, st_ref,
               *, bsz, n_sub):
    @pl.when(pl.program_id(1) == 0)
    def _():
        st_ref[...] = jnp.zeros_like(st_ref)

    half = S5_SC // 2
    a_re = a_ref[0, :, :half]
    a_im = a_ref[0, :, half:]
    s_re = st_ref[:, :half]
    s_im = st_ref[:, half:]
    d2 = d_ref[0]
    sub = u_ref.shape[0] // n_sub
    ksub = sub // bsz

    def slot(ref, c, i):
        return ref.at[:, pl.ds(c * ksub * S5_TAU + i, ksub, stride=S5_TAU), :]

    def project_in(c):
        for i in range(S5_TAU):
            x = pltpu.einshape("bkc->kbc", slot(x_ref, c, i)[...]).reshape(sub, S5_CH)
            u_ref[c * sub:(c + 1) * sub, i * S5_CH:(i + 1) * S5_CH] = x.astype(BF16)
        w_ref[c * sub:(c + 1) * sub, :] = jnp.dot(u_ref[c * sub:(c + 1) * sub, :], bw_ref[0],
                                                  preferred_element_type=F32)

    project_in(0)
    for c in range(n_sub):
        r0 = c * sub
        if c + 1 < n_sub:
            project_in(c + 1)
        u = u_ref[r0:r0 + sub, :]
        for k in range(0, ksub, 2):
            before = []
            for kk in range(2):
                r = r0 + (k + kk) * bsz
                before.append(jnp.concatenate([s_re, s_im], axis=1))
                w = w_ref[r:r + bsz, :]
                s_re, s_im = (a_re * s_re - a_im * s_im + w[:, :half],
                              a_re * s_im + a_im * s_re + w[:, half:])
            s_ref[r0 + k * bsz:r0 + (k + 2) * bsz, :] = (
                jnp.concatenate(before, axis=0).astype(BF16))
        y = (jnp.dot(s_ref[r0:r0 + sub, :], cw_ref[0], preferred_element_type=F32)
             + jnp.dot(u, kw_ref[0], preferred_element_type=F32)
             + d2 * u.astype(F32))
        y = jax.nn.gelu(y)
        for i in range(S5_TAU):
            slot(y_ref, c, i)[...] = pltpu.einshape(
                "kbc->bkc", y[:, i * S5_CH:(i + 1) * S5_CH].reshape(ksub, bsz, S5_CH))
    st_ref[:, :half] = s_re
    st_ref[:, half:] = s_im


def _s5(u, bmat, cmat, kmat, atv, d2):
    assert S5_CH == 128, "a block's channels must be one lane tile"
    bsz, seq_len, width = u.shape
    tpc = S5_ROWS // bsz * S5_TAU
    return pl.pallas_call(
        functools.partial(_s5_kernel, bsz=bsz, n_sub=4),
        out_shape=jax.ShapeDtypeStruct(u.shape, F32),
        grid=(S5_NBLK, seq_len // tpc),
        in_specs=[
            pl.BlockSpec((bsz, tpc, S5_CH), lambda g, c: (0, c, g)),
            pl.BlockSpec((1, S5_UC, S5_SC), lambda g, c: (g, 0, 0)),
            pl.BlockSpec((1, S5_SC, S5_UC), lambda g, c: (g, 0, 0)),
            pl.BlockSpec((1, S5_UC, S5_UC), lambda g, c: (g, 0, 0)),
            pl.BlockSpec((1, bsz, S5_SC), lambda g, c: (g, 0, 0)),
            pl.BlockSpec((1, 1, S5_UC), lambda g, c: (g, 0, 0))],
        out_specs=pl.BlockSpec((bsz, tpc, S5_CH), lambda g, c: (0, c, g)),
        scratch_shapes=[pltpu.VMEM((S5_ROWS, S5_UC), BF16),
                        pltpu.VMEM((S5_ROWS, S5_SC), F32),
                        pltpu.VMEM((S5_ROWS, S5_SC), BF16),
                        pltpu.VMEM((bsz, S5_SC), F32)],
        compiler_params=_cparams(("parallel", "arbitrary")),
        name="s5",
    )(u, bmat, cmat, kmat, atv, d2)


def _glu_kernel(y_ref, w_ref, b_ref, o_ref):
    y = y_ref[...]
    z = jnp.dot(y.astype(BF16), w_ref[...], preferred_element_type=F32) + b_ref[...]
    o_ref[...] = (y * jax.nn.sigmoid(z)).astype(o_ref.dtype)


def _glu(y, w, b, tm=1024):
    t, n = y.shape
    return pl.pallas_call(
        _glu_kernel,
        out_shape=jax.ShapeDtypeStruct((t, n), BF16),
        grid=(t // tm,),
        in_specs=[pl.BlockSpec((tm, n), lambda i: (i, 0)),
                  pl.BlockSpec((n, n), lambda i: (0, 0)),
                  pl.BlockSpec((1, n), lambda i: (0, 0))],
        out_specs=pl.BlockSpec((tm, n), lambda i: (i, 0)),
        compiler_params=_cparams(("parallel",)),
        name="glu",
    )(y, w, b.reshape(1, n))


def _t5_bucket(rel):
    n = jnp.maximum(rel, 0)
    max_exact = REL_BUCKETS // 2
    nf = jnp.maximum(n, 1).astype(F32)
    large = max_exact + (jnp.log(nf / max_exact) / math.log(REL_MAX_DIST / max_exact)
                         * (REL_BUCKETS - max_exact)).astype(jnp.int32)
    large = jnp.minimum(large, REL_BUCKETS - 1)
    return jnp.where(n < max_exact, n, large)


def _bias_kernel(relb_ref, bucket_ref, o_ref):
    h = pl.program_id(0)
    bucket = bucket_ref[0]
    out = jnp.zeros(bucket.shape, F32)
    for b in range(REL_BUCKETS):
        out = jnp.where(bucket == b, relb_ref[b * ATTN_HEADS + h], out)
    o_ref[0, 0] = out * LOG2E


def _bias_tiles(rel_bias):
    i = jnp.arange(MOBA_BLOCK)
    rel = (jnp.arange(3)[:, None, None] * MOBA_BLOCK + i[None, None, :] - i[None, :, None])
    bucket = _t5_bucket(rel)
    return pl.pallas_call(
        _bias_kernel,
        out_shape=jax.ShapeDtypeStruct((ATTN_HEADS, 3, MOBA_BLOCK, MOBA_BLOCK), F32),
        grid=(ATTN_HEADS, 3),
        in_specs=[pl.BlockSpec(memory_space=pltpu.SMEM),
                  pl.BlockSpec((1, MOBA_BLOCK, MOBA_BLOCK), lambda h, d: (d, 0, 0))],
        out_specs=pl.BlockSpec((1, 1, MOBA_BLOCK, MOBA_BLOCK), lambda h, d: (h, d, 0, 0)),
        compiler_params=_cparams(("parallel", "parallel")),
        name="bias_tiles",
    )(rel_bias.astype(F32).reshape(-1), bucket)


_NT = (((1,), (1,)), ((), ()))


def _moba_scores(own, slot0, state, q_ref, k_ref, bias_ref, km_ref, s_ref):
    bs = MOBA_BLOCK
    q = q_ref[own * bs:(own + 1) * bs, :]

    ranks = []
    if own > 0:
        gate = lax.dot_general(km_ref[...].astype(BF16), q, _NT, preferred_element_type=F32)
        blk = lax.broadcasted_iota(jnp.int32, gate.shape, 0)
        gm = jnp.where(blk < own, gate, NEG_INF)
        for n in range(own):
            g_n = gm[n:n + 1, :]
            ge = jnp.where(gm >= g_n, 1.0, 0.0)
            gt = jnp.where(gm > g_n, 1.0, 0.0)
            ranks.append(jnp.sum(jnp.where(blk < n, ge, gt), axis=0, keepdims=True))

    c1 = HEAD_DIM ** -0.5 * LOG2E
    m = None
    for idx, j in enumerate([own] + list(range(own))):
        k_j = k_ref[j * bs:(j + 1) * bs, :]
        raw = lax.dot_general(k_j, q, _NT, preferred_element_type=F32)
        if j == own:
            key = lax.broadcasted_iota(jnp.int32, (bs, bs), 0)
            qry = lax.broadcasted_iota(jnp.int32, (bs, bs), 1)
            s = jnp.where(key <= qry, raw * c1 + bias_ref[0, 0], NEG_INF)
        elif own - j == 1:
            s = jnp.where(ranks[j] < float(MOBA_TOPK), raw * c1 + bias_ref[0, 1], NEG_INF)
        else:
            row = jnp.where(ranks[j] < float(MOBA_TOPK), bias_ref[0, 2, 0:1, :], NEG_INF)
            s = raw * c1 + row
        s_ref[slot0 + idx] = s
        m_j = jnp.max(s, axis=0, keepdims=True)
        m = m_j if m is None else jnp.maximum(m, m_j)
        state["m"] = m
        yield


def _moba_values(own, slot0, state, vt_ref, o_ref, s_ref):
    bs = MOBA_BLOCK
    m = state["m"]
    acc = None
    for idx, j in enumerate([own] + list(range(own))):
        p = jnp.exp2((s_ref[slot0 + idx] - m).astype(BF16))
        a_j = jnp.dot(vt_ref[:, j * bs:(j + 1) * bs], p, preferred_element_type=F32)
        acc = a_j if acc is None else acc + a_j
        if idx == own:
            o = acc[:HEAD_DIM, :] / acc[HEAD_DIM:HEAD_DIM + 1, :]
            o_ref[own * bs:(own + 1) * bs, :] = o.T.astype(o_ref.dtype)
        yield


def _moba_kernel(q_ref, k_ref, v_ref, bias_ref, o_ref, km_ref, vt_ref, s_ref, *, n_blk):
    bs = MOBA_BLOCK
    vt_ref[HEAD_DIM:, :] = jnp.ones((vt_ref.shape[0] - HEAD_DIM, vt_ref.shape[1]), BF16)
    for n in range(n_blk):
        kb = k_ref[n * bs:(n + 1) * bs, :].astype(F32)
        km_ref[n:n + 1, :] = jnp.mean(kb, axis=0, keepdims=True)
        vt_ref[:HEAD_DIM, n * bs:(n + 1) * bs] = (
            v_ref[n * bs:(n + 1) * bs, :].astype(F32).T.astype(BF16))

    slots = [own * (own + 1) // 2 for own in range(n_blk)]
    states = [dict() for _ in range(n_blk)]
    pending = iter(())
    for own in range(n_blk):
        scores = _moba_scores(own, slots[own], states[own], q_ref, k_ref, bias_ref, km_ref, s_ref)
        for _ in itertools.zip_longest(scores, pending):
            pass
        pending = _moba_values(own, slots[own], states[own], vt_ref, o_ref, s_ref)
    for _ in pending:
        pass


def _moba(proj, bias_tiles, bsz, seq_len):
    n_blk = seq_len // MOBA_BLOCK
    q_off = SSM_WIDTH // HEAD_DIM
    k_off = q_off + ATTN_HEADS
    v_off = k_off + ATTN_HEADS
    n_slots = n_blk * (n_blk + 1) // 2
    return pl.pallas_call(
        functools.partial(_moba_kernel, n_blk=n_blk),
        out_shape=jax.ShapeDtypeStruct((bsz * seq_len, ATTN_WIDTH), BF16),
        grid=(bsz, ATTN_HEADS),
        in_specs=[pl.BlockSpec((seq_len, HEAD_DIM), lambda b, h: (b, q_off + h)),
                  pl.BlockSpec((seq_len, HEAD_DIM), lambda b, h: (b, k_off + h)),
                  pl.BlockSpec((seq_len, HEAD_DIM), lambda b, h: (b, v_off + h)),
                  pl.BlockSpec((1, 3, MOBA_BLOCK, MOBA_BLOCK), lambda b, h: (h, 0, 0, 0))],
        out_specs=pl.BlockSpec((seq_len, HEAD_DIM), lambda b, h: (b, h)),
        scratch_shapes=[pltpu.VMEM((n_blk, HEAD_DIM), F32),
                        pltpu.VMEM((HEAD_DIM + 16, seq_len), BF16),
                        pltpu.VMEM((n_slots, MOBA_BLOCK, MOBA_BLOCK), F32)],
        compiler_params=_cparams(("parallel", "parallel")),
        name="moba",
    )(proj, proj, proj, bias_tiles)


def _merge_kernel(ys_ref, ya_ref, ws_ref, wa_ref, ga_ref, gb_ref, o_ref):
    a = jnp.dot(ys_ref[...], ws_ref[...], preferred_element_type=F32)
    b = jnp.dot(ya_ref[...], wa_ref[...], preferred_element_type=F32)
    o_ref[...] = (jax.nn.sigmoid(ga_ref[...].astype(F32)) * a
                  + jax.nn.sigmoid(gb_ref[...].astype(F32)) * b).astype(o_ref.dtype)


def _merge(y_ssm, y_att, w_ps, w_pa, proj, tm=512, tn=2048):
    t, k = y_ssm.shape
    n = w_ps.shape[1]
    ga_off = (SSM_WIDTH + 3 * ATTN_WIDTH) // tn
    gb_off = ga_off + D_MODEL // tn
    return pl.pallas_call(
        _merge_kernel,
        out_shape=jax.ShapeDtypeStruct((t, n), BF16),
        grid=(t // tm, n // tn),
        in_specs=[pl.BlockSpec((tm, k), lambda i, j: (i, 0)),
                  pl.BlockSpec((tm, k), lambda i, j: (i, 0)),
                  pl.BlockSpec((k, tn), lambda i, j: (0, j)),
                  pl.BlockSpec((k, tn), lambda i, j: (0, j)),
                  pl.BlockSpec((tm, tn), lambda i, j: (i, ga_off + j)),
                  pl.BlockSpec((tm, tn), lambda i, j: (i, gb_off + j))],
        out_specs=pl.BlockSpec((tm, tn), lambda i, j: (i, j)),
        compiler_params=_cparams(("parallel", "parallel")),
        name="merge",
    )(y_ssm, y_att, w_ps, w_pa, proj, proj)


def _resid_mm_kernel(a_ref, w_ref, x_ref, g_ref, o_ref):
    acc = jnp.dot(a_ref[...], w_ref[...], preferred_element_type=F32)
    o_ref[...] = x_ref[...] + g_ref[0] * acc


def _resid_mm(a, w, x2d, mod4, g_idx, seq_len, tm=512, tn=2048):
    t, k = a.shape
    n = w.shape[1]
    per_b = seq_len // tm
    return pl.pallas_call(
        _resid_mm_kernel,
        out_shape=jax.ShapeDtypeStruct((t, n), F32),
        grid=(t // tm, n // tn),
        in_specs=[pl.BlockSpec((tm, k), lambda i, j: (i, 0)),
                  pl.BlockSpec((k, tn), lambda i, j: (0, j)),
                  pl.BlockSpec((tm, tn), lambda i, j: (i, j)),
                  pl.BlockSpec((1, 1, tn), lambda i, j: ((i // per_b) * N_MOD + g_idx, 0, j))],
        out_specs=pl.BlockSpec((tm, tn), lambda i, j: (i, j)),
        compiler_params=_cparams(("parallel", "parallel")),
        name="resid_mm",
    )(a, w, x2d, mod4)


def _ff2_kernel(h_ref, w_ref, x_ref, g_ref, gf_ref, o_ref, *, rows):
    k = pl.program_id(1)

    @pl.when(k == 0)
    def _():
        o_ref[...] = jnp.dot(h_ref[...], w_ref[...], preferred_element_type=F32)

    @pl.when(k != 0)
    def _():
        o_ref[...] += jnp.dot(h_ref[...], w_ref[...], preferred_element_type=F32)

    @pl.when(k == pl.num_programs(1) - 1)
    def _():
        for r in range(0, o_ref.shape[0], rows):
            xo = x_ref[r:r + rows, :] + g_ref[0] * o_ref[r:r + rows, :]
            ms = jnp.mean(xo * xo, axis=-1, keepdims=True)
            o_ref[r:r + rows, :] = xo * lax.rsqrt(ms + EPS) * gf_ref[...]


def _ff2(hid, w, x2d, mod4, g_idx, gf, seq_len, tm=1024, tk=1024):
    t, kdim = hid.shape
    n = w.shape[1]
    per_b = seq_len // tm
    return pl.pallas_call(
        functools.partial(_ff2_kernel, rows=256),
        out_shape=jax.ShapeDtypeStruct((t, n), F32),
        grid=(t // tm, kdim // tk),
        in_specs=[pl.BlockSpec((tm, tk), lambda i, k: (i, k)),
                  pl.BlockSpec((tk, n), lambda i, k: (k, 0)),
                  pl.BlockSpec((tm, n), lambda i, k: (i, 0)),
                  pl.BlockSpec((1, 1, n), lambda i, k: ((i // per_b) * N_MOD + g_idx, 0, 0)),
                  pl.BlockSpec((1, n), lambda i, k: (0, 0))],
        out_specs=pl.BlockSpec((tm, n), lambda i, k: (i, 0)),
        compiler_params=_cparams(("parallel", "arbitrary")),
        name="ff2_final",
    )(hid, w, x2d, mod4, gf.reshape(1, n))


def kernel(x, c, rel_bias, w_ada, b_ada, norm_mix_g, w_in, ssm_a_re, ssm_a_im, ssm_log_dt,
           ssm_b_re, ssm_b_im, ssm_c_re, ssm_c_im, ssm_d, w_glu, b_glu, w_proj_ssm,
           w_proj_attn, w_out, norm_mlp_g, w_ff1, w_ff2, norm_final_g):
    bsz, seq_len, d = x.shape
    depth = w_in.shape[0]
    assert depth == 1, "the final rms_norm is fused into the single layer's ff2 kernel"
    t = bsz * seq_len
    x2d = x.reshape(t, d)
    bias_tiles = _bias_tiles(rel_bias)

    for l in range(depth):
        mod4 = _mod(c, w_ada[l], b_ada[l]).reshape(bsz * N_MOD, 1, d)

        assert SSM_WIDTH == 1024
        proj, u = _norm_mm(x2d, norm_mix_g[l], mod4, 1, 0, w_in[l].astype(BF16), seq_len,
                           act="none", first_tile_f32=True)

        at, bw, cw, kw = _s5prep(ssm_a_re[l], ssm_a_im[l], ssm_log_dt[l],
                                 ssm_b_re[l], ssm_b_im[l], ssm_c_re[l], ssm_c_im[l])
        bmat, cmat, kmat, atv, d2 = _s5_weights(at, bw, cw, kw, ssm_d[l], bsz)
        y = _s5(u.reshape(bsz, seq_len, SSM_WIDTH), bmat, cmat, kmat, atv, d2)
        y_ssm = _glu(y.reshape(t, SSM_WIDTH), w_glu[l].astype(BF16), b_glu[l])

        y_att = _moba(proj, bias_tiles, bsz, seq_len)

        merged = _merge(y_ssm, y_att, w_proj_ssm[l].astype(BF16), w_proj_attn[l].astype(BF16), proj)
        x2d = _resid_mm(merged, w_out[l].astype(BF16), x2d, mod4, 2, seq_len)

        hid = _norm_mm(x2d, norm_mlp_g[l], mod4, 4, 3, w_ff1[l].astype(BF16), seq_len, act="relu2")
        x2d = _ff2(hid, w_ff2[l].astype(BF16), x2d, mod4, 5, norm_final_g, seq_len)
    return x2d.reshape(bsz, seq_len, d)
```

```python
import functools
import itertools
import math

import jax
import jax.numpy as jnp
from jax import lax
from jax.experimental import pallas as pl
from jax.experimental.pallas import tpu as pltpu

F32 = jnp.float32
BF16 = jnp.bfloat16

D_MODEL = 2048
SSM_WIDTH = 1024
SSM_GROUP = 16
SSM_GROUPS = 64
SSM_STATE = 64
ATTN_HEADS = 8
HEAD_DIM = 128
ATTN_WIDTH = 1024
MOBA_BLOCK = 256
MOBA_TOPK = 3
REL_BUCKETS = 32
REL_MAX_DIST = 128
D_FF = 4 * D_MODEL
N_MOD = 6
EPS = 1e-6
NEG_INF = -1e30
LOG2E = math.log2(math.e)
IN_WIDTH = SSM_WIDTH + 3 * ATTN_WIDTH + 2 * D_MODEL

VMEM_LIMIT_BYTES = 56 * 1024 * 1024

S5_TAU = 2
S5_GROUPS = 8
S5_CH = S5_GROUPS * SSM_GROUP
S5_UC = S5_TAU * S5_CH
S5_SC = 2 * S5_GROUPS * SSM_STATE
S5_NBLK = SSM_GROUPS // S5_GROUPS
S5_ROWS = 1024


def _cparams(sem):
    return pltpu.CompilerParams(dimension_semantics=sem,
                                vmem_limit_bytes=VMEM_LIMIT_BYTES)


def _mod_kernel(c_ref, w_ref, b_ref, o_ref):
    c = c_ref[...]
    ca = (c * jax.nn.sigmoid(c)).astype(BF16)
    o_ref[...] = jnp.dot(ca, w_ref[...].astype(BF16),
                         preferred_element_type=F32) + b_ref[...]


def _mod(c, w_ada, b_ada):
    bsz, d = c.shape
    n = w_ada.shape[1]
    tn = 1024
    return pl.pallas_call(
        _mod_kernel,
        out_shape=jax.ShapeDtypeStruct((bsz, n), F32),
        grid=(n // tn,),
        in_specs=[pl.BlockSpec((bsz, d), lambda j: (0, 0)),
                  pl.BlockSpec((d, tn), lambda j: (0, j)),
                  pl.BlockSpec((1, tn), lambda j: (0, j))],
        out_specs=pl.BlockSpec((bsz, tn), lambda j: (0, j)),
        compiler_params=_cparams(("parallel",)),
        name="mod",
    )(c, w_ada, b_ada.reshape(1, n))


def _norm_mm_kernel(*refs, act, rows, first_tile_f32, n_side):
    x_ref, g_ref, sc_ref, sh_ref, w_ref = refs[:5]
    side_in = refs[5:5 + n_side]
    outs = refs[5 + n_side:-1]
    h_ref = refs[-1]
    o_ref = outs[0]
    first_f32_ref = outs[1] if first_tile_f32 else None
    side_out = outs[len(outs) - n_side:]

    def mm(h):
        acc = jnp.dot(h, w_ref[...], preferred_element_type=F32)
        if act == "relu2":
            acc = jnp.square(jnp.maximum(acc, 0.0))
        return acc

    @pl.when(pl.program_id(1) == 0)
    def _():
        g = g_ref[...]
        sc = 1.0 + sc_ref[0]
        sh = sh_ref[0]
        tm = x_ref.shape[0]
        for r in range(0, tm, rows):
            x = x_ref[r:r + rows, :]
            ms = jnp.mean(x * x, axis=-1, keepdims=True)
            y = x * lax.rsqrt(ms + EPS) * g
            h = (y * sc + sh).astype(BF16)
            h_ref[r:r + rows, :] = h
            acc = mm(h)
            o_ref[r:r + rows, :] = acc.astype(o_ref.dtype)
            if first_f32_ref is not None:
                first_f32_ref[r:r + rows, :] = acc

    @pl.when(pl.program_id(1) != 0)
    def _():
        o_ref[...] = mm(h_ref[...]).astype(o_ref.dtype)

    for src, dst in zip(side_in, side_out):
        dst[...] = src[...].astype(dst.dtype)


def _norm_mm(x2d, g, mod4, sc_idx, sh_idx, w, seq_len, *, act, first_tile_f32=False,
             side_casts=(), tm=1024, tn=1024):
    t, d = x2d.shape
    n = w.shape[1]
    per_b = seq_len // tm
    nj = n // tn
    steps = (t // tm) * nj
    out_shape = [jax.ShapeDtypeStruct((t, n), BF16)]
    out_specs = [pl.BlockSpec((tm, tn), lambda i, j: (i, j))]
    if first_tile_f32:
        out_shape.append(jax.ShapeDtypeStruct((t, tn), F32))
        out_specs.append(pl.BlockSpec((tm, tn), lambda i, j: (i, 0)))
    side_specs = []
    for a in side_casts:
        r, c = a.shape
        rb = max(16, r // steps)
        nblk = r // rb
        assert r % rb == 0 and steps % nblk == 0
        spec = pl.BlockSpec((rb, c), lambda i, j, nblk=nblk: (((i * nj + j) * nblk) // steps, 0))
        side_specs.append(spec)
        out_shape.append(jax.ShapeDtypeStruct((r, c), BF16))
        out_specs.append(spec)
    res = pl.pallas_call(
        functools.partial(_norm_mm_kernel, act=act, rows=256, first_tile_f32=first_tile_f32,
                          n_side=len(side_casts)),
        out_shape=tuple(out_shape),
        grid=(t // tm, nj),
        in_specs=[pl.BlockSpec((tm, d), lambda i, j: (i, 0)),
                  pl.BlockSpec((1, d), lambda i, j: (0, 0)),
                  pl.BlockSpec((1, 1, d), lambda i, j: ((i // per_b) * N_MOD + sc_idx, 0, 0)),
                  pl.BlockSpec((1, 1, d), lambda i, j: ((i // per_b) * N_MOD + sh_idx, 0, 0)),
                  pl.BlockSpec((d, tn), lambda i, j: (0, j))] + side_specs,
        out_specs=tuple(out_specs),
        scratch_shapes=[pltpu.VMEM((tm, d), BF16)],
        compiler_params=_cparams(("arbitrary", "arbitrary")),
        name="norm_mm_" + act,
    )(x2d, g.reshape(1, d), mod4, mod4, w, *side_casts)
    return res if len(res) > 1 else res[0]


def _s5prep_kernel(are_ref, aim_ref, ldt_ref, bre_ref, bim_ref, cre_ref, cim_ref,
                   at_ref, bw_ref, cw_ref, kw_ref):
    tau = bw_ref.shape[0]
    a_re = are_ref[...]
    a_im = aim_ref[...]
    dt = jnp.exp(ldt_ref[...])
    mag = jnp.exp(dt * a_re)
    abar_re = mag * jnp.cos(dt * a_im)
    abar_im = mag * jnp.sin(dt * a_im)
    den = a_re * a_re + a_im * a_im
    p_re = abar_re - 1.0
    f_re = (p_re * a_re + abar_im * a_im) / den
    f_im = (abar_im * a_re - p_re * a_im) / den

    def cmul(x, y):
        return x[0] * y[0] - x[1] * y[1], x[0] * y[1] + x[1] * y[0]

    def cat(x):
        return jnp.concatenate([x[0], x[1]], axis=-1)

    def re_dot(x):
        return jnp.concatenate([x[0], -x[1]], axis=-1)

    abar = (abar_re, abar_im)
    bb = cmul((f_re, f_im), (bre_ref[...], bim_ref[...]))
    cc = (cre_ref[...], cim_ref[...])
    apow = [None, abar]
    for _ in range(tau - 1):
        apow.append(cmul(apow[-1], abar))
    at_ref[...] = cat(apow[tau])
    y0 = cat(bb)
    for i in range(tau):
        j = tau - 1 - i
        bw_ref[i] = y0 if j == 0 else cat(cmul(apow[j], bb))
        cw_ref[i] = re_dot(cmul(cc, apow[i + 1]))
        x = re_dot(cc if i == 0 else cmul(cc, apow[i]))
        kw_ref[i] = jnp.einsum("gpn,gqn->gpq", x, y0, precision=lax.Precision.HIGHEST,
                               preferred_element_type=F32)


def _s5prep(a_re, a_im, log_dt, b_re, b_im, c_re, c_im):
    g, n = a_re.shape
    p = b_re.shape[2]
    b_re_t = jnp.transpose(b_re, (0, 2, 1))
    b_im_t = jnp.transpose(b_im, (0, 2, 1))
    return pl.pallas_call(
        _s5prep_kernel,
        out_shape=(jax.ShapeDtypeStruct((g, 1, 2 * n), F32),
                   jax.ShapeDtypeStruct((S5_TAU, g, p, 2 * n), F32),
                   jax.ShapeDtypeStruct((S5_TAU, g, p, 2 * n), F32),
                   jax.ShapeDtypeStruct((S5_TAU, g, p, p), F32)),
        name="s5prep",
    )(a_re.reshape(g, 1, n), a_im.reshape(g, 1, n), log_dt.reshape(g, 1, 1),
      b_re_t, b_im_t, c_re, c_im)


def _block_diag(x):
    nb, ng, r, c = x.shape
    eye = jnp.eye(ng, dtype=x.dtype)
    return jnp.einsum("agrc,gh->agrhc", x, eye).reshape(nb, ng * r, ng * c)


def _s5_weights(at, bw, cw, kw, d_skip, bsz):
    n = SSM_STATE
    blk = (S5_NBLK, S5_GROUPS)

    def bd(x):
        return _block_diag(x.reshape(blk + x.shape[1:]))

    def bd_t(x):
        return bd(jnp.transpose(x, (0, 2, 1)))

    bmat = jnp.concatenate(
        [jnp.concatenate([bd(bw[i, :, :, :n]), bd(bw[i, :, :, n:])], axis=-1)
         for i in range(S5_TAU)], axis=1).astype(BF16)
    cmat = jnp.concatenate(
        [jnp.concatenate([bd_t(cw[i, :, :, :n]), bd_t(cw[i, :, :, n:])], axis=1)
         for i in range(S5_TAU)], axis=-1).astype(BF16)
    lag = [bd_t(kw[i]) for i in range(S5_TAU)]
    zero = jnp.zeros_like(lag[0])
    kmat = jnp.concatenate(
        [jnp.concatenate([lag[i - ip] if ip <= i else zero for i in range(S5_TAU)], axis=-1)
         for ip in range(S5_TAU)], axis=1).astype(BF16)
    half = S5_SC // 2
    atv = jnp.concatenate([at[:, 0, :n].reshape(S5_NBLK, half),
                           at[:, 0, n:].reshape(S5_NBLK, half)], axis=-1)
    atv = jnp.broadcast_to(atv.reshape(S5_NBLK, 1, S5_SC), (S5_NBLK, bsz, S5_SC))
    d = jnp.broadcast_to(d_skip.reshape(S5_NBLK, 1, S5_CH), (S5_NBLK, S5_TAU, S5_CH))
    return bmat, cmat, kmat, atv, d.reshape(S5_NBLK, 1, S5_UC)


def _s5_kernel(x_ref, bw_ref, cw_ref, kw_ref, a_ref, d_ref, y_ref, u_ref, w_ref, s_ref, st_ref,
               *, bsz, n_sub):
    @pl.when(pl.program_id(1) == 0)
    def _():
        st_ref[...] = jnp.zeros_like(st_ref)

    half = S5_SC // 2
    a_re = a_ref[0, :, :half]
    a_im = a_ref[0, :, half:]
    s_re = st_ref[:, :half]
    s_im = st_ref[:, half:]
    d2 = d_ref[0]
    sub = u_ref.shape[0] // n_sub
    ksub = sub // bsz

    def slot(ref, c, i):
        return ref.at[:, pl.ds(c * ksub * S5_TAU + i, ksub, stride=S5_TAU), :]

    def project_in(c):
        for i in range(S5_TAU):
            x = pltpu.einshape("bkc->kbc", slot(x_ref, c, i)[...]).reshape(sub, S5_CH)
            u_ref[c * sub:(c + 1) * sub, i * S5_CH:(i + 1) * S5_CH] = x.astype(BF16)
        w_ref[c * sub:(c + 1) * sub, :] = jnp.dot(u_ref[c * sub:(c + 1) * sub, :], bw_ref[0],
                                                  preferred_element_type=F32)

    project_in(0)
    for c in range(n_sub):
        r0 = c * sub
        if c + 1 < n_sub:
            project_in(c + 1)
        u = u_ref[r0:r0 + sub, :]
        for k in range(0, ksub, 2):
            before = []
            for kk in range(2):
                r = r0 + (k + kk) * bsz
                before.append(jnp.concatenate([s_re, s_im], axis=1))
                w = w_ref[r:r + bsz, :]
                s_re, s_im = (a_re * s_re - a_im * s_im + w[:, :half],
                              a_re * s_im + a_im * s_re + w[:, half:])
            s_ref[r0 + k * bsz:r0 + (k + 2) * bsz, :] = (
                jnp.concatenate(before, axis=0).astype(BF16))
        y = (jnp.dot(s_ref[r0:r0 + sub, :], cw_ref[0], preferred_element_type=F32)
             + jnp.dot(u, kw_ref[0], preferred_element_type=F32)
             + d2 * u.astype(F32))
        y = jax.nn.gelu(y)
        for i in range(S5_TAU):
            slot(y_ref, c, i)[...] = pltpu.einshape(
                "kbc->bkc", y[:, i * S5_CH:(i + 1) * S5_CH].reshape(ksub, bsz, S5_CH))
    st_ref[:, :half] = s_re
    st_ref[:, half:] = s_im


def _s5(u, bmat, cmat, kmat, atv, d2):
    assert S5_CH == 128, "a block's channels must be one lane tile"
    bsz, seq_len, width = u.shape
    tpc = S5_ROWS // bsz * S5_TAU
    return pl.pallas_call(
        functools.partial(_s5_kernel, bsz=bsz, n_sub=4),
        out_shape=jax.ShapeDtypeStruct(u.shape, F32),
        grid=(S5_NBLK, seq_len // tpc),
        in_specs=[
            pl.BlockSpec((bsz, tpc, S5_CH), lambda g, c: (0, c, g)),
            pl.BlockSpec((1, S5_UC, S5_SC), lambda g, c: (g, 0, 0)),
            pl.BlockSpec((1, S5_SC, S5_UC), lambda g, c: (g, 0, 0)),
            pl.BlockSpec((1, S5_UC, S5_UC), lambda g, c: (g, 0, 0)),
            pl.BlockSpec((1, bsz, S5_SC), lambda g, c: (g, 0, 0)),
            pl.BlockSpec((1, 1, S5_UC), lambda g, c: (g, 0, 0))],
        out_specs=pl.BlockSpec((bsz, tpc, S5_CH), lambda g, c: (0, c, g)),
        scratch_shapes=[pltpu.VMEM((S5_ROWS, S5_UC), BF16),
                        pltpu.VMEM((S5_ROWS, S5_SC), F32),
                        pltpu.VMEM((S5_ROWS, S5_SC), BF16),
                        pltpu.VMEM((bsz, S5_SC), F32)],
        compiler_params=_cparams(("parallel", "arbitrary")),
        name="s5",
    )(u, bmat, cmat, kmat, atv, d2)


def _glu_kernel(y_ref, w_ref, b_ref, o_ref):
    y = y_ref[...]
    z = jnp.dot(y.astype(BF16), w_ref[...], preferred_element_type=F32) + b_ref[...]
    o_ref[...] = (y * jax.nn.sigmoid(z)).astype(o_ref.dtype)


def _glu(y, w, b, tm=1024):
    t, n = y.shape
    return pl.pallas_call(
        _glu_kernel,
        out_shape=jax.ShapeDtypeStruct((t, n), BF16),
        grid=(t // tm,),
        in_specs=[pl.BlockSpec((tm, n), lambda i: (i, 0)),
                  pl.BlockSpec((n, n), lambda i: (0, 0)),
                  pl.BlockSpec((1, n), lambda i: (0, 0))],
        out_specs=pl.BlockSpec((tm, n), lambda i: (i, 0)),
        compiler_params=_cparams(("parallel",)),
        name="glu",
    )(y, w, b.reshape(1, n))


def _t5_bucket(rel):
    n = jnp.maximum(rel, 0)
    max_exact = REL_BUCKETS // 2
    nf = jnp.maximum(n, 1).astype(F32)
    large = max_exact + (jnp.log(nf / max_exact) / math.log(REL_MAX_DIST / max_exact)
                         * (REL_BUCKETS - max_exact)).astype(jnp.int32)
    large = jnp.minimum(large, REL_BUCKETS - 1)
    return jnp.where(n < max_exact, n, large)


def _bias_kernel(relb_ref, bucket_ref, o_ref):
    h = pl.program_id(0)
    bucket = bucket_ref[0]
    out = jnp.zeros(bucket.shape, F32)
    for b in range(REL_BUCKETS):
        out = jnp.where(bucket == b, relb_ref[b * ATTN_HEADS + h], out)
    o_ref[0, 0] = out * LOG2E


def _bias_tiles(rel_bias):
    i = jnp.arange(MOBA_BLOCK)
    rel = (jnp.arange(3)[:, None, None] * MOBA_BLOCK + i[None, None, :] - i[None, :, None])
    bucket = _t5_bucket(rel)
    return pl.pallas_call(
        _bias_kernel,
        out_shape=jax.ShapeDtypeStruct((ATTN_HEADS, 3, MOBA_BLOCK, MOBA_BLOCK), F32),
        grid=(ATTN_HEADS, 3),
        in_specs=[pl.BlockSpec(memory_space=pltpu.SMEM),
                  pl.BlockSpec((1, MOBA_BLOCK, MOBA_BLOCK), lambda h, d: (d, 0, 0))],
        out_specs=pl.BlockSpec((1, 1, MOBA_BLOCK, MOBA_BLOCK), lambda h, d: (h, d, 0, 0)),
        compiler_params=_cparams(("parallel", "parallel")),
        name="bias_tiles",
    )(rel_bias.astype(F32).reshape(-1), bucket)


_NT = (((1,), (1,)), ((), ()))


def _moba_scores(own, slot0, state, q_ref, k_ref, bias_ref, km_ref, s_ref):
    bs = MOBA_BLOCK
    q = q_ref[own * bs:(own + 1) * bs, :]

    ranks = []
    if own > 0:
        gate = lax.dot_general(km_ref[...].astype(BF16), q, _NT, preferred_element_type=F32)
        blk = lax.broadcasted_iota(jnp.int32, gate.shape, 0)
        gm = jnp.where(blk < own, gate, NEG_INF)
        for n in range(own):
            g_n = gm[n:n + 1, :]
            ge = jnp.where(gm >= g_n, 1.0, 0.0)
            gt = jnp.where(gm > g_n, 1.0, 0.0)
            ranks.append(jnp.sum(jnp.where(blk < n, ge, gt), axis=0, keepdims=True))

    c1 = HEAD_DIM ** -0.5 * LOG2E
    m = None
    for idx, j in enumerate([own] + list(range(own))):
        k_j = k_ref[j * bs:(j + 1) * bs, :]
        raw = lax.dot_general(k_j, q, _NT, preferred_element_type=F32)
        if j == own:
            key = lax.broadcasted_iota(jnp.int32, (bs, bs), 0)
            qry = lax.broadcasted_iota(jnp.int32, (bs, bs), 1)
            s = jnp.where(key <= qry, raw * c1 + bias_ref[0, 0], NEG_INF)
        elif own - j == 1:
            s = jnp.where(ranks[j] < float(MOBA_TOPK), raw * c1 + bias_ref[0, 1], NEG_INF)
        else:
            row = jnp.where(ranks[j] < float(MOBA_TOPK), bias_ref[0, 2, 0:1, :], NEG_INF)
            s = raw * c1 + row
        s_ref[slot0 + idx] = s
        m_j = jnp.max(s, axis=0, keepdims=True)
        m = m_j if m is None else jnp.maximum(m, m_j)
        state["m"] = m
        yield


def _moba_values(own, slot0, state, vt_ref, o_ref, s_ref):
    bs = MOBA_BLOCK
    m = state["m"]
    acc = None
    for idx, j in enumerate([own] + list(range(own))):
        p = jnp.exp2((s_ref[slot0 + idx] - m).astype(BF16))
        a_j = jnp.dot(vt_ref[:, j * bs:(j + 1) * bs], p, preferred_element_type=F32)
        acc = a_j if acc is None else acc + a_j
        if idx == own:
            o = acc[:HEAD_DIM, :] / acc[HEAD_DIM:HEAD_DIM + 1, :]
            o_ref[own * bs:(own + 1) * bs, :] = o.T.astype(o_ref.dtype)
        yield


def _moba_kernel(q_ref, k_ref, v_ref, bias_ref, o_ref, km_ref, vt_ref, s_ref, *, n_blk):
    bs = MOBA_BLOCK
    vt_ref[HEAD_DIM:, :] = jnp.ones((vt_ref.shape[0] - HEAD_DIM, vt_ref.shape[1]), BF16)
    for n in range(n_blk):
        kb = k_ref[n * bs:(n + 1) * bs, :].astype(F32)
        km_ref[n:n + 1, :] = jnp.mean(kb, axis=0, keepdims=True)
        vt_ref[:HEAD_DIM, n * bs:(n + 1) * bs] = (
            v_ref[n * bs:(n + 1) * bs, :].astype(F32).T.astype(BF16))

    slots = [own * (own + 1) // 2 for own in range(n_blk)]
    states = [dict() for _ in range(n_blk)]
    pending = iter(())
    for own in range(n_blk):
        scores = _moba_scores(own, slots[own], states[own], q_ref, k_ref, bias_ref, km_ref, s_ref)
        for _ in itertools.zip_longest(scores, pending):
            pass
        pending = _moba_values(own, slots[own], states[own], vt_ref, o_ref, s_ref)
    for _ in pending:
        pass


def _moba(proj, bias_tiles, bsz, seq_len):
    n_blk = seq_len // MOBA_BLOCK
    q_off = SSM_WIDTH // HEAD_DIM
    k_off = q_off + ATTN_HEADS
    v_off = k_off + ATTN_HEADS
    n_slots = n_blk * (n_blk + 1) // 2
    return pl.pallas_call(
        functools.partial(_moba_kernel, n_blk=n_blk),
        out_shape=jax.ShapeDtypeStruct((bsz * seq_len, ATTN_WIDTH), BF16),
        grid=(bsz, ATTN_HEADS),
        in_specs=[pl.BlockSpec((seq_len, HEAD_DIM), lambda b, h: (b, q_off + h)),
                  pl.BlockSpec((seq_len, HEAD_DIM), lambda b, h: (b, k_off + h)),
                  pl.BlockSpec((seq_len, HEAD_DIM), lambda b, h: (b, v_off + h)),
                  pl.BlockSpec((1, 3, MOBA_BLOCK, MOBA_BLOCK), lambda b, h: (h, 0, 0, 0))],
        out_specs=pl.BlockSpec((seq_len, HEAD_DIM), lambda b, h: (b, h)),
        scratch_shapes=[pltpu.VMEM((n_blk, HEAD_DIM), F32),
                        pltpu.VMEM((HEAD_DIM + 16, seq_len), BF16),
                        pltpu.VMEM((n_slots, MOBA_BLOCK, MOBA_BLOCK), F32)],
        compiler_params=_cparams(("parallel", "parallel")),
        name="moba",
    )(proj, proj, proj, bias_tiles)


def _merge_kernel(ys_ref, ya_ref, ws_ref, wa_ref, ga_ref, gb_ref, o_ref):
    a = jnp.dot(ys_ref[...], ws_ref[...], preferred_element_type=F32)
    b = jnp.dot(ya_ref[...], wa_ref[...], preferred_element_type=F32)
    o_ref[...] = (jax.nn.sigmoid(ga_ref[...].astype(F32)) * a
                  + jax.nn.sigmoid(gb_ref[...].astype(F32)) * b).astype(o_ref.dtype)


def _merge(y_ssm, y_att, w_ps, w_pa, proj, tm=512, tn=2048):
    t, k = y_ssm.shape
    n = w_ps.shape[1]
    ga_off = (SSM_WIDTH + 3 * ATTN_WIDTH) // tn
    gb_off = ga_off + D_MODEL // tn
    return pl.pallas_call(
        _merge_kernel,
        out_shape=jax.ShapeDtypeStruct((t, n), BF16),
        grid=(t // tm, n // tn),
        in_specs=[pl.BlockSpec((tm, k), lambda i, j: (i, 0)),
                  pl.BlockSpec((tm, k), lambda i, j: (i, 0)),
                  pl.BlockSpec((k, tn), lambda i, j: (0, j)),
                  pl.BlockSpec((k, tn), lambda i, j: (0, j)),
                  pl.BlockSpec((tm, tn), lambda i, j: (i, ga_off + j)),
                  pl.BlockSpec((tm, tn), lambda i, j: (i, gb_off + j))],
        out_specs=pl.BlockSpec((tm, tn), lambda i, j: (i, j)),
        compiler_params=_cparams(("parallel", "parallel")),
        name="merge",
    )(y_ssm, y_att, w_ps, w_pa, proj, proj)


def _resid_mm_kernel(a_ref, w_ref, x_ref, g_ref, o_ref):
    acc = jnp.dot(a_ref[...], w_ref[...], preferred_element_type=F32)
    o_ref[...] = x_ref[...] + g_ref[0] * acc


def _resid_mm(a, w, x2d, mod4, g_idx, seq_len, tm=512, tn=2048):
    t, k = a.shape
    n = w.shape[1]
    per_b = seq_len // tm
    return pl.pallas_call(
        _resid_mm_kernel,
        out_shape=jax.ShapeDtypeStruct((t, n), F32),
        grid=(t // tm, n // tn),
        in_specs=[pl.BlockSpec((tm, k), lambda i, j: (i, 0)),
                  pl.BlockSpec((k, tn), lambda i, j: (0, j)),
                  pl.BlockSpec((tm, tn), lambda i, j: (i, j)),
                  pl.BlockSpec((1, 1, tn), lambda i, j: ((i // per_b) * N_MOD + g_idx, 0, j))],
        out_specs=pl.BlockSpec((tm, tn), lambda i, j: (i, j)),
        compiler_params=_cparams(("parallel", "parallel")),
        name="resid_mm",
    )(a, w, x2d, mod4)


def _ff2_kernel(h_ref, w_ref, x_ref, g_ref, gf_ref, o_ref, *, rows):
    k = pl.program_id(1)
    last = pl.num_programs(1) - 1

    @pl.when(k == 0)
    def _():
        o_ref[...] = jnp.dot(h_ref[...], w_ref[...], preferred_element_type=F32)

    @pl.when((k != 0) & (k != last))
    def _():
        o_ref[...] += jnp.dot(h_ref[...], w_ref[...], preferred_element_type=F32)

    @pl.when(k == last)
    def _():
        for r in range(0, o_ref.shape[0], rows):
            acc = o_ref[r:r + rows, :] + jnp.dot(h_ref[r:r + rows, :], w_ref[...],
                                                 preferred_element_type=F32)
            xo = x_ref[r:r + rows, :] + g_ref[0] * acc
            ms = jnp.mean(xo * xo, axis=-1, keepdims=True)
            o_ref[r:r + rows, :] = xo * lax.rsqrt(ms + EPS) * gf_ref[...]


def _ff2(hid, w, x2d, mod4, g_idx, gf, seq_len, tm=1024, tk=1024):
    t, kdim = hid.shape
    n = w.shape[1]
    per_b = seq_len // tm
    return pl.pallas_call(
        functools.partial(_ff2_kernel, rows=256),
        out_shape=jax.ShapeDtypeStruct((t, n), F32),
        grid=(t // tm, kdim // tk),
        in_specs=[pl.BlockSpec((tm, tk), lambda i, k: (i, k)),
                  pl.BlockSpec((tk, n), lambda i, k: (k, 0)),
                  pl.BlockSpec((tm, n), lambda i, k: (i, 0)),
                  pl.BlockSpec((1, 1, n), lambda i, k: ((i // per_b) * N_MOD + g_idx, 0, 0)),
                  pl.BlockSpec((1, n), lambda i, k: (0, 0))],
        out_specs=pl.BlockSpec((tm, n), lambda i, k: (i, 0)),
        compiler_params=_cparams(("parallel", "arbitrary")),
        name="ff2_final",
    )(hid, w, x2d, mod4, gf.reshape(1, n))


def kernel(x, c, rel_bias, w_ada, b_ada, norm_mix_g, w_in, ssm_a_re, ssm_a_im, ssm_log_dt,
           ssm_b_re, ssm_b_im, ssm_c_re, ssm_c_im, ssm_d, w_glu, b_glu, w_proj_ssm,
           w_proj_attn, w_out, norm_mlp_g, w_ff1, w_ff2, norm_final_g):
    bsz, seq_len, d = x.shape
    depth = w_in.shape[0]
    assert depth == 1, "the final rms_norm is fused into the single layer's ff2 kernel"
    t = bsz * seq_len
    x2d = x.reshape(t, d)
    bias_tiles = _bias_tiles(rel_bias)

    for l in range(depth):
        mod4 = _mod(c, w_ada[l], b_ada[l]).reshape(bsz * N_MOD, 1, d)

        assert SSM_WIDTH == 1024
        casts = (w_glu[l], w_proj_ssm[l], w_proj_attn[l], w_out[l], w_ff1[l], w_ff2[l])
        proj, u, w_glu_b, w_ps_b, w_pa_b, w_out_b, w_ff1_b, w_ff2_b = _norm_mm(
            x2d, norm_mix_g[l], mod4, 1, 0, w_in[l].astype(BF16), seq_len,
            act="none", first_tile_f32=True, side_casts=casts)

        at, bw, cw, kw = _s5prep(ssm_a_re[l], ssm_a_im[l], ssm_log_dt[l],
                                 ssm_b_re[l], ssm_b_im[l], ssm_c_re[l], ssm_c_im[l])
        bmat, cmat, kmat, atv, d2 = _s5_weights(at, bw, cw, kw, ssm_d[l], bsz)
        y = _s5(u.reshape(bsz, seq_len, SSM_WIDTH), bmat, cmat, kmat, atv, d2)
        y_ssm = _glu(y.reshape(t, SSM_WIDTH), w_glu_b, b_glu[l])

        y_att = _moba(proj, bias_tiles, bsz, seq_len)

        merged = _merge(y_ssm, y_att, w_ps_b, w_pa_b, proj)
        x2d = _resid_mm(merged, w_out_b, x2d, mod4, 2, seq_len)

        hid = _norm_mm(x2d, norm_mlp_g[l], mod4, 4, 3, w_ff1_b, seq_len, act="relu2")
        x2d = _ff2(hid, w_ff2_b, x2d, mod4, 5, norm_final_g, seq_len)
    return x2d.reshape(bsz, seq_len, d)
```

```python
import functools
import itertools
import math

import jax
import jax.numpy as jnp
from jax import lax
from jax.experimental import pallas as pl
from jax.experimental.pallas import tpu as pltpu

F32 = jnp.float32
BF16 = jnp.bfloat16

D_MODEL = 2048
SSM_WIDTH = 1024
SSM_GROUP = 16
SSM_GROUPS = 64
SSM_STATE = 64
ATTN_HEADS = 8
HEAD_DIM = 128
ATTN_WIDTH = 1024
MOBA_BLOCK = 256
MOBA_TOPK = 3
REL_BUCKETS = 32
REL_MAX_DIST = 128
D_FF = 4 * D_MODEL
N_MOD = 6
EPS = 1e-6
NEG_INF = -1e30
LOG2E = math.log2(math.e)
IN_WIDTH = SSM_WIDTH + 3 * ATTN_WIDTH + 2 * D_MODEL

VMEM_LIMIT_BYTES = 56 * 1024 * 1024

S5_TAU = 2
S5_GROUPS = 8
S5_CH = S5_GROUPS * SSM_GROUP
S5_UC = S5_TAU * S5_CH
S5_SC = 2 * S5_GROUPS * SSM_STATE
S5_NBLK = SSM_GROUPS // S5_GROUPS
S5_ROWS = 1024


def _cparams(sem):
    return pltpu.CompilerParams(dimension_semantics=sem,
                                vmem_limit_bytes=VMEM_LIMIT_BYTES)


def _mod_kernel(c_ref, w_ref, b_ref, o_ref):
    c = c_ref[...]
    ca = (c * jax.nn.sigmoid(c)).astype(BF16)
    o_ref[...] = jnp.dot(ca, w_ref[...].astype(BF16),
                         preferred_element_type=F32) + b_ref[...]


def _mod(c, w_ada, b_ada):
    bsz, d = c.shape
    n = w_ada.shape[1]
    tn = 1024
    return pl.pallas_call(
        _mod_kernel,
        out_shape=jax.ShapeDtypeStruct((bsz, n), F32),
        grid=(n // tn,),
        in_specs=[pl.BlockSpec((bsz, d), lambda j: (0, 0)),
                  pl.BlockSpec((d, tn), lambda j: (0, j)),
                  pl.BlockSpec((1, tn), lambda j: (0, j))],
        out_specs=pl.BlockSpec((bsz, tn), lambda j: (0, j)),
        compiler_params=_cparams(("parallel",)),
        name="mod",
    )(c, w_ada, b_ada.reshape(1, n))


def _norm_mm_kernel(*refs, act, rows, first_tile_f32, n_side):
    x_ref, g_ref, sc_ref, sh_ref, w_ref = refs[:5]
    side_in = refs[5:5 + n_side]
    outs = refs[5 + n_side:-1]
    h_ref = refs[-1]
    o_ref = outs[0]
    first_f32_ref = outs[1] if first_tile_f32 else None
    side_out = outs[len(outs) - n_side:]

    def mm(h):
        acc = jnp.dot(h, w_ref[...], preferred_element_type=F32)
        if act == "relu2":
            acc = jnp.square(jnp.maximum(acc, 0.0))
        return acc

    @pl.when(pl.program_id(1) == 0)
    def _():
        g = g_ref[...]
        sc = 1.0 + sc_ref[0]
        sh = sh_ref[0]
        tm = x_ref.shape[0]
        for r in range(0, tm, rows):
            x = x_ref[r:r + rows, :]
            ms = jnp.mean(x * x, axis=-1, keepdims=True)
            y = x * lax.rsqrt(ms + EPS) * g
            h = (y * sc + sh).astype(BF16)
            h_ref[r:r + rows, :] = h
            acc = mm(h)
            o_ref[r:r + rows, :] = acc.astype(o_ref.dtype)
            if first_f32_ref is not None:
                first_f32_ref[r:r + rows, :] = acc

    @pl.when(pl.program_id(1) != 0)
    def _():
        o_ref[...] = mm(h_ref[...]).astype(o_ref.dtype)

    for src, dst in zip(side_in, side_out):
        dst[...] = src[...].astype(dst.dtype)


def _norm_mm(x2d, g, mod4, sc_idx, sh_idx, w, seq_len, *, act, first_tile_f32=False,
             side_casts=(), tm=1024, tn=1024):
    t, d = x2d.shape
    n = w.shape[1]
    per_b = seq_len // tm
    nj = n // tn
    steps = (t // tm) * nj
    out_shape = [jax.ShapeDtypeStruct((t, n), BF16)]
    out_specs = [pl.BlockSpec((tm, tn), lambda i, j: (i, j))]
    if first_tile_f32:
        out_shape.append(jax.ShapeDtypeStruct((t, tn), F32))
        out_specs.append(pl.BlockSpec((tm, tn), lambda i, j: (i, 0)))
    side_specs = []
    for a in side_casts:
        r, c = a.shape
        rb = max(16, r // steps)
        nblk = r // rb
        assert r % rb == 0 and steps % nblk == 0
        spec = pl.BlockSpec((rb, c), lambda i, j, nblk=nblk: (((i * nj + j) * nblk) // steps, 0))
        side_specs.append(spec)
        out_shape.append(jax.ShapeDtypeStruct((r, c), BF16))
        out_specs.append(spec)
    res = pl.pallas_call(
        functools.partial(_norm_mm_kernel, act=act, rows=256, first_tile_f32=first_tile_f32,
                          n_side=len(side_casts)),
        out_shape=tuple(out_shape),
        grid=(t // tm, nj),
        in_specs=[pl.BlockSpec((tm, d), lambda i, j: (i, 0)),
                  pl.BlockSpec((1, d), lambda i, j: (0, 0)),
                  pl.BlockSpec((1, 1, d), lambda i, j: ((i // per_b) * N_MOD + sc_idx, 0, 0)),
                  pl.BlockSpec((1, 1, d), lambda i, j: ((i // per_b) * N_MOD + sh_idx, 0, 0)),
                  pl.BlockSpec((d, tn), lambda i, j: (0, j))] + side_specs,
        out_specs=tuple(out_specs),
        scratch_shapes=[pltpu.VMEM((tm, d), BF16)],
        compiler_params=_cparams(("arbitrary", "arbitrary")),
        name="norm_mm_" + act,
    )(x2d, g.reshape(1, d), mod4, mod4, w, *side_casts)
    return res if len(res) > 1 else res[0]


def _s5prep_kernel(are_ref, aim_ref, ldt_ref, bre_ref, bim_ref, cre_ref, cim_ref,
                   at_ref, bw_ref, cw_ref, kw_ref):
    tau = bw_ref.shape[0]
    a_re = are_ref[...]
    a_im = aim_ref[...]
    dt = jnp.exp(ldt_ref[...])
    mag = jnp.exp(dt * a_re)
    abar_re = mag * jnp.cos(dt * a_im)
    abar_im = mag * jnp.sin(dt * a_im)
    den = a_re * a_re + a_im * a_im
    p_re = abar_re - 1.0
    f_re = (p_re * a_re + abar_im * a_im) / den
    f_im = (abar_im * a_re - p_re * a_im) / den

    def cmul(x, y):
        return x[0] * y[0] - x[1] * y[1], x[0] * y[1] + x[1] * y[0]

    def cat(x):
        return jnp.concatenate([x[0], x[1]], axis=-1)

    def re_dot(x):
        return jnp.concatenate([x[0], -x[1]], axis=-1)

    abar = (abar_re, abar_im)
    bb = cmul((f_re, f_im), (bre_ref[...], bim_ref[...]))
    cc = (cre_ref[...], cim_ref[...])
    apow = [None, abar]
    for _ in range(tau - 1):
        apow.append(cmul(apow[-1], abar))
    at_ref[...] = cat(apow[tau])
    y0 = cat(bb)
    for i in range(tau):
        j = tau - 1 - i
        bw_ref[i] = y0 if j == 0 else cat(cmul(apow[j], bb))
        cw_ref[i] = re_dot(cmul(cc, apow[i + 1]))
        x = re_dot(cc if i == 0 else cmul(cc, apow[i]))
        kw_ref[i] = jnp.einsum("gpn,gqn->gpq", x, y0, precision=lax.Precision.HIGHEST,
                               preferred_element_type=F32)


def _s5prep(a_re, a_im, log_dt, b_re, b_im, c_re, c_im):
    g, n = a_re.shape
    p = b_re.shape[2]
    b_re_t = jnp.transpose(b_re, (0, 2, 1))
    b_im_t = jnp.transpose(b_im, (0, 2, 1))
    return pl.pallas_call(
        _s5prep_kernel,
        out_shape=(jax.ShapeDtypeStruct((g, 1, 2 * n), F32),
                   jax.ShapeDtypeStruct((S5_TAU, g, p, 2 * n), F32),
                   jax.ShapeDtypeStruct((S5_TAU, g, p, 2 * n), F32),
                   jax.ShapeDtypeStruct((S5_TAU, g, p, p), F32)),
        name="s5prep",
    )(a_re.reshape(g, 1, n), a_im.reshape(g, 1, n), log_dt.reshape(g, 1, 1),
      b_re_t, b_im_t, c_re, c_im)


def _block_diag(x):
    nb, ng, r, c = x.shape
    eye = jnp.eye(ng, dtype=x.dtype)
    return jnp.einsum("agrc,gh->agrhc", x, eye).reshape(nb, ng * r, ng * c)


def _s5_weights(at, bw, cw, kw, d_skip, bsz):
    n = SSM_STATE
    blk = (S5_NBLK, S5_GROUPS)

    def bd(x):
        return _block_diag(x.reshape(blk + x.shape[1:]))

    def bd_t(x):
        return bd(jnp.transpose(x, (0, 2, 1)))

    bmat = jnp.concatenate(
        [jnp.concatenate([bd(bw[i, :, :, :n]), bd(bw[i, :, :, n:])], axis=-1)
         for i in range(S5_TAU)], axis=1).astype(BF16)
    cmat = jnp.concatenate(
        [jnp.concatenate([bd_t(cw[i, :, :, :n]), bd_t(cw[i, :, :, n:])], axis=1)
         for i in range(S5_TAU)], axis=-1).astype(BF16)
    lag = [bd_t(kw[i]) for i in range(S5_TAU)]
    zero = jnp.zeros_like(lag[0])
    kmat = jnp.concatenate(
        [jnp.concatenate([lag[i - ip] if ip <= i else zero for i in range(S5_TAU)], axis=-1)
         for ip in range(S5_TAU)], axis=1).astype(BF16)
    half = S5_SC // 2
    atv = jnp.concatenate([at[:, 0, :n].reshape(S5_NBLK, half),
                           at[:, 0, n:].reshape(S5_NBLK, half)], axis=-1)
    atv = jnp.broadcast_to(atv.reshape(S5_NBLK, 1, S5_SC), (S5_NBLK, bsz, S5_SC))
    d = jnp.broadcast_to(d_skip.reshape(S5_NBLK, 1, S5_CH), (S5_NBLK, S5_TAU, S5_CH))
    return bmat, cmat, kmat, atv, d.reshape(S5_NBLK, 1, S5_UC)


def _s5_kernel(x_ref, bw_ref, cw_ref, kw_ref, a_ref, d_ref, y_ref, u_ref, w_ref, s_ref, st_ref,
               *, bsz, n_sub):
    @pl.when(pl.program_id(1) == 0)
    def _():
        st_ref[...] = jnp.zeros_like(st_ref)

    half = S5_SC // 2
    a_re = a_ref[0, :, :half]
    a_im = a_ref[0, :, half:]
    s_re = st_ref[:, :half]
    s_im = st_ref[:, half:]
    d2 = d_ref[0]
    sub = u_ref.shape[0] // n_sub
    ksub = sub // bsz

    def slot(ref, c, i):
        return ref.at[:, pl.ds(c * ksub * S5_TAU + i, ksub, stride=S5_TAU), :]

    def project_in(c):
        for i in range(S5_TAU):
            x = pltpu.einshape("bkc->kbc", slot(x_ref, c, i)[...]).reshape(sub, S5_CH)
            u_ref[c * sub:(c + 1) * sub, i * S5_CH:(i + 1) * S5_CH] = x.astype(BF16)
        w_ref[c * sub:(c + 1) * sub, :] = jnp.dot(u_ref[c * sub:(c + 1) * sub, :], bw_ref[0],
                                                  preferred_element_type=F32)

    project_in(0)
    for c in range(n_sub):
        r0 = c * sub
        if c + 1 < n_sub:
            project_in(c + 1)
        u = u_ref[r0:r0 + sub, :]
        for k in range(0, ksub, 2):
            before = []
            for kk in range(2):
                r = r0 + (k + kk) * bsz
                before.append(jnp.concatenate([s_re, s_im], axis=1))
                w = w_ref[r:r + bsz, :]
                s_re, s_im = (a_re * s_re - a_im * s_im + w[:, :half],
                              a_re * s_im + a_im * s_re + w[:, half:])
            s_ref[r0 + k * bsz:r0 + (k + 2) * bsz, :] = (
                jnp.concatenate(before, axis=0).astype(BF16))
        y = (jnp.dot(s_ref[r0:r0 + sub, :], cw_ref[0], preferred_element_type=F32)
             + jnp.dot(u, kw_ref[0], preferred_element_type=F32)
             + d2 * u.astype(F32))
        y = jax.nn.gelu(y)
        for i in range(S5_TAU):
            slot(y_ref, c, i)[...] = pltpu.einshape(
                "kbc->bkc", y[:, i * S5_CH:(i + 1) * S5_CH].reshape(ksub, bsz, S5_CH))
    st_ref[:, :half] = s_re
    st_ref[:, half:] = s_im


def _s5(u, bmat, cmat, kmat, atv, d2):
    assert S5_CH == 128, "a block's channels must be one lane tile"
    bsz, seq_len, width = u.shape
    tpc = S5_ROWS // bsz * S5_TAU
    return pl.pallas_call(
        functools.partial(_s5_kernel, bsz=bsz, n_sub=4),
        out_shape=jax.ShapeDtypeStruct(u.shape, F32),
        grid=(S5_NBLK, seq_len // tpc),
        in_specs=[
            pl.BlockSpec((bsz, tpc, S5_CH), lambda g, c: (0, c, g)),
            pl.BlockSpec((1, S5_UC, S5_SC), lambda g, c: (g, 0, 0)),
            pl.BlockSpec((1, S5_SC, S5_UC), lambda g, c: (g, 0, 0)),
            pl.BlockSpec((1, S5_UC, S5_UC), lambda g, c: (g, 0, 0)),
            pl.BlockSpec((1, bsz, S5_SC), lambda g, c: (g, 0, 0)),
            pl.BlockSpec((1, 1, S5_UC), lambda g, c: (g, 0, 0))],
        out_specs=pl.BlockSpec((bsz, tpc, S5_CH), lambda g, c: (0, c, g)),
        scratch_shapes=[pltpu.VMEM((S5_ROWS, S5_UC), BF16),
                        pltpu.VMEM((S5_ROWS, S5_SC), F32),
                        pltpu.VMEM((S5_ROWS, S5_SC), BF16),
                        pltpu.VMEM((bsz, S5_SC), F32)],
        compiler_params=_cparams(("parallel", "arbitrary")),
        name="s5",
    )(u, bmat, cmat, kmat, atv, d2)


def _glu_kernel(y_ref, w_ref, b_ref, o_ref):
    y = y_ref[...]
    z = jnp.dot(y.astype(BF16), w_ref[...], preferred_element_type=F32) + b_ref[...]
    o_ref[...] = (y * jax.nn.sigmoid(z)).astype(o_ref.dtype)


def _glu(y, w, b, tm=1024):
    t, n = y.shape
    return pl.pallas_call(
        _glu_kernel,
        out_shape=jax.ShapeDtypeStruct((t, n), BF16),
        grid=(t // tm,),
        in_specs=[pl.BlockSpec((tm, n), lambda i: (i, 0)),
                  pl.BlockSpec((n, n), lambda i: (0, 0)),
                  pl.BlockSpec((1, n), lambda i: (0, 0))],
        out_specs=pl.BlockSpec((tm, n), lambda i: (i, 0)),
        compiler_params=_cparams(("parallel",)),
        name="glu",
    )(y, w, b.reshape(1, n))


def _t5_bucket(rel):
    n = jnp.maximum(rel, 0)
    max_exact = REL_BUCKETS // 2
    nf = jnp.maximum(n, 1).astype(F32)
    large = max_exact + (jnp.log(nf / max_exact) / math.log(REL_MAX_DIST / max_exact)
                         * (REL_BUCKETS - max_exact)).astype(jnp.int32)
    large = jnp.minimum(large, REL_BUCKETS - 1)
    return jnp.where(n < max_exact, n, large)


def _bias_kernel(relb_ref, bucket_ref, w_ref, o_ref, wb_ref):
    h = pl.program_id(0)
    bucket = bucket_ref[0]
    out = jnp.zeros(bucket.shape, F32)
    for b in range(REL_BUCKETS):
        out = jnp.where(bucket == b, relb_ref[b * ATTN_HEADS + h], out)
    o_ref[0, 0] = out * LOG2E
    wb_ref[...] = w_ref[...].astype(wb_ref.dtype)


def _bias_tiles(rel_bias, w_in):
    i = jnp.arange(MOBA_BLOCK)
    rel = (jnp.arange(3)[:, None, None] * MOBA_BLOCK + i[None, None, :] - i[None, :, None])
    bucket = _t5_bucket(rel)
    n_d = 4
    rows, cols = w_in.shape
    rb = rows // (ATTN_HEADS * n_d)
    w_spec = pl.BlockSpec((rb, cols), lambda h, d: (h * n_d + d, 0))
    return pl.pallas_call(
        _bias_kernel,
        out_shape=(jax.ShapeDtypeStruct((ATTN_HEADS, 3, MOBA_BLOCK, MOBA_BLOCK), F32),
                   jax.ShapeDtypeStruct((rows, cols), BF16)),
        grid=(ATTN_HEADS, n_d),
        in_specs=[pl.BlockSpec(memory_space=pltpu.SMEM),
                  pl.BlockSpec((1, MOBA_BLOCK, MOBA_BLOCK), lambda h, d: (jnp.minimum(d, 2), 0, 0)),
                  w_spec],
        out_specs=(pl.BlockSpec((1, 1, MOBA_BLOCK, MOBA_BLOCK),
                                lambda h, d: (h, jnp.minimum(d, 2), 0, 0)),
                   w_spec),
        compiler_params=_cparams(("arbitrary", "arbitrary")),
        name="bias_tiles",
    )(rel_bias.astype(F32).reshape(-1), bucket, w_in)


_NT = (((1,), (1,)), ((), ()))


def _moba_scores(own, slot0, state, q_ref, k_ref, bias_ref, km_ref, s_ref):
    bs = MOBA_BLOCK
    q = q_ref[own * bs:(own + 1) * bs, :]

    ranks = []
    if own > 0:
        gate = lax.dot_general(km_ref[...].astype(BF16), q, _NT, preferred_element_type=F32)
        blk = lax.broadcasted_iota(jnp.int32, gate.shape, 0)
        gm = jnp.where(blk < own, gate, NEG_INF)
        for n in range(own):
            g_n = gm[n:n + 1, :]
            ge = jnp.where(gm >= g_n, 1.0, 0.0)
            gt = jnp.where(gm > g_n, 1.0, 0.0)
            ranks.append(jnp.sum(jnp.where(blk < n, ge, gt), axis=0, keepdims=True))

    c1 = HEAD_DIM ** -0.5 * LOG2E
    m = None
    for idx, j in enumerate([own] + list(range(own))):
        k_j = k_ref[j * bs:(j + 1) * bs, :]
        raw = lax.dot_general(k_j, q, _NT, preferred_element_type=F32)
        if j == own:
            key = lax.broadcasted_iota(jnp.int32, (bs, bs), 0)
            qry = lax.broadcasted_iota(jnp.int32, (bs, bs), 1)
            s = jnp.where(key <= qry, raw * c1 + bias_ref[0, 0], NEG_INF)
        elif own - j == 1:
            s = jnp.where(ranks[j] < float(MOBA_TOPK), raw * c1 + bias_ref[0, 1], NEG_INF)
        else:
            row = jnp.where(ranks[j] < float(MOBA_TOPK), bias_ref[0, 2, 0:1, :], NEG_INF)
            s = raw * c1 + row
        s_ref[slot0 + idx] = s
        m_j = jnp.max(s, axis=0, keepdims=True)
        m = m_j if m is None else jnp.maximum(m, m_j)
        state["m"] = m
        yield


def _moba_values(own, slot0, state, vt_ref, o_ref, s_ref):
    bs = MOBA_BLOCK
    m = state["m"]
    acc = None
    for idx, j in enumerate([own] + list(range(own))):
        p = jnp.exp2((s_ref[slot0 + idx] - m).astype(BF16))
        a_j = jnp.dot(vt_ref[:, j * bs:(j + 1) * bs], p, preferred_element_type=F32)
        acc = a_j if acc is None else acc + a_j
        if idx == own:
            o = acc[:HEAD_DIM, :] / acc[HEAD_DIM:HEAD_DIM + 1, :]
            o_ref[own * bs:(own + 1) * bs, :] = o.T.astype(o_ref.dtype)
        yield


def _moba_kernel(q_ref, k_ref, v_ref, bias_ref, o_ref, km_ref, vt_ref, s_ref, *, n_blk):
    bs = MOBA_BLOCK
    vt_ref[HEAD_DIM:, :] = jnp.ones((vt_ref.shape[0] - HEAD_DIM, vt_ref.shape[1]), BF16)
    for n in range(n_blk):
        kb = k_ref[n * bs:(n + 1) * bs, :].astype(F32)
        km_ref[n:n + 1, :] = jnp.mean(kb, axis=0, keepdims=True)
        vt_ref[:HEAD_DIM, n * bs:(n + 1) * bs] = (
            v_ref[n * bs:(n + 1) * bs, :].astype(F32).T.astype(BF16))

    slots = [own * (own + 1) // 2 for own in range(n_blk)]
    states = [dict() for _ in range(n_blk)]
    pending = iter(())
    for own in range(n_blk):
        scores = _moba_scores(own, slots[own], states[own], q_ref, k_ref, bias_ref, km_ref, s_ref)
        for _ in itertools.zip_longest(scores, pending):
            pass
        pending = _moba_values(own, slots[own], states[own], vt_ref, o_ref, s_ref)
    for _ in pending:
        pass


def _moba(proj, bias_tiles, bsz, seq_len):
    n_blk = seq_len // MOBA_BLOCK
    q_off = SSM_WIDTH // HEAD_DIM
    k_off = q_off + ATTN_HEADS
    v_off = k_off + ATTN_HEADS
    n_slots = n_blk * (n_blk + 1) // 2
    return pl.pallas_call(
        functools.partial(_moba_kernel, n_blk=n_blk),
        out_shape=jax.ShapeDtypeStruct((bsz * seq_len, ATTN_WIDTH), BF16),
        grid=(bsz, ATTN_HEADS),
        in_specs=[pl.BlockSpec((seq_len, HEAD_DIM), lambda b, h: (b, q_off + h)),
                  pl.BlockSpec((seq_len, HEAD_DIM), lambda b, h: (b, k_off + h)),
                  pl.BlockSpec((seq_len, HEAD_DIM), lambda b, h: (b, v_off + h)),
                  pl.BlockSpec((1, 3, MOBA_BLOCK, MOBA_BLOCK), lambda b, h: (h, 0, 0, 0))],
        out_specs=pl.BlockSpec((seq_len, HEAD_DIM), lambda b, h: (b, h)),
        scratch_shapes=[pltpu.VMEM((n_blk, HEAD_DIM), F32),
                        pltpu.VMEM((HEAD_DIM + 16, seq_len), BF16),
                        pltpu.VMEM((n_slots, MOBA_BLOCK, MOBA_BLOCK), F32)],
        compiler_params=_cparams(("parallel", "parallel")),
        name="moba",
    )(proj, proj, proj, bias_tiles)


def _merge_kernel(ys_ref, ya_ref, ws_ref, wa_ref, ga_ref, gb_ref, o_ref):
    a = jnp.dot(ys_ref[...], ws_ref[...], preferred_element_type=F32)
    b = jnp.dot(ya_ref[...], wa_ref[...], preferred_element_type=F32)
    o_ref[...] = (jax.nn.sigmoid(ga_ref[...].astype(F32)) * a
                  + jax.nn.sigmoid(gb_ref[...].astype(F32)) * b).astype(o_ref.dtype)


def _merge(y_ssm, y_att, w_ps, w_pa, proj, tm=512, tn=2048):
    t, k = y_ssm.shape
    n = w_ps.shape[1]
    ga_off = (SSM_WIDTH + 3 * ATTN_WIDTH) // tn
    gb_off = ga_off + D_MODEL // tn
    return pl.pallas_call(
        _merge_kernel,
        out_shape=jax.ShapeDtypeStruct((t, n), BF16),
        grid=(t // tm, n // tn),
        in_specs=[pl.BlockSpec((tm, k), lambda i, j: (i, 0)),
                  pl.BlockSpec((tm, k), lambda i, j: (i, 0)),
                  pl.BlockSpec((k, tn), lambda i, j: (0, j)),
                  pl.BlockSpec((k, tn), lambda i, j: (0, j)),
                  pl.BlockSpec((tm, tn), lambda i, j: (i, ga_off + j)),
                  pl.BlockSpec((tm, tn), lambda i, j: (i, gb_off + j))],
        out_specs=pl.BlockSpec((tm, tn), lambda i, j: (i, j)),
        compiler_params=_cparams(("parallel", "parallel")),
        name="merge",
    )(y_ssm, y_att, w_ps, w_pa, proj, proj)


def _resid_mm_kernel(a_ref, w_ref, x_ref, g_ref, o_ref):
    acc = jnp.dot(a_ref[...], w_ref[...], preferred_element_type=F32)
    o_ref[...] = x_ref[...] + g_ref[0] * acc


def _resid_mm(a, w, x2d, mod4, g_idx, seq_len, tm=512, tn=2048):
    t, k = a.shape
    n = w.shape[1]
    per_b = seq_len // tm
    return pl.pallas_call(
        _resid_mm_kernel,
        out_shape=jax.ShapeDtypeStruct((t, n), F32),
        grid=(t // tm, n // tn),
        in_specs=[pl.BlockSpec((tm, k), lambda i, j: (i, 0)),
                  pl.BlockSpec((k, tn), lambda i, j: (0, j)),
                  pl.BlockSpec((tm, tn), lambda i, j: (i, j)),
                  pl.BlockSpec((1, 1, tn), lambda i, j: ((i // per_b) * N_MOD + g_idx, 0, j))],
        out_specs=pl.BlockSpec((tm, tn), lambda i, j: (i, j)),
        compiler_params=_cparams(("parallel", "parallel")),
        name="resid_mm",
    )(a, w, x2d, mod4)


def _ff2_kernel(h_ref, w_ref, x_ref, g_ref, gf_ref, o_ref, *, rows):
    k = pl.program_id(1)
    last = pl.num_programs(1) - 1

    @pl.when(k == 0)
    def _():
        o_ref[...] = jnp.dot(h_ref[...], w_ref[...], preferred_element_type=F32)

    @pl.when((k != 0) & (k != last))
    def _():
        o_ref[...] += jnp.dot(h_ref[...], w_ref[...], preferred_element_type=F32)

    @pl.when(k == last)
    def _():
        for r in range(0, o_ref.shape[0], rows):
            acc = o_ref[r:r + rows, :] + jnp.dot(h_ref[r:r + rows, :], w_ref[...],
                                                 preferred_element_type=F32)
            xo = x_ref[r:r + rows, :] + g_ref[0] * acc
            ms = jnp.mean(xo * xo, axis=-1, keepdims=True)
            o_ref[r:r + rows, :] = xo * lax.rsqrt(ms + EPS) * gf_ref[...]


def _ff2(hid, w, x2d, mod4, g_idx, gf, seq_len, tm=1024, tk=1024):
    t, kdim = hid.shape
    n = w.shape[1]
    per_b = seq_len // tm
    return pl.pallas_call(
        functools.partial(_ff2_kernel, rows=256),
        out_shape=jax.ShapeDtypeStruct((t, n), F32),
        grid=(t // tm, kdim // tk),
        in_specs=[pl.BlockSpec((tm, tk), lambda i, k: (i, k)),
                  pl.BlockSpec((tk, n), lambda i, k: (k, 0)),
                  pl.BlockSpec((tm, n), lambda i, k: (i, 0)),
                  pl.BlockSpec((1, 1, n), lambda i, k: ((i // per_b) * N_MOD + g_idx, 0, 0)),
                  pl.BlockSpec((1, n), lambda i, k: (0, 0))],
        out_specs=pl.BlockSpec((tm, n), lambda i, k: (i, 0)),
        compiler_params=_cparams(("parallel", "arbitrary")),
        name="ff2_final",
    )(hid, w, x2d, mod4, gf.reshape(1, n))


def kernel(x, c, rel_bias, w_ada, b_ada, norm_mix_g, w_in, ssm_a_re, ssm_a_im, ssm_log_dt,
           ssm_b_re, ssm_b_im, ssm_c_re, ssm_c_im, ssm_d, w_glu, b_glu, w_proj_ssm,
           w_proj_attn, w_out, norm_mlp_g, w_ff1, w_ff2, norm_final_g):
    bsz, seq_len, d = x.shape
    depth = w_in.shape[0]
    assert depth == 1, "the final rms_norm is fused into the single layer's ff2 kernel"
    t = bsz * seq_len
    x2d = x.reshape(t, d)
    bias_tiles, w_in_b = _bias_tiles(rel_bias, w_in[0])

    for l in range(depth):
        mod4 = _mod(c, w_ada[l], b_ada[l]).reshape(bsz * N_MOD, 1, d)

        assert SSM_WIDTH == 1024
        casts = (w_glu[l], w_proj_ssm[l], w_proj_attn[l], w_out[l], w_ff1[l], w_ff2[l])
        proj, u, w_glu_b, w_ps_b, w_pa_b, w_out_b, w_ff1_b, w_ff2_b = _norm_mm(
            x2d, norm_mix_g[l], mod4, 1, 0, w_in_b, seq_len,
            act="none", first_tile_f32=True, side_casts=casts)

        at, bw, cw, kw = _s5prep(ssm_a_re[l], ssm_a_im[l], ssm_log_dt[l],
                                 ssm_b_re[l], ssm_b_im[l], ssm_c_re[l], ssm_c_im[l])
        bmat, cmat, kmat, atv, d2 = _s5_weights(at, bw, cw, kw, ssm_d[l], bsz)
        y = _s5(u.reshape(bsz, seq_len, SSM_WIDTH), bmat, cmat, kmat, atv, d2)
        y_ssm = _glu(y.reshape(t, SSM_WIDTH), w_glu_b, b_glu[l])

        y_att = _moba(proj, bias_tiles, bsz, seq_len)

        merged = _merge(y_ssm, y_att, w_ps_b, w_pa_b, proj)
        x2d = _resid_mm(merged, w_out_b, x2d, mod4, 2, seq_len)

        hid = _norm_mm(x2d, norm_mlp_g[l], mod4, 4, 3, w_ff1_b, seq_len, act="relu2", tn=2048)
        x2d = _ff2(hid, w_ff2_b, x2d, mod4, 5, norm_final_g, seq_len)
    return x2d.reshape(bsz, seq_len, d)
```

```python
import functools
import itertools
import math

import jax
import jax.numpy as jnp
from jax import lax
from jax.experimental import pallas as pl
from jax.experimental.pallas import tpu as pltpu

F32 = jnp.float32
BF16 = jnp.bfloat16

D_MODEL = 2048
SSM_WIDTH = 1024
SSM_GROUP = 16
SSM_GROUPS = 64
SSM_STATE = 64
ATTN_HEADS = 8
HEAD_DIM = 128
ATTN_WIDTH = 1024
MOBA_BLOCK = 256
MOBA_TOPK = 3
REL_BUCKETS = 32
REL_MAX_DIST = 128
D_FF = 4 * D_MODEL
N_MOD = 6
EPS = 1e-6
NEG_INF = -1e30
LOG2E = math.log2(math.e)
IN_WIDTH = SSM_WIDTH + 3 * ATTN_WIDTH + 2 * D_MODEL

VMEM_LIMIT_BYTES = 56 * 1024 * 1024

S5_TAU = 2
S5_GROUPS = 8
S5_CH = S5_GROUPS * SSM_GROUP
S5_UC = S5_TAU * S5_CH
S5_SC = 2 * S5_GROUPS * SSM_STATE
S5_NBLK = SSM_GROUPS // S5_GROUPS
S5_ROWS = 1024


_RESIDENT = pl.Buffered(1)


def _cparams(sem):
    return pltpu.CompilerParams(dimension_semantics=sem,
                                vmem_limit_bytes=VMEM_LIMIT_BYTES)


def _mod_kernel(c_ref, w_ref, b_ref, o_ref):
    c = c_ref[...]
    ca = (c * jax.nn.sigmoid(c)).astype(BF16)
    o_ref[...] = jnp.dot(ca, w_ref[...].astype(BF16),
                         preferred_element_type=F32) + b_ref[...]


def _mod(c, w_ada, b_ada):
    bsz, d = c.shape
    n = w_ada.shape[1]
    tn = 1024
    return pl.pallas_call(
        _mod_kernel,
        out_shape=jax.ShapeDtypeStruct((bsz, n), F32),
        grid=(n // tn,),
        in_specs=[pl.BlockSpec((bsz, d), lambda j: (0, 0)),
                  pl.BlockSpec((d, tn), lambda j: (0, j)),
                  pl.BlockSpec((1, tn), lambda j: (0, j))],
        out_specs=pl.BlockSpec((bsz, tn), lambda j: (0, j)),
        compiler_params=_cparams(("parallel",)),
        name="mod",
    )(c, w_ada, b_ada.reshape(1, n))


def _side_casts(arrays, steps, step_of):
    specs, shapes = [], []
    for a in arrays:
        r, c = a.shape
        rb = max(16, r // steps)
        nblk = r // rb
        assert r % rb == 0 and steps % nblk == 0
        specs.append(pl.BlockSpec(
            (rb, c), lambda *ids, nblk=nblk: ((step_of(*ids) * nblk) // steps, 0)))
        shapes.append(jax.ShapeDtypeStruct((r, c), BF16))
    return specs, shapes


def _cast_sides(side_in, side_out):
    for src, dst in zip(side_in, side_out):
        dst[...] = src[...].astype(dst.dtype)


def _norm_mm_kernel(*refs, act, rows, n_side):
    x_ref, g_ref, sc_ref, sh_ref, w_ref = refs[:5]
    side_in = refs[5:5 + n_side]
    o_ref = refs[5 + n_side]
    side_out = refs[6 + n_side:6 + 2 * n_side]
    h_ref = refs[-1]

    def mm(h):
        acc = jnp.dot(h, w_ref[...], preferred_element_type=F32)
        if act == "relu2":
            acc = jnp.square(jnp.maximum(acc, 0.0))
        return acc.astype(o_ref.dtype)

    @pl.when(pl.program_id(1) == 0)
    def _():
        g = g_ref[...]
        sc = 1.0 + sc_ref[0]
        sh = sh_ref[0]
        tm = x_ref.shape[0]
        for r in range(0, tm, rows):
            x = x_ref[r:r + rows, :]
            ms = jnp.mean(x * x, axis=-1, keepdims=True)
            y = x * lax.rsqrt(ms + EPS) * g
            h = (y * sc + sh).astype(BF16)
            h_ref[r:r + rows, :] = h
            o_ref[r:r + rows, :] = mm(h)

    @pl.when(pl.program_id(1) != 0)
    def _():
        o_ref[...] = mm(h_ref[...])

    _cast_sides(side_in, side_out)


def _norm_mm(x2d, g, mod4, sc_idx, sh_idx, w, seq_len, *, act, side_casts=(),
             tm=1024, tn=2048):
    t, d = x2d.shape
    n = w.shape[1]
    per_b = seq_len // tm
    nj = n // tn
    side_specs, side_shapes = _side_casts(side_casts, (t // tm) * nj, lambda i, j: i * nj + j)
    res = pl.pallas_call(
        functools.partial(_norm_mm_kernel, act=act, rows=256, n_side=len(side_casts)),
        out_shape=tuple([jax.ShapeDtypeStruct((t, n), BF16)] + side_shapes),
        grid=(t // tm, nj),
        in_specs=[pl.BlockSpec((tm, d), lambda i, j: (i, 0)),
                  pl.BlockSpec((1, d), lambda i, j: (0, 0)),
                  pl.BlockSpec((1, 1, d), lambda i, j: ((i // per_b) * N_MOD + sc_idx, 0, 0)),
                  pl.BlockSpec((1, 1, d), lambda i, j: ((i // per_b) * N_MOD + sh_idx, 0, 0)),
                  pl.BlockSpec((d, tn), lambda i, j: (0, j))] + side_specs,
        out_specs=tuple([pl.BlockSpec((tm, tn), lambda i, j: (i, j))] + side_specs),
        scratch_shapes=[pltpu.VMEM((tm, d), BF16)],
        compiler_params=_cparams(("arbitrary", "arbitrary")),
        name="norm_mm_" + act,
    )(x2d, g.reshape(1, d), mod4, mod4, w, *side_casts)
    return res if len(res) > 1 else res[0]


def _s5prep_kernel(are_ref, aim_ref, ldt_ref, bre_ref, bim_ref, cre_ref, cim_ref,
                   at_ref, bw_ref, cw_ref, kw_ref):
    tau = bw_ref.shape[0]
    a_re = are_ref[...]
    a_im = aim_ref[...]
    dt = jnp.exp(ldt_ref[...])
    mag = jnp.exp(dt * a_re)
    abar_re = mag * jnp.cos(dt * a_im)
    abar_im = mag * jnp.sin(dt * a_im)
    den = a_re * a_re + a_im * a_im
    p_re = abar_re - 1.0
    f_re = (p_re * a_re + abar_im * a_im) / den
    f_im = (abar_im * a_re - p_re * a_im) / den

    def cmul(x, y):
        return x[0] * y[0] - x[1] * y[1], x[0] * y[1] + x[1] * y[0]

    def cat(x):
        return jnp.concatenate([x[0], x[1]], axis=-1)

    def re_dot(x):
        return jnp.concatenate([x[0], -x[1]], axis=-1)

    abar = (abar_re, abar_im)
    bb = cmul((f_re, f_im), (bre_ref[...], bim_ref[...]))
    cc = (cre_ref[...], cim_ref[...])
    apow = [None, abar]
    for _ in range(tau - 1):
        apow.append(cmul(apow[-1], abar))
    at_ref[...] = cat(apow[tau])
    y0 = cat(bb)
    for i in range(tau):
        j = tau - 1 - i
        bw_ref[i] = y0 if j == 0 else cat(cmul(apow[j], bb))
        cw_ref[i] = re_dot(cmul(cc, apow[i + 1]))
        x = re_dot(cc if i == 0 else cmul(cc, apow[i]))
        kw_ref[i] = jnp.einsum("gpn,gqn->gpq", x, y0, precision=lax.Precision.HIGHEST,
                               preferred_element_type=F32)


def _s5prep(a_re, a_im, log_dt, b_re, b_im, c_re, c_im):
    g, n = a_re.shape
    p = b_re.shape[2]
    b_re_t = jnp.transpose(b_re, (0, 2, 1))
    b_im_t = jnp.transpose(b_im, (0, 2, 1))
    return pl.pallas_call(
        _s5prep_kernel,
        out_shape=(jax.ShapeDtypeStruct((g, 1, 2 * n), F32),
                   jax.ShapeDtypeStruct((S5_TAU, g, p, 2 * n), F32),
                   jax.ShapeDtypeStruct((S5_TAU, g, p, 2 * n), F32),
                   jax.ShapeDtypeStruct((S5_TAU, g, p, p), F32)),
        name="s5prep",
    )(a_re.reshape(g, 1, n), a_im.reshape(g, 1, n), log_dt.reshape(g, 1, 1),
      b_re_t, b_im_t, c_re, c_im)


def _block_diag(x):
    nb, ng, r, c = x.shape
    eye = jnp.eye(ng, dtype=x.dtype)
    return jnp.einsum("agrc,gh->agrhc", x, eye).reshape(nb, ng * r, ng * c)


def _s5_weights(at, bw, cw, kw, d_skip, bsz):
    n = SSM_STATE
    blk = (S5_NBLK, S5_GROUPS)

    def bd(x):
        return _block_diag(x.reshape(blk + x.shape[1:]))

    def bd_t(x):
        return bd(jnp.transpose(x, (0, 2, 1)))

    bmat = jnp.concatenate(
        [jnp.concatenate([bd(bw[i, :, :, :n]), bd(bw[i, :, :, n:])], axis=-1)
         for i in range(S5_TAU)], axis=1).astype(BF16)
    cmat = jnp.concatenate(
        [jnp.concatenate([bd_t(cw[i, :, :, :n]), bd_t(cw[i, :, :, n:])], axis=1)
         for i in range(S5_TAU)], axis=-1).astype(BF16)
    lag = [bd_t(kw[i]) for i in range(S5_TAU)]
    zero = jnp.zeros_like(lag[0])
    kmat = jnp.concatenate(
        [jnp.concatenate([lag[i - ip] if ip <= i else zero for i in range(S5_TAU)], axis=-1)
         for ip in range(S5_TAU)], axis=1).astype(BF16)
    half = S5_SC // 2
    atv = jnp.concatenate([at[:, 0, :n].reshape(S5_NBLK, half),
                           at[:, 0, n:].reshape(S5_NBLK, half)], axis=-1)
    atv = jnp.broadcast_to(atv.reshape(S5_NBLK, 1, S5_SC), (S5_NBLK, bsz, S5_SC))
    d = jnp.broadcast_to(d_skip.reshape(S5_NBLK, 1, S5_CH), (S5_NBLK, S5_TAU, S5_CH))
    return bmat, cmat, kmat, atv, d.reshape(S5_NBLK, 1, S5_UC)


def _s5_kernel(x_ref, bw_ref, cw_ref, kw_ref, a_ref, d_ref, y_ref,
               xf_ref, u_ref, w_ref, s_ref, st_ref, *, bsz, n_sub):
    @pl.when(pl.program_id(1) == 0)
    def _():
        st_ref[...] = jnp.zeros_like(st_ref)

    half = S5_SC // 2
    a_re = a_ref[0, :, :half]
    a_im = a_ref[0, :, half:]
    s_re = st_ref[:, :half]
    s_im = st_ref[:, half:]
    d2 = d_ref[0]
    sub = u_ref.shape[0] // n_sub
    ksub = sub // bsz

    def slot(ref, c, i):
        return ref.at[:, pl.ds(c * ksub * S5_TAU + i, ksub, stride=S5_TAU), :]

    def project_in(c):
        t0 = c * ksub * S5_TAU
        xf_ref[:, t0:t0 + ksub * S5_TAU, :] = x_ref[:, t0:t0 + ksub * S5_TAU, :].astype(F32)
        for i in range(S5_TAU):
            x = pltpu.einshape("bkc->kbc", slot(xf_ref, c, i)[...]).reshape(sub, S5_CH)
            u_ref[c * sub:(c + 1) * sub, i * S5_CH:(i + 1) * S5_CH] = x.astype(BF16)
        w_ref[c * sub:(c + 1) * sub, :] = jnp.dot(u_ref[c * sub:(c + 1) * sub, :], bw_ref[0],
                                                  preferred_element_type=F32)

    project_in(0)
    for c in range(n_sub):
        r0 = c * sub
        if c + 1 < n_sub:
            project_in(c + 1)
        u = u_ref[r0:r0 + sub, :]
        for k in range(0, ksub, 2):
            before = []
            for kk in range(2):
                r = r0 + (k + kk) * bsz
                before.append(jnp.concatenate([s_re, s_im], axis=1))
                w = w_ref[r:r + bsz, :]
                s_re, s_im = (a_re * s_re - a_im * s_im + w[:, :half],
                              a_re * s_im + a_im * s_re + w[:, half:])
            s_ref[r0 + k * bsz:r0 + (k + 2) * bsz, :] = (
                jnp.concatenate(before, axis=0).astype(BF16))
        y = (jnp.dot(s_ref[r0:r0 + sub, :], cw_ref[0], preferred_element_type=F32)
             + jnp.dot(u, kw_ref[0], preferred_element_type=F32)
             + d2 * u.astype(F32))
        y = jax.nn.gelu(y)
        for i in range(S5_TAU):
            slot(y_ref, c, i)[...] = pltpu.einshape(
                "kbc->bkc", y[:, i * S5_CH:(i + 1) * S5_CH].reshape(ksub, bsz, S5_CH))
    st_ref[:, :half] = s_re
    st_ref[:, half:] = s_im


def _s5(proj, bmat, cmat, kmat, atv, d2):
    assert S5_CH == 128, "a block's channels must be one lane tile"
    bsz, seq_len, _ = proj.shape
    tpc = S5_ROWS // bsz * S5_TAU
    return pl.pallas_call(
        functools.partial(_s5_kernel, bsz=bsz, n_sub=4),
        out_shape=jax.ShapeDtypeStruct((bsz, seq_len, SSM_WIDTH), F32),
        grid=(S5_NBLK, seq_len // tpc),
        in_specs=[
            pl.BlockSpec((bsz, tpc, S5_CH), lambda g, c: (0, c, g)),
            pl.BlockSpec((1, S5_UC, S5_SC), lambda g, c: (g, 0, 0)),
            pl.BlockSpec((1, S5_SC, S5_UC), lambda g, c: (g, 0, 0)),
            pl.BlockSpec((1, S5_UC, S5_UC), lambda g, c: (g, 0, 0)),
            pl.BlockSpec((1, bsz, S5_SC), lambda g, c: (g, 0, 0)),
            pl.BlockSpec((1, 1, S5_UC), lambda g, c: (g, 0, 0))],
        out_specs=pl.BlockSpec((bsz, tpc, S5_CH), lambda g, c: (0, c, g)),
        scratch_shapes=[pltpu.VMEM((bsz, tpc, S5_CH), F32),
                        pltpu.VMEM((S5_ROWS, S5_UC), BF16),
                        pltpu.VMEM((S5_ROWS, S5_SC), F32),
                        pltpu.VMEM((S5_ROWS, S5_SC), BF16),
                        pltpu.VMEM((bsz, S5_SC), F32)],
        compiler_params=_cparams(("parallel", "arbitrary")),
        name="s5",
    )(proj, bmat, cmat, kmat, atv, d2)


def _glu_kernel(y_ref, w_ref, b_ref, o_ref):
    y = y_ref[...]
    z = jnp.dot(y.astype(BF16), w_ref[...], preferred_element_type=F32) + b_ref[...]
    o_ref[...] = (y * jax.nn.sigmoid(z)).astype(o_ref.dtype)


def _glu(y, w, b, tm=1024):
    t, n = y.shape
    return pl.pallas_call(
        _glu_kernel,
        out_shape=jax.ShapeDtypeStruct((t, n), BF16),
        grid=(t // tm,),
        in_specs=[pl.BlockSpec((tm, n), lambda i: (i, 0)),
                  pl.BlockSpec((n, n), lambda i: (0, 0)),
                  pl.BlockSpec((1, n), lambda i: (0, 0))],
        out_specs=pl.BlockSpec((tm, n), lambda i: (i, 0)),
        compiler_params=_cparams(("parallel",)),
        name="glu",
    )(y, w, b.reshape(1, n))


def _t5_bucket(rel):
    n = jnp.maximum(rel, 0)
    max_exact = REL_BUCKETS // 2
    nf = jnp.maximum(n, 1).astype(F32)
    large = max_exact + (jnp.log(nf / max_exact) / math.log(REL_MAX_DIST / max_exact)
                         * (REL_BUCKETS - max_exact)).astype(jnp.int32)
    large = jnp.minimum(large, REL_BUCKETS - 1)
    return jnp.where(n < max_exact, n, large)


def _bias_kernel(relb_ref, bucket_ref, w_ref, o_ref, wb_ref):
    h = pl.program_id(0)
    bucket = bucket_ref[0]
    out = jnp.zeros(bucket.shape, F32)
    for b in range(REL_BUCKETS):
        out = jnp.where(bucket == b, relb_ref[b * ATTN_HEADS + h], out)
    o_ref[0, 0] = out * LOG2E
    wb_ref[...] = w_ref[...].astype(wb_ref.dtype)


def _bias_tiles(rel_bias, w_in):
    i = jnp.arange(MOBA_BLOCK)
    rel = (jnp.arange(3)[:, None, None] * MOBA_BLOCK + i[None, None, :] - i[None, :, None])
    bucket = _t5_bucket(rel)
    n_d = 4
    rows, cols = w_in.shape
    rb = rows // (ATTN_HEADS * n_d)
    w_spec = pl.BlockSpec((rb, cols), lambda h, d: (h * n_d + d, 0))
    return pl.pallas_call(
        _bias_kernel,
        out_shape=(jax.ShapeDtypeStruct((ATTN_HEADS, 3, MOBA_BLOCK, MOBA_BLOCK), F32),
                   jax.ShapeDtypeStruct((rows, cols), BF16)),
        grid=(ATTN_HEADS, n_d),
        in_specs=[pl.BlockSpec(memory_space=pltpu.SMEM),
                  pl.BlockSpec((1, MOBA_BLOCK, MOBA_BLOCK), lambda h, d: (jnp.minimum(d, 2), 0, 0)),
                  w_spec],
        out_specs=(pl.BlockSpec((1, 1, MOBA_BLOCK, MOBA_BLOCK),
                                lambda h, d: (h, jnp.minimum(d, 2), 0, 0)),
                   w_spec),
        compiler_params=_cparams(("arbitrary", "arbitrary")),
        name="bias_tiles",
    )(rel_bias.astype(F32).reshape(-1), bucket, w_in)


_NT = (((1,), (1,)), ((), ()))


def _moba_scores(own, slot0, state, q_ref, k_ref, bias_ref, km_ref, s_ref):
    bs = MOBA_BLOCK
    q = q_ref[own * bs:(own + 1) * bs, :]

    ranks = []
    if own > 0:
        gate = lax.dot_general(km_ref[...].astype(BF16), q, _NT, preferred_element_type=F32)
        blk = lax.broadcasted_iota(jnp.int32, gate.shape, 0)
        gm = jnp.where(blk < own, gate, NEG_INF)
        for n in range(own):
            g_n = gm[n:n + 1, :]
            ge = jnp.where(gm >= g_n, 1.0, 0.0)
            gt = jnp.where(gm > g_n, 1.0, 0.0)
            ranks.append(jnp.sum(jnp.where(blk < n, ge, gt), axis=0, keepdims=True))

    c1 = HEAD_DIM ** -0.5 * LOG2E
    m = None
    for idx, j in enumerate([own] + list(range(own))):
        k_j = k_ref[j * bs:(j + 1) * bs, :]
        raw = lax.dot_general(k_j, q, _NT, preferred_element_type=F32)
        if j == own:
            key = lax.broadcasted_iota(jnp.int32, (bs, bs), 0)
            qry = lax.broadcasted_iota(jnp.int32, (bs, bs), 1)
            s = jnp.where(key <= qry, raw * c1 + bias_ref[0, 0], NEG_INF)
        elif own - j == 1:
            s = jnp.where(ranks[j] < float(MOBA_TOPK), raw * c1 + bias_ref[0, 1], NEG_INF)
        else:
            row = jnp.where(ranks[j] < float(MOBA_TOPK), bias_ref[0, 2, 0:1, :], NEG_INF)
            s = raw * c1 + row
        s_ref[slot0 + idx] = s
        m_j = jnp.max(s, axis=0, keepdims=True)
        m = m_j if m is None else jnp.maximum(m, m_j)
        state["m"] = m
        yield


def _moba_values(own, slot0, state, vt_ref, o_ref, s_ref):
    bs = MOBA_BLOCK
    m = state["m"]
    acc = None
    for idx, j in enumerate([own] + list(range(own))):
        p = jnp.exp2((s_ref[slot0 + idx] - m).astype(BF16))
        a_j = jnp.dot(vt_ref[:, j * bs:(j + 1) * bs], p, preferred_element_type=F32)
        acc = a_j if acc is None else acc + a_j
        if idx == own:
            o = acc[:HEAD_DIM, :] / acc[HEAD_DIM:HEAD_DIM + 1, :]
            o_ref[own * bs:(own + 1) * bs, :] = o.T.astype(o_ref.dtype)
        yield


def _moba_kernel(*refs, n_blk, n_side):
    q_ref, k_ref, v_ref, bias_ref = refs[:4]
    side_in = refs[4:4 + n_side]
    o_ref = refs[4 + n_side]
    side_out = refs[5 + n_side:5 + 2 * n_side]
    km_ref, vt_ref, s_ref = refs[-3:]
    bs = MOBA_BLOCK
    _cast_sides(side_in, side_out)
    vt_ref[HEAD_DIM:, :] = jnp.ones((vt_ref.shape[0] - HEAD_DIM, vt_ref.shape[1]), BF16)
    for n in range(n_blk):
        kb = k_ref[n * bs:(n + 1) * bs, :].astype(F32)
        km_ref[n:n + 1, :] = jnp.mean(kb, axis=0, keepdims=True)
        vt_ref[:HEAD_DIM, n * bs:(n + 1) * bs] = (
            v_ref[n * bs:(n + 1) * bs, :].astype(F32).T.astype(BF16))

    slots = [own * (own + 1) // 2 for own in range(n_blk)]
    states = [dict() for _ in range(n_blk)]
    pending = iter(())
    for own in range(n_blk):
        scores = _moba_scores(own, slots[own], states[own], q_ref, k_ref, bias_ref, km_ref, s_ref)
        for _ in itertools.zip_longest(scores, pending):
            pass
        pending = _moba_values(own, slots[own], states[own], vt_ref, o_ref, s_ref)
    for _ in pending:
        pass


def _moba(proj, bias_tiles, bsz, seq_len, side_casts=()):
    n_blk = seq_len // MOBA_BLOCK
    q_off = SSM_WIDTH // HEAD_DIM
    k_off = q_off + ATTN_HEADS
    v_off = k_off + ATTN_HEADS
    n_slots = n_blk * (n_blk + 1) // 2
    side_specs, side_shapes = _side_casts(side_casts, bsz * ATTN_HEADS,
                                          lambda b, h: b * ATTN_HEADS + h)
    res = pl.pallas_call(
        functools.partial(_moba_kernel, n_blk=n_blk, n_side=len(side_casts)),
        out_shape=tuple([jax.ShapeDtypeStruct((bsz * seq_len, ATTN_WIDTH), BF16)] + side_shapes),
        grid=(bsz, ATTN_HEADS),
        in_specs=[pl.BlockSpec((seq_len, HEAD_DIM), lambda b, h: (b, q_off + h)),
                  pl.BlockSpec((seq_len, HEAD_DIM), lambda b, h: (b, k_off + h)),
                  pl.BlockSpec((seq_len, HEAD_DIM), lambda b, h: (b, v_off + h)),
                  pl.BlockSpec((1, 3, MOBA_BLOCK, MOBA_BLOCK), lambda b, h: (h, 0, 0, 0))]
        + side_specs,
        out_specs=tuple([pl.BlockSpec((seq_len, HEAD_DIM), lambda b, h: (b, h))] + side_specs),
        scratch_shapes=[pltpu.VMEM((n_blk, HEAD_DIM), F32),
                        pltpu.VMEM((HEAD_DIM + 16, seq_len), BF16),
                        pltpu.VMEM((n_slots, MOBA_BLOCK, MOBA_BLOCK), F32)],
        compiler_params=_cparams(("arbitrary", "arbitrary")),
        name="moba",
    )(proj, proj, proj, bias_tiles, *side_casts)
    return res if len(res) > 1 else res[0]


def _merge_kernel(ys_ref, ya_ref, ws_ref, wa_ref, ga_ref, gb_ref, o_ref):
    a = jnp.dot(ys_ref[...], ws_ref[...], preferred_element_type=F32)
    b = jnp.dot(ya_ref[...], wa_ref[...], preferred_element_type=F32)
    o_ref[...] = (jax.nn.sigmoid(ga_ref[...].astype(F32)) * a
                  + jax.nn.sigmoid(gb_ref[...].astype(F32)) * b).astype(o_ref.dtype)


def _merge(y_ssm, y_att, w_ps, w_pa, proj, tm=1024, tn=2048):
    t, k = y_ssm.shape
    n = w_ps.shape[1]
    ga_off = (SSM_WIDTH + 3 * ATTN_WIDTH) // tn
    gb_off = ga_off + D_MODEL // tn
    return pl.pallas_call(
        _merge_kernel,
        out_shape=jax.ShapeDtypeStruct((t, n), BF16),
        grid=(t // tm, n // tn),
        in_specs=[pl.BlockSpec((tm, k), lambda i, j: (i, 0)),
                  pl.BlockSpec((tm, k), lambda i, j: (i, 0)),
                  pl.BlockSpec((k, tn), lambda i, j: (0, j), pipeline_mode=_RESIDENT),
                  pl.BlockSpec((k, tn), lambda i, j: (0, j), pipeline_mode=_RESIDENT),
                  pl.BlockSpec((tm, tn), lambda i, j: (i, ga_off + j)),
                  pl.BlockSpec((tm, tn), lambda i, j: (i, gb_off + j))],
        out_specs=pl.BlockSpec((tm, tn), lambda i, j: (i, j)),
        compiler_params=_cparams(("parallel", "parallel")),
        name="merge",
    )(y_ssm, y_att, w_ps, w_pa, proj, proj)


def _resid_mm_kernel(a_ref, w_ref, x_ref, g_ref, o_ref):
    acc = jnp.dot(a_ref[...], w_ref[...], preferred_element_type=F32)
    o_ref[...] = x_ref[...] + g_ref[0] * acc


def _resid_mm(a, w, x2d, mod4, g_idx, seq_len, tm=1024, tn=2048):
    t, k = a.shape
    n = w.shape[1]
    per_b = seq_len // tm
    return pl.pallas_call(
        _resid_mm_kernel,
        out_shape=jax.ShapeDtypeStruct((t, n), F32),
        grid=(t // tm, n // tn),
        in_specs=[pl.BlockSpec((tm, k), lambda i, j: (i, 0)),
                  pl.BlockSpec((k, tn), lambda i, j: (0, j), pipeline_mode=_RESIDENT),
                  pl.BlockSpec((tm, tn), lambda i, j: (i, j)),
                  pl.BlockSpec((1, 1, tn), lambda i, j: ((i // per_b) * N_MOD + g_idx, 0, j))],
        out_specs=pl.BlockSpec((tm, tn), lambda i, j: (i, j)),
        compiler_params=_cparams(("parallel", "parallel")),
        name="resid_mm",
    )(a, w, x2d, mod4)


def _ff2_kernel(h_ref, w_ref, x_ref, g_ref, gf_ref, o_ref, *, rows):
    k = pl.program_id(1)
    last = pl.num_programs(1) - 1

    @pl.when(k == 0)
    def _():
        o_ref[...] = jnp.dot(h_ref[...], w_ref[...], preferred_element_type=F32)

    @pl.when((k != 0) & (k != last))
    def _():
        o_ref[...] += jnp.dot(h_ref[...], w_ref[...], preferred_element_type=F32)

    @pl.when(k == last)
    def _():
        for r in range(0, o_ref.shape[0], rows):
            acc = o_ref[r:r + rows, :] + jnp.dot(h_ref[r:r + rows, :], w_ref[...],
                                                 preferred_element_type=F32)
            xo = x_ref[r:r + rows, :] + g_ref[0] * acc
            ms = jnp.mean(xo * xo, axis=-1, keepdims=True)
            o_ref[r:r + rows, :] = xo * lax.rsqrt(ms + EPS) * gf_ref[...]


def _ff2(hid, w, x2d, mod4, g_idx, gf, seq_len, tm=1024, tk=1024):
    t, kdim = hid.shape
    n = w.shape[1]
    per_b = seq_len // tm
    return pl.pallas_call(
        functools.partial(_ff2_kernel, rows=256),
        out_shape=jax.ShapeDtypeStruct((t, n), F32),
        grid=(t // tm, kdim // tk),
        in_specs=[pl.BlockSpec((tm, tk), lambda i, k: (i, k)),
                  pl.BlockSpec((tk, n), lambda i, k: (k, 0)),
                  pl.BlockSpec((tm, n), lambda i, k: (i, 0)),
                  pl.BlockSpec((1, 1, n), lambda i, k: ((i // per_b) * N_MOD + g_idx, 0, 0)),
                  pl.BlockSpec((1, n), lambda i, k: (0, 0))],
        out_specs=pl.BlockSpec((tm, n), lambda i, k: (i, 0)),
        compiler_params=_cparams(("parallel", "arbitrary")),
        name="ff2_final",
    )(hid, w, x2d, mod4, gf.reshape(1, n))


def kernel(x, c, rel_bias, w_ada, b_ada, norm_mix_g, w_in, ssm_a_re, ssm_a_im, ssm_log_dt,
           ssm_b_re, ssm_b_im, ssm_c_re, ssm_c_im, ssm_d, w_glu, b_glu, w_proj_ssm,
           w_proj_attn, w_out, norm_mlp_g, w_ff1, w_ff2, norm_final_g):
    bsz, seq_len, d = x.shape
    depth = w_in.shape[0]
    assert depth == 1, "the final rms_norm is fused into the single layer's ff2 kernel"
    t = bsz * seq_len
    x2d = x.reshape(t, d)
    bias_tiles, w_in_b = _bias_tiles(rel_bias, w_in[0])

    for l in range(depth):
        mod4 = _mod(c, w_ada[l], b_ada[l]).reshape(bsz * N_MOD, 1, d)

        proj, w_glu_b, w_ps_b, w_pa_b, w_out_b = _norm_mm(
            x2d, norm_mix_g[l], mod4, 1, 0, w_in_b, seq_len, act="none",
            side_casts=(w_glu[l], w_proj_ssm[l], w_proj_attn[l], w_out[l]))

        at, bw, cw, kw = _s5prep(ssm_a_re[l], ssm_a_im[l], ssm_log_dt[l],
                                 ssm_b_re[l], ssm_b_im[l], ssm_c_re[l], ssm_c_im[l])
        bmat, cmat, kmat, atv, d2 = _s5_weights(at, bw, cw, kw, ssm_d[l], bsz)
        y = _s5(proj.reshape(bsz, seq_len, IN_WIDTH), bmat, cmat, kmat, atv, d2)
        y_ssm = _glu(y.reshape(t, SSM_WIDTH), w_glu_b, b_glu[l])

        y_att, w_ff1_b, w_ff2_b = _moba(proj, bias_tiles, bsz, seq_len,
                                        side_casts=(w_ff1[l], w_ff2[l]))

        merged = _merge(y_ssm, y_att, w_ps_b, w_pa_b, proj)
        x2d = _resid_mm(merged, w_out_b, x2d, mod4, 2, seq_len)

        hid = _norm_mm(x2d, norm_mlp_g[l], mod4, 4, 3, w_ff1_b, seq_len, act="relu2")
        x2d = _ff2(hid, w_ff2_b, x2d, mod4, 5, norm_final_g, seq_len)
    return x2d.reshape(bsz, seq_len, d)
```

```python
import functools
import itertools
import math

import jax
import jax.numpy as jnp
from jax import lax
from jax.experimental import pallas as pl
from jax.experimental.pallas import tpu as pltpu

F32 = jnp.float32
BF16 = jnp.bfloat16

D_MODEL = 2048
SSM_WIDTH = 1024
SSM_GROUP = 16
SSM_GROUPS = 64
SSM_STATE = 64
ATTN_HEADS = 8
HEAD_DIM = 128
ATTN_WIDTH = 1024
MOBA_BLOCK = 256
MOBA_TOPK = 3
REL_BUCKETS = 32
REL_MAX_DIST = 128
D_FF = 4 * D_MODEL
N_MOD = 6
EPS = 1e-6
NEG_INF = -1e30
LOG2E = math.log2(math.e)
IN_WIDTH = SSM_WIDTH + 3 * ATTN_WIDTH + 2 * D_MODEL

VMEM_LIMIT_BYTES = 56 * 1024 * 1024

S5_TAU = 2
S5_GROUPS = 8
S5_CH = S5_GROUPS * SSM_GROUP
S5_UC = S5_TAU * S5_CH
S5_SC = 2 * S5_GROUPS * SSM_STATE
S5_NBLK = SSM_GROUPS // S5_GROUPS
S5_ROWS = 1024


_RESIDENT = pl.Buffered(1)


def _cparams(sem):
    return pltpu.CompilerParams(dimension_semantics=sem,
                                vmem_limit_bytes=VMEM_LIMIT_BYTES)


def _mod_kernel(c_ref, w_ref, b_ref, o_ref):
    c = c_ref[...]
    ca = (c * jax.nn.sigmoid(c)).astype(BF16)
    o_ref[...] = jnp.dot(ca, w_ref[...].astype(BF16),
                         preferred_element_type=F32) + b_ref[...]


def _mod(c, w_ada, b_ada):
    bsz, d = c.shape
    n = w_ada.shape[1]
    tn = 1024
    return pl.pallas_call(
        _mod_kernel,
        out_shape=jax.ShapeDtypeStruct((bsz, n), F32),
        grid=(n // tn,),
        in_specs=[pl.BlockSpec((bsz, d), lambda j: (0, 0)),
                  pl.BlockSpec((d, tn), lambda j: (0, j)),
                  pl.BlockSpec((1, tn), lambda j: (0, j))],
        out_specs=pl.BlockSpec((bsz, tn), lambda j: (0, j)),
        compiler_params=_cparams(("parallel",)),
        name="mod",
    )(c, w_ada, b_ada.reshape(1, n))


def _side_casts(arrays, steps, step_of):
    specs, shapes = [], []
    for a in arrays:
        r, c = a.shape
        rb = max(16, r // steps)
        nblk = r // rb
        assert r % rb == 0 and steps % nblk == 0
        specs.append(pl.BlockSpec(
            (rb, c), lambda *ids, nblk=nblk: ((step_of(*ids) * nblk) // steps, 0)))
        shapes.append(jax.ShapeDtypeStruct((r, c), BF16))
    return specs, shapes


def _cast_sides(side_in, side_out):
    for src, dst in zip(side_in, side_out):
        dst[...] = src[...].astype(dst.dtype)


def _norm_mm_kernel(*refs, act, rows, n_side):
    x_ref, g_ref, sc_ref, sh_ref, w_ref = refs[:5]
    side_in = refs[5:5 + n_side]
    o_ref = refs[5 + n_side]
    side_out = refs[6 + n_side:6 + 2 * n_side]
    h_ref = refs[-1]

    def mm(h):
        acc = jnp.dot(h, w_ref[...], preferred_element_type=F32)
        if act == "relu2":
            acc = jnp.square(jnp.maximum(acc, 0.0))
        return acc.astype(o_ref.dtype)

    @pl.when(pl.program_id(1) == 0)
    def _():
        g = g_ref[...]
        sc = 1.0 + sc_ref[0]
        sh = sh_ref[0]
        tm = x_ref.shape[0]
        for r in range(0, tm, rows):
            x = x_ref[r:r + rows, :]
            ms = jnp.mean(x * x, axis=-1, keepdims=True)
            y = x * lax.rsqrt(ms + EPS) * g
            h = (y * sc + sh).astype(BF16)
            h_ref[r:r + rows, :] = h
            o_ref[r:r + rows, :] = mm(h)

    @pl.when(pl.program_id(1) != 0)
    def _():
        o_ref[...] = mm(h_ref[...])

    _cast_sides(side_in, side_out)


def _norm_mm(x2d, g, mod4, sc_idx, sh_idx, w, seq_len, *, act, side_casts=(),
             tm=1024, tn=2048):
    t, d = x2d.shape
    n = w.shape[1]
    per_b = seq_len // tm
    nj = n // tn
    side_specs, side_shapes = _side_casts(side_casts, (t // tm) * nj, lambda i, j: i * nj + j)
    res = pl.pallas_call(
        functools.partial(_norm_mm_kernel, act=act, rows=256, n_side=len(side_casts)),
        out_shape=tuple([jax.ShapeDtypeStruct((t, n), BF16)] + side_shapes),
        grid=(t // tm, nj),
        in_specs=[pl.BlockSpec((tm, d), lambda i, j: (i, 0)),
                  pl.BlockSpec((1, d), lambda i, j: (0, 0)),
                  pl.BlockSpec((1, 1, d), lambda i, j: ((i // per_b) * N_MOD + sc_idx, 0, 0)),
                  pl.BlockSpec((1, 1, d), lambda i, j: ((i // per_b) * N_MOD + sh_idx, 0, 0)),
                  pl.BlockSpec((d, tn), lambda i, j: (0, j))] + side_specs,
        out_specs=tuple([pl.BlockSpec((tm, tn), lambda i, j: (i, j))] + side_specs),
        scratch_shapes=[pltpu.VMEM((tm, d), BF16)],
        compiler_params=_cparams(("arbitrary", "arbitrary")),
        name="norm_mm_" + act,
    )(x2d, g.reshape(1, d), mod4, mod4, w, *side_casts)
    return res if len(res) > 1 else res[0]


def _s5prep_kernel(are_ref, aim_ref, ldt_ref, bre_ref, bim_ref, cre_ref, cim_ref,
                   at_ref, bw_ref, cw_ref, kw_ref):
    tau = bw_ref.shape[0]
    a_re = are_ref[...]
    a_im = aim_ref[...]
    dt = jnp.exp(ldt_ref[...])
    mag = jnp.exp(dt * a_re)
    abar_re = mag * jnp.cos(dt * a_im)
    abar_im = mag * jnp.sin(dt * a_im)
    den = a_re * a_re + a_im * a_im
    p_re = abar_re - 1.0
    f_re = (p_re * a_re + abar_im * a_im) / den
    f_im = (abar_im * a_re - p_re * a_im) / den

    def cmul(x, y):
        return x[0] * y[0] - x[1] * y[1], x[0] * y[1] + x[1] * y[0]

    def cat(x):
        return jnp.concatenate([x[0], x[1]], axis=-1)

    def re_dot(x):
        return jnp.concatenate([x[0], -x[1]], axis=-1)

    abar = (abar_re, abar_im)
    bb = cmul((f_re, f_im), (bre_ref[...], bim_ref[...]))
    cc = (cre_ref[...], cim_ref[...])
    apow = [None, abar]
    for _ in range(tau - 1):
        apow.append(cmul(apow[-1], abar))
    at_ref[...] = cat(apow[tau])
    y0 = cat(bb)
    for i in range(tau):
        j = tau - 1 - i
        bw_ref[i] = y0 if j == 0 else cat(cmul(apow[j], bb))
        cw_ref[i] = re_dot(cmul(cc, apow[i + 1]))
        x = re_dot(cc if i == 0 else cmul(cc, apow[i]))
        kw_ref[i] = jnp.einsum("gpn,gqn->gpq", x, y0, precision=lax.Precision.HIGHEST,
                               preferred_element_type=F32)


def _s5prep(a_re, a_im, log_dt, b_re, b_im, c_re, c_im):
    g, n = a_re.shape
    p = b_re.shape[2]
    b_re_t = jnp.transpose(b_re, (0, 2, 1))
    b_im_t = jnp.transpose(b_im, (0, 2, 1))
    return pl.pallas_call(
        _s5prep_kernel,
        out_shape=(jax.ShapeDtypeStruct((g, 1, 2 * n), F32),
                   jax.ShapeDtypeStruct((S5_TAU, g, p, 2 * n), F32),
                   jax.ShapeDtypeStruct((S5_TAU, g, p, 2 * n), F32),
                   jax.ShapeDtypeStruct((S5_TAU, g, p, p), F32)),
        name="s5prep",
    )(a_re.reshape(g, 1, n), a_im.reshape(g, 1, n), log_dt.reshape(g, 1, 1),
      b_re_t, b_im_t, c_re, c_im)


def _block_diag(x):
    nb, ng, r, c = x.shape
    eye = jnp.eye(ng, dtype=x.dtype)
    return jnp.einsum("agrc,gh->agrhc", x, eye).reshape(nb, ng * r, ng * c)


def _s5_weights(at, bw, cw, kw, d_skip, bsz):
    n = SSM_STATE
    blk = (S5_NBLK, S5_GROUPS)

    def bd(x):
        return _block_diag(x.reshape(blk + x.shape[1:]))

    def bd_t(x):
        return bd(jnp.transpose(x, (0, 2, 1)))

    bmat = jnp.concatenate(
        [jnp.concatenate([bd(bw[i, :, :, :n]), bd(bw[i, :, :, n:])], axis=-1)
         for i in range(S5_TAU)], axis=1).astype(BF16)
    cmat = jnp.concatenate(
        [jnp.concatenate([bd_t(cw[i, :, :, :n]), bd_t(cw[i, :, :, n:])], axis=1)
         for i in range(S5_TAU)], axis=-1).astype(BF16)
    lag = [bd_t(kw[i]) for i in range(S5_TAU)]
    zero = jnp.zeros_like(lag[0])
    kmat = jnp.concatenate(
        [jnp.concatenate([lag[i - ip] if ip <= i else zero for i in range(S5_TAU)], axis=-1)
         for ip in range(S5_TAU)], axis=1).astype(BF16)
    half = S5_SC // 2
    atv = jnp.concatenate([at[:, 0, :n].reshape(S5_NBLK, half),
                           at[:, 0, n:].reshape(S5_NBLK, half)], axis=-1)
    atv = jnp.broadcast_to(atv.reshape(S5_NBLK, 1, S5_SC), (S5_NBLK, bsz, S5_SC))
    d = jnp.broadcast_to(d_skip.reshape(S5_NBLK, 1, S5_CH), (S5_NBLK, S5_TAU, S5_CH))
    return bmat, cmat, kmat, atv, d.reshape(S5_NBLK, 1, S5_UC)


def _s5_kernel(x_ref, bw_ref, cw_ref, kw_ref, a_ref, d_ref, y_ref,
               xf_ref, u_ref, w_ref, s_ref, st_ref, *, bsz, n_sub):
    @pl.when(pl.program_id(1) == 0)
    def _():
        st_ref[...] = jnp.zeros_like(st_ref)

    half = S5_SC // 2
    a_re = a_ref[0, :, :half]
    a_im = a_ref[0, :, half:]
    s_re = st_ref[:, :half]
    s_im = st_ref[:, half:]
    d2 = d_ref[0]
    sub = u_ref.shape[0] // n_sub
    ksub = sub // bsz

    def slot(ref, c, i):
        return ref.at[:, pl.ds(c * ksub * S5_TAU + i, ksub, stride=S5_TAU), :]

    def project_in(c):
        t0 = c * ksub * S5_TAU
        xf_ref[:, t0:t0 + ksub * S5_TAU, :] = x_ref[:, t0:t0 + ksub * S5_TAU, :].astype(F32)
        for i in range(S5_TAU):
            x = pltpu.einshape("bkc->kbc", slot(xf_ref, c, i)[...]).reshape(sub, S5_CH)
            u_ref[c * sub:(c + 1) * sub, i * S5_CH:(i + 1) * S5_CH] = x.astype(BF16)
        w_ref[c * sub:(c + 1) * sub, :] = jnp.dot(u_ref[c * sub:(c + 1) * sub, :], bw_ref[0],
                                                  preferred_element_type=F32)

    project_in(0)
    for c in range(n_sub):
        r0 = c * sub
        if c + 1 < n_sub:
            project_in(c + 1)
        u = u_ref[r0:r0 + sub, :]
        for k in range(0, ksub, 2):
            before = []
            for kk in range(2):
                r = r0 + (k + kk) * bsz
                before.append(jnp.concatenate([s_re, s_im], axis=1))
                w = w_ref[r:r + bsz, :]
                s_re, s_im = (a_re * s_re - a_im * s_im + w[:, :half],
                              a_re * s_im + a_im * s_re + w[:, half:])
            s_ref[r0 + k * bsz:r0 + (k + 2) * bsz, :] = (
                jnp.concatenate(before, axis=0).astype(BF16))
        y = (jnp.dot(s_ref[r0:r0 + sub, :], cw_ref[0], preferred_element_type=F32)
             + jnp.dot(u, kw_ref[0], preferred_element_type=F32)
             + d2 * u.astype(F32))
        y = jax.nn.gelu(y)
        for i in range(S5_TAU):
            slot(y_ref, c, i)[...] = pltpu.einshape(
                "kbc->bkc", y[:, i * S5_CH:(i + 1) * S5_CH].reshape(ksub, bsz, S5_CH))
    st_ref[:, :half] = s_re
    st_ref[:, half:] = s_im


def _s5(proj, bmat, cmat, kmat, atv, d2):
    assert S5_CH == 128, "a block's channels must be one lane tile"
    bsz, seq_len, _ = proj.shape
    tpc = S5_ROWS // bsz * S5_TAU
    return pl.pallas_call(
        functools.partial(_s5_kernel, bsz=bsz, n_sub=4),
        out_shape=jax.ShapeDtypeStruct((bsz, seq_len, SSM_WIDTH), F32),
        grid=(S5_NBLK, seq_len // tpc),
        in_specs=[
            pl.BlockSpec((bsz, tpc, S5_CH), lambda g, c: (0, c, g)),
            pl.BlockSpec((1, S5_UC, S5_SC), lambda g, c: (g, 0, 0)),
            pl.BlockSpec((1, S5_SC, S5_UC), lambda g, c: (g, 0, 0)),
            pl.BlockSpec((1, S5_UC, S5_UC), lambda g, c: (g, 0, 0)),
            pl.BlockSpec((1, bsz, S5_SC), lambda g, c: (g, 0, 0)),
            pl.BlockSpec((1, 1, S5_UC), lambda g, c: (g, 0, 0))],
        out_specs=pl.BlockSpec((bsz, tpc, S5_CH), lambda g, c: (0, c, g)),
        scratch_shapes=[pltpu.VMEM((bsz, tpc, S5_CH), F32),
                        pltpu.VMEM((S5_ROWS, S5_UC), BF16),
                        pltpu.VMEM((S5_ROWS, S5_SC), F32),
                        pltpu.VMEM((S5_ROWS, S5_SC), BF16),
                        pltpu.VMEM((bsz, S5_SC), F32)],
        compiler_params=_cparams(("parallel", "arbitrary")),
        name="s5",
    )(proj, bmat, cmat, kmat, atv, d2)


def _glu_kernel(y_ref, w_ref, b_ref, o_ref):
    y = y_ref[...]
    z = jnp.dot(y.astype(BF16), w_ref[...], preferred_element_type=F32) + b_ref[...]
    o_ref[...] = (y * jax.nn.sigmoid(z)).astype(o_ref.dtype)


def _glu(y, w, b, tm=1024):
    t, n = y.shape
    return pl.pallas_call(
        _glu_kernel,
        out_shape=jax.ShapeDtypeStruct((t, n), BF16),
        grid=(t // tm,),
        in_specs=[pl.BlockSpec((tm, n), lambda i: (i, 0)),
                  pl.BlockSpec((n, n), lambda i: (0, 0)),
                  pl.BlockSpec((1, n), lambda i: (0, 0))],
        out_specs=pl.BlockSpec((tm, n), lambda i: (i, 0)),
        compiler_params=_cparams(("parallel",)),
        name="glu",
    )(y, w, b.reshape(1, n))


def _t5_bucket(rel):
    n = jnp.maximum(rel, 0)
    max_exact = REL_BUCKETS // 2
    nf = jnp.maximum(n, 1).astype(F32)
    large = max_exact + (jnp.log(nf / max_exact) / math.log(REL_MAX_DIST / max_exact)
                         * (REL_BUCKETS - max_exact)).astype(jnp.int32)
    large = jnp.minimum(large, REL_BUCKETS - 1)
    return jnp.where(n < max_exact, n, large)


def _bias_kernel(relb_ref, bucket_ref, w_ref, o_ref, wb_ref):
    h = pl.program_id(0)
    bucket = bucket_ref[0]
    out = jnp.zeros(bucket.shape, F32)
    for b in range(REL_BUCKETS):
        out = jnp.where(bucket == b, relb_ref[b * ATTN_HEADS + h], out)
    o_ref[0, 0] = out * LOG2E
    wb_ref[...] = w_ref[...].astype(wb_ref.dtype)


def _bias_tiles(rel_bias, w_in):
    i = jnp.arange(MOBA_BLOCK)
    rel = (jnp.arange(3)[:, None, None] * MOBA_BLOCK + i[None, None, :] - i[None, :, None])
    bucket = _t5_bucket(rel)
    n_d = 4
    rows, cols = w_in.shape
    rb = rows // (ATTN_HEADS * n_d)
    w_spec = pl.BlockSpec((rb, cols), lambda h, d: (h * n_d + d, 0))
    return pl.pallas_call(
        _bias_kernel,
        out_shape=(jax.ShapeDtypeStruct((ATTN_HEADS, 3, MOBA_BLOCK, MOBA_BLOCK), F32),
                   jax.ShapeDtypeStruct((rows, cols), BF16)),
        grid=(ATTN_HEADS, n_d),
        in_specs=[pl.BlockSpec(memory_space=pltpu.SMEM),
                  pl.BlockSpec((1, MOBA_BLOCK, MOBA_BLOCK), lambda h, d: (jnp.minimum(d, 2), 0, 0)),
                  w_spec],
        out_specs=(pl.BlockSpec((1, 1, MOBA_BLOCK, MOBA_BLOCK),
                                lambda h, d: (h, jnp.minimum(d, 2), 0, 0)),
                   w_spec),
        compiler_params=_cparams(("arbitrary", "arbitrary")),
        name="bias_tiles",
    )(rel_bias.astype(F32).reshape(-1), bucket, w_in)


_NT = (((1,), (1,)), ((), ()))


def _moba_scores(own, slot0, state, q_ref, k_ref, bias_ref, km_ref, s_ref):
    bs = MOBA_BLOCK
    q = q_ref[own * bs:(own + 1) * bs, :]

    ranks = []
    if own > 0:
        gate = lax.dot_general(km_ref[...].astype(BF16), q, _NT, preferred_element_type=F32)
        blk = lax.broadcasted_iota(jnp.int32, gate.shape, 0)
        gm = jnp.where(blk < own, gate, NEG_INF)
        for n in range(own):
            g_n = gm[n:n + 1, :]
            ge = jnp.where(gm >= g_n, 1.0, 0.0)
            gt = jnp.where(gm > g_n, 1.0, 0.0)
            ranks.append(jnp.sum(jnp.where(blk < n, ge, gt), axis=0, keepdims=True))

    c1 = HEAD_DIM ** -0.5 * LOG2E
    m = None
    for idx, j in enumerate([own] + list(range(own))):
        k_j = k_ref[j * bs:(j + 1) * bs, :]
        raw = lax.dot_general(k_j, q, _NT, preferred_element_type=F32)
        if j == own:
            key = lax.broadcasted_iota(jnp.int32, (bs, bs), 0)
            qry = lax.broadcasted_iota(jnp.int32, (bs, bs), 1)
            s = jnp.where(key <= qry, raw * c1 + bias_ref[0, 0], NEG_INF)
        elif own - j == 1:
            s = jnp.where(ranks[j] < float(MOBA_TOPK), raw * c1 + bias_ref[0, 1], NEG_INF)
        else:
            row = jnp.where(ranks[j] < float(MOBA_TOPK), bias_ref[0, 2, 0:1, :], NEG_INF)
            s = raw * c1 + row
        s_ref[slot0 + idx] = s
        m_j = jnp.max(s, axis=0, keepdims=True)
        m = m_j if m is None else jnp.maximum(m, m_j)
        state["m"] = m
        yield


def _moba_values(own, slot0, state, vt_ref, o_ref, s_ref):
    bs = MOBA_BLOCK
    m = state["m"]
    acc = None
    for idx, j in enumerate([own] + list(range(own))):
        p = jnp.exp2((s_ref[slot0 + idx] - m).astype(BF16))
        a_j = jnp.dot(vt_ref[:, j * bs:(j + 1) * bs], p, preferred_element_type=F32)
        acc = a_j if acc is None else acc + a_j
        if idx == own:
            o = acc[:HEAD_DIM, :] / acc[HEAD_DIM:HEAD_DIM + 1, :]
            o_ref[own * bs:(own + 1) * bs, :] = o.T.astype(o_ref.dtype)
        yield


def _moba_kernel(*refs, n_blk, n_side):
    q_ref, k_ref, v_ref, bias_ref = refs[:4]
    side_in = refs[4:4 + n_side]
    o_ref = refs[4 + n_side]
    side_out = refs[5 + n_side:5 + 2 * n_side]
    km_ref, vt_ref, s_ref = refs[-3:]
    bs = MOBA_BLOCK
    _cast_sides(side_in, side_out)
    vt_ref[HEAD_DIM:, :] = jnp.ones((vt_ref.shape[0] - HEAD_DIM, vt_ref.shape[1]), BF16)
    for n in range(n_blk):
        kb = k_ref[n * bs:(n + 1) * bs, :].astype(F32)
        km_ref[n:n + 1, :] = jnp.mean(kb, axis=0, keepdims=True)
        vt_ref[:HEAD_DIM, n * bs:(n + 1) * bs] = (
            v_ref[n * bs:(n + 1) * bs, :].astype(F32).T.astype(BF16))

    slots = [own * (own + 1) // 2 for own in range(n_blk)]
    states = [dict() for _ in range(n_blk)]
    pending = iter(())
    for own in range(n_blk):
        scores = _moba_scores(own, slots[own], states[own], q_ref, k_ref, bias_ref, km_ref, s_ref)
        for _ in itertools.zip_longest(scores, pending):
            pass
        pending = _moba_values(own, slots[own], states[own], vt_ref, o_ref, s_ref)
    for _ in pending:
        pass


def _moba(proj, bias_tiles, bsz, seq_len, side_casts=()):
    n_blk = seq_len // MOBA_BLOCK
    q_off = SSM_WIDTH // HEAD_DIM
    k_off = q_off + ATTN_HEADS
    v_off = k_off + ATTN_HEADS
    n_slots = n_blk * (n_blk + 1) // 2
    side_specs, side_shapes = _side_casts(side_casts, bsz * ATTN_HEADS,
                                          lambda b, h: b * ATTN_HEADS + h)
    res = pl.pallas_call(
        functools.partial(_moba_kernel, n_blk=n_blk, n_side=len(side_casts)),
        out_shape=tuple([jax.ShapeDtypeStruct((bsz * seq_len, ATTN_WIDTH), BF16)] + side_shapes),
        grid=(bsz, ATTN_HEADS),
        in_specs=[pl.BlockSpec((seq_len, HEAD_DIM), lambda b, h: (b, q_off + h)),
                  pl.BlockSpec((seq_len, HEAD_DIM), lambda b, h: (b, k_off + h)),
                  pl.BlockSpec((seq_len, HEAD_DIM), lambda b, h: (b, v_off + h)),
                  pl.BlockSpec((1, 3, MOBA_BLOCK, MOBA_BLOCK), lambda b, h: (h, 0, 0, 0))]
        + side_specs,
        out_specs=tuple([pl.BlockSpec((seq_len, HEAD_DIM), lambda b, h: (b, h))] + side_specs),
        scratch_shapes=[pltpu.VMEM((n_blk, HEAD_DIM), F32),
                        pltpu.VMEM((HEAD_DIM + 16, seq_len), BF16),
                        pltpu.VMEM((n_slots, MOBA_BLOCK, MOBA_BLOCK), F32)],
        compiler_params=_cparams(("arbitrary", "arbitrary")),
        name="moba",
    )(proj, proj, proj, bias_tiles, *side_casts)
    return res if len(res) > 1 else res[0]


def _merge_kernel(ys_ref, ya_ref, ws_ref, wa_ref, ga_ref, gb_ref, o_ref):
    a = jnp.dot(ys_ref[...], ws_ref[...], preferred_element_type=F32)
    b = jnp.dot(ya_ref[...], wa_ref[...], preferred_element_type=F32)
    o_ref[...] = (jax.nn.sigmoid(ga_ref[...].astype(F32)) * a
                  + jax.nn.sigmoid(gb_ref[...].astype(F32)) * b).astype(o_ref.dtype)


def _merge(y_ssm, y_att, w_ps, w_pa, proj, tm=1024, tn=2048):
    t, k = y_ssm.shape
    n = w_ps.shape[1]
    ga_off = (SSM_WIDTH + 3 * ATTN_WIDTH) // tn
    gb_off = ga_off + D_MODEL // tn
    return pl.pallas_call(
        _merge_kernel,
        out_shape=jax.ShapeDtypeStruct((t, n), BF16),
        grid=(t // tm, n // tn),
        in_specs=[pl.BlockSpec((tm, k), lambda i, j: (i, 0)),
                  pl.BlockSpec((tm, k), lambda i, j: (i, 0)),
                  pl.BlockSpec((k, tn), lambda i, j: (0, j), pipeline_mode=_RESIDENT),
                  pl.BlockSpec((k, tn), lambda i, j: (0, j), pipeline_mode=_RESIDENT),
                  pl.BlockSpec((tm, tn), lambda i, j: (i, ga_off + j)),
                  pl.BlockSpec((tm, tn), lambda i, j: (i, gb_off + j))],
        out_specs=pl.BlockSpec((tm, tn), lambda i, j: (i, j)),
        compiler_params=_cparams(("parallel", "parallel")),
        name="merge",
    )(y_ssm, y_att, w_ps, w_pa, proj, proj)


def _resid_mm_kernel(a_ref, w_ref, x_ref, g_ref, o_ref):
    acc = jnp.dot(a_ref[...], w_ref[...], preferred_element_type=F32)
    o_ref[...] = x_ref[...] + g_ref[0] * acc


def _resid_mm(a, w, x2d, mod4, g_idx, seq_len, tm=1024, tn=2048):
    t, k = a.shape
    n = w.shape[1]
    per_b = seq_len // tm
    return pl.pallas_call(
        _resid_mm_kernel,
        out_shape=jax.ShapeDtypeStruct((t, n), F32),
        grid=(t // tm, n // tn),
        in_specs=[pl.BlockSpec((tm, k), lambda i, j: (i, 0)),
                  pl.BlockSpec((k, tn), lambda i, j: (0, j), pipeline_mode=_RESIDENT),
                  pl.BlockSpec((tm, tn), lambda i, j: (i, j)),
                  pl.BlockSpec((1, 1, tn), lambda i, j: ((i // per_b) * N_MOD + g_idx, 0, j))],
        out_specs=pl.BlockSpec((tm, tn), lambda i, j: (i, j)),
        compiler_params=_cparams(("parallel", "parallel")),
        name="resid_mm",
    )(a, w, x2d, mod4)


def _ff2_kernel(h_ref, w_ref, x_hbm, g_ref, gf_ref, o_ref, x_ref, x_sem, *, rows):
    i = pl.program_id(0)
    k = pl.program_id(1)
    last = pl.num_programs(1) - 1
    tm = o_ref.shape[0]

    def x_copy():
        return pltpu.make_async_copy(x_hbm.at[pl.ds(pl.multiple_of(i * tm, tm), tm), :],
                                     x_ref, x_sem)

    @pl.when(k == 0)
    def _():
        x_copy().start()
        o_ref[...] = jnp.dot(h_ref[...], w_ref[...], preferred_element_type=F32)

    @pl.when((k != 0) & (k != last))
    def _():
        o_ref[...] += jnp.dot(h_ref[...], w_ref[...], preferred_element_type=F32)

    @pl.when(k == last)
    def _():
        x_copy().wait()
        for r in range(0, tm, rows):
            acc = o_ref[r:r + rows, :] + jnp.dot(h_ref[r:r + rows, :], w_ref[...],
                                                 preferred_element_type=F32)
            xo = x_ref[r:r + rows, :] + g_ref[0] * acc
            ms = jnp.mean(xo * xo, axis=-1, keepdims=True)
            o_ref[r:r + rows, :] = xo * lax.rsqrt(ms + EPS) * gf_ref[...]


def _ff2(hid, w, x2d, mod4, g_idx, gf, seq_len, tm=1024, tk=2048):
    t, kdim = hid.shape
    n = w.shape[1]
    per_b = seq_len // tm
    assert kdim // tk >= 2, "the residual copy is started and waited in different steps"
    return pl.pallas_call(
        functools.partial(_ff2_kernel, rows=256),
        out_shape=jax.ShapeDtypeStruct((t, n), F32),
        grid=(t // tm, kdim // tk),
        in_specs=[pl.BlockSpec((tm, tk), lambda i, k: (i, k)),
                  pl.BlockSpec((tk, n), lambda i, k: (k, 0)),
                  pl.BlockSpec(memory_space=pl.ANY),
                  pl.BlockSpec((1, 1, n), lambda i, k: ((i // per_b) * N_MOD + g_idx, 0, 0)),
                  pl.BlockSpec((1, n), lambda i, k: (0, 0))],
        out_specs=pl.BlockSpec((tm, n), lambda i, k: (i, 0)),
        scratch_shapes=[pltpu.VMEM((tm, n), F32), pltpu.SemaphoreType.DMA(())],
        compiler_params=_cparams(("arbitrary", "arbitrary")),
        name="ff2_final",
    )(hid, w, x2d, mod4, gf.reshape(1, n))


def kernel(x, c, rel_bias, w_ada, b_ada, norm_mix_g, w_in, ssm_a_re, ssm_a_im, ssm_log_dt,
           ssm_b_re, ssm_b_im, ssm_c_re, ssm_c_im, ssm_d, w_glu, b_glu, w_proj_ssm,
           w_proj_attn, w_out, norm_mlp_g, w_ff1, w_ff2, norm_final_g):
    bsz, seq_len, d = x.shape
    depth = w_in.shape[0]
    assert depth == 1, "the final rms_norm is fused into the single layer's ff2 kernel"
    t = bsz * seq_len
    x2d = x.reshape(t, d)
    bias_tiles, w_in_b = _bias_tiles(rel_bias, w_in[0])

    for l in range(depth):
        mod4 = _mod(c, w_ada[l], b_ada[l]).reshape(bsz * N_MOD, 1, d)

        proj, w_glu_b, w_ps_b, w_pa_b, w_out_b = _norm_mm(
            x2d, norm_mix_g[l], mod4, 1, 0, w_in_b, seq_len, act="none",
            side_casts=(w_glu[l], w_proj_ssm[l], w_proj_attn[l], w_out[l]))

        at, bw, cw, kw = _s5prep(ssm_a_re[l], ssm_a_im[l], ssm_log_dt[l],
                                 ssm_b_re[l], ssm_b_im[l], ssm_c_re[l], ssm_c_im[l])
        bmat, cmat, kmat, atv, d2 = _s5_weights(at, bw, cw, kw, ssm_d[l], bsz)
        y = _s5(proj.reshape(bsz, seq_len, IN_WIDTH), bmat, cmat, kmat, atv, d2)
        y_ssm = _glu(y.reshape(t, SSM_WIDTH), w_glu_b, b_glu[l])

        y_att, w_ff1_b, w_ff2_b = _moba(proj, bias_tiles, bsz, seq_len,
                                        side_casts=(w_ff1[l], w_ff2[l]))

        merged = _merge(y_ssm, y_att, w_ps_b, w_pa_b, proj)
        x2d = _resid_mm(merged, w_out_b, x2d, mod4, 2, seq_len)

        hid = _norm_mm(x2d, norm_mlp_g[l], mod4, 4, 3, w_ff1_b, seq_len, act="relu2")
        x2d = _ff2(hid, w_ff2_b, x2d, mod4, 5, norm_final_g, seq_len)
    return x2d.reshape(bsz, seq_len, d)
```

```python
import functools
import itertools
import math

import jax
import jax.numpy as jnp
from jax import lax
from jax.experimental import pallas as pl
from jax.experimental.pallas import tpu as pltpu

F32 = jnp.float32
BF16 = jnp.bfloat16

D_MODEL = 2048
SSM_WIDTH = 1024
SSM_GROUP = 16
SSM_GROUPS = 64
SSM_STATE = 64
ATTN_HEADS = 8
HEAD_DIM = 128
ATTN_WIDTH = 1024
MOBA_BLOCK = 256
MOBA_TOPK = 3
REL_BUCKETS = 32
REL_MAX_DIST = 128
D_FF = 4 * D_MODEL
N_MOD = 6
EPS = 1e-6
NEG_INF = -1e30
LOG2E = math.log2(math.e)
IN_WIDTH = SSM_WIDTH + 3 * ATTN_WIDTH + 2 * D_MODEL

VMEM_LIMIT_BYTES = 56 * 1024 * 1024

S5_TAU = 2
S5_GROUPS = 8
S5_CH = S5_GROUPS * SSM_GROUP
S5_UC = S5_TAU * S5_CH
S5_SC = 2 * S5_GROUPS * SSM_STATE
S5_NBLK = SSM_GROUPS // S5_GROUPS
S5_ROWS = 4096
S5_SUB_ROWS = 256


_RESIDENT = pl.Buffered(1)


def _cparams(sem):
    return pltpu.CompilerParams(dimension_semantics=sem,
                                vmem_limit_bytes=VMEM_LIMIT_BYTES)


def _mod_kernel(c_ref, w_ref, b_ref, o_ref):
    c = c_ref[...]
    ca = (c * jax.nn.sigmoid(c)).astype(BF16)
    o_ref[...] = jnp.dot(ca, w_ref[...].astype(BF16),
                         preferred_element_type=F32) + b_ref[...]


def _mod(c, w_ada, b_ada):
    bsz, d = c.shape
    n = w_ada.shape[1]
    tn = 1024
    return pl.pallas_call(
        _mod_kernel,
        out_shape=jax.ShapeDtypeStruct((bsz, n), F32),
        grid=(n // tn,),
        in_specs=[pl.BlockSpec((bsz, d), lambda j: (0, 0)),
                  pl.BlockSpec((d, tn), lambda j: (0, j)),
                  pl.BlockSpec((1, tn), lambda j: (0, j))],
        out_specs=pl.BlockSpec((bsz, tn), lambda j: (0, j)),
        compiler_params=_cparams(("parallel",)),
        name="mod",
    )(c, w_ada, b_ada.reshape(1, n))


def _side_casts(arrays, steps, step_of):
    specs, shapes = [], []
    for a in arrays:
        r, c = a.shape
        rb = max(16, r // steps)
        nblk = r // rb
        assert r % rb == 0 and steps % nblk == 0
        specs.append(pl.BlockSpec(
            (rb, c), lambda *ids, nblk=nblk: ((step_of(*ids) * nblk) // steps, 0)))
        shapes.append(jax.ShapeDtypeStruct((r, c), BF16))
    return specs, shapes


def _cast_sides(side_in, side_out):
    for src, dst in zip(side_in, side_out):
        dst[...] = src[...].astype(dst.dtype)


def _norm_mm_kernel(*refs, act, rows, n_side):
    x_ref, g_ref, sc_ref, sh_ref, w_ref = refs[:5]
    side_in = refs[5:5 + n_side]
    o_ref = refs[5 + n_side]
    side_out = refs[6 + n_side:6 + 2 * n_side]
    h_ref = refs[-1]

    def mm(h):
        acc = jnp.dot(h, w_ref[...], preferred_element_type=F32)
        if act == "relu2":
            acc = jnp.square(jnp.maximum(acc, 0.0))
        return acc.astype(o_ref.dtype)

    @pl.when(pl.program_id(1) == 0)
    def _():
        g = g_ref[...]
        sc = 1.0 + sc_ref[0]
        sh = sh_ref[0]
        tm = x_ref.shape[0]
        for r in range(0, tm, rows):
            x = x_ref[r:r + rows, :]
            ms = jnp.mean(x * x, axis=-1, keepdims=True)
            y = x * lax.rsqrt(ms + EPS) * g
            h = (y * sc + sh).astype(BF16)
            h_ref[r:r + rows, :] = h
            o_ref[r:r + rows, :] = mm(h)

    @pl.when(pl.program_id(1) != 0)
    def _():
        o_ref[...] = mm(h_ref[...])

    _cast_sides(side_in, side_out)


def _norm_mm(x2d, g, mod4, sc_idx, sh_idx, w, seq_len, *, act, side_casts=(),
             tm=1024, tn=2048):
    t, d = x2d.shape
    n = w.shape[1]
    per_b = seq_len // tm
    nj = n // tn
    side_specs, side_shapes = _side_casts(side_casts, (t // tm) * nj, lambda i, j: i * nj + j)
    res = pl.pallas_call(
        functools.partial(_norm_mm_kernel, act=act, rows=256, n_side=len(side_casts)),
        out_shape=tuple([jax.ShapeDtypeStruct((t, n), BF16)] + side_shapes),
        grid=(t // tm, nj),
        in_specs=[pl.BlockSpec((tm, d), lambda i, j: (i, 0)),
                  pl.BlockSpec((1, d), lambda i, j: (0, 0)),
                  pl.BlockSpec((1, 1, d), lambda i, j: ((i // per_b) * N_MOD + sc_idx, 0, 0)),
                  pl.BlockSpec((1, 1, d), lambda i, j: ((i // per_b) * N_MOD + sh_idx, 0, 0)),
                  pl.BlockSpec((d, tn), lambda i, j: (0, j))] + side_specs,
        out_specs=tuple([pl.BlockSpec((tm, tn), lambda i, j: (i, j))] + side_specs),
        scratch_shapes=[pltpu.VMEM((tm, d), BF16)],
        compiler_params=_cparams(("arbitrary", "arbitrary")),
        name="norm_mm_" + act,
    )(x2d, g.reshape(1, d), mod4, mod4, w, *side_casts)
    return res if len(res) > 1 else res[0]


def _s5prep_kernel(are_ref, aim_ref, ldt_ref, bre_ref, bim_ref, cre_ref, cim_ref,
                   at_ref, bw_ref, cw_ref, kw_ref):
    tau = bw_ref.shape[0]
    a_re = are_ref[...]
    a_im = aim_ref[...]
    dt = jnp.exp(ldt_ref[...])
    mag = jnp.exp(dt * a_re)
    abar_re = mag * jnp.cos(dt * a_im)
    abar_im = mag * jnp.sin(dt * a_im)
    den = a_re * a_re + a_im * a_im
    p_re = abar_re - 1.0
    f_re = (p_re * a_re + abar_im * a_im) / den
    f_im = (abar_im * a_re - p_re * a_im) / den

    def cmul(x, y):
        return x[0] * y[0] - x[1] * y[1], x[0] * y[1] + x[1] * y[0]

    def cat(x):
        return jnp.concatenate([x[0], x[1]], axis=-1)

    def re_dot(x):
        return jnp.concatenate([x[0], -x[1]], axis=-1)

    abar = (abar_re, abar_im)
    bb = cmul((f_re, f_im), (bre_ref[...], bim_ref[...]))
    cc = (cre_ref[...], cim_ref[...])
    apow = [None, abar]
    for _ in range(tau - 1):
        apow.append(cmul(apow[-1], abar))
    at_ref[...] = cat(apow[tau])
    y0 = cat(bb)
    for i in range(tau):
        j = tau - 1 - i
        bw_ref[i] = y0 if j == 0 else cat(cmul(apow[j], bb))
        cw_ref[i] = re_dot(cmul(cc, apow[i + 1]))
        x = re_dot(cc if i == 0 else cmul(cc, apow[i]))
        kw_ref[i] = jnp.einsum("gpn,gqn->gpq", x, y0, precision=lax.Precision.HIGHEST,
                               preferred_element_type=F32)


def _s5prep(a_re, a_im, log_dt, b_re, b_im, c_re, c_im):
    g, n = a_re.shape
    p = b_re.shape[2]
    b_re_t = jnp.transpose(b_re, (0, 2, 1))
    b_im_t = jnp.transpose(b_im, (0, 2, 1))
    return pl.pallas_call(
        _s5prep_kernel,
        out_shape=(jax.ShapeDtypeStruct((g, 1, 2 * n), F32),
                   jax.ShapeDtypeStruct((S5_TAU, g, p, 2 * n), F32),
                   jax.ShapeDtypeStruct((S5_TAU, g, p, 2 * n), F32),
                   jax.ShapeDtypeStruct((S5_TAU, g, p, p), F32)),
        name="s5prep",
    )(a_re.reshape(g, 1, n), a_im.reshape(g, 1, n), log_dt.reshape(g, 1, 1),
      b_re_t, b_im_t, c_re, c_im)


def _block_diag(x):
    nb, ng, r, c = x.shape
    eye = jnp.eye(ng, dtype=x.dtype)
    return jnp.einsum("agrc,gh->agrhc", x, eye).reshape(nb, ng * r, ng * c)


def _s5_weights(at, bw, cw, kw, d_skip, bsz):
    n = SSM_STATE
    blk = (S5_NBLK, S5_GROUPS)

    def bd(x):
        return _block_diag(x.reshape(blk + x.shape[1:]))

    def bd_t(x):
        return bd(jnp.transpose(x, (0, 2, 1)))

    bmat = jnp.concatenate(
        [jnp.concatenate([bd(bw[i, :, :, :n]), bd(bw[i, :, :, n:])], axis=-1)
         for i in range(S5_TAU)], axis=1).astype(BF16)
    cmat = jnp.concatenate(
        [jnp.concatenate([bd_t(cw[i, :, :, :n]), bd_t(cw[i, :, :, n:])], axis=1)
         for i in range(S5_TAU)], axis=-1).astype(BF16)
    lag = [bd_t(kw[i]) for i in range(S5_TAU)]
    zero = jnp.zeros_like(lag[0])
    kmat = jnp.concatenate(
        [jnp.concatenate([lag[i - ip] if ip <= i else zero for i in range(S5_TAU)], axis=-1)
         for ip in range(S5_TAU)], axis=1).astype(BF16)
    half = S5_SC // 2
    atv = jnp.concatenate([at[:, 0, :n].reshape(S5_NBLK, half),
                           at[:, 0, n:].reshape(S5_NBLK, half)], axis=-1)
    atv = jnp.broadcast_to(atv.reshape(S5_NBLK, 1, S5_SC), (S5_NBLK, bsz, S5_SC))
    d = jnp.broadcast_to(d_skip.reshape(S5_NBLK, 1, S5_CH), (S5_NBLK, S5_TAU, S5_CH))
    return bmat, cmat, kmat, atv, d.reshape(S5_NBLK, 1, S5_UC)


def _s5_kernel(x_ref, bw_ref, cw_ref, kw_ref, a_ref, d_ref, y_ref,
               xf_ref, u_ref, w_ref, s_ref, st_ref, *, bsz, n_sub):
    @pl.when(pl.program_id(1) == 0)
    def _():
        st_ref[...] = jnp.zeros_like(st_ref)

    half = S5_SC // 2
    a_re = a_ref[0, :, :half]
    a_im = a_ref[0, :, half:]
    s_re = st_ref[:, :half]
    s_im = st_ref[:, half:]
    d2 = d_ref[0]
    sub = u_ref.shape[0] // n_sub
    ksub = sub // bsz

    def slot(ref, c, i):
        return ref.at[:, pl.ds(c * ksub * S5_TAU + i, ksub, stride=S5_TAU), :]

    def project_in(c):
        t0 = c * ksub * S5_TAU
        xf_ref[:, t0:t0 + ksub * S5_TAU, :] = x_ref[:, t0:t0 + ksub * S5_TAU, :].astype(F32)
        for i in range(S5_TAU):
            x = pltpu.einshape("bkc->kbc", slot(xf_ref, c, i)[...]).reshape(sub, S5_CH)
            u_ref[c * sub:(c + 1) * sub, i * S5_CH:(i + 1) * S5_CH] = x.astype(BF16)
        w_ref[c * sub:(c + 1) * sub, :] = jnp.dot(u_ref[c * sub:(c + 1) * sub, :], bw_ref[0],
                                                  preferred_element_type=F32)

    project_in(0)
    for c in range(n_sub):
        r0 = c * sub
        if c + 1 < n_sub:
            project_in(c + 1)
        u = u_ref[r0:r0 + sub, :]
        for k in range(0, ksub, 2):
            before = []
            for kk in range(2):
                r = r0 + (k + kk) * bsz
                before.append(jnp.concatenate([s_re, s_im], axis=1))
                w = w_ref[r:r + bsz, :]
                s_re, s_im = (a_re * s_re - a_im * s_im + w[:, :half],
                              a_re * s_im + a_im * s_re + w[:, half:])
            s_ref[r0 + k * bsz:r0 + (k + 2) * bsz, :] = (
                jnp.concatenate(before, axis=0).astype(BF16))
        y = (jnp.dot(s_ref[r0:r0 + sub, :], cw_ref[0], preferred_element_type=F32)
             + jnp.dot(u, kw_ref[0], preferred_element_type=F32)
             + d2 * u.astype(F32))
        y = jax.nn.gelu(y)
        for i in range(S5_TAU):
            slot(y_ref, c, i)[...] = pltpu.einshape(
                "kbc->bkc", y[:, i * S5_CH:(i + 1) * S5_CH].reshape(ksub, bsz, S5_CH))
    st_ref[:, :half] = s_re
    st_ref[:, half:] = s_im


def _s5(proj, bmat, cmat, kmat, atv, d2):
    assert S5_CH == 128, "a block's channels must be one lane tile"
    bsz, seq_len, _ = proj.shape
    tpc = S5_ROWS // bsz * S5_TAU
    return pl.pallas_call(
        functools.partial(_s5_kernel, bsz=bsz, n_sub=S5_ROWS // S5_SUB_ROWS),
        out_shape=jax.ShapeDtypeStruct((bsz, seq_len, SSM_WIDTH), F32),
        grid=(S5_NBLK, seq_len // tpc),
        in_specs=[
            pl.BlockSpec((bsz, tpc, S5_CH), lambda g, c: (0, c, g)),
            pl.BlockSpec((1, S5_UC, S5_SC), lambda g, c: (g, 0, 0)),
            pl.BlockSpec((1, S5_SC, S5_UC), lambda g, c: (g, 0, 0)),
            pl.BlockSpec((1, S5_UC, S5_UC), lambda g, c: (g, 0, 0)),
            pl.BlockSpec((1, bsz, S5_SC), lambda g, c: (g, 0, 0)),
            pl.BlockSpec((1, 1, S5_UC), lambda g, c: (g, 0, 0))],
        out_specs=pl.BlockSpec((bsz, tpc, S5_CH), lambda g, c: (0, c, g)),
        scratch_shapes=[pltpu.VMEM((bsz, tpc, S5_CH), F32),
                        pltpu.VMEM((S5_ROWS, S5_UC), BF16),
                        pltpu.VMEM((S5_ROWS, S5_SC), F32),
                        pltpu.VMEM((S5_ROWS, S5_SC), BF16),
                        pltpu.VMEM((bsz, S5_SC), F32)],
        compiler_params=_cparams(("parallel", "arbitrary")),
        name="s5",
    )(proj, bmat, cmat, kmat, atv, d2)


def _glu_kernel(y_ref, w_ref, b_ref, o_ref):
    y = y_ref[...]
    z = jnp.dot(y.astype(BF16), w_ref[...], preferred_element_type=F32) + b_ref[...]
    o_ref[...] = (y * jax.nn.sigmoid(z)).astype(o_ref.dtype)


def _glu(y, w, b, tm=1024):
    t, n = y.shape
    return pl.pallas_call(
        _glu_kernel,
        out_shape=jax.ShapeDtypeStruct((t, n), BF16),
        grid=(t // tm,),
        in_specs=[pl.BlockSpec((tm, n), lambda i: (i, 0)),
                  pl.BlockSpec((n, n), lambda i: (0, 0)),
                  pl.BlockSpec((1, n), lambda i: (0, 0))],
        out_specs=pl.BlockSpec((tm, n), lambda i: (i, 0)),
        compiler_params=_cparams(("parallel",)),
        name="glu",
    )(y, w, b.reshape(1, n))


def _t5_bucket(rel):
    n = jnp.maximum(rel, 0)
    max_exact = REL_BUCKETS // 2
    nf = jnp.maximum(n, 1).astype(F32)
    large = max_exact + (jnp.log(nf / max_exact) / math.log(REL_MAX_DIST / max_exact)
                         * (REL_BUCKETS - max_exact)).astype(jnp.int32)
    large = jnp.minimum(large, REL_BUCKETS - 1)
    return jnp.where(n < max_exact, n, large)


def _bias_kernel(relb_ref, bucket_ref, w_ref, o_ref, wb_ref):
    h = pl.program_id(0)
    bucket = bucket_ref[0]
    out = jnp.zeros(bucket.shape, F32)
    for b in range(REL_BUCKETS):
        out = jnp.where(bucket == b, relb_ref[b * ATTN_HEADS + h], out)
    o_ref[0, 0] = out * LOG2E
    wb_ref[...] = w_ref[...].astype(wb_ref.dtype)


def _bias_tiles(rel_bias, w_in):
    i = jnp.arange(MOBA_BLOCK)
    rel = (jnp.arange(3)[:, None, None] * MOBA_BLOCK + i[None, None, :] - i[None, :, None])
    bucket = _t5_bucket(rel)
    n_d = 4
    rows, cols = w_in.shape
    rb = rows // (ATTN_HEADS * n_d)
    w_spec = pl.BlockSpec((rb, cols), lambda h, d: (h * n_d + d, 0))
    return pl.pallas_call(
        _bias_kernel,
        out_shape=(jax.ShapeDtypeStruct((ATTN_HEADS, 3, MOBA_BLOCK, MOBA_BLOCK), F32),
                   jax.ShapeDtypeStruct((rows, cols), BF16)),
        grid=(ATTN_HEADS, n_d),
        in_specs=[pl.BlockSpec(memory_space=pltpu.SMEM),
                  pl.BlockSpec((1, MOBA_BLOCK, MOBA_BLOCK), lambda h, d: (jnp.minimum(d, 2), 0, 0)),
                  w_spec],
        out_specs=(pl.BlockSpec((1, 1, MOBA_BLOCK, MOBA_BLOCK),
                                lambda h, d: (h, jnp.minimum(d, 2), 0, 0)),
                   w_spec),
        compiler_params=_cparams(("arbitrary", "arbitrary")),
        name="bias_tiles",
    )(rel_bias.astype(F32).reshape(-1), bucket, w_in)


_NT = (((1,), (1,)), ((), ()))


def _moba_scores(own, slot0, state, q_ref, k_ref, bias_ref, km_ref, s_ref):
    bs = MOBA_BLOCK
    q = q_ref[own * bs:(own + 1) * bs, :]

    ranks = []
    if own > 0:
        gate = lax.dot_general(km_ref[...].astype(BF16), q, _NT, preferred_element_type=F32)
        blk = lax.broadcasted_iota(jnp.int32, gate.shape, 0)
        gm = jnp.where(blk < own, gate, NEG_INF)
        for n in range(own):
            g_n = gm[n:n + 1, :]
            ge = jnp.where(gm >= g_n, 1.0, 0.0)
            gt = jnp.where(gm > g_n, 1.0, 0.0)
            ranks.append(jnp.sum(jnp.where(blk < n, ge, gt), axis=0, keepdims=True))

    c1 = HEAD_DIM ** -0.5 * LOG2E
    m = None
    for idx, j in enumerate([own] + list(range(own))):
        k_j = k_ref[j * bs:(j + 1) * bs, :]
        raw = lax.dot_general(k_j, q, _NT, preferred_element_type=F32)
        if j == own:
            key = lax.broadcasted_iota(jnp.int32, (bs, bs), 0)
            qry = lax.broadcasted_iota(jnp.int32, (bs, bs), 1)
            s = jnp.where(key <= qry, raw * c1 + bias_ref[0], NEG_INF)
        elif own - j == 1:
            s = jnp.where(ranks[j] < float(MOBA_TOPK), raw * c1 + bias_ref[1], NEG_INF)
        else:
            row = jnp.where(ranks[j] < float(MOBA_TOPK), bias_ref[2, 0:1, :], NEG_INF)
            s = raw * c1 + row
        s_ref[slot0 + idx] = s
        m_j = jnp.max(s, axis=0, keepdims=True)
        m = m_j if m is None else jnp.maximum(m, m_j)
        state["m"] = m
        yield


def _moba_values(own, slot0, state, vt_ref, o_ref, s_ref):
    bs = MOBA_BLOCK
    m = state["m"]
    acc = None
    for idx, j in enumerate([own] + list(range(own))):
        p = jnp.exp2((s_ref[slot0 + idx] - m).astype(BF16))
        a_j = jnp.dot(vt_ref[:, j * bs:(j + 1) * bs], p, preferred_element_type=F32)
        acc = a_j if acc is None else acc + a_j
        if idx == own:
            o = acc[:HEAD_DIM, :] / acc[HEAD_DIM:HEAD_DIM + 1, :]
            o_ref[own * bs:(own + 1) * bs, :] = o.T.astype(o_ref.dtype)
        yield


def _moba_kernel(*refs, n_blk, n_side):
    q_ref, k_ref, v_ref, bias_ref = refs[:4]
    side_in = refs[4:4 + n_side]
    o_ref = refs[4 + n_side]
    side_out = refs[5 + n_side:5 + 2 * n_side]
    km_ref, vt_ref, s_ref = refs[-3:]
    bs = MOBA_BLOCK
    heads = bias_ref.shape[0]
    _cast_sides(side_in, side_out)

    def head(ref, hh):
        return ref.at[:, hh * HEAD_DIM:(hh + 1) * HEAD_DIM]

    for hh in range(heads):
        vt_ref[hh, HEAD_DIM:, :] = jnp.ones((vt_ref.shape[1] - HEAD_DIM, vt_ref.shape[2]), BF16)
        for n in range(n_blk):
            kb = head(k_ref, hh)[n * bs:(n + 1) * bs, :].astype(F32)
            km_ref[hh, n:n + 1, :] = jnp.mean(kb, axis=0, keepdims=True)
            vt_ref[hh, :HEAD_DIM, n * bs:(n + 1) * bs] = (
                head(v_ref, hh)[n * bs:(n + 1) * bs, :].astype(F32).T.astype(BF16))

    per_head = n_blk * (n_blk + 1) // 2
    pending = iter(())
    for own in range(n_blk):
        for hh in range(heads):
            slot0 = hh * per_head + own * (own + 1) // 2
            state = {}
            scores = _moba_scores(own, slot0, state, head(q_ref, hh), head(k_ref, hh),
                                  bias_ref.at[hh], km_ref.at[hh], s_ref)
            for _ in itertools.zip_longest(scores, pending):
                pass
            pending = _moba_values(own, slot0, state, vt_ref.at[hh], head(o_ref, hh), s_ref)
    for _ in pending:
        pass


def _moba(proj, bias_tiles, bsz, seq_len, side_casts=(), heads=2):
    n_blk = seq_len // MOBA_BLOCK
    hw = heads * HEAD_DIM
    q_off = SSM_WIDTH // hw
    k_off = q_off + ATTN_WIDTH // hw
    v_off = k_off + ATTN_WIDTH // hw
    n_slots = heads * n_blk * (n_blk + 1) // 2
    hg = ATTN_HEADS // heads
    side_specs, side_shapes = _side_casts(side_casts, bsz * hg, lambda b, h: b * hg + h)
    res = pl.pallas_call(
        functools.partial(_moba_kernel, n_blk=n_blk, n_side=len(side_casts)),
        out_shape=tuple([jax.ShapeDtypeStruct((bsz * seq_len, ATTN_WIDTH), BF16)] + side_shapes),
        grid=(bsz, hg),
        in_specs=[pl.BlockSpec((seq_len, hw), lambda b, h: (b, q_off + h)),
                  pl.BlockSpec((seq_len, hw), lambda b, h: (b, k_off + h)),
                  pl.BlockSpec((seq_len, hw), lambda b, h: (b, v_off + h)),
                  pl.BlockSpec((heads, 3, MOBA_BLOCK, MOBA_BLOCK), lambda b, h: (h, 0, 0, 0))]
        + side_specs,
        out_specs=tuple([pl.BlockSpec((seq_len, hw), lambda b, h: (b, h))] + side_specs),
        scratch_shapes=[pltpu.VMEM((heads, n_blk, HEAD_DIM), F32),
                        pltpu.VMEM((heads, HEAD_DIM + 16, seq_len), BF16),
                        pltpu.VMEM((n_slots, MOBA_BLOCK, MOBA_BLOCK), F32)],
        compiler_params=_cparams(("arbitrary", "arbitrary")),
        name="moba",
    )(proj, proj, proj, bias_tiles, *side_casts)
    return res if len(res) > 1 else res[0]


def _merge_kernel(ys_ref, ya_ref, ws_ref, wa_ref, ga_ref, gb_ref, o_ref):
    a = jnp.dot(ys_ref[...], ws_ref[...], preferred_element_type=F32)
    b = jnp.dot(ya_ref[...], wa_ref[...], preferred_element_type=F32)
    o_ref[...] = (jax.nn.sigmoid(ga_ref[...].astype(F32)) * a
                  + jax.nn.sigmoid(gb_ref[...].astype(F32)) * b).astype(o_ref.dtype)


def _merge(y_ssm, y_att, w_ps, w_pa, proj, tm=1024, tn=2048):
    t, k = y_ssm.shape
    n = w_ps.shape[1]
    ga_off = (SSM_WIDTH + 3 * ATTN_WIDTH) // tn
    gb_off = ga_off + D_MODEL // tn
    return pl.pallas_call(
        _merge_kernel,
        out_shape=jax.ShapeDtypeStruct((t, n), BF16),
        grid=(t // tm, n // tn),
        in_specs=[pl.BlockSpec((tm, k), lambda i, j: (i, 0)),
                  pl.BlockSpec((tm, k), lambda i, j: (i, 0)),
                  pl.BlockSpec((k, tn), lambda i, j: (0, j), pipeline_mode=_RESIDENT),
                  pl.BlockSpec((k, tn), lambda i, j: (0, j), pipeline_mode=_RESIDENT),
                  pl.BlockSpec((tm, tn), lambda i, j: (i, ga_off + j)),
                  pl.BlockSpec((tm, tn), lambda i, j: (i, gb_off + j))],
        out_specs=pl.BlockSpec((tm, tn), lambda i, j: (i, j)),
        compiler_params=_cparams(("parallel", "parallel")),
        name="merge",
    )(y_ssm, y_att, w_ps, w_pa, proj, proj)


def _resid_mm_kernel(a_ref, w_ref, x_ref, g_ref, o_ref):
    acc = jnp.dot(a_ref[...], w_ref[...], preferred_element_type=F32)
    o_ref[...] = x_ref[...] + g_ref[0] * acc


def _resid_mm(a, w, x2d, mod4, g_idx, seq_len, tm=1024, tn=2048):
    t, k = a.shape
    n = w.shape[1]
    per_b = seq_len // tm
    return pl.pallas_call(
        _resid_mm_kernel,
        out_shape=jax.ShapeDtypeStruct((t, n), F32),
        grid=(t // tm, n // tn),
        in_specs=[pl.BlockSpec((tm, k), lambda i, j: (i, 0)),
                  pl.BlockSpec((k, tn), lambda i, j: (0, j), pipeline_mode=_RESIDENT),
                  pl.BlockSpec((tm, tn), lambda i, j: (i, j)),
                  pl.BlockSpec((1, 1, tn), lambda i, j: ((i // per_b) * N_MOD + g_idx, 0, j))],
        out_specs=pl.BlockSpec((tm, tn), lambda i, j: (i, j)),
        compiler_params=_cparams(("parallel", "parallel")),
        name="resid_mm",
    )(a, w, x2d, mod4)


def _ff2_kernel(h_ref, w_ref, x_hbm, g_ref, gf_ref, o_ref, x_ref, x_sem, *, rows):
    i = pl.program_id(0)
    k = pl.program_id(1)
    last = pl.num_programs(1) - 1
    tm = o_ref.shape[0]

    def x_copy():
        return pltpu.make_async_copy(x_hbm.at[pl.ds(pl.multiple_of(i * tm, tm), tm), :],
                                     x_ref, x_sem)

    @pl.when(k == 0)
    def _():
        x_copy().start()
        o_ref[...] = jnp.dot(h_ref[...], w_ref[...], preferred_element_type=F32)

    @pl.when((k != 0) & (k != last))
    def _():
        o_ref[...] += jnp.dot(h_ref[...], w_ref[...], preferred_element_type=F32)

    @pl.when(k == last)
    def _():
        x_copy().wait()
        for r in range(0, tm, rows):
            acc = o_ref[r:r + rows, :] + jnp.dot(h_ref[r:r + rows, :], w_ref[...],
                                                 preferred_element_type=F32)
            xo = x_ref[r:r + rows, :] + g_ref[0] * acc
            ms = jnp.mean(xo * xo, axis=-1, keepdims=True)
            o_ref[r:r + rows, :] = xo * lax.rsqrt(ms + EPS) * gf_ref[...]


def _ff2(hid, w, x2d, mod4, g_idx, gf, seq_len, tm=1024, tk=2048):
    t, kdim = hid.shape
    n = w.shape[1]
    per_b = seq_len // tm
    assert kdim // tk >= 2, "the residual copy is started and waited in different steps"
    return pl.pallas_call(
        functools.partial(_ff2_kernel, rows=256),
        out_shape=jax.ShapeDtypeStruct((t, n), F32),
        grid=(t // tm, kdim // tk),
        in_specs=[pl.BlockSpec((tm, tk), lambda i, k: (i, k)),
                  pl.BlockSpec((tk, n), lambda i, k: (k, 0)),
                  pl.BlockSpec(memory_space=pl.ANY),
                  pl.BlockSpec((1, 1, n), lambda i, k: ((i // per_b) * N_MOD + g_idx, 0, 0)),
                  pl.BlockSpec((1, n), lambda i, k: (0, 0))],
        out_specs=pl.BlockSpec((tm, n), lambda i, k: (i, 0)),
        scratch_shapes=[pltpu.VMEM((tm, n), F32), pltpu.SemaphoreType.DMA(())],
        compiler_params=_cparams(("arbitrary", "arbitrary")),
        name="ff2_final",
    )(hid, w, x2d, mod4, gf.reshape(1, n))


def kernel(x, c, rel_bias, w_ada, b_ada, norm_mix_g, w_in, ssm_a_re, ssm_a_im, ssm_log_dt,
           ssm_b_re, ssm_b_im, ssm_c_re, ssm_c_im, ssm_d, w_glu, b_glu, w_proj_ssm,
           w_proj_attn, w_out, norm_mlp_g, w_ff1, w_ff2, norm_final_g):
    bsz, seq_len, d = x.shape
    depth = w_in.shape[0]
    assert depth == 1, "the final rms_norm is fused into the single layer's ff2 kernel"
    t = bsz * seq_len
    x2d = x.reshape(t, d)
    bias_tiles, w_in_b = _bias_tiles(rel_bias, w_in[0])

    for l in range(depth):
        mod4 = _mod(c, w_ada[l], b_ada[l]).reshape(bsz * N_MOD, 1, d)

        proj, w_glu_b, w_ps_b, w_pa_b, w_out_b = _norm_mm(
            x2d, norm_mix_g[l], mod4, 1, 0, w_in_b, seq_len, act="none",
            side_casts=(w_glu[l], w_proj_ssm[l], w_proj_attn[l], w_out[l]))

        at, bw, cw, kw = _s5prep(ssm_a_re[l], ssm_a_im[l], ssm_log_dt[l],
                                 ssm_b_re[l], ssm_b_im[l], ssm_c_re[l], ssm_c_im[l])
        bmat, cmat, kmat, atv, d2 = _s5_weights(at, bw, cw, kw, ssm_d[l], bsz)
        y = _s5(proj.reshape(bsz, seq_len, IN_WIDTH), bmat, cmat, kmat, atv, d2)
        y_ssm = _glu(y.reshape(t, SSM_WIDTH), w_glu_b, b_glu[l])

        y_att, w_ff1_b, w_ff2_b = _moba(proj, bias_tiles, bsz, seq_len,
                                        side_casts=(w_ff1[l], w_ff2[l]))

        merged = _merge(y_ssm, y_att, w_ps_b, w_pa_b, proj)
        x2d = _resid_mm(merged, w_out_b, x2d, mod4, 2, seq_len)

        hid = _norm_mm(x2d, norm_mlp_g[l], mod4, 4, 3, w_ff1_b, seq_len, act="relu2")
        x2d = _ff2(hid, w_ff2_b, x2d, mod4, 5, norm_final_g, seq_len)
    return x2d.reshape(bsz, seq_len, d)
```

```python
import functools
import itertools
import math

import jax
import jax.numpy as jnp
from jax import lax
from jax.experimental import pallas as pl
from jax.experimental.pallas import tpu as pltpu

F32 = jnp.float32
BF16 = jnp.bfloat16

D_MODEL = 2048
SSM_WIDTH = 1024
SSM_GROUP = 16
SSM_GROUPS = 64
SSM_STATE = 64
ATTN_HEADS = 8
HEAD_DIM = 128
ATTN_WIDTH = 1024
MOBA_BLOCK = 256
MOBA_TOPK = 3
REL_BUCKETS = 32
REL_MAX_DIST = 128
D_FF = 4 * D_MODEL
N_MOD = 6
EPS = 1e-6
NEG_INF = -1e30
LOG2E = math.log2(math.e)
IN_WIDTH = SSM_WIDTH + 3 * ATTN_WIDTH + 2 * D_MODEL

VMEM_LIMIT_BYTES = 56 * 1024 * 1024

S5_TAU = 2
S5_GROUPS = 8
S5_CH = S5_GROUPS * SSM_GROUP
S5_UC = S5_TAU * S5_CH
S5_SC = 2 * S5_GROUPS * SSM_STATE
S5_NBLK = SSM_GROUPS // S5_GROUPS
S5_ROWS = 4096
S5_SUB_ROWS = 256


_RESIDENT = pl.Buffered(1)


def _cparams(sem):
    return pltpu.CompilerParams(dimension_semantics=sem,
                                vmem_limit_bytes=VMEM_LIMIT_BYTES)


def _mod_kernel(c_ref, w_ref, b_ref, o_ref):
    c = c_ref[...]
    ca = (c * jax.nn.sigmoid(c)).astype(BF16)
    o_ref[...] = jnp.dot(ca, w_ref[...].astype(BF16),
                         preferred_element_type=F32) + b_ref[...]


def _mod(c, w_ada, b_ada):
    bsz, d = c.shape
    n = w_ada.shape[1]
    tn = 1024
    return pl.pallas_call(
        _mod_kernel,
        out_shape=jax.ShapeDtypeStruct((bsz, n), F32),
        grid=(n // tn,),
        in_specs=[pl.BlockSpec((bsz, d), lambda j: (0, 0)),
                  pl.BlockSpec((d, tn), lambda j: (0, j)),
                  pl.BlockSpec((1, tn), lambda j: (0, j))],
        out_specs=pl.BlockSpec((bsz, tn), lambda j: (0, j)),
        compiler_params=_cparams(("parallel",)),
        name="mod",
    )(c, w_ada, b_ada.reshape(1, n))


def _side_casts(arrays, steps, step_of):
    specs, shapes = [], []
    for a in arrays:
        r, c = a.shape
        rb = max(16, r // steps)
        nblk = r // rb
        assert r % rb == 0 and steps % nblk == 0
        specs.append(pl.BlockSpec(
            (rb, c), lambda *ids, nblk=nblk: ((step_of(*ids) * nblk) // steps, 0)))
        shapes.append(jax.ShapeDtypeStruct((r, c), BF16))
    return specs, shapes


def _cast_sides(side_in, side_out):
    for src, dst in zip(side_in, side_out):
        dst[...] = src[...].astype(dst.dtype)


def _norm_mm_kernel(*refs, act, rows, n_side):
    x_ref, g_ref, sc_ref, sh_ref, w_ref = refs[:5]
    side_in = refs[5:5 + n_side]
    o_ref = refs[5 + n_side]
    side_out = refs[6 + n_side:6 + 2 * n_side]
    h_ref = refs[-1]

    def mm(h):
        acc = jnp.dot(h, w_ref[...], preferred_element_type=F32)
        if act == "relu2":
            acc = jnp.square(jnp.maximum(acc, 0.0))
        return acc.astype(o_ref.dtype)

    @pl.when(pl.program_id(1) == 0)
    def _():
        g = g_ref[...]
        sc = 1.0 + sc_ref[0]
        sh = sh_ref[0]
        tm = x_ref.shape[0]
        for r in range(0, tm, rows):
            x = x_ref[r:r + rows, :]
            ms = jnp.mean(x * x, axis=-1, keepdims=True)
            y = x * lax.rsqrt(ms + EPS) * g
            h = (y * sc + sh).astype(BF16)
            h_ref[r:r + rows, :] = h
            o_ref[r:r + rows, :] = mm(h)

    @pl.when(pl.program_id(1) != 0)
    def _():
        o_ref[...] = mm(h_ref[...])

    _cast_sides(side_in, side_out)


def _norm_mm(x2d, g, mod4, sc_idx, sh_idx, w, seq_len, *, act, side_casts=(),
             tm=1024, tn=2048):
    t, d = x2d.shape
    n = w.shape[1]
    per_b = seq_len // tm
    nj = n // tn
    side_specs, side_shapes = _side_casts(side_casts, (t // tm) * nj, lambda i, j: i * nj + j)
    res = pl.pallas_call(
        functools.partial(_norm_mm_kernel, act=act, rows=256, n_side=len(side_casts)),
        out_shape=tuple([jax.ShapeDtypeStruct((t, n), BF16)] + side_shapes),
        grid=(t // tm, nj),
        in_specs=[pl.BlockSpec((tm, d), lambda i, j: (i, 0)),
                  pl.BlockSpec((1, d), lambda i, j: (0, 0)),
                  pl.BlockSpec((1, 1, d), lambda i, j: ((i // per_b) * N_MOD + sc_idx, 0, 0)),
                  pl.BlockSpec((1, 1, d), lambda i, j: ((i // per_b) * N_MOD + sh_idx, 0, 0)),
                  pl.BlockSpec((d, tn), lambda i, j: (0, j))] + side_specs,
        out_specs=tuple([pl.BlockSpec((tm, tn), lambda i, j: (i, j))] + side_specs),
        scratch_shapes=[pltpu.VMEM((tm, d), BF16)],
        compiler_params=_cparams(("arbitrary", "arbitrary")),
        name="norm_mm_" + act,
    )(x2d, g.reshape(1, d), mod4, mod4, w, *side_casts)
    return res if len(res) > 1 else res[0]


def _s5prep_kernel(are_ref, aim_ref, ldt_ref, bre_ref, bim_ref, cre_ref, cim_ref,
                   at_ref, bw_ref, cw_ref, kw_ref):
    tau = bw_ref.shape[0]
    a_re = are_ref[...]
    a_im = aim_ref[...]
    dt = jnp.exp(ldt_ref[...])
    mag = jnp.exp(dt * a_re)
    abar_re = mag * jnp.cos(dt * a_im)
    abar_im = mag * jnp.sin(dt * a_im)
    den = a_re * a_re + a_im * a_im
    p_re = abar_re - 1.0
    f_re = (p_re * a_re + abar_im * a_im) / den
    f_im = (abar_im * a_re - p_re * a_im) / den

    def cmul(x, y):
        return x[0] * y[0] - x[1] * y[1], x[0] * y[1] + x[1] * y[0]

    def cat(x):
        return jnp.concatenate([x[0], x[1]], axis=-1)

    def re_dot(x):
        return jnp.concatenate([x[0], -x[1]], axis=-1)

    abar = (abar_re, abar_im)
    bb = cmul((f_re, f_im), (bre_ref[...], bim_ref[...]))
    cc = (cre_ref[...], cim_ref[...])
    apow = [None, abar]
    for _ in range(tau - 1):
        apow.append(cmul(apow[-1], abar))
    at_ref[...] = cat(apow[tau])
    y0 = cat(bb)
    for i in range(tau):
        j = tau - 1 - i
        bw_ref[i] = y0 if j == 0 else cat(cmul(apow[j], bb))
        cw_ref[i] = re_dot(cmul(cc, apow[i + 1]))
        x = re_dot(cc if i == 0 else cmul(cc, apow[i]))
        kw_ref[i] = jnp.einsum("gpn,gqn->gpq", x, y0, precision=lax.Precision.HIGHEST,
                               preferred_element_type=F32)


def _s5prep(a_re, a_im, log_dt, b_re, b_im, c_re, c_im):
    g, n = a_re.shape
    p = b_re.shape[2]
    b_re_t = jnp.transpose(b_re, (0, 2, 1))
    b_im_t = jnp.transpose(b_im, (0, 2, 1))
    return pl.pallas_call(
        _s5prep_kernel,
        out_shape=(jax.ShapeDtypeStruct((g, 1, 2 * n), F32),
                   jax.ShapeDtypeStruct((S5_TAU, g, p, 2 * n), F32),
                   jax.ShapeDtypeStruct((S5_TAU, g, p, 2 * n), F32),
                   jax.ShapeDtypeStruct((S5_TAU, g, p, p), F32)),
        name="s5prep",
    )(a_re.reshape(g, 1, n), a_im.reshape(g, 1, n), log_dt.reshape(g, 1, 1),
      b_re_t, b_im_t, c_re, c_im)


def _s5_weights(at, bw, cw, kw, d_skip, bsz):
    n, p, g, tau = SSM_STATE, SSM_GROUP, S5_GROUPS, S5_TAU
    eye = jnp.eye(g, dtype=F32)
    bw6 = bw.reshape(tau, S5_NBLK, g, p, 2, n)
    cw6 = cw.reshape(tau, S5_NBLK, g, p, 2, n)
    bmat = jnp.einsum("iagpqn,gh->aigpqhn", bw6, eye).reshape(S5_NBLK, S5_UC, S5_SC)
    cmat = jnp.einsum("iagpqn,gh->aqgnihp", cw6, eye).reshape(S5_NBLK, S5_SC, S5_UC)
    lag = (jnp.arange(tau)[:, None, None] ==
           jnp.arange(tau)[None, None, :] - jnp.arange(tau)[None, :, None]).astype(F32)
    kw5 = kw.reshape(tau, S5_NBLK, g, p, p)
    kmat = jnp.einsum("lagop,lji,gh->ajgpiho", kw5, lag, eye).reshape(S5_NBLK, S5_UC, S5_UC)
    atv = jnp.transpose(at.reshape(S5_NBLK, g, 2, n), (0, 2, 1, 3)).reshape(S5_NBLK, 1, S5_SC)
    atv = jnp.broadcast_to(atv, (S5_NBLK, bsz, S5_SC))
    d = jnp.broadcast_to(d_skip.reshape(S5_NBLK, 1, S5_CH), (S5_NBLK, tau, S5_CH))
    return (bmat.astype(BF16), cmat.astype(BF16), kmat.astype(BF16), atv,
            d.reshape(S5_NBLK, 1, S5_UC))


def _s5_kernel(x_ref, bw_ref, cw_ref, kw_ref, a_ref, d_ref, y_ref,
               xf_ref, u_ref, w_ref, s_ref, st_ref, *, bsz, n_sub):
    @pl.when(pl.program_id(1) == 0)
    def _():
        st_ref[...] = jnp.zeros_like(st_ref)

    half = S5_SC // 2
    a_re = a_ref[0, :, :half]
    a_im = a_ref[0, :, half:]
    s_re = st_ref[:, :half]
    s_im = st_ref[:, half:]
    d2 = d_ref[0]
    sub = u_ref.shape[0] // n_sub
    ksub = sub // bsz

    def slot(ref, c, i):
        return ref.at[:, pl.ds(c * ksub * S5_TAU + i, ksub, stride=S5_TAU), :]

    def project_in(c):
        t0 = c * ksub * S5_TAU
        xf_ref[:, t0:t0 + ksub * S5_TAU, :] = x_ref[:, t0:t0 + ksub * S5_TAU, :].astype(F32)
        for i in range(S5_TAU):
            x = pltpu.einshape("bkc->kbc", slot(xf_ref, c, i)[...]).reshape(sub, S5_CH)
            u_ref[c * sub:(c + 1) * sub, i * S5_CH:(i + 1) * S5_CH] = x.astype(BF16)
        w_ref[c * sub:(c + 1) * sub, :] = jnp.dot(u_ref[c * sub:(c + 1) * sub, :], bw_ref[0],
                                                  preferred_element_type=F32)

    project_in(0)
    for c in range(n_sub):
        r0 = c * sub
        if c + 1 < n_sub:
            project_in(c + 1)
        u = u_ref[r0:r0 + sub, :]
        for k in range(0, ksub, 2):
            before = []
            for kk in range(2):
                r = r0 + (k + kk) * bsz
                before.append(jnp.concatenate([s_re, s_im], axis=1))
                w = w_ref[r:r + bsz, :]
                s_re, s_im = (a_re * s_re - a_im * s_im + w[:, :half],
                              a_re * s_im + a_im * s_re + w[:, half:])
            s_ref[r0 + k * bsz:r0 + (k + 2) * bsz, :] = (
                jnp.concatenate(before, axis=0).astype(BF16))
        y = (jnp.dot(s_ref[r0:r0 + sub, :], cw_ref[0], preferred_element_type=F32)
             + jnp.dot(u, kw_ref[0], preferred_element_type=F32)
             + d2 * u.astype(F32))
        y = jax.nn.gelu(y)
        for i in range(S5_TAU):
            slot(y_ref, c, i)[...] = pltpu.einshape(
                "kbc->bkc", y[:, i * S5_CH:(i + 1) * S5_CH].reshape(ksub, bsz, S5_CH))
    st_ref[:, :half] = s_re
    st_ref[:, half:] = s_im


def _s5(proj, bmat, cmat, kmat, atv, d2):
    assert S5_CH == 128, "a block's channels must be one lane tile"
    bsz, seq_len, _ = proj.shape
    tpc = S5_ROWS // bsz * S5_TAU
    return pl.pallas_call(
        functools.partial(_s5_kernel, bsz=bsz, n_sub=S5_ROWS // S5_SUB_ROWS),
        out_shape=jax.ShapeDtypeStruct((bsz, seq_len, SSM_WIDTH), F32),
        grid=(S5_NBLK, seq_len // tpc),
        in_specs=[
            pl.BlockSpec((bsz, tpc, S5_CH), lambda g, c: (0, c, g)),
            pl.BlockSpec((1, S5_UC, S5_SC), lambda g, c: (g, 0, 0)),
            pl.BlockSpec((1, S5_SC, S5_UC), lambda g, c: (g, 0, 0)),
            pl.BlockSpec((1, S5_UC, S5_UC), lambda g, c: (g, 0, 0)),
            pl.BlockSpec((1, bsz, S5_SC), lambda g, c: (g, 0, 0)),
            pl.BlockSpec((1, 1, S5_UC), lambda g, c: (g, 0, 0))],
        out_specs=pl.BlockSpec((bsz, tpc, S5_CH), lambda g, c: (0, c, g)),
        scratch_shapes=[pltpu.VMEM((bsz, tpc, S5_CH), F32),
                        pltpu.VMEM((S5_ROWS, S5_UC), BF16),
                        pltpu.VMEM((S5_ROWS, S5_SC), F32),
                        pltpu.VMEM((S5_ROWS, S5_SC), BF16),
                        pltpu.VMEM((bsz, S5_SC), F32)],
        compiler_params=_cparams(("parallel", "arbitrary")),
        name="s5",
    )(proj, bmat, cmat, kmat, atv, d2)


def _glu_kernel(y_ref, w_ref, b_ref, o_ref):
    y = y_ref[...]
    z = jnp.dot(y.astype(BF16), w_ref[...], preferred_element_type=F32) + b_ref[...]
    o_ref[...] = (y * jax.nn.sigmoid(z)).astype(o_ref.dtype)


def _glu(y, w, b, tm=1024):
    t, n = y.shape
    return pl.pallas_call(
        _glu_kernel,
        out_shape=jax.ShapeDtypeStruct((t, n), BF16),
        grid=(t // tm,),
        in_specs=[pl.BlockSpec((tm, n), lambda i: (i, 0)),
                  pl.BlockSpec((n, n), lambda i: (0, 0)),
                  pl.BlockSpec((1, n), lambda i: (0, 0))],
        out_specs=pl.BlockSpec((tm, n), lambda i: (i, 0)),
        compiler_params=_cparams(("parallel",)),
        name="glu",
    )(y, w, b.reshape(1, n))


def _t5_bucket(rel):
    n = jnp.maximum(rel, 0)
    max_exact = REL_BUCKETS // 2
    nf = jnp.maximum(n, 1).astype(F32)
    large = max_exact + (jnp.log(nf / max_exact) / math.log(REL_MAX_DIST / max_exact)
                         * (REL_BUCKETS - max_exact)).astype(jnp.int32)
    large = jnp.minimum(large, REL_BUCKETS - 1)
    return jnp.where(n < max_exact, n, large)


def _bias_kernel(relb_ref, bucket_ref, w_ref, o_ref, wb_ref):
    h = pl.program_id(0)
    bucket = bucket_ref[0]
    out = jnp.zeros(bucket.shape, F32)
    for b in range(REL_BUCKETS):
        out = jnp.where(bucket == b, relb_ref[b * ATTN_HEADS + h], out)
    o_ref[0, 0] = out * LOG2E
    wb_ref[...] = w_ref[...].astype(wb_ref.dtype)


def _bias_tiles(rel_bias, w_in):
    i = jnp.arange(MOBA_BLOCK)
    rel = (jnp.arange(3)[:, None, None] * MOBA_BLOCK + i[None, None, :] - i[None, :, None])
    bucket = _t5_bucket(rel)
    n_d = 4
    rows, cols = w_in.shape
    rb = rows // (ATTN_HEADS * n_d)
    w_spec = pl.BlockSpec((rb, cols), lambda h, d: (h * n_d + d, 0))
    return pl.pallas_call(
        _bias_kernel,
        out_shape=(jax.ShapeDtypeStruct((ATTN_HEADS, 3, MOBA_BLOCK, MOBA_BLOCK), F32),
                   jax.ShapeDtypeStruct((rows, cols), BF16)),
        grid=(ATTN_HEADS, n_d),
        in_specs=[pl.BlockSpec(memory_space=pltpu.SMEM),
                  pl.BlockSpec((1, MOBA_BLOCK, MOBA_BLOCK), lambda h, d: (jnp.minimum(d, 2), 0, 0)),
                  w_spec],
        out_specs=(pl.BlockSpec((1, 1, MOBA_BLOCK, MOBA_BLOCK),
                                lambda h, d: (h, jnp.minimum(d, 2), 0, 0)),
                   w_spec),
        compiler_params=_cparams(("arbitrary", "arbitrary")),
        name="bias_tiles",
    )(rel_bias.astype(F32).reshape(-1), bucket, w_in)


_NT = (((1,), (1,)), ((), ()))


def _moba_scores(own, slot0, state, q_ref, k_ref, bias_ref, km_ref, s_ref):
    bs = MOBA_BLOCK
    q = q_ref[own * bs:(own + 1) * bs, :]

    ranks = []
    if own > 0:
        gate = lax.dot_general(km_ref[...].astype(BF16), q, _NT, preferred_element_type=F32)
        blk = lax.broadcasted_iota(jnp.int32, gate.shape, 0)
        gm = jnp.where(blk < own, gate, NEG_INF)
        for n in range(own):
            g_n = gm[n:n + 1, :]
            ge = jnp.where(gm >= g_n, 1.0, 0.0)
            gt = jnp.where(gm > g_n, 1.0, 0.0)
            ranks.append(jnp.sum(jnp.where(blk < n, ge, gt), axis=0, keepdims=True))

    c1 = HEAD_DIM ** -0.5 * LOG2E
    m = None
    for idx, j in enumerate([own] + list(range(own))):
        k_j = k_ref[j * bs:(j + 1) * bs, :]
        raw = lax.dot_general(k_j, q, _NT, preferred_element_type=F32)
        if j == own:
            key = lax.broadcasted_iota(jnp.int32, (bs, bs), 0)
            qry = lax.broadcasted_iota(jnp.int32, (bs, bs), 1)
            s = jnp.where(key <= qry, raw * c1 + bias_ref[0], NEG_INF)
        elif own - j == 1:
            s = jnp.where(ranks[j] < float(MOBA_TOPK), raw * c1 + bias_ref[1], NEG_INF)
        else:
            row = jnp.where(ranks[j] < float(MOBA_TOPK), bias_ref[2, 0:1, :], NEG_INF)
            s = raw * c1 + row
        s_ref[slot0 + idx] = s
        m_j = jnp.max(s, axis=0, keepdims=True)
        m = m_j if m is None else jnp.maximum(m, m_j)
        state["m"] = m
        yield


def _moba_values(own, slot0, state, vt_ref, o_ref, s_ref):
    bs = MOBA_BLOCK
    m = state["m"]
    acc = None
    for idx, j in enumerate([own] + list(range(own))):
        p = jnp.exp2((s_ref[slot0 + idx] - m).astype(BF16))
        a_j = jnp.dot(vt_ref[:, j * bs:(j + 1) * bs], p, preferred_element_type=F32)
        acc = a_j if acc is None else acc + a_j
        if idx == own:
            o = acc[:HEAD_DIM, :] / acc[HEAD_DIM:HEAD_DIM + 1, :]
            o_ref[own * bs:(own + 1) * bs, :] = o.T.astype(o_ref.dtype)
        yield


def _moba_kernel(*refs, n_blk, n_side):
    q_ref, k_ref, v_ref, bias_ref = refs[:4]
    side_in = refs[4:4 + n_side]
    o_ref = refs[4 + n_side]
    side_out = refs[5 + n_side:5 + 2 * n_side]
    km_ref, vt_ref, s_ref = refs[-3:]
    bs = MOBA_BLOCK
    heads = bias_ref.shape[0]
    _cast_sides(side_in, side_out)

    def head(ref, hh):
        return ref.at[:, hh * HEAD_DIM:(hh + 1) * HEAD_DIM]

    for hh in range(heads):
        vt_ref[hh, HEAD_DIM:, :] = jnp.ones((vt_ref.shape[1] - HEAD_DIM, vt_ref.shape[2]), BF16)
        for n in range(n_blk):
            kb = head(k_ref, hh)[n * bs:(n + 1) * bs, :].astype(F32)
            km_ref[hh, n:n + 1, :] = jnp.mean(kb, axis=0, keepdims=True)
            vt_ref[hh, :HEAD_DIM, n * bs:(n + 1) * bs] = (
                head(v_ref, hh)[n * bs:(n + 1) * bs, :].astype(F32).T.astype(BF16))

    per_head = n_blk * (n_blk + 1) // 2
    pending = iter(())
    for own in range(n_blk):
        for hh in range(heads):
            slot0 = hh * per_head + own * (own + 1) // 2
            state = {}
            scores = _moba_scores(own, slot0, state, head(q_ref, hh), head(k_ref, hh),
                                  bias_ref.at[hh], km_ref.at[hh], s_ref)
            for _ in itertools.zip_longest(scores, pending):
                pass
            pending = _moba_values(own, slot0, state, vt_ref.at[hh], head(o_ref, hh), s_ref)
    for _ in pending:
        pass


def _moba(proj, bias_tiles, bsz, seq_len, side_casts=(), heads=2):
    n_blk = seq_len // MOBA_BLOCK
    hw = heads * HEAD_DIM
    q_off = SSM_WIDTH // hw
    k_off = q_off + ATTN_WIDTH // hw
    v_off = k_off + ATTN_WIDTH // hw
    n_slots = heads * n_blk * (n_blk + 1) // 2
    hg = ATTN_HEADS // heads
    side_specs, side_shapes = _side_casts(side_casts, bsz * hg, lambda b, h: b * hg + h)
    res = pl.pallas_call(
        functools.partial(_moba_kernel, n_blk=n_blk, n_side=len(side_casts)),
        out_shape=tuple([jax.ShapeDtypeStruct((bsz * seq_len, ATTN_WIDTH), BF16)] + side_shapes),
        grid=(bsz, hg),
        in_specs=[pl.BlockSpec((seq_len, hw), lambda b, h: (b, q_off + h)),
                  pl.BlockSpec((seq_len, hw), lambda b, h: (b, k_off + h)),
                  pl.BlockSpec((seq_len, hw), lambda b, h: (b, v_off + h)),
                  pl.BlockSpec((heads, 3, MOBA_BLOCK, MOBA_BLOCK), lambda b, h: (h, 0, 0, 0))]
        + side_specs,
        out_specs=tuple([pl.BlockSpec((seq_len, hw), lambda b, h: (b, h))] + side_specs),
        scratch_shapes=[pltpu.VMEM((heads, n_blk, HEAD_DIM), F32),
                        pltpu.VMEM((heads, HEAD_DIM + 16, seq_len), BF16),
                        pltpu.VMEM((n_slots, MOBA_BLOCK, MOBA_BLOCK), F32)],
        compiler_params=_cparams(("arbitrary", "arbitrary")),
        name="moba",
    )(proj, proj, proj, bias_tiles, *side_casts)
    return res if len(res) > 1 else res[0]


def _merge_kernel(ys_ref, ya_ref, ws_ref, wa_ref, ga_ref, gb_ref, o_ref):
    a = jnp.dot(ys_ref[...], ws_ref[...], preferred_element_type=F32)
    b = jnp.dot(ya_ref[...], wa_ref[...], preferred_element_type=F32)
    o_ref[...] = (jax.nn.sigmoid(ga_ref[...].astype(F32)) * a
                  + jax.nn.sigmoid(gb_ref[...].astype(F32)) * b).astype(o_ref.dtype)


def _merge(y_ssm, y_att, w_ps, w_pa, proj, tm=1024, tn=2048):
    t, k = y_ssm.shape
    n = w_ps.shape[1]
    ga_off = (SSM_WIDTH + 3 * ATTN_WIDTH) // tn
    gb_off = ga_off + D_MODEL // tn
    return pl.pallas_call(
        _merge_kernel,
        out_shape=jax.ShapeDtypeStruct((t, n), BF16),
        grid=(t // tm, n // tn),
        in_specs=[pl.BlockSpec((tm, k), lambda i, j: (i, 0)),
                  pl.BlockSpec((tm, k), lambda i, j: (i, 0)),
                  pl.BlockSpec((k, tn), lambda i, j: (0, j), pipeline_mode=_RESIDENT),
                  pl.BlockSpec((k, tn), lambda i, j: (0, j), pipeline_mode=_RESIDENT),
                  pl.BlockSpec((tm, tn), lambda i, j: (i, ga_off + j)),
                  pl.BlockSpec((tm, tn), lambda i, j: (i, gb_off + j))],
        out_specs=pl.BlockSpec((tm, tn), lambda i, j: (i, j)),
        compiler_params=_cparams(("parallel", "parallel")),
        name="merge",
    )(y_ssm, y_att, w_ps, w_pa, proj, proj)


def _resid_mm_kernel(a_ref, w_ref, x_ref, g_ref, o_ref):
    acc = jnp.dot(a_ref[...], w_ref[...], preferred_element_type=F32)
    o_ref[...] = x_ref[...] + g_ref[0] * acc


def _resid_mm(a, w, x2d, mod4, g_idx, seq_len, tm=1024, tn=2048):
    t, k = a.shape
    n = w.shape[1]
    per_b = seq_len // tm
    return pl.pallas_call(
        _resid_mm_kernel,
        out_shape=jax.ShapeDtypeStruct((t, n), F32),
        grid=(t // tm, n // tn),
        in_specs=[pl.BlockSpec((tm, k), lambda i, j: (i, 0)),
                  pl.BlockSpec((k, tn), lambda i, j: (0, j), pipeline_mode=_RESIDENT),
                  pl.BlockSpec((tm, tn), lambda i, j: (i, j)),
                  pl.BlockSpec((1, 1, tn), lambda i, j: ((i // per_b) * N_MOD + g_idx, 0, j))],
        out_specs=pl.BlockSpec((tm, tn), lambda i, j: (i, j)),
        compiler_params=_cparams(("parallel", "parallel")),
        name="resid_mm",
    )(a, w, x2d, mod4)


def _ff2_kernel(h_ref, w_ref, x_hbm, g_ref, gf_ref, o_ref, x_ref, x_sem, *, rows):
    i = pl.program_id(0)
    k = pl.program_id(1)
    last = pl.num_programs(1) - 1
    tm = o_ref.shape[0]

    def x_copy():
        return pltpu.make_async_copy(x_hbm.at[pl.ds(pl.multiple_of(i * tm, tm), tm), :],
                                     x_ref, x_sem)

    @pl.when(k == 0)
    def _():
        x_copy().start()
        o_ref[...] = jnp.dot(h_ref[...], w_ref[...], preferred_element_type=F32)

    @pl.when((k != 0) & (k != last))
    def _():
        o_ref[...] += jnp.dot(h_ref[...], w_ref[...], preferred_element_type=F32)

    @pl.when(k == last)
    def _():
        x_copy().wait()
        for r in range(0, tm, rows):
            acc = o_ref[r:r + rows, :] + jnp.dot(h_ref[r:r + rows, :], w_ref[...],
                                                 preferred_element_type=F32)
            xo = x_ref[r:r + rows, :] + g_ref[0] * acc
            ms = jnp.mean(xo * xo, axis=-1, keepdims=True)
            o_ref[r:r + rows, :] = xo * lax.rsqrt(ms + EPS) * gf_ref[...]


def _ff2(hid, w, x2d, mod4, g_idx, gf, seq_len, tm=1024, tk=2048):
    t, kdim = hid.shape
    n = w.shape[1]
    per_b = seq_len // tm
    assert kdim // tk >= 2, "the residual copy is started and waited in different steps"
    return pl.pallas_call(
        functools.partial(_ff2_kernel, rows=256),
        out_shape=jax.ShapeDtypeStruct((t, n), F32),
        grid=(t // tm, kdim // tk),
        in_specs=[pl.BlockSpec((tm, tk), lambda i, k: (i, k)),
                  pl.BlockSpec((tk, n), lambda i, k: (k, 0)),
                  pl.BlockSpec(memory_space=pl.ANY),
                  pl.BlockSpec((1, 1, n), lambda i, k: ((i // per_b) * N_MOD + g_idx, 0, 0)),
                  pl.BlockSpec((1, n), lambda i, k: (0, 0))],
        out_specs=pl.BlockSpec((tm, n), lambda i, k: (i, 0)),
        scratch_shapes=[pltpu.VMEM((tm, n), F32), pltpu.SemaphoreType.DMA(())],
        compiler_params=_cparams(("arbitrary", "arbitrary")),
        name="ff2_final",
    )(hid, w, x2d, mod4, gf.reshape(1, n))


def kernel(x, c, rel_bias, w_ada, b_ada, norm_mix_g, w_in, ssm_a_re, ssm_a_im, ssm_log_dt,
           ssm_b_re, ssm_b_im, ssm_c_re, ssm_c_im, ssm_d, w_glu, b_glu, w_proj_ssm,
           w_proj_attn, w_out, norm_mlp_g, w_ff1, w_ff2, norm_final_g):
    bsz, seq_len, d = x.shape
    depth = w_in.shape[0]
    assert depth == 1, "the final rms_norm is fused into the single layer's ff2 kernel"
    t = bsz * seq_len
    x2d = x.reshape(t, d)
    bias_tiles, w_in_b = _bias_tiles(rel_bias, w_in[0])

    for l in range(depth):
        mod4 = _mod(c, w_ada[l], b_ada[l]).reshape(bsz * N_MOD, 1, d)

        proj, w_glu_b, w_ps_b, w_pa_b, w_out_b = _norm_mm(
            x2d, norm_mix_g[l], mod4, 1, 0, w_in_b, seq_len, act="none",
            side_casts=(w_glu[l], w_proj_ssm[l], w_proj_attn[l], w_out[l]))

        at, bw, cw, kw = _s5prep(ssm_a_re[l], ssm_a_im[l], ssm_log_dt[l],
                                 ssm_b_re[l], ssm_b_im[l], ssm_c_re[l], ssm_c_im[l])
        bmat, cmat, kmat, atv, d2 = _s5_weights(at, bw, cw, kw, ssm_d[l], bsz)
        y = _s5(proj.reshape(bsz, seq_len, IN_WIDTH), bmat, cmat, kmat, atv, d2)
        y_ssm = _glu(y.reshape(t, SSM_WIDTH), w_glu_b, b_glu[l])

        y_att, w_ff1_b, w_ff2_b = _moba(proj, bias_tiles, bsz, seq_len,
                                        side_casts=(w_ff1[l], w_ff2[l]))

        merged = _merge(y_ssm, y_att, w_ps_b, w_pa_b, proj)
        x2d = _resid_mm(merged, w_out_b, x2d, mod4, 2, seq_len)

        hid = _norm_mm(x2d, norm_mlp_g[l], mod4, 4, 3, w_ff1_b, seq_len, act="relu2")
        x2d = _ff2(hid, w_ff2_b, x2d, mod4, 5, norm_final_g, seq_len)
    return x2d.reshape(bsz, seq_len, d)
```

```python
import functools
import itertools
import math

import jax
import jax.numpy as jnp
from jax import lax
from jax.experimental import pallas as pl
from jax.experimental.pallas import tpu as pltpu

F32 = jnp.float32
BF16 = jnp.bfloat16

D_MODEL = 2048
SSM_WIDTH = 1024
SSM_GROUP = 16
SSM_GROUPS = 64
SSM_STATE = 64
ATTN_HEADS = 8
HEAD_DIM = 128
ATTN_WIDTH = 1024
MOBA_BLOCK = 256
MOBA_TOPK = 3
REL_BUCKETS = 32
REL_MAX_DIST = 128
D_FF = 4 * D_MODEL
N_MOD = 6
EPS = 1e-6
NEG_INF = -1e30
LOG2E = math.log2(math.e)
IN_WIDTH = SSM_WIDTH + 3 * ATTN_WIDTH + 2 * D_MODEL

VMEM_LIMIT_BYTES = 56 * 1024 * 1024

S5_TAU = 2
S5_GROUPS = 8
S5_CH = S5_GROUPS * SSM_GROUP
S5_UC = S5_TAU * S5_CH
S5_SC = 2 * S5_GROUPS * SSM_STATE
S5_NBLK = SSM_GROUPS // S5_GROUPS
S5_ROWS = 4096
S5_SUB_ROWS = 256


_RESIDENT = pl.Buffered(1)


def _cparams(sem):
    return pltpu.CompilerParams(dimension_semantics=sem,
                                vmem_limit_bytes=VMEM_LIMIT_BYTES)


def _mod_kernel(c_ref, w_ref, b_ref, o_ref):
    c = c_ref[...]
    ca = (c * jax.nn.sigmoid(c)).astype(BF16)
    o_ref[...] = jnp.dot(ca, w_ref[...].astype(BF16),
                         preferred_element_type=F32) + b_ref[...]


def _mod(c, w_ada, b_ada):
    bsz, d = c.shape
    n = w_ada.shape[1]
    tn = 1024
    return pl.pallas_call(
        _mod_kernel,
        out_shape=jax.ShapeDtypeStruct((bsz, n), F32),
        grid=(n // tn,),
        in_specs=[pl.BlockSpec((bsz, d), lambda j: (0, 0)),
                  pl.BlockSpec((d, tn), lambda j: (0, j)),
                  pl.BlockSpec((1, tn), lambda j: (0, j))],
        out_specs=pl.BlockSpec((bsz, tn), lambda j: (0, j)),
        compiler_params=_cparams(("parallel",)),
        name="mod",
    )(c, w_ada, b_ada.reshape(1, n))


def _side_casts(arrays, steps, step_of):
    specs, shapes = [], []
    for a in arrays:
        r, c = a.shape
        rb = max(16, r // steps)
        nblk = r // rb
        assert r % rb == 0 and steps % nblk == 0
        specs.append(pl.BlockSpec(
            (rb, c), lambda *ids, nblk=nblk: ((step_of(*ids) * nblk) // steps, 0)))
        shapes.append(jax.ShapeDtypeStruct((r, c), BF16))
    return specs, shapes


def _cast_sides(side_in, side_out):
    for src, dst in zip(side_in, side_out):
        dst[...] = src[...].astype(dst.dtype)


def _norm_mm_kernel(*refs, act, rows, n_side):
    x_ref, g_ref, sc_ref, sh_ref, w_ref = refs[:5]
    side_in = refs[5:5 + n_side]
    o_ref = refs[5 + n_side]
    side_out = refs[6 + n_side:6 + 2 * n_side]
    h_ref = refs[-1]

    def mm(h):
        acc = jnp.dot(h, w_ref[...], preferred_element_type=F32)
        if act == "relu2":
            acc = jnp.square(jnp.maximum(acc, 0.0))
        return acc.astype(o_ref.dtype)

    @pl.when(pl.program_id(1) == 0)
    def _():
        g = g_ref[...]
        sc = 1.0 + sc_ref[0]
        sh = sh_ref[0]
        tm = x_ref.shape[0]
        for r in range(0, tm, rows):
            x = x_ref[r:r + rows, :]
            ms = jnp.mean(x * x, axis=-1, keepdims=True)
            y = x * lax.rsqrt(ms + EPS) * g
            h = (y * sc + sh).astype(BF16)
            h_ref[r:r + rows, :] = h
            o_ref[r:r + rows, :] = mm(h)

    @pl.when(pl.program_id(1) != 0)
    def _():
        o_ref[...] = mm(h_ref[...])

    _cast_sides(side_in, side_out)


def _norm_mm(x2d, g, mod4, sc_idx, sh_idx, w, seq_len, *, act, side_casts=(),
             tm=1024, tn=2048):
    t, d = x2d.shape
    n = w.shape[1]
    per_b = seq_len // tm
    nj = n // tn
    side_specs, side_shapes = _side_casts(side_casts, (t // tm) * nj, lambda i, j: i * nj + j)
    res = pl.pallas_call(
        functools.partial(_norm_mm_kernel, act=act, rows=256, n_side=len(side_casts)),
        out_shape=tuple([jax.ShapeDtypeStruct((t, n), BF16)] + side_shapes),
        grid=(t // tm, nj),
        in_specs=[pl.BlockSpec((tm, d), lambda i, j: (i, 0)),
                  pl.BlockSpec((1, d), lambda i, j: (0, 0)),
                  pl.BlockSpec((1, 1, d), lambda i, j: ((i // per_b) * N_MOD + sc_idx, 0, 0)),
                  pl.BlockSpec((1, 1, d), lambda i, j: ((i // per_b) * N_MOD + sh_idx, 0, 0)),
                  pl.BlockSpec((d, tn), lambda i, j: (0, j))] + side_specs,
        out_specs=tuple([pl.BlockSpec((tm, tn), lambda i, j: (i, j))] + side_specs),
        scratch_shapes=[pltpu.VMEM((tm, d), BF16)],
        compiler_params=_cparams(("arbitrary", "arbitrary")),
        name="norm_mm_" + act,
    )(x2d, g.reshape(1, d), mod4, mod4, w, *side_casts)
    return res if len(res) > 1 else res[0]


def _s5prep_kernel(are_ref, aim_ref, ldt_ref, bre_ref, bim_ref, cre_ref, cim_ref,
                   at_ref, bmat_ref, cmat_ref, kmat_ref):
    tau = S5_TAU
    p, n = SSM_GROUP, SSM_STATE
    a_re = are_ref[...]
    a_im = aim_ref[...]
    dt = jnp.exp(ldt_ref[...])
    mag = jnp.exp(dt * a_re)
    abar_re = mag * jnp.cos(dt * a_im)
    abar_im = mag * jnp.sin(dt * a_im)
    den = a_re * a_re + a_im * a_im
    p_re = abar_re - 1.0
    f_re = (p_re * a_re + abar_im * a_im) / den
    f_im = (abar_im * a_re - p_re * a_im) / den

    def cmul(x, y):
        return x[0] * y[0] - x[1] * y[1], x[0] * y[1] + x[1] * y[0]

    def cat(x):
        return jnp.concatenate([x[0], x[1]], axis=-1)

    abar = (abar_re, abar_im)
    b_in = (jnp.swapaxes(bre_ref[...], 1, 2), jnp.swapaxes(bim_ref[...], 1, 2))
    bb = cmul((f_re, f_im), b_in)
    cc = (cre_ref[...], cim_ref[...])
    apow = [None, abar]
    for _ in range(tau - 1):
        apow.append(cmul(apow[-1], abar))
    at_ref[...] = cat(apow[tau])

    bw, cwt, kwt = [], [], []
    y0 = cat(bb)
    for i in range(tau):
        j = tau - 1 - i
        bw.append(bb if j == 0 else cmul(apow[j], bb))
        ca = cmul(cc, apow[i + 1])
        cwt.append((jnp.swapaxes(ca[0], 1, 2), jnp.swapaxes(-ca[1], 1, 2)))
        x = cc if i == 0 else cmul(cc, apow[i])
        x = jnp.concatenate([x[0], -x[1]], axis=-1)
        kwt.append(jnp.einsum("gqn,gpn->gqp", y0, x, precision=lax.Precision.HIGHEST,
                              preferred_element_type=F32))

    bmat_ref[...] = jnp.zeros_like(bmat_ref)
    cmat_ref[...] = jnp.zeros_like(cmat_ref)
    kmat_ref[...] = jnp.zeros_like(kmat_ref)
    half = S5_SC // 2
    for g in range(are_ref.shape[0]):
        blk, gl = divmod(g, S5_GROUPS)
        for i in range(tau):
            rows_i = slice(i * S5_CH + gl * p, i * S5_CH + (gl + 1) * p)
            for part in range(2):
                st = slice(part * half + gl * n, part * half + (gl + 1) * n)
                bmat_ref[blk, rows_i, st] = bw[i][part][g].astype(BF16)
                cmat_ref[blk, st, rows_i] = cwt[i][part][g].astype(BF16)
            for ip in range(i + 1):
                rows_ip = slice(ip * S5_CH + gl * p, ip * S5_CH + (gl + 1) * p)
                kmat_ref[blk, rows_ip, rows_i] = kwt[i - ip][g].astype(BF16)


def _s5prep(a_re, a_im, log_dt, b_re, b_im, c_re, c_im, d_skip, bsz):
    g, n = a_re.shape
    at, bmat, cmat, kmat = pl.pallas_call(
        _s5prep_kernel,
        out_shape=(jax.ShapeDtypeStruct((g, 1, 2 * n), F32),
                   jax.ShapeDtypeStruct((S5_NBLK, S5_UC, S5_SC), BF16),
                   jax.ShapeDtypeStruct((S5_NBLK, S5_SC, S5_UC), BF16),
                   jax.ShapeDtypeStruct((S5_NBLK, S5_UC, S5_UC), BF16)),
        compiler_params=pltpu.CompilerParams(vmem_limit_bytes=VMEM_LIMIT_BYTES),
        name="s5prep",
    )(a_re.reshape(g, 1, n), a_im.reshape(g, 1, n), log_dt.reshape(g, 1, 1),
      b_re, b_im, c_re, c_im)
    half = S5_SC // 2
    atv = jnp.concatenate([at[:, 0, :n].reshape(S5_NBLK, half),
                           at[:, 0, n:].reshape(S5_NBLK, half)], axis=-1)
    atv = jnp.broadcast_to(atv.reshape(S5_NBLK, 1, S5_SC), (S5_NBLK, bsz, S5_SC))
    d = jnp.broadcast_to(d_skip.reshape(S5_NBLK, 1, S5_CH), (S5_NBLK, S5_TAU, S5_CH))
    return bmat, cmat, kmat, atv, d.reshape(S5_NBLK, 1, S5_UC)


def _s5_kernel(x_ref, bw_ref, cw_ref, kw_ref, a_ref, d_ref, y_ref,
               xf_ref, u_ref, w_ref, s_ref, st_ref, *, bsz, n_sub):
    @pl.when(pl.program_id(1) == 0)
    def _():
        st_ref[...] = jnp.zeros_like(st_ref)

    half = S5_SC // 2
    a_re = a_ref[0, :, :half]
    a_im = a_ref[0, :, half:]
    s_re = st_ref[:, :half]
    s_im = st_ref[:, half:]
    d2 = d_ref[0]
    sub = u_ref.shape[0] // n_sub
    ksub = sub // bsz

    def slot(ref, c, i):
        return ref.at[:, pl.ds(c * ksub * S5_TAU + i, ksub, stride=S5_TAU), :]

    def project_in(c):
        t0 = c * ksub * S5_TAU
        xf_ref[:, t0:t0 + ksub * S5_TAU, :] = x_ref[:, t0:t0 + ksub * S5_TAU, :].astype(F32)
        for i in range(S5_TAU):
            x = pltpu.einshape("bkc->kbc", slot(xf_ref, c, i)[...]).reshape(sub, S5_CH)
            u_ref[c * sub:(c + 1) * sub, i * S5_CH:(i + 1) * S5_CH] = x.astype(BF16)
        w_ref[c * sub:(c + 1) * sub, :] = jnp.dot(u_ref[c * sub:(c + 1) * sub, :], bw_ref[0],
                                                  preferred_element_type=F32)

    project_in(0)
    for c in range(n_sub):
        r0 = c * sub
        if c + 1 < n_sub:
            project_in(c + 1)
        u = u_ref[r0:r0 + sub, :]
        for k in range(0, ksub, 2):
            before = []
            for kk in range(2):
                r = r0 + (k + kk) * bsz
                before.append(jnp.concatenate([s_re, s_im], axis=1))
                w = w_ref[r:r + bsz, :]
                s_re, s_im = (a_re * s_re - a_im * s_im + w[:, :half],
                              a_re * s_im + a_im * s_re + w[:, half:])
            s_ref[r0 + k * bsz:r0 + (k + 2) * bsz, :] = (
                jnp.concatenate(before, axis=0).astype(BF16))
        y = (jnp.dot(s_ref[r0:r0 + sub, :], cw_ref[0], preferred_element_type=F32)
             + jnp.dot(u, kw_ref[0], preferred_element_type=F32)
             + d2 * u.astype(F32))
        y = jax.nn.gelu(y)
        for i in range(S5_TAU):
            slot(y_ref, c, i)[...] = pltpu.einshape(
                "kbc->bkc", y[:, i * S5_CH:(i + 1) * S5_CH].reshape(ksub, bsz, S5_CH))
    st_ref[:, :half] = s_re
    st_ref[:, half:] = s_im


def _s5(proj, bmat, cmat, kmat, atv, d2):
    assert S5_CH == 128, "a block's channels must be one lane tile"
    bsz, seq_len, _ = proj.shape
    tpc = S5_ROWS // bsz * S5_TAU
    return pl.pallas_call(
        functools.partial(_s5_kernel, bsz=bsz, n_sub=S5_ROWS // S5_SUB_ROWS),
        out_shape=jax.ShapeDtypeStruct((bsz, seq_len, SSM_WIDTH), F32),
        grid=(S5_NBLK, seq_len // tpc),
        in_specs=[
            pl.BlockSpec((bsz, tpc, S5_CH), lambda g, c: (0, c, g)),
            pl.BlockSpec((1, S5_UC, S5_SC), lambda g, c: (g, 0, 0)),
            pl.BlockSpec((1, S5_SC, S5_UC), lambda g, c: (g, 0, 0)),
            pl.BlockSpec((1, S5_UC, S5_UC), lambda g, c: (g, 0, 0)),
            pl.BlockSpec((1, bsz, S5_SC), lambda g, c: (g, 0, 0)),
            pl.BlockSpec((1, 1, S5_UC), lambda g, c: (g, 0, 0))],
        out_specs=pl.BlockSpec((bsz, tpc, S5_CH), lambda g, c: (0, c, g)),
        scratch_shapes=[pltpu.VMEM((bsz, tpc, S5_CH), F32),
                        pltpu.VMEM((S5_ROWS, S5_UC), BF16),
                        pltpu.VMEM((S5_ROWS, S5_SC), F32),
                        pltpu.VMEM((S5_ROWS, S5_SC), BF16),
                        pltpu.VMEM((bsz, S5_SC), F32)],
        compiler_params=_cparams(("parallel", "arbitrary")),
        name="s5",
    )(proj, bmat, cmat, kmat, atv, d2)


def _glu_kernel(y_ref, w_ref, b_ref, o_ref):
    y = y_ref[...]
    z = jnp.dot(y.astype(BF16), w_ref[...], preferred_element_type=F32) + b_ref[...]
    o_ref[...] = (y * jax.nn.sigmoid(z)).astype(o_ref.dtype)


def _glu(y, w, b, tm=1024):
    t, n = y.shape
    return pl.pallas_call(
        _glu_kernel,
        out_shape=jax.ShapeDtypeStruct((t, n), BF16),
        grid=(t // tm,),
        in_specs=[pl.BlockSpec((tm, n), lambda i: (i, 0)),
                  pl.BlockSpec((n, n), lambda i: (0, 0)),
                  pl.BlockSpec((1, n), lambda i: (0, 0))],
        out_specs=pl.BlockSpec((tm, n), lambda i: (i, 0)),
        compiler_params=_cparams(("parallel",)),
        name="glu",
    )(y, w, b.reshape(1, n))


def _t5_bucket(rel):
    n = jnp.maximum(rel, 0)
    max_exact = REL_BUCKETS // 2
    nf = jnp.maximum(n, 1).astype(F32)
    large = max_exact + (jnp.log(nf / max_exact) / math.log(REL_MAX_DIST / max_exact)
                         * (REL_BUCKETS - max_exact)).astype(jnp.int32)
    large = jnp.minimum(large, REL_BUCKETS - 1)
    return jnp.where(n < max_exact, n, large)


def _bias_kernel(relb_ref, bucket_ref, w_ref, o_ref, wb_ref):
    h = pl.program_id(0)
    bucket = bucket_ref[0]
    out = jnp.zeros(bucket.shape, F32)
    for b in range(REL_BUCKETS):
        out = jnp.where(bucket == b, relb_ref[b * ATTN_HEADS + h], out)
    o_ref[0, 0] = out * LOG2E
    wb_ref[...] = w_ref[...].astype(wb_ref.dtype)


def _bias_tiles(rel_bias, w_in):
    i = jnp.arange(MOBA_BLOCK)
    rel = (jnp.arange(3)[:, None, None] * MOBA_BLOCK + i[None, None, :] - i[None, :, None])
    bucket = _t5_bucket(rel)
    n_d = 4
    rows, cols = w_in.shape
    rb = rows // (ATTN_HEADS * n_d)
    w_spec = pl.BlockSpec((rb, cols), lambda h, d: (h * n_d + d, 0))
    return pl.pallas_call(
        _bias_kernel,
        out_shape=(jax.ShapeDtypeStruct((ATTN_HEADS, 3, MOBA_BLOCK, MOBA_BLOCK), F32),
                   jax.ShapeDtypeStruct((rows, cols), BF16)),
        grid=(ATTN_HEADS, n_d),
        in_specs=[pl.BlockSpec(memory_space=pltpu.SMEM),
                  pl.BlockSpec((1, MOBA_BLOCK, MOBA_BLOCK), lambda h, d: (jnp.minimum(d, 2), 0, 0)),
                  w_spec],
        out_specs=(pl.BlockSpec((1, 1, MOBA_BLOCK, MOBA_BLOCK),
                                lambda h, d: (h, jnp.minimum(d, 2), 0, 0)),
                   w_spec),
        compiler_params=_cparams(("arbitrary", "arbitrary")),
        name="bias_tiles",
    )(rel_bias.astype(F32).reshape(-1), bucket, w_in)


_NT = (((1,), (1,)), ((), ()))


def _moba_scores(own, slot0, state, q_ref, k_ref, bias_ref, km_ref, s_ref):
    bs = MOBA_BLOCK
    q = q_ref[own * bs:(own + 1) * bs, :]

    ranks = []
    if own > 0:
        gate = lax.dot_general(km_ref[...].astype(BF16), q, _NT, preferred_element_type=F32)
        blk = lax.broadcasted_iota(jnp.int32, gate.shape, 0)
        gm = jnp.where(blk < own, gate, NEG_INF)
        for n in range(own):
            g_n = gm[n:n + 1, :]
            ge = jnp.where(gm >= g_n, 1.0, 0.0)
            gt = jnp.where(gm > g_n, 1.0, 0.0)
            ranks.append(jnp.sum(jnp.where(blk < n, ge, gt), axis=0, keepdims=True))

    c1 = HEAD_DIM ** -0.5 * LOG2E
    m = None
    for idx, j in enumerate([own] + list(range(own))):
        k_j = k_ref[j * bs:(j + 1) * bs, :]
        raw = lax.dot_general(k_j, q, _NT, preferred_element_type=F32)
        if j == own:
            key = lax.broadcasted_iota(jnp.int32, (bs, bs), 0)
            qry = lax.broadcasted_iota(jnp.int32, (bs, bs), 1)
            s = jnp.where(key <= qry, raw * c1 + bias_ref[0], NEG_INF)
        elif own - j == 1:
            s = jnp.where(ranks[j] < float(MOBA_TOPK), raw * c1 + bias_ref[1], NEG_INF)
        else:
            row = jnp.where(ranks[j] < float(MOBA_TOPK), bias_ref[2, 0:1, :], NEG_INF)
            s = raw * c1 + row
        s_ref[slot0 + idx] = s
        m_j = jnp.max(s, axis=0, keepdims=True)
        m = m_j if m is None else jnp.maximum(m, m_j)
        state["m"] = m
        yield


def _moba_values(own, slot0, state, vt_ref, o_ref, s_ref):
    bs = MOBA_BLOCK
    m = state["m"]
    acc = None
    for idx, j in enumerate([own] + list(range(own))):
        p = jnp.exp2((s_ref[slot0 + idx] - m).astype(BF16))
        a_j = jnp.dot(vt_ref[:, j * bs:(j + 1) * bs], p, preferred_element_type=F32)
        acc = a_j if acc is None else acc + a_j
        if idx == own:
            o = acc[:HEAD_DIM, :] / acc[HEAD_DIM:HEAD_DIM + 1, :]
            o_ref[own * bs:(own + 1) * bs, :] = o.T.astype(o_ref.dtype)
        yield


def _moba_kernel(*refs, n_blk, n_side):
    q_ref, k_ref, v_ref, bias_ref = refs[:4]
    side_in = refs[4:4 + n_side]
    o_ref = refs[4 + n_side]
    side_out = refs[5 + n_side:5 + 2 * n_side]
    km_ref, vt_ref, s_ref = refs[-3:]
    bs = MOBA_BLOCK
    heads = bias_ref.shape[0]
    _cast_sides(side_in, side_out)

    def head(ref, hh):
        return ref.at[:, hh * HEAD_DIM:(hh + 1) * HEAD_DIM]

    for hh in range(heads):
        vt_ref[hh, HEAD_DIM:, :] = jnp.ones((vt_ref.shape[1] - HEAD_DIM, vt_ref.shape[2]), BF16)
        for n in range(n_blk):
            kb = head(k_ref, hh)[n * bs:(n + 1) * bs, :].astype(F32)
            km_ref[hh, n:n + 1, :] = jnp.mean(kb, axis=0, keepdims=True)
            vt_ref[hh, :HEAD_DIM, n * bs:(n + 1) * bs] = (
                head(v_ref, hh)[n * bs:(n + 1) * bs, :].astype(F32).T.astype(BF16))

    per_head = n_blk * (n_blk + 1) // 2
    pending = iter(())
    for own in range(n_blk):
        for hh in range(heads):
            slot0 = hh * per_head + own * (own + 1) // 2
            state = {}
            scores = _moba_scores(own, slot0, state, head(q_ref, hh), head(k_ref, hh),
                                  bias_ref.at[hh], km_ref.at[hh], s_ref)
            for _ in itertools.zip_longest(scores, pending):
                pass
            pending = _moba_values(own, slot0, state, vt_ref.at[hh], head(o_ref, hh), s_ref)
    for _ in pending:
        pass


def _moba(proj, bias_tiles, bsz, seq_len, side_casts=(), heads=2):
    n_blk = seq_len // MOBA_BLOCK
    hw = heads * HEAD_DIM
    q_off = SSM_WIDTH // hw
    k_off = q_off + ATTN_WIDTH // hw
    v_off = k_off + ATTN_WIDTH // hw
    n_slots = heads * n_blk * (n_blk + 1) // 2
    hg = ATTN_HEADS // heads
    side_specs, side_shapes = _side_casts(side_casts, bsz * hg, lambda b, h: b * hg + h)
    res = pl.pallas_call(
        functools.partial(_moba_kernel, n_blk=n_blk, n_side=len(side_casts)),
        out_shape=tuple([jax.ShapeDtypeStruct((bsz * seq_len, ATTN_WIDTH), BF16)] + side_shapes),
        grid=(bsz, hg),
        in_specs=[pl.BlockSpec((seq_len, hw), lambda b, h: (b, q_off + h)),
                  pl.BlockSpec((seq_len, hw), lambda b, h: (b, k_off + h)),
                  pl.BlockSpec((seq_len, hw), lambda b, h: (b, v_off + h)),
                  pl.BlockSpec((heads, 3, MOBA_BLOCK, MOBA_BLOCK), lambda b, h: (h, 0, 0, 0))]
        + side_specs,
        out_specs=tuple([pl.BlockSpec((seq_len, hw), lambda b, h: (b, h))] + side_specs),
        scratch_shapes=[pltpu.VMEM((heads, n_blk, HEAD_DIM), F32),
                        pltpu.VMEM((heads, HEAD_DIM + 16, seq_len), BF16),
                        pltpu.VMEM((n_slots, MOBA_BLOCK, MOBA_BLOCK), F32)],
        compiler_params=_cparams(("arbitrary", "arbitrary")),
        name="moba",
    )(proj, proj, proj, bias_tiles, *side_casts)
    return res if len(res) > 1 else res[0]


def _merge_kernel(ys_ref, ya_ref, ws_ref, wa_ref, ga_ref, gb_ref, o_ref):
    a = jnp.dot(ys_ref[...], ws_ref[...], preferred_element_type=F32)
    b = jnp.dot(ya_ref[...], wa_ref[...], preferred_element_type=F32)
    o_ref[...] = (jax.nn.sigmoid(ga_ref[...].astype(F32)) * a
                  + jax.nn.sigmoid(gb_ref[...].astype(F32)) * b).astype(o_ref.dtype)


def _merge(y_ssm, y_att, w_ps, w_pa, proj, tm=1024, tn=2048):
    t, k = y_ssm.shape
    n = w_ps.shape[1]
    ga_off = (SSM_WIDTH + 3 * ATTN_WIDTH) // tn
    gb_off = ga_off + D_MODEL // tn
    return pl.pallas_call(
        _merge_kernel,
        out_shape=jax.ShapeDtypeStruct((t, n), BF16),
        grid=(t // tm, n // tn),
        in_specs=[pl.BlockSpec((tm, k), lambda i, j: (i, 0)),
                  pl.BlockSpec((tm, k), lambda i, j: (i, 0)),
                  pl.BlockSpec((k, tn), lambda i, j: (0, j), pipeline_mode=_RESIDENT),
                  pl.BlockSpec((k, tn), lambda i, j: (0, j), pipeline_mode=_RESIDENT),
                  pl.BlockSpec((tm, tn), lambda i, j: (i, ga_off + j)),
                  pl.BlockSpec((tm, tn), lambda i, j: (i, gb_off + j))],
        out_specs=pl.BlockSpec((tm, tn), lambda i, j: (i, j)),
        compiler_params=_cparams(("parallel", "parallel")),
        name="merge",
    )(y_ssm, y_att, w_ps, w_pa, proj, proj)


def _resid_mm_kernel(a_ref, w_ref, x_ref, g_ref, o_ref):
    acc = jnp.dot(a_ref[...], w_ref[...], preferred_element_type=F32)
    o_ref[...] = x_ref[...] + g_ref[0] * acc


def _resid_mm(a, w, x2d, mod4, g_idx, seq_len, tm=1024, tn=2048):
    t, k = a.shape
    n = w.shape[1]
    per_b = seq_len // tm
    return pl.pallas_call(
        _resid_mm_kernel,
        out_shape=jax.ShapeDtypeStruct((t, n), F32),
        grid=(t // tm, n // tn),
        in_specs=[pl.BlockSpec((tm, k), lambda i, j: (i, 0)),
                  pl.BlockSpec((k, tn), lambda i, j: (0, j), pipeline_mode=_RESIDENT),
                  pl.BlockSpec((tm, tn), lambda i, j: (i, j)),
                  pl.BlockSpec((1, 1, tn), lambda i, j: ((i // per_b) * N_MOD + g_idx, 0, j))],
        out_specs=pl.BlockSpec((tm, tn), lambda i, j: (i, j)),
        compiler_params=_cparams(("parallel", "parallel")),
        name="resid_mm",
    )(a, w, x2d, mod4)


def _ff2_kernel(h_ref, w_ref, x_hbm, g_ref, gf_ref, o_ref, x_ref, x_sem, *, rows):
    i = pl.program_id(0)
    k = pl.program_id(1)
    last = pl.num_programs(1) - 1
    tm = o_ref.shape[0]

    def x_copy():
        return pltpu.make_async_copy(x_hbm.at[pl.ds(pl.multiple_of(i * tm, tm), tm), :],
                                     x_ref, x_sem)

    @pl.when(k == 0)
    def _():
        x_copy().start()
        o_ref[...] = jnp.dot(h_ref[...], w_ref[...], preferred_element_type=F32)

    @pl.when((k != 0) & (k != last))
    def _():
        o_ref[...] += jnp.dot(h_ref[...], w_ref[...], preferred_element_type=F32)

    @pl.when(k == last)
    def _():
        x_copy().wait()
        for r in range(0, tm, rows):
            acc = o_ref[r:r + rows, :] + jnp.dot(h_ref[r:r + rows, :], w_ref[...],
                                                 preferred_element_type=F32)
            xo = x_ref[r:r + rows, :] + g_ref[0] * acc
            ms = jnp.mean(xo * xo, axis=-1, keepdims=True)
            o_ref[r:r + rows, :] = xo * lax.rsqrt(ms + EPS) * gf_ref[...]


def _ff2(hid, w, x2d, mod4, g_idx, gf, seq_len, tm=1024, tk=2048):
    t, kdim = hid.shape
    n = w.shape[1]
    per_b = seq_len // tm
    assert kdim // tk >= 2, "the residual copy is started and waited in different steps"
    return pl.pallas_call(
        functools.partial(_ff2_kernel, rows=256),
        out_shape=jax.ShapeDtypeStruct((t, n), F32),
        grid=(t // tm, kdim // tk),
        in_specs=[pl.BlockSpec((tm, tk), lambda i, k: (i, k)),
                  pl.BlockSpec((tk, n), lambda i, k: (k, 0)),
                  pl.BlockSpec(memory_space=pl.ANY),
                  pl.BlockSpec((1, 1, n), lambda i, k: ((i // per_b) * N_MOD + g_idx, 0, 0)),
                  pl.BlockSpec((1, n), lambda i, k: (0, 0))],
        out_specs=pl.BlockSpec((tm, n), lambda i, k: (i, 0)),
        scratch_shapes=[pltpu.VMEM((tm, n), F32), pltpu.SemaphoreType.DMA(())],
        compiler_params=_cparams(("arbitrary", "arbitrary")),
        name="ff2_final",
    )(hid, w, x2d, mod4, gf.reshape(1, n))


def kernel(x, c, rel_bias, w_ada, b_ada, norm_mix_g, w_in, ssm_a_re, ssm_a_im, ssm_log_dt,
           ssm_b_re, ssm_b_im, ssm_c_re, ssm_c_im, ssm_d, w_glu, b_glu, w_proj_ssm,
           w_proj_attn, w_out, norm_mlp_g, w_ff1, w_ff2, norm_final_g):
    bsz, seq_len, d = x.shape
    depth = w_in.shape[0]
    assert depth == 1, "the final rms_norm is fused into the single layer's ff2 kernel"
    t = bsz * seq_len
    x2d = x.reshape(t, d)
    bias_tiles, w_in_b = _bias_tiles(rel_bias, w_in[0])

    for l in range(depth):
        mod4 = _mod(c, w_ada[l], b_ada[l]).reshape(bsz * N_MOD, 1, d)

        proj, w_glu_b, w_ps_b, w_pa_b, w_out_b = _norm_mm(
            x2d, norm_mix_g[l], mod4, 1, 0, w_in_b, seq_len, act="none",
            side_casts=(w_glu[l], w_proj_ssm[l], w_proj_attn[l], w_out[l]))

        bmat, cmat, kmat, atv, d2 = _s5prep(ssm_a_re[l], ssm_a_im[l], ssm_log_dt[l],
                                            ssm_b_re[l], ssm_b_im[l], ssm_c_re[l], ssm_c_im[l],
                                            ssm_d[l], bsz)
        y = _s5(proj.reshape(bsz, seq_len, IN_WIDTH), bmat, cmat, kmat, atv, d2)
        y_ssm = _glu(y.reshape(t, SSM_WIDTH), w_glu_b, b_glu[l])

        y_att, w_ff1_b, w_ff2_b = _moba(proj, bias_tiles, bsz, seq_len,
                                        side_casts=(w_ff1[l], w_ff2[l]))

        merged = _merge(y_ssm, y_att, w_ps_b, w_pa_b, proj)
        x2d = _resid_mm(merged, w_out_b, x2d, mod4, 2, seq_len)

        hid = _norm_mm(x2d, norm_mlp_g[l], mod4, 4, 3, w_ff1_b, seq_len, act="relu2")
        x2d = _ff2(hid, w_ff2_b, x2d, mod4, 5, norm_final_g, seq_len)
    return x2d.reshape(bsz, seq_len, d)
```

```python
import functools
import itertools
import math

import jax
import jax.numpy as jnp
from jax import lax
from jax.experimental import pallas as pl
from jax.experimental.pallas import tpu as pltpu

F32 = jnp.float32
BF16 = jnp.bfloat16

D_MODEL = 2048
SSM_WIDTH = 1024
SSM_GROUP = 16
SSM_GROUPS = 64
SSM_STATE = 64
ATTN_HEADS = 8
HEAD_DIM = 128
ATTN_WIDTH = 1024
MOBA_BLOCK = 256
MOBA_TOPK = 3
REL_BUCKETS = 32
REL_MAX_DIST = 128
D_FF = 4 * D_MODEL
N_MOD = 6
EPS = 1e-6
NEG_INF = -1e30
LOG2E = math.log2(math.e)
IN_WIDTH = SSM_WIDTH + 3 * ATTN_WIDTH + 2 * D_MODEL

VMEM_LIMIT_BYTES = 56 * 1024 * 1024

S5_TAU = 2
S5_GROUPS = 8
S5_CH = S5_GROUPS * SSM_GROUP
S5_UC = S5_TAU * S5_CH
S5_SC = 2 * S5_GROUPS * SSM_STATE
S5_NBLK = SSM_GROUPS // S5_GROUPS
S5_ROWS = 4096
S5_SUB_ROWS = 256


_RESIDENT = pl.Buffered(1)


def _cparams(sem):
    return pltpu.CompilerParams(dimension_semantics=sem,
                                vmem_limit_bytes=VMEM_LIMIT_BYTES)


def _mod_kernel(c_ref, w_ref, b_ref, relb_ref, bucket_ref, win_ref, o_ref, bias_ref, winb_ref):
    c = c_ref[...]
    ca = (c * jax.nn.sigmoid(c)).astype(BF16)
    o_ref[...] = jnp.dot(ca, w_ref[...].astype(BF16),
                         preferred_element_type=F32) + b_ref[...]
    winb_ref[...] = win_ref[...].astype(winb_ref.dtype)
    h = pl.program_id(0)

    @pl.when(h < ATTN_HEADS)
    def _():
        for d in range(bucket_ref.shape[0]):
            bucket = bucket_ref[d]
            out = jnp.zeros(bucket.shape, F32)
            for b in range(REL_BUCKETS):
                out = jnp.where(bucket == b, relb_ref[b * ATTN_HEADS + h], out)
            bias_ref[0, d] = out * LOG2E


def _t5_bucket(rel):
    n = jnp.maximum(rel, 0)
    max_exact = REL_BUCKETS // 2
    nf = jnp.maximum(n, 1).astype(F32)
    large = max_exact + (jnp.log(nf / max_exact) / math.log(REL_MAX_DIST / max_exact)
                         * (REL_BUCKETS - max_exact)).astype(jnp.int32)
    large = jnp.minimum(large, REL_BUCKETS - 1)
    return jnp.where(n < max_exact, n, large)


def _mod(c, w_ada, b_ada, rel_bias, w_in, steps=16):
    bsz, d = c.shape
    n = w_ada.shape[1]
    tn = n // steps
    assert steps >= ATTN_HEADS and tn % 128 == 0
    i = jnp.arange(MOBA_BLOCK)
    rel = (jnp.arange(3)[:, None, None] * MOBA_BLOCK + i[None, None, :] - i[None, :, None])
    bucket = _t5_bucket(rel)
    rows, cols = w_in.shape
    w_spec = pl.BlockSpec((rows // steps, cols), lambda j: (j, 0))
    return pl.pallas_call(
        _mod_kernel,
        out_shape=(jax.ShapeDtypeStruct((bsz, n), F32),
                   jax.ShapeDtypeStruct((ATTN_HEADS, 3, MOBA_BLOCK, MOBA_BLOCK), F32),
                   jax.ShapeDtypeStruct((rows, cols), BF16)),
        grid=(steps,),
        in_specs=[pl.BlockSpec((bsz, d), lambda j: (0, 0)),
                  pl.BlockSpec((d, tn), lambda j: (0, j)),
                  pl.BlockSpec((1, tn), lambda j: (0, j)),
                  pl.BlockSpec(memory_space=pltpu.SMEM),
                  pl.BlockSpec((3, MOBA_BLOCK, MOBA_BLOCK), lambda j: (0, 0, 0)),
                  w_spec],
        out_specs=(pl.BlockSpec((bsz, tn), lambda j: (0, j)),
                   pl.BlockSpec((1, 3, MOBA_BLOCK, MOBA_BLOCK),
                                lambda j: (jnp.minimum(j, ATTN_HEADS - 1), 0, 0, 0)),
                   w_spec),
        compiler_params=_cparams(("arbitrary",)),
        name="mod",
    )(c, w_ada, b_ada.reshape(1, n), rel_bias.astype(F32).reshape(-1), bucket, w_in)


def _side_casts(arrays, steps, step_of):
    specs, shapes = [], []
    for a in arrays:
        r, c = a.shape
        rb = max(16, r // steps)
        nblk = r // rb
        assert r % rb == 0 and steps % nblk == 0
        specs.append(pl.BlockSpec(
            (rb, c), lambda *ids, nblk=nblk: ((step_of(*ids) * nblk) // steps, 0)))
        shapes.append(jax.ShapeDtypeStruct((r, c), BF16))
    return specs, shapes


def _cast_sides(side_in, side_out):
    for src, dst in zip(side_in, side_out):
        dst[...] = src[...].astype(dst.dtype)


def _norm_mm_kernel(*refs, act, rows, n_side):
    x_ref, g_ref, sc_ref, sh_ref, w_ref = refs[:5]
    side_in = refs[5:5 + n_side]
    o_ref = refs[5 + n_side]
    side_out = refs[6 + n_side:6 + 2 * n_side]
    h_ref = refs[-1]

    def mm(h):
        acc = jnp.dot(h, w_ref[...], preferred_element_type=F32)
        if act == "relu2":
            acc = jnp.square(jnp.maximum(acc, 0.0))
        return acc.astype(o_ref.dtype)

    @pl.when(pl.program_id(1) == 0)
    def _():
        g = g_ref[...]
        sc = 1.0 + sc_ref[0]
        sh = sh_ref[0]
        tm = x_ref.shape[0]
        for r in range(0, tm, rows):
            x = x_ref[r:r + rows, :]
            ms = jnp.mean(x * x, axis=-1, keepdims=True)
            y = x * lax.rsqrt(ms + EPS) * g
            h = (y * sc + sh).astype(BF16)
            h_ref[r:r + rows, :] = h
            o_ref[r:r + rows, :] = mm(h)

    @pl.when(pl.program_id(1) != 0)
    def _():
        o_ref[...] = mm(h_ref[...])

    _cast_sides(side_in, side_out)


def _norm_mm(x2d, g, mod4, sc_idx, sh_idx, w, seq_len, *, act, side_casts=(),
             tm=1024, tn=2048):
    t, d = x2d.shape
    n = w.shape[1]
    per_b = seq_len // tm
    nj = n // tn
    side_specs, side_shapes = _side_casts(side_casts, (t // tm) * nj, lambda i, j: i * nj + j)
    res = pl.pallas_call(
        functools.partial(_norm_mm_kernel, act=act, rows=256, n_side=len(side_casts)),
        out_shape=tuple([jax.ShapeDtypeStruct((t, n), BF16)] + side_shapes),
        grid=(t // tm, nj),
        in_specs=[pl.BlockSpec((tm, d), lambda i, j: (i, 0)),
                  pl.BlockSpec((1, d), lambda i, j: (0, 0)),
                  pl.BlockSpec((1, 1, d), lambda i, j: ((i // per_b) * N_MOD + sc_idx, 0, 0)),
                  pl.BlockSpec((1, 1, d), lambda i, j: ((i // per_b) * N_MOD + sh_idx, 0, 0)),
                  pl.BlockSpec((d, tn), lambda i, j: (0, j))] + side_specs,
        out_specs=tuple([pl.BlockSpec((tm, tn), lambda i, j: (i, j))] + side_specs),
        scratch_shapes=[pltpu.VMEM((tm, d), BF16)],
        compiler_params=_cparams(("arbitrary", "arbitrary")),
        name="norm_mm_" + act,
    )(x2d, g.reshape(1, d), mod4, mod4, w, *side_casts)
    return res if len(res) > 1 else res[0]


def _s5prep_kernel(are_ref, aim_ref, ldt_ref, bre_ref, bim_ref, cre_ref, cim_ref,
                   at_ref, bmat_ref, cmat_ref, kmat_ref):
    tau = S5_TAU
    p, n = SSM_GROUP, SSM_STATE
    a_re = are_ref[...]
    a_im = aim_ref[...]
    dt = jnp.exp(ldt_ref[...])
    mag = jnp.exp(dt * a_re)
    abar_re = mag * jnp.cos(dt * a_im)
    abar_im = mag * jnp.sin(dt * a_im)
    den = a_re * a_re + a_im * a_im
    p_re = abar_re - 1.0
    f_re = (p_re * a_re + abar_im * a_im) / den
    f_im = (abar_im * a_re - p_re * a_im) / den

    def cmul(x, y):
        return x[0] * y[0] - x[1] * y[1], x[0] * y[1] + x[1] * y[0]

    def cat(x):
        return jnp.concatenate([x[0], x[1]], axis=-1)

    abar = (abar_re, abar_im)
    b_in = (jnp.swapaxes(bre_ref[...], 1, 2), jnp.swapaxes(bim_ref[...], 1, 2))
    bb = cmul((f_re, f_im), b_in)
    cc = (cre_ref[...], cim_ref[...])
    apow = [None, abar]
    for _ in range(tau - 1):
        apow.append(cmul(apow[-1], abar))
    at_ref[...] = cat(apow[tau])

    bw, cwt, kwt = [], [], []
    y0 = cat(bb)
    for i in range(tau):
        j = tau - 1 - i
        bw.append(bb if j == 0 else cmul(apow[j], bb))
        ca = cmul(cc, apow[i + 1])
        cwt.append((jnp.swapaxes(ca[0], 1, 2), jnp.swapaxes(-ca[1], 1, 2)))
        x = cc if i == 0 else cmul(cc, apow[i])
        x = jnp.concatenate([x[0], -x[1]], axis=-1)
        kwt.append(jnp.einsum("gqn,gpn->gqp", y0, x, precision=lax.Precision.HIGHEST,
                              preferred_element_type=F32))

    bmat_ref[...] = jnp.zeros_like(bmat_ref)
    cmat_ref[...] = jnp.zeros_like(cmat_ref)
    kmat_ref[...] = jnp.zeros_like(kmat_ref)
    half = S5_SC // 2
    for g in range(are_ref.shape[0]):
        blk, gl = divmod(g, S5_GROUPS)
        for i in range(tau):
            rows_i = slice(i * S5_CH + gl * p, i * S5_CH + (gl + 1) * p)
            for part in range(2):
                st = slice(part * half + gl * n, part * half + (gl + 1) * n)
                bmat_ref[blk, rows_i, st] = bw[i][part][g].astype(BF16)
                cmat_ref[blk, st, rows_i] = cwt[i][part][g].astype(BF16)
            for ip in range(i + 1):
                rows_ip = slice(ip * S5_CH + gl * p, ip * S5_CH + (gl + 1) * p)
                kmat_ref[blk, rows_ip, rows_i] = kwt[i - ip][g].astype(BF16)


def _s5prep(a_re, a_im, log_dt, b_re, b_im, c_re, c_im, d_skip, bsz):
    g, n = a_re.shape
    at, bmat, cmat, kmat = pl.pallas_call(
        _s5prep_kernel,
        out_shape=(jax.ShapeDtypeStruct((g, 1, 2 * n), F32),
                   jax.ShapeDtypeStruct((S5_NBLK, S5_UC, S5_SC), BF16),
                   jax.ShapeDtypeStruct((S5_NBLK, S5_SC, S5_UC), BF16),
                   jax.ShapeDtypeStruct((S5_NBLK, S5_UC, S5_UC), BF16)),
        compiler_params=pltpu.CompilerParams(vmem_limit_bytes=VMEM_LIMIT_BYTES),
        name="s5prep",
    )(a_re.reshape(g, 1, n), a_im.reshape(g, 1, n), log_dt.reshape(g, 1, 1),
      b_re, b_im, c_re, c_im)
    half = S5_SC // 2
    atv = jnp.concatenate([at[:, 0, :n].reshape(S5_NBLK, half),
                           at[:, 0, n:].reshape(S5_NBLK, half)], axis=-1)
    atv = jnp.broadcast_to(atv.reshape(S5_NBLK, 1, S5_SC), (S5_NBLK, bsz, S5_SC))
    d = jnp.broadcast_to(d_skip.reshape(S5_NBLK, 1, S5_CH), (S5_NBLK, S5_TAU, S5_CH))
    return bmat, cmat, kmat, atv, d.reshape(S5_NBLK, 1, S5_UC)


def _s5_kernel(x_ref, bw_ref, cw_ref, kw_ref, a_ref, d_ref, y_ref,
               xf_ref, u_ref, w_ref, s_ref, st_ref, *, bsz, n_sub):
    @pl.when(pl.program_id(1) == 0)
    def _():
        st_ref[...] = jnp.zeros_like(st_ref)

    half = S5_SC // 2
    a_re = a_ref[0, :, :half]
    a_im = a_ref[0, :, half:]
    s_re = st_ref[:, :half]
    s_im = st_ref[:, half:]
    d2 = d_ref[0]
    sub = u_ref.shape[0] // n_sub
    ksub = sub // bsz

    def slot(ref, c, i):
        return ref.at[:, pl.ds(c * ksub * S5_TAU + i, ksub, stride=S5_TAU), :]

    def project_in(c):
        t0 = c * ksub * S5_TAU
        xf_ref[:, t0:t0 + ksub * S5_TAU, :] = x_ref[:, t0:t0 + ksub * S5_TAU, :].astype(F32)
        for i in range(S5_TAU):
            x = pltpu.einshape("bkc->kbc", slot(xf_ref, c, i)[...]).reshape(sub, S5_CH)
            u_ref[c * sub:(c + 1) * sub, i * S5_CH:(i + 1) * S5_CH] = x.astype(BF16)
        w_ref[c * sub:(c + 1) * sub, :] = jnp.dot(u_ref[c * sub:(c + 1) * sub, :], bw_ref[0],
                                                  preferred_element_type=F32)

    project_in(0)
    for c in range(n_sub):
        r0 = c * sub
        if c + 1 < n_sub:
            project_in(c + 1)
        u = u_ref[r0:r0 + sub, :]
        for k in range(0, ksub, 2):
            before = []
            for kk in range(2):
                r = r0 + (k + kk) * bsz
                before.append(jnp.concatenate([s_re, s_im], axis=1))
                w = w_ref[r:r + bsz, :]
                s_re, s_im = (a_re * s_re - a_im * s_im + w[:, :half],
                              a_re * s_im + a_im * s_re + w[:, half:])
            s_ref[r0 + k * bsz:r0 + (k + 2) * bsz, :] = (
                jnp.concatenate(before, axis=0).astype(BF16))
        y = (jnp.dot(s_ref[r0:r0 + sub, :], cw_ref[0], preferred_element_type=F32)
             + jnp.dot(u, kw_ref[0], preferred_element_type=F32)
             + d2 * u.astype(F32))
        y = jax.nn.gelu(y)
        for i in range(S5_TAU):
            slot(y_ref, c, i)[...] = pltpu.einshape(
                "kbc->bkc", y[:, i * S5_CH:(i + 1) * S5_CH].reshape(ksub, bsz, S5_CH))
    st_ref[:, :half] = s_re
    st_ref[:, half:] = s_im


def _s5(proj, bmat, cmat, kmat, atv, d2):
    assert S5_CH == 128, "a block's channels must be one lane tile"
    bsz, seq_len, _ = proj.shape
    tpc = S5_ROWS // bsz * S5_TAU
    return pl.pallas_call(
        functools.partial(_s5_kernel, bsz=bsz, n_sub=S5_ROWS // S5_SUB_ROWS),
        out_shape=jax.ShapeDtypeStruct((bsz, seq_len, SSM_WIDTH), F32),
        grid=(S5_NBLK, seq_len // tpc),
        in_specs=[
            pl.BlockSpec((bsz, tpc, S5_CH), lambda g, c: (0, c, g)),
            pl.BlockSpec((1, S5_UC, S5_SC), lambda g, c: (g, 0, 0)),
            pl.BlockSpec((1, S5_SC, S5_UC), lambda g, c: (g, 0, 0)),
            pl.BlockSpec((1, S5_UC, S5_UC), lambda g, c: (g, 0, 0)),
            pl.BlockSpec((1, bsz, S5_SC), lambda g, c: (g, 0, 0)),
            pl.BlockSpec((1, 1, S5_UC), lambda g, c: (g, 0, 0))],
        out_specs=pl.BlockSpec((bsz, tpc, S5_CH), lambda g, c: (0, c, g)),
        scratch_shapes=[pltpu.VMEM((bsz, tpc, S5_CH), F32),
                        pltpu.VMEM((S5_ROWS, S5_UC), BF16),
                        pltpu.VMEM((S5_ROWS, S5_SC), F32),
                        pltpu.VMEM((S5_ROWS, S5_SC), BF16),
                        pltpu.VMEM((bsz, S5_SC), F32)],
        compiler_params=_cparams(("parallel", "arbitrary")),
        name="s5",
    )(proj, bmat, cmat, kmat, atv, d2)


def _glu_kernel(y_ref, w_ref, b_ref, o_ref):
    y = y_ref[...]
    z = jnp.dot(y.astype(BF16), w_ref[...], preferred_element_type=F32) + b_ref[...]
    o_ref[...] = (y * jax.nn.sigmoid(z)).astype(o_ref.dtype)


def _glu(y, w, b, tm=1024):
    t, n = y.shape
    return pl.pallas_call(
        _glu_kernel,
        out_shape=jax.ShapeDtypeStruct((t, n), BF16),
        grid=(t // tm,),
        in_specs=[pl.BlockSpec((tm, n), lambda i: (i, 0)),
                  pl.BlockSpec((n, n), lambda i: (0, 0)),
                  pl.BlockSpec((1, n), lambda i: (0, 0))],
        out_specs=pl.BlockSpec((tm, n), lambda i: (i, 0)),
        compiler_params=_cparams(("parallel",)),
        name="glu",
    )(y, w, b.reshape(1, n))


_NT = (((1,), (1,)), ((), ()))


def _moba_scores(own, slot0, state, q_ref, k_ref, bias_ref, km_ref, s_ref):
    bs = MOBA_BLOCK
    q = q_ref[own * bs:(own + 1) * bs, :]

    ranks = []
    if own > 0:
        gate = lax.dot_general(km_ref[...].astype(BF16), q, _NT, preferred_element_type=F32)
        blk = lax.broadcasted_iota(jnp.int32, gate.shape, 0)
        gm = jnp.where(blk < own, gate, NEG_INF)
        for n in range(own):
            g_n = gm[n:n + 1, :]
            ge = jnp.where(gm >= g_n, 1.0, 0.0)
            gt = jnp.where(gm > g_n, 1.0, 0.0)
            ranks.append(jnp.sum(jnp.where(blk < n, ge, gt), axis=0, keepdims=True))

    c1 = HEAD_DIM ** -0.5 * LOG2E
    m = None
    for idx, j in enumerate([own] + list(range(own))):
        k_j = k_ref[j * bs:(j + 1) * bs, :]
        raw = lax.dot_general(k_j, q, _NT, preferred_element_type=F32)
        if j == own:
            key = lax.broadcasted_iota(jnp.int32, (bs, bs), 0)
            qry = lax.broadcasted_iota(jnp.int32, (bs, bs), 1)
            s = jnp.where(key <= qry, raw * c1 + bias_ref[0], NEG_INF)
        elif own - j == 1:
            s = jnp.where(ranks[j] < float(MOBA_TOPK), raw * c1 + bias_ref[1], NEG_INF)
        else:
            row = jnp.where(ranks[j] < float(MOBA_TOPK), bias_ref[2, 0:1, :], NEG_INF)
            s = raw * c1 + row
        s_ref[slot0 + idx] = s
        m_j = jnp.max(s, axis=0, keepdims=True)
        m = m_j if m is None else jnp.maximum(m, m_j)
        state["m"] = m
        yield


def _moba_values(own, slot0, state, vt_ref, o_ref, s_ref):
    bs = MOBA_BLOCK
    m = state["m"]
    acc = None
    for idx, j in enumerate([own] + list(range(own))):
        p = jnp.exp2((s_ref[slot0 + idx] - m).astype(BF16))
        a_j = jnp.dot(vt_ref[:, j * bs:(j + 1) * bs], p, preferred_element_type=F32)
        acc = a_j if acc is None else acc + a_j
        if idx == own:
            o = acc[:HEAD_DIM, :] / acc[HEAD_DIM:HEAD_DIM + 1, :]
            o_ref[own * bs:(own + 1) * bs, :] = o.T.astype(o_ref.dtype)
        yield


def _moba_kernel(*refs, n_blk, n_side):
    q_ref, k_ref, v_ref, bias_ref = refs[:4]
    side_in = refs[4:4 + n_side]
    o_ref = refs[4 + n_side]
    side_out = refs[5 + n_side:5 + 2 * n_side]
    km_ref, vt_ref, s_ref = refs[-3:]
    bs = MOBA_BLOCK
    heads = bias_ref.shape[0]
    _cast_sides(side_in, side_out)

    def head(ref, hh):
        return ref.at[:, hh * HEAD_DIM:(hh + 1) * HEAD_DIM]

    for hh in range(heads):
        vt_ref[hh, HEAD_DIM:, :] = jnp.ones((vt_ref.shape[1] - HEAD_DIM, vt_ref.shape[2]), BF16)
        for n in range(n_blk):
            kb = head(k_ref, hh)[n * bs:(n + 1) * bs, :].astype(F32)
            km_ref[hh, n:n + 1, :] = jnp.mean(kb, axis=0, keepdims=True)
            vt_ref[hh, :HEAD_DIM, n * bs:(n + 1) * bs] = (
                head(v_ref, hh)[n * bs:(n + 1) * bs, :].astype(F32).T.astype(BF16))

    per_head = n_blk * (n_blk + 1) // 2
    pending = iter(())
    for own in range(n_blk):
        for hh in range(heads):
            slot0 = hh * per_head + own * (own + 1) // 2
            state = {}
            scores = _moba_scores(own, slot0, state, head(q_ref, hh), head(k_ref, hh),
                                  bias_ref.at[hh], km_ref.at[hh], s_ref)
            for _ in itertools.zip_longest(scores, pending):
                pass
            pending = _moba_values(own, slot0, state, vt_ref.at[hh], head(o_ref, hh), s_ref)
    for _ in pending:
        pass


def _moba(proj, bias_tiles, bsz, seq_len, side_casts=(), heads=2):
    n_blk = seq_len // MOBA_BLOCK
    hw = heads * HEAD_DIM
    q_off = SSM_WIDTH // hw
    k_off = q_off + ATTN_WIDTH // hw
    v_off = k_off + ATTN_WIDTH // hw
    n_slots = heads * n_blk * (n_blk + 1) // 2
    hg = ATTN_HEADS // heads
    side_specs, side_shapes = _side_casts(side_casts, bsz * hg, lambda b, h: b * hg + h)
    res = pl.pallas_call(
        functools.partial(_moba_kernel, n_blk=n_blk, n_side=len(side_casts)),
        out_shape=tuple([jax.ShapeDtypeStruct((bsz * seq_len, ATTN_WIDTH), BF16)] + side_shapes),
        grid=(bsz, hg),
        in_specs=[pl.BlockSpec((seq_len, hw), lambda b, h: (b, q_off + h)),
                  pl.BlockSpec((seq_len, hw), lambda b, h: (b, k_off + h)),
                  pl.BlockSpec((seq_len, hw), lambda b, h: (b, v_off + h)),
                  pl.BlockSpec((heads, 3, MOBA_BLOCK, MOBA_BLOCK), lambda b, h: (h, 0, 0, 0))]
        + side_specs,
        out_specs=tuple([pl.BlockSpec((seq_len, hw), lambda b, h: (b, h))] + side_specs),
        scratch_shapes=[pltpu.VMEM((heads, n_blk, HEAD_DIM), F32),
                        pltpu.VMEM((heads, HEAD_DIM + 16, seq_len), BF16),
                        pltpu.VMEM((n_slots, MOBA_BLOCK, MOBA_BLOCK), F32)],
        compiler_params=_cparams(("arbitrary", "arbitrary")),
        name="moba",
    )(proj, proj, proj, bias_tiles, *side_casts)
    return res if len(res) > 1 else res[0]


def _merge_kernel(ys_ref, ya_ref, ws_ref, wa_ref, ga_ref, gb_ref, o_ref):
    a = jnp.dot(ys_ref[...], ws_ref[...], preferred_element_type=F32)
    b = jnp.dot(ya_ref[...], wa_ref[...], preferred_element_type=F32)
    o_ref[...] = (jax.nn.sigmoid(ga_ref[...].astype(F32)) * a
                  + jax.nn.sigmoid(gb_ref[...].astype(F32)) * b).astype(o_ref.dtype)


def _merge(y_ssm, y_att, w_ps, w_pa, proj, tm=1024, tn=2048):
    t, k = y_ssm.shape
    n = w_ps.shape[1]
    ga_off = (SSM_WIDTH + 3 * ATTN_WIDTH) // tn
    gb_off = ga_off + D_MODEL // tn
    return pl.pallas_call(
        _merge_kernel,
        out_shape=jax.ShapeDtypeStruct((t, n), BF16),
        grid=(t // tm, n // tn),
        in_specs=[pl.BlockSpec((tm, k), lambda i, j: (i, 0)),
                  pl.BlockSpec((tm, k), lambda i, j: (i, 0)),
                  pl.BlockSpec((k, tn), lambda i, j: (0, j), pipeline_mode=_RESIDENT),
                  pl.BlockSpec((k, tn), lambda i, j: (0, j), pipeline_mode=_RESIDENT),
                  pl.BlockSpec((tm, tn), lambda i, j: (i, ga_off + j)),
                  pl.BlockSpec((tm, tn), lambda i, j: (i, gb_off + j))],
        out_specs=pl.BlockSpec((tm, tn), lambda i, j: (i, j)),
        compiler_params=_cparams(("parallel", "parallel")),
        name="merge",
    )(y_ssm, y_att, w_ps, w_pa, proj, proj)


def _resid_mm_kernel(a_ref, w_ref, x_ref, g_ref, o_ref):
    acc = jnp.dot(a_ref[...], w_ref[...], preferred_element_type=F32)
    o_ref[...] = x_ref[...] + g_ref[0] * acc


def _resid_mm(a, w, x2d, mod4, g_idx, seq_len, tm=1024, tn=2048):
    t, k = a.shape
    n = w.shape[1]
    per_b = seq_len // tm
    return pl.pallas_call(
        _resid_mm_kernel,
        out_shape=jax.ShapeDtypeStruct((t, n), F32),
        grid=(t // tm, n // tn),
        in_specs=[pl.BlockSpec((tm, k), lambda i, j: (i, 0)),
                  pl.BlockSpec((k, tn), lambda i, j: (0, j), pipeline_mode=_RESIDENT),
                  pl.BlockSpec((tm, tn), lambda i, j: (i, j)),
                  pl.BlockSpec((1, 1, tn), lambda i, j: ((i // per_b) * N_MOD + g_idx, 0, j))],
        out_specs=pl.BlockSpec((tm, tn), lambda i, j: (i, j)),
        compiler_params=_cparams(("parallel", "parallel")),
        name="resid_mm",
    )(a, w, x2d, mod4)


def _ff2_kernel(h_ref, w_ref, x_hbm, g_ref, gf_ref, o_ref, x_ref, x_sem, *, rows):
    i = pl.program_id(0)
    k = pl.program_id(1)
    last = pl.num_programs(1) - 1
    tm = o_ref.shape[0]

    def x_copy():
        return pltpu.make_async_copy(x_hbm.at[pl.ds(pl.multiple_of(i * tm, tm), tm), :],
                                     x_ref, x_sem)

    @pl.when(k == 0)
    def _():
        x_copy().start()
        o_ref[...] = jnp.dot(h_ref[...], w_ref[...], preferred_element_type=F32)

    @pl.when((k != 0) & (k != last))
    def _():
        o_ref[...] += jnp.dot(h_ref[...], w_ref[...], preferred_element_type=F32)

    @pl.when(k == last)
    def _():
        x_copy().wait()
        for r in range(0, tm, rows):
            acc = o_ref[r:r + rows, :] + jnp.dot(h_ref[r:r + rows, :], w_ref[...],
                                                 preferred_element_type=F32)
            xo = x_ref[r:r + rows, :] + g_ref[0] * acc
            ms = jnp.mean(xo * xo, axis=-1, keepdims=True)
            o_ref[r:r + rows, :] = xo * lax.rsqrt(ms + EPS) * gf_ref[...]


def _ff2(hid, w, x2d, mod4, g_idx, gf, seq_len, tm=1024, tk=2048):
    t, kdim = hid.shape
    n = w.shape[1]
    per_b = seq_len // tm
    assert kdim // tk >= 2, "the residual copy is started and waited in different steps"
    return pl.pallas_call(
        functools.partial(_ff2_kernel, rows=256),
        out_shape=jax.ShapeDtypeStruct((t, n), F32),
        grid=(t // tm, kdim // tk),
        in_specs=[pl.BlockSpec((tm, tk), lambda i, k: (i, k)),
                  pl.BlockSpec((tk, n), lambda i, k: (k, 0)),
                  pl.BlockSpec(memory_space=pl.ANY),
                  pl.BlockSpec((1, 1, n), lambda i, k: ((i // per_b) * N_MOD + g_idx, 0, 0)),
                  pl.BlockSpec((1, n), lambda i, k: (0, 0))],
        out_specs=pl.BlockSpec((tm, n), lambda i, k: (i, 0)),
        scratch_shapes=[pltpu.VMEM((tm, n), F32), pltpu.SemaphoreType.DMA(())],
        compiler_params=_cparams(("arbitrary", "arbitrary")),
        name="ff2_final",
    )(hid, w, x2d, mod4, gf.reshape(1, n))


def kernel(x, c, rel_bias, w_ada, b_ada, norm_mix_g, w_in, ssm_a_re, ssm_a_im, ssm_log_dt,
           ssm_b_re, ssm_b_im, ssm_c_re, ssm_c_im, ssm_d, w_glu, b_glu, w_proj_ssm,
           w_proj_attn, w_out, norm_mlp_g, w_ff1, w_ff2, norm_final_g):
    bsz, seq_len, d = x.shape
    depth = w_in.shape[0]
    assert depth == 1, "the final rms_norm is fused into the single layer's ff2 kernel"
    t = bsz * seq_len
    x2d = x.reshape(t, d)

    for l in range(depth):
        mod, bias_tiles, w_in_b = _mod(c, w_ada[l], b_ada[l], rel_bias, w_in[l])
        mod4 = mod.reshape(bsz * N_MOD, 1, d)

        proj, w_glu_b, w_ps_b, w_pa_b, w_out_b = _norm_mm(
            x2d, norm_mix_g[l], mod4, 1, 0, w_in_b, seq_len, act="none",
            side_casts=(w_glu[l], w_proj_ssm[l], w_proj_attn[l], w_out[l]))

        bmat, cmat, kmat, atv, d2 = _s5prep(ssm_a_re[l], ssm_a_im[l], ssm_log_dt[l],
                                            ssm_b_re[l], ssm_b_im[l], ssm_c_re[l], ssm_c_im[l],
                                            ssm_d[l], bsz)
        y = _s5(proj.reshape(bsz, seq_len, IN_WIDTH), bmat, cmat, kmat, atv, d2)
        y_ssm = _glu(y.reshape(t, SSM_WIDTH), w_glu_b, b_glu[l])

        y_att, w_ff1_b, w_ff2_b = _moba(proj, bias_tiles, bsz, seq_len,
                                        side_casts=(w_ff1[l], w_ff2[l]))

        merged = _merge(y_ssm, y_att, w_ps_b, w_pa_b, proj)
        x2d = _resid_mm(merged, w_out_b, x2d, mod4, 2, seq_len)

        hid = _norm_mm(x2d, norm_mlp_g[l], mod4, 4, 3, w_ff1_b, seq_len, act="relu2")
        x2d = _ff2(hid, w_ff2_b, x2d, mod4, 5, norm_final_g, seq_len)
    return x2d.reshape(bsz, seq_len, d)
```

```python
import functools
import itertools
import math

import jax
import jax.numpy as jnp
from jax import lax
from jax.experimental import pallas as pl
from jax.experimental.pallas import tpu as pltpu

F32 = jnp.float32
BF16 = jnp.bfloat16

D_MODEL = 2048
SSM_WIDTH = 1024
SSM_GROUP = 16
SSM_GROUPS = 64
SSM_STATE = 64
ATTN_HEADS = 8
HEAD_DIM = 128
ATTN_WIDTH = 1024
MOBA_BLOCK = 256
MOBA_TOPK = 3
REL_BUCKETS = 32
REL_MAX_DIST = 128
D_FF = 4 * D_MODEL
N_MOD = 6
EPS = 1e-6
NEG_INF = -1e30
LOG2E = math.log2(math.e)
IN_WIDTH = SSM_WIDTH + 3 * ATTN_WIDTH + 2 * D_MODEL

VMEM_LIMIT_BYTES = 56 * 1024 * 1024

S5_TAU = 2
S5_GROUPS = 8
S5_CH = S5_GROUPS * SSM_GROUP
S5_UC = S5_TAU * S5_CH
S5_SC = 2 * S5_GROUPS * SSM_STATE
S5_NBLK = SSM_GROUPS // S5_GROUPS
S5_ROWS = 4096
S5_SUB_ROWS = 256


_RESIDENT = pl.Buffered(1)


def _cparams(sem):
    return pltpu.CompilerParams(dimension_semantics=sem,
                                vmem_limit_bytes=VMEM_LIMIT_BYTES)


def _mod_kernel(c_ref, w_ref, b_ref, relb_ref, bucket_ref, win_ref, o_ref, bias_ref, winb_ref):
    c = c_ref[...]
    ca = (c * jax.nn.sigmoid(c)).astype(BF16)
    o_ref[...] = jnp.dot(ca, w_ref[...].astype(BF16),
                         preferred_element_type=F32) + b_ref[...]
    winb_ref[...] = win_ref[...].astype(winb_ref.dtype)
    h = pl.program_id(0)

    @pl.when(h < ATTN_HEADS)
    def _():
        for d in range(bucket_ref.shape[0]):
            bucket = bucket_ref[d]
            out = jnp.zeros(bucket.shape, F32)
            for b in range(REL_BUCKETS):
                out = jnp.where(bucket == b, relb_ref[b * ATTN_HEADS + h], out)
            bias_ref[0, d] = out * LOG2E


def _t5_bucket(rel):
    n = jnp.maximum(rel, 0)
    max_exact = REL_BUCKETS // 2
    nf = jnp.maximum(n, 1).astype(F32)
    large = max_exact + (jnp.log(nf / max_exact) / math.log(REL_MAX_DIST / max_exact)
                         * (REL_BUCKETS - max_exact)).astype(jnp.int32)
    large = jnp.minimum(large, REL_BUCKETS - 1)
    return jnp.where(n < max_exact, n, large)


def _mod(c, w_ada, b_ada, rel_bias, w_in, steps=16):
    bsz, d = c.shape
    n = w_ada.shape[1]
    tn = n // steps
    assert steps >= ATTN_HEADS and tn % 128 == 0
    i = jnp.arange(MOBA_BLOCK)
    rel = (jnp.arange(3)[:, None, None] * MOBA_BLOCK + i[None, None, :] - i[None, :, None])
    bucket = _t5_bucket(rel)
    rows, cols = w_in.shape
    w_spec = pl.BlockSpec((rows // steps, cols), lambda j: (j, 0))
    return pl.pallas_call(
        _mod_kernel,
        out_shape=(jax.ShapeDtypeStruct((bsz, n), F32),
                   jax.ShapeDtypeStruct((ATTN_HEADS, 3, MOBA_BLOCK, MOBA_BLOCK), F32),
                   jax.ShapeDtypeStruct((rows, cols), BF16)),
        grid=(steps,),
        in_specs=[pl.BlockSpec((bsz, d), lambda j: (0, 0)),
                  pl.BlockSpec((d, tn), lambda j: (0, j)),
                  pl.BlockSpec((1, tn), lambda j: (0, j)),
                  pl.BlockSpec(memory_space=pltpu.SMEM),
                  pl.BlockSpec((3, MOBA_BLOCK, MOBA_BLOCK), lambda j: (0, 0, 0)),
                  w_spec],
        out_specs=(pl.BlockSpec((bsz, tn), lambda j: (0, j)),
                   pl.BlockSpec((1, 3, MOBA_BLOCK, MOBA_BLOCK),
                                lambda j: (jnp.minimum(j, ATTN_HEADS - 1), 0, 0, 0)),
                   w_spec),
        compiler_params=_cparams(("arbitrary",)),
        name="mod",
    )(c, w_ada, b_ada.reshape(1, n), rel_bias.astype(F32).reshape(-1), bucket, w_in)


def _side_casts(arrays, steps, step_of):
    specs, shapes = [], []
    for a in arrays:
        r, c = a.shape
        rb = max(16, r // steps)
        nblk = r // rb
        assert r % rb == 0 and steps % nblk == 0
        specs.append(pl.BlockSpec(
            (rb, c), lambda *ids, nblk=nblk: ((step_of(*ids) * nblk) // steps, 0)))
        shapes.append(jax.ShapeDtypeStruct((r, c), BF16))
    return specs, shapes


def _cast_sides(side_in, side_out):
    for src, dst in zip(side_in, side_out):
        dst[...] = src[...].astype(dst.dtype)


def _norm_mm_kernel(*refs, act, rows, n_side):
    x_ref, g_ref, sc_ref, sh_ref, w_ref = refs[:5]
    side_in = refs[5:5 + n_side]
    o_ref = refs[5 + n_side]
    side_out = refs[6 + n_side:6 + 2 * n_side]
    h_ref = refs[-1]

    def mm(h):
        acc = jnp.dot(h, w_ref[...], preferred_element_type=F32)
        if act == "relu2":
            acc = jnp.square(jnp.maximum(acc, 0.0))
        return acc.astype(o_ref.dtype)

    @pl.when(pl.program_id(1) == 0)
    def _():
        g = g_ref[...]
        sc = 1.0 + sc_ref[0]
        sh = sh_ref[0]
        tm = x_ref.shape[0]
        for r in range(0, tm, rows):
            x = x_ref[r:r + rows, :]
            ms = jnp.mean(x * x, axis=-1, keepdims=True)
            y = x * lax.rsqrt(ms + EPS) * g
            h = (y * sc + sh).astype(BF16)
            h_ref[r:r + rows, :] = h
            o_ref[r:r + rows, :] = mm(h)

    @pl.when(pl.program_id(1) != 0)
    def _():
        o_ref[...] = mm(h_ref[...])

    _cast_sides(side_in, side_out)


def _norm_mm(x2d, g, mod4, sc_idx, sh_idx, w, seq_len, *, act, side_casts=(),
             tm=1024, tn=2048):
    t, d = x2d.shape
    n = w.shape[1]
    per_b = seq_len // tm
    nj = n // tn
    side_specs, side_shapes = _side_casts(side_casts, (t // tm) * nj, lambda i, j: i * nj + j)
    res = pl.pallas_call(
        functools.partial(_norm_mm_kernel, act=act, rows=256, n_side=len(side_casts)),
        out_shape=tuple([jax.ShapeDtypeStruct((t, n), BF16)] + side_shapes),
        grid=(t // tm, nj),
        in_specs=[pl.BlockSpec((tm, d), lambda i, j: (i, 0)),
                  pl.BlockSpec((1, d), lambda i, j: (0, 0)),
                  pl.BlockSpec((1, 1, d), lambda i, j: ((i // per_b) * N_MOD + sc_idx, 0, 0)),
                  pl.BlockSpec((1, 1, d), lambda i, j: ((i // per_b) * N_MOD + sh_idx, 0, 0)),
                  pl.BlockSpec((d, tn), lambda i, j: (0, j))] + side_specs,
        out_specs=tuple([pl.BlockSpec((tm, tn), lambda i, j: (i, j))] + side_specs),
        scratch_shapes=[pltpu.VMEM((tm, d), BF16)],
        compiler_params=_cparams(("arbitrary", "arbitrary")),
        name="norm_mm_" + act,
    )(x2d, g.reshape(1, d), mod4, mod4, w, *side_casts)
    return res if len(res) > 1 else res[0]


def _s5prep_kernel(are_ref, aim_ref, ldt_ref, bre_ref, bim_ref, cre_ref, cim_ref,
                   at_ref, bmat_ref, cmat_ref, kmat_ref):
    tau = S5_TAU
    p, n = SSM_GROUP, SSM_STATE
    a_re = are_ref[...]
    a_im = aim_ref[...]
    dt = jnp.exp(ldt_ref[...])
    mag = jnp.exp(dt * a_re)
    abar_re = mag * jnp.cos(dt * a_im)
    abar_im = mag * jnp.sin(dt * a_im)
    den = a_re * a_re + a_im * a_im
    p_re = abar_re - 1.0
    f_re = (p_re * a_re + abar_im * a_im) / den
    f_im = (abar_im * a_re - p_re * a_im) / den

    def cmul(x, y):
        return x[0] * y[0] - x[1] * y[1], x[0] * y[1] + x[1] * y[0]

    def cat(x):
        return jnp.concatenate([x[0], x[1]], axis=-1)

    abar = (abar_re, abar_im)
    b_in = (jnp.swapaxes(bre_ref[...], 1, 2), jnp.swapaxes(bim_ref[...], 1, 2))
    bb = cmul((f_re, f_im), b_in)
    cc = (cre_ref[...], cim_ref[...])
    apow = [None, abar]
    for _ in range(tau - 1):
        apow.append(cmul(apow[-1], abar))
    at_ref[...] = cat(apow[tau])

    bw, cwt, kwt = [], [], []
    y0 = cat(bb)
    for i in range(tau):
        j = tau - 1 - i
        bw.append(bb if j == 0 else cmul(apow[j], bb))
        ca = cmul(cc, apow[i + 1])
        cwt.append((jnp.swapaxes(ca[0], 1, 2), jnp.swapaxes(-ca[1], 1, 2)))
        x = cc if i == 0 else cmul(cc, apow[i])
        x = jnp.concatenate([x[0], -x[1]], axis=-1)
        kwt.append(jnp.einsum("gqn,gpn->gqp", y0, x, precision=lax.Precision.HIGHEST,
                              preferred_element_type=F32))

    bmat_ref[...] = jnp.zeros_like(bmat_ref)
    cmat_ref[...] = jnp.zeros_like(cmat_ref)
    kmat_ref[...] = jnp.zeros_like(kmat_ref)
    half = S5_SC // 2
    for g in range(are_ref.shape[0]):
        blk, gl = divmod(g, S5_GROUPS)
        for i in range(tau):
            rows_i = slice(i * S5_CH + gl * p, i * S5_CH + (gl + 1) * p)
            for part in range(2):
                st = slice(part * half + gl * n, part * half + (gl + 1) * n)
                bmat_ref[blk, rows_i, st] = bw[i][part][g].astype(BF16)
                cmat_ref[blk, st, rows_i] = cwt[i][part][g].astype(BF16)
            for ip in range(i + 1):
                rows_ip = slice(ip * S5_CH + gl * p, ip * S5_CH + (gl + 1) * p)
                kmat_ref[blk, rows_ip, rows_i] = kwt[i - ip][g].astype(BF16)


def _s5prep(a_re, a_im, log_dt, b_re, b_im, c_re, c_im, d_skip, bsz):
    g, n = a_re.shape
    at, bmat, cmat, kmat = pl.pallas_call(
        _s5prep_kernel,
        out_shape=(jax.ShapeDtypeStruct((g, 1, 2 * n), F32),
                   jax.ShapeDtypeStruct((S5_NBLK, S5_UC, S5_SC), BF16),
                   jax.ShapeDtypeStruct((S5_NBLK, S5_SC, S5_UC), BF16),
                   jax.ShapeDtypeStruct((S5_NBLK, S5_UC, S5_UC), BF16)),
        compiler_params=pltpu.CompilerParams(vmem_limit_bytes=VMEM_LIMIT_BYTES),
        name="s5prep",
    )(a_re.reshape(g, 1, n), a_im.reshape(g, 1, n), log_dt.reshape(g, 1, 1),
      b_re, b_im, c_re, c_im)
    half = S5_SC // 2
    atv = jnp.concatenate([at[:, 0, :n].reshape(S5_NBLK, half),
                           at[:, 0, n:].reshape(S5_NBLK, half)], axis=-1)
    atv = jnp.broadcast_to(atv.reshape(S5_NBLK, 1, S5_SC), (S5_NBLK, bsz, S5_SC))
    d = jnp.broadcast_to(d_skip.reshape(S5_NBLK, 1, S5_CH), (S5_NBLK, S5_TAU, S5_CH))
    return bmat, cmat, kmat, atv, d.reshape(S5_NBLK, 1, S5_UC)


def _s5_kernel(x_ref, bw_ref, cw_ref, kw_ref, a_ref, d_ref, y_ref,
               xf_ref, u_ref, w_ref, s_ref, st_ref, *, bsz, n_sub):
    @pl.when(pl.program_id(1) == 0)
    def _():
        st_ref[...] = jnp.zeros_like(st_ref)

    half = S5_SC // 2
    a_re = a_ref[0, :, :half]
    a_im = a_ref[0, :, half:]
    s_re = st_ref[:, :half]
    s_im = st_ref[:, half:]
    d2 = d_ref[0]
    sub = u_ref.shape[0] // n_sub
    ksub = sub // bsz

    def slot(ref, c, i):
        return ref.at[:, pl.ds(c * ksub * S5_TAU + i, ksub, stride=S5_TAU), :]

    def project_in(c):
        t0 = c * ksub * S5_TAU
        xf_ref[:, t0:t0 + ksub * S5_TAU, :] = x_ref[:, t0:t0 + ksub * S5_TAU, :].astype(F32)
        for i in range(S5_TAU):
            x = pltpu.einshape("bkc->kbc", slot(xf_ref, c, i)[...]).reshape(sub, S5_CH)
            u_ref[c * sub:(c + 1) * sub, i * S5_CH:(i + 1) * S5_CH] = x.astype(BF16)
        w_ref[c * sub:(c + 1) * sub, :] = jnp.dot(u_ref[c * sub:(c + 1) * sub, :], bw_ref[0],
                                                  preferred_element_type=F32)

    project_in(0)
    for c in range(n_sub):
        r0 = c * sub
        if c + 1 < n_sub:
            project_in(c + 1)
        u = u_ref[r0:r0 + sub, :]
        for k in range(0, ksub, 2):
            before = []
            for kk in range(2):
                r = r0 + (k + kk) * bsz
                before.append(jnp.concatenate([s_re, s_im], axis=1))
                w = w_ref[r:r + bsz, :]
                s_re, s_im = (a_re * s_re - a_im * s_im + w[:, :half],
                              a_re * s_im + a_im * s_re + w[:, half:])
            s_ref[r0 + k * bsz:r0 + (k + 2) * bsz, :] = (
                jnp.concatenate(before, axis=0).astype(BF16))
        y = (jnp.dot(s_ref[r0:r0 + sub, :], cw_ref[0], preferred_element_type=F32)
             + jnp.dot(u, kw_ref[0], preferred_element_type=F32)
             + d2 * u.astype(F32))
        y = jax.nn.gelu(y)
        for i in range(S5_TAU):
            slot(y_ref, c, i)[...] = pltpu.einshape(
                "kbc->bkc", y[:, i * S5_CH:(i + 1) * S5_CH].reshape(ksub, bsz, S5_CH))
    st_ref[:, :half] = s_re
    st_ref[:, half:] = s_im


def _s5(proj, bmat, cmat, kmat, atv, d2):
    assert S5_CH == 128, "a block's channels must be one lane tile"
    bsz, seq_len, _ = proj.shape
    tpc = S5_ROWS // bsz * S5_TAU
    return pl.pallas_call(
        functools.partial(_s5_kernel, bsz=bsz, n_sub=S5_ROWS // S5_SUB_ROWS),
        out_shape=jax.ShapeDtypeStruct((bsz, seq_len, SSM_WIDTH), F32),
        grid=(S5_NBLK, seq_len // tpc),
        in_specs=[
            pl.BlockSpec((bsz, tpc, S5_CH), lambda g, c: (0, c, g)),
            pl.BlockSpec((1, S5_UC, S5_SC), lambda g, c: (g, 0, 0)),
            pl.BlockSpec((1, S5_SC, S5_UC), lambda g, c: (g, 0, 0)),
            pl.BlockSpec((1, S5_UC, S5_UC), lambda g, c: (g, 0, 0)),
            pl.BlockSpec((1, bsz, S5_SC), lambda g, c: (g, 0, 0)),
            pl.BlockSpec((1, 1, S5_UC), lambda g, c: (g, 0, 0))],
        out_specs=pl.BlockSpec((bsz, tpc, S5_CH), lambda g, c: (0, c, g)),
        scratch_shapes=[pltpu.VMEM((bsz, tpc, S5_CH), F32),
                        pltpu.VMEM((S5_ROWS, S5_UC), BF16),
                        pltpu.VMEM((S5_ROWS, S5_SC), F32),
                        pltpu.VMEM((S5_ROWS, S5_SC), BF16),
                        pltpu.VMEM((bsz, S5_SC), F32)],
        compiler_params=_cparams(("parallel", "arbitrary")),
        name="s5",
    )(proj, bmat, cmat, kmat, atv, d2)


_NT = (((1,), (1,)), ((), ()))


def _moba_scores(own, slot0, state, q_ref, k_ref, bias_ref, km_ref, s_ref):
    bs = MOBA_BLOCK
    q = q_ref[own * bs:(own + 1) * bs, :]

    ranks = []
    if own > 0:
        gate = lax.dot_general(km_ref[...].astype(BF16), q, _NT, preferred_element_type=F32)
        blk = lax.broadcasted_iota(jnp.int32, gate.shape, 0)
        gm = jnp.where(blk < own, gate, NEG_INF)
        for n in range(own):
            g_n = gm[n:n + 1, :]
            ge = jnp.where(gm >= g_n, 1.0, 0.0)
            gt = jnp.where(gm > g_n, 1.0, 0.0)
            ranks.append(jnp.sum(jnp.where(blk < n, ge, gt), axis=0, keepdims=True))

    c1 = HEAD_DIM ** -0.5 * LOG2E
    m = None
    for idx, j in enumerate([own] + list(range(own))):
        k_j = k_ref[j * bs:(j + 1) * bs, :]
        raw = lax.dot_general(k_j, q, _NT, preferred_element_type=F32)
        if j == own:
            key = lax.broadcasted_iota(jnp.int32, (bs, bs), 0)
            qry = lax.broadcasted_iota(jnp.int32, (bs, bs), 1)
            s = jnp.where(key <= qry, raw * c1 + bias_ref[0], NEG_INF)
        elif own - j == 1:
            s = jnp.where(ranks[j] < float(MOBA_TOPK), raw * c1 + bias_ref[1], NEG_INF)
        else:
            row = jnp.where(ranks[j] < float(MOBA_TOPK), bias_ref[2, 0:1, :], NEG_INF)
            s = raw * c1 + row
        s_ref[slot0 + idx] = s
        m_j = jnp.max(s, axis=0, keepdims=True)
        m = m_j if m is None else jnp.maximum(m, m_j)
        state["m"] = m
        yield


def _moba_values(own, slot0, state, vt_ref, o_ref, s_ref):
    bs = MOBA_BLOCK
    m = state["m"]
    acc = None
    for idx, j in enumerate([own] + list(range(own))):
        p = jnp.exp2((s_ref[slot0 + idx] - m).astype(BF16))
        a_j = jnp.dot(vt_ref[:, j * bs:(j + 1) * bs], p, preferred_element_type=F32)
        acc = a_j if acc is None else acc + a_j
        if idx == own:
            o = acc[:HEAD_DIM, :] / acc[HEAD_DIM:HEAD_DIM + 1, :]
            o_ref[own * bs:(own + 1) * bs, :] = o.T.astype(o_ref.dtype)
        yield


def _moba_kernel(*refs, n_blk, n_side):
    q_ref, k_ref, v_ref, bias_ref = refs[:4]
    side_in = refs[4:4 + n_side]
    o_ref = refs[4 + n_side]
    side_out = refs[5 + n_side:5 + 2 * n_side]
    km_ref, vt_ref, s_ref = refs[-3:]
    bs = MOBA_BLOCK
    heads = bias_ref.shape[0]
    _cast_sides(side_in, side_out)

    def head(ref, hh):
        return ref.at[:, hh * HEAD_DIM:(hh + 1) * HEAD_DIM]

    for hh in range(heads):
        vt_ref[hh, HEAD_DIM:, :] = jnp.ones((vt_ref.shape[1] - HEAD_DIM, vt_ref.shape[2]), BF16)
        for n in range(n_blk):
            kb = head(k_ref, hh)[n * bs:(n + 1) * bs, :].astype(F32)
            km_ref[hh, n:n + 1, :] = jnp.mean(kb, axis=0, keepdims=True)
            vt_ref[hh, :HEAD_DIM, n * bs:(n + 1) * bs] = (
                head(v_ref, hh)[n * bs:(n + 1) * bs, :].astype(F32).T.astype(BF16))

    per_head = n_blk * (n_blk + 1) // 2
    pending = iter(())
    for own in range(n_blk):
        for hh in range(heads):
            slot0 = hh * per_head + own * (own + 1) // 2
            state = {}
            scores = _moba_scores(own, slot0, state, head(q_ref, hh), head(k_ref, hh),
                                  bias_ref.at[hh], km_ref.at[hh], s_ref)
            for _ in itertools.zip_longest(scores, pending):
                pass
            pending = _moba_values(own, slot0, state, vt_ref.at[hh], head(o_ref, hh), s_ref)
    for _ in pending:
        pass


def _moba(proj, bias_tiles, bsz, seq_len, side_casts=(), heads=2):
    n_blk = seq_len // MOBA_BLOCK
    hw = heads * HEAD_DIM
    q_off = SSM_WIDTH // hw
    k_off = q_off + ATTN_WIDTH // hw
    v_off = k_off + ATTN_WIDTH // hw
    n_slots = heads * n_blk * (n_blk + 1) // 2
    hg = ATTN_HEADS // heads
    side_specs, side_shapes = _side_casts(side_casts, bsz * hg, lambda b, h: b * hg + h)
    res = pl.pallas_call(
        functools.partial(_moba_kernel, n_blk=n_blk, n_side=len(side_casts)),
        out_shape=tuple([jax.ShapeDtypeStruct((bsz * seq_len, ATTN_WIDTH), BF16)] + side_shapes),
        grid=(bsz, hg),
        in_specs=[pl.BlockSpec((seq_len, hw), lambda b, h: (b, q_off + h)),
                  pl.BlockSpec((seq_len, hw), lambda b, h: (b, k_off + h)),
                  pl.BlockSpec((seq_len, hw), lambda b, h: (b, v_off + h)),
                  pl.BlockSpec((heads, 3, MOBA_BLOCK, MOBA_BLOCK), lambda b, h: (h, 0, 0, 0))]
        + side_specs,
        out_specs=tuple([pl.BlockSpec((seq_len, hw), lambda b, h: (b, h))] + side_specs),
        scratch_shapes=[pltpu.VMEM((heads, n_blk, HEAD_DIM), F32),
                        pltpu.VMEM((heads, HEAD_DIM + 16, seq_len), BF16),
                        pltpu.VMEM((n_slots, MOBA_BLOCK, MOBA_BLOCK), F32)],
        compiler_params=_cparams(("arbitrary", "arbitrary")),
        name="moba",
    )(proj, proj, proj, bias_tiles, *side_casts)
    return res if len(res) > 1 else res[0]


def _merge_kernel(y_ref, ya_ref, wg_ref, bg_ref, ws_ref, wa_ref, ga_ref, gb_ref, o_ref):
    y = y_ref[...]
    z = jnp.dot(y.astype(BF16), wg_ref[...], preferred_element_type=F32) + bg_ref[...]
    y_ssm = (y * jax.nn.sigmoid(z)).astype(BF16)
    a = jnp.dot(y_ssm, ws_ref[...], preferred_element_type=F32)
    b = jnp.dot(ya_ref[...], wa_ref[...], preferred_element_type=F32)
    o_ref[...] = (jax.nn.sigmoid(ga_ref[...].astype(F32)) * a
                  + jax.nn.sigmoid(gb_ref[...].astype(F32)) * b).astype(o_ref.dtype)


def _merge(y, y_att, w_glu, b_glu, w_ps, w_pa, proj, tm=512):
    t, k = y.shape
    n = w_ps.shape[1]
    ga_off = (SSM_WIDTH + 3 * ATTN_WIDTH) // n
    gb_off = ga_off + 1
    return pl.pallas_call(
        _merge_kernel,
        out_shape=jax.ShapeDtypeStruct((t, n), BF16),
        grid=(t // tm,),
        in_specs=[pl.BlockSpec((tm, k), lambda i: (i, 0)),
                  pl.BlockSpec((tm, k), lambda i: (i, 0)),
                  pl.BlockSpec((k, k), lambda i: (0, 0), pipeline_mode=_RESIDENT),
                  pl.BlockSpec((1, k), lambda i: (0, 0)),
                  pl.BlockSpec((k, n), lambda i: (0, 0), pipeline_mode=_RESIDENT),
                  pl.BlockSpec((k, n), lambda i: (0, 0), pipeline_mode=_RESIDENT),
                  pl.BlockSpec((tm, n), lambda i: (i, ga_off)),
                  pl.BlockSpec((tm, n), lambda i: (i, gb_off))],
        out_specs=pl.BlockSpec((tm, n), lambda i: (i, 0)),
        compiler_params=_cparams(("parallel",)),
        name="merge",
    )(y, y_att, w_glu, b_glu.reshape(1, k), w_ps, w_pa, proj, proj)


def _resid_mm_kernel(a_ref, w_ref, x_ref, g_ref, o_ref):
    acc = jnp.dot(a_ref[...], w_ref[...], preferred_element_type=F32)
    o_ref[...] = x_ref[...] + g_ref[0] * acc


def _resid_mm(a, w, x2d, mod4, g_idx, seq_len, tm=1024, tn=2048):
    t, k = a.shape
    n = w.shape[1]
    per_b = seq_len // tm
    return pl.pallas_call(
        _resid_mm_kernel,
        out_shape=jax.ShapeDtypeStruct((t, n), F32),
        grid=(t // tm, n // tn),
        in_specs=[pl.BlockSpec((tm, k), lambda i, j: (i, 0)),
                  pl.BlockSpec((k, tn), lambda i, j: (0, j), pipeline_mode=_RESIDENT),
                  pl.BlockSpec((tm, tn), lambda i, j: (i, j)),
                  pl.BlockSpec((1, 1, tn), lambda i, j: ((i // per_b) * N_MOD + g_idx, 0, j))],
        out_specs=pl.BlockSpec((tm, tn), lambda i, j: (i, j)),
        compiler_params=_cparams(("parallel", "parallel")),
        name="resid_mm",
    )(a, w, x2d, mod4)


def _ff2_kernel(h_ref, w_ref, x_hbm, g_ref, gf_ref, o_ref, x_ref, x_sem, *, rows):
    i = pl.program_id(0)
    k = pl.program_id(1)
    last = pl.num_programs(1) - 1
    tm = o_ref.shape[0]

    def x_copy():
        return pltpu.make_async_copy(x_hbm.at[pl.ds(pl.multiple_of(i * tm, tm), tm), :],
                                     x_ref, x_sem)

    @pl.when(k == 0)
    def _():
        x_copy().start()
        o_ref[...] = jnp.dot(h_ref[...], w_ref[...], preferred_element_type=F32)

    @pl.when((k != 0) & (k != last))
    def _():
        o_ref[...] += jnp.dot(h_ref[...], w_ref[...], preferred_element_type=F32)

    @pl.when(k == last)
    def _():
        x_copy().wait()
        for r in range(0, tm, rows):
            acc = o_ref[r:r + rows, :] + jnp.dot(h_ref[r:r + rows, :], w_ref[...],
                                                 preferred_element_type=F32)
            xo = x_ref[r:r + rows, :] + g_ref[0] * acc
            ms = jnp.mean(xo * xo, axis=-1, keepdims=True)
            o_ref[r:r + rows, :] = xo * lax.rsqrt(ms + EPS) * gf_ref[...]


def _ff2(hid, w, x2d, mod4, g_idx, gf, seq_len, tm=1024, tk=2048):
    t, kdim = hid.shape
    n = w.shape[1]
    per_b = seq_len // tm
    assert kdim // tk >= 2, "the residual copy is started and waited in different steps"
    return pl.pallas_call(
        functools.partial(_ff2_kernel, rows=256),
        out_shape=jax.ShapeDtypeStruct((t, n), F32),
        grid=(t // tm, kdim // tk),
        in_specs=[pl.BlockSpec((tm, tk), lambda i, k: (i, k)),
                  pl.BlockSpec((tk, n), lambda i, k: (k, 0)),
                  pl.BlockSpec(memory_space=pl.ANY),
                  pl.BlockSpec((1, 1, n), lambda i, k: ((i // per_b) * N_MOD + g_idx, 0, 0)),
                  pl.BlockSpec((1, n), lambda i, k: (0, 0))],
        out_specs=pl.BlockSpec((tm, n), lambda i, k: (i, 0)),
        scratch_shapes=[pltpu.VMEM((tm, n), F32), pltpu.SemaphoreType.DMA(())],
        compiler_params=_cparams(("arbitrary", "arbitrary")),
        name="ff2_final",
    )(hid, w, x2d, mod4, gf.reshape(1, n))


def kernel(x, c, rel_bias, w_ada, b_ada, norm_mix_g, w_in, ssm_a_re, ssm_a_im, ssm_log_dt,
           ssm_b_re, ssm_b_im, ssm_c_re, ssm_c_im, ssm_d, w_glu, b_glu, w_proj_ssm,
           w_proj_attn, w_out, norm_mlp_g, w_ff1, w_ff2, norm_final_g):
    bsz, seq_len, d = x.shape
    depth = w_in.shape[0]
    assert depth == 1, "the final rms_norm is fused into the single layer's ff2 kernel"
    t = bsz * seq_len
    x2d = x.reshape(t, d)

    for l in range(depth):
        mod, bias_tiles, w_in_b = _mod(c, w_ada[l], b_ada[l], rel_bias, w_in[l])
        mod4 = mod.reshape(bsz * N_MOD, 1, d)

        proj, w_glu_b, w_ps_b, w_pa_b, w_out_b = _norm_mm(
            x2d, norm_mix_g[l], mod4, 1, 0, w_in_b, seq_len, act="none",
            side_casts=(w_glu[l], w_proj_ssm[l], w_proj_attn[l], w_out[l]))

        bmat, cmat, kmat, atv, d2 = _s5prep(ssm_a_re[l], ssm_a_im[l], ssm_log_dt[l],
                                            ssm_b_re[l], ssm_b_im[l], ssm_c_re[l], ssm_c_im[l],
                                            ssm_d[l], bsz)
        y = _s5(proj.reshape(bsz, seq_len, IN_WIDTH), bmat, cmat, kmat, atv, d2)

        y_att, w_ff1_b, w_ff2_b = _moba(proj, bias_tiles, bsz, seq_len,
                                        side_casts=(w_ff1[l], w_ff2[l]))

        merged = _merge(y.reshape(t, SSM_WIDTH), y_att, w_glu_b, b_glu[l], w_ps_b, w_pa_b, proj)
        x2d = _resid_mm(merged, w_out_b, x2d, mod4, 2, seq_len)

        hid = _norm_mm(x2d, norm_mlp_g[l], mod4, 4, 3, w_ff1_b, seq_len, act="relu2")
        x2d = _ff2(hid, w_ff2_b, x2d, mod4, 5, norm_final_g, seq_len)
    return x2d.reshape(bsz, seq_len, d)
```

```python
import functools
import itertools
import math

import jax
import jax.numpy as jnp
from jax import lax
from jax.experimental import pallas as pl
from jax.experimental.pallas import tpu as pltpu

F32 = jnp.float32
BF16 = jnp.bfloat16

D_MODEL = 2048
SSM_WIDTH = 1024
SSM_GROUP = 16
SSM_GROUPS = 64
SSM_STATE = 64
ATTN_HEADS = 8
HEAD_DIM = 128
ATTN_WIDTH = 1024
MOBA_BLOCK = 256
MOBA_TOPK = 3
REL_BUCKETS = 32
REL_MAX_DIST = 128
N_MOD = 6
EPS = 1e-6
NEG_INF = -1e30
LOG2E = math.log2(math.e)
IN_WIDTH = SSM_WIDTH + 3 * ATTN_WIDTH + 2 * D_MODEL

VMEM_LIMIT_BYTES = 56 * 1024 * 1024
LANES = 128
BF16_ROW_TILE = 16

S5_TAU = 2
S5_GROUPS = 8
S5_CH = S5_GROUPS * SSM_GROUP
S5_UC = S5_TAU * S5_CH
S5_SC = 2 * S5_GROUPS * SSM_STATE
S5_NBLK = SSM_GROUPS // S5_GROUPS
S5_ROWS = 4096
S5_SUB_ROWS = 256


_RESIDENT = pl.Buffered(1)


def _cparams(sem):
    return pltpu.CompilerParams(dimension_semantics=sem,
                                vmem_limit_bytes=VMEM_LIMIT_BYTES)


def _mod_kernel(c_ref, w_ref, b_ref, relb_ref, bucket_ref, win_ref, o_ref, bias_ref, winb_ref):
    c = c_ref[...]
    ca = (c * jax.nn.sigmoid(c)).astype(BF16)
    o_ref[...] = jnp.dot(ca, w_ref[...].astype(BF16),
                         preferred_element_type=F32) + b_ref[...]
    winb_ref[...] = win_ref[...].astype(winb_ref.dtype)
    h = pl.program_id(0)

    @pl.when(h < ATTN_HEADS)
    def _():
        for d in range(bucket_ref.shape[0]):
            bucket = bucket_ref[d]
            out = jnp.zeros(bucket.shape, F32)
            for b in range(REL_BUCKETS):
                out = jnp.where(bucket == b, relb_ref[b * ATTN_HEADS + h], out)
            bias_ref[0, d] = out * LOG2E


def _t5_bucket(rel):
    n = jnp.maximum(rel, 0)
    max_exact = REL_BUCKETS // 2
    nf = jnp.maximum(n, 1).astype(F32)
    large = max_exact + (jnp.log(nf / max_exact) / math.log(REL_MAX_DIST / max_exact)
                         * (REL_BUCKETS - max_exact)).astype(jnp.int32)
    large = jnp.minimum(large, REL_BUCKETS - 1)
    return jnp.where(n < max_exact, n, large)


def _mod(c, w_ada, b_ada, rel_bias, w_in, steps=16):
    bsz, d = c.shape
    n = w_ada.shape[1]
    tn = n // steps
    assert steps >= ATTN_HEADS and tn % LANES == 0
    i = jnp.arange(MOBA_BLOCK)
    rel = (jnp.arange(3)[:, None, None] * MOBA_BLOCK + i[None, None, :] - i[None, :, None])
    bucket = _t5_bucket(rel)
    rows, cols = w_in.shape
    w_spec = pl.BlockSpec((rows // steps, cols), lambda j: (j, 0))
    return pl.pallas_call(
        _mod_kernel,
        out_shape=(jax.ShapeDtypeStruct((bsz, n), F32),
                   jax.ShapeDtypeStruct((ATTN_HEADS, 3, MOBA_BLOCK, MOBA_BLOCK), F32),
                   jax.ShapeDtypeStruct((rows, cols), BF16)),
        grid=(steps,),
        in_specs=[pl.BlockSpec((bsz, d), lambda j: (0, 0)),
                  pl.BlockSpec((d, tn), lambda j: (0, j)),
                  pl.BlockSpec((1, tn), lambda j: (0, j)),
                  pl.BlockSpec(memory_space=pltpu.SMEM),
                  pl.BlockSpec((3, MOBA_BLOCK, MOBA_BLOCK), lambda j: (0, 0, 0)),
                  w_spec],
        out_specs=(pl.BlockSpec((bsz, tn), lambda j: (0, j)),
                   pl.BlockSpec((1, 3, MOBA_BLOCK, MOBA_BLOCK),
                                lambda j: (jnp.minimum(j, ATTN_HEADS - 1), 0, 0, 0)),
                   w_spec),
        compiler_params=_cparams(("arbitrary",)),
        name="mod",
    )(c, w_ada, b_ada.reshape(1, n), rel_bias.astype(F32).reshape(-1), bucket, w_in)


def _side_casts(arrays, steps, step_of):
    specs, shapes = [], []
    for a in arrays:
        r, c = a.shape
        rb = max(BF16_ROW_TILE, r // steps)
        nblk = r // rb
        assert r % rb == 0 and steps % nblk == 0
        specs.append(pl.BlockSpec(
            (rb, c), lambda *ids, nblk=nblk: ((step_of(*ids) * nblk) // steps, 0)))
        shapes.append(jax.ShapeDtypeStruct((r, c), BF16))
    return specs, shapes


def _cast_sides(side_in, side_out):
    for src, dst in zip(side_in, side_out):
        dst[...] = src[...].astype(dst.dtype)


def _norm_mm_kernel(*refs, act, rows, n_side):
    x_ref, g_ref, sc_ref, sh_ref, w_ref = refs[:5]
    side_in = refs[5:5 + n_side]
    o_ref = refs[5 + n_side]
    side_out = refs[6 + n_side:6 + 2 * n_side]
    h_ref = refs[-1]

    def mm(h):
        acc = jnp.dot(h, w_ref[...], preferred_element_type=F32)
        if act == "relu2":
            acc = jnp.square(jnp.maximum(acc, 0.0))
        return acc.astype(o_ref.dtype)

    @pl.when(pl.program_id(1) == 0)
    def _():
        g = g_ref[...]
        sc = 1.0 + sc_ref[0]
        sh = sh_ref[0]
        tm = x_ref.shape[0]
        for r in range(0, tm, rows):
            x = x_ref[r:r + rows, :]
            ms = jnp.mean(x * x, axis=-1, keepdims=True)
            y = x * lax.rsqrt(ms + EPS) * g
            h = (y * sc + sh).astype(BF16)
            h_ref[r:r + rows, :] = h
            o_ref[r:r + rows, :] = mm(h)

    @pl.when(pl.program_id(1) != 0)
    def _():
        o_ref[...] = mm(h_ref[...])

    _cast_sides(side_in, side_out)


def _norm_mm(x2d, g, mod4, sc_idx, sh_idx, w, seq_len, *, act, side_casts=(),
             tm=1024, tn=2048):
    t, d = x2d.shape
    n = w.shape[1]
    per_b = seq_len // tm
    nj = n // tn
    side_specs, side_shapes = _side_casts(side_casts, (t // tm) * nj, lambda i, j: i * nj + j)
    res = pl.pallas_call(
        functools.partial(_norm_mm_kernel, act=act, rows=256, n_side=len(side_casts)),
        out_shape=tuple([jax.ShapeDtypeStruct((t, n), BF16)] + side_shapes),
        grid=(t // tm, nj),
        in_specs=[pl.BlockSpec((tm, d), lambda i, j: (i, 0)),
                  pl.BlockSpec((1, d), lambda i, j: (0, 0)),
                  pl.BlockSpec((1, 1, d), lambda i, j: ((i // per_b) * N_MOD + sc_idx, 0, 0)),
                  pl.BlockSpec((1, 1, d), lambda i, j: ((i // per_b) * N_MOD + sh_idx, 0, 0)),
                  pl.BlockSpec((d, tn), lambda i, j: (0, j))] + side_specs,
        out_specs=tuple([pl.BlockSpec((tm, tn), lambda i, j: (i, j))] + side_specs),
        scratch_shapes=[pltpu.VMEM((tm, d), BF16)],
        compiler_params=_cparams(("arbitrary", "arbitrary")),
        name="norm_mm_" + act,
    )(x2d, g.reshape(1, d), mod4, mod4, w, *side_casts)
    return res if len(res) > 1 else res[0]


def _s5prep_kernel(are_ref, aim_ref, ldt_ref, bre_ref, bim_ref, cre_ref, cim_ref,
                   at_ref, bmat_ref, cmat_ref, kmat_ref):
    tau = S5_TAU
    p, n = SSM_GROUP, SSM_STATE
    a_re = are_ref[...]
    a_im = aim_ref[...]
    dt = jnp.exp(ldt_ref[...])
    mag = jnp.exp(dt * a_re)
    abar_re = mag * jnp.cos(dt * a_im)
    abar_im = mag * jnp.sin(dt * a_im)
    den = a_re * a_re + a_im * a_im
    p_re = abar_re - 1.0
    f_re = (p_re * a_re + abar_im * a_im) / den
    f_im = (abar_im * a_re - p_re * a_im) / den

    def cmul(x, y):
        return x[0] * y[0] - x[1] * y[1], x[0] * y[1] + x[1] * y[0]

    def cat(x):
        return jnp.concatenate([x[0], x[1]], axis=-1)

    abar = (abar_re, abar_im)
    b_in = (jnp.swapaxes(bre_ref[...], 1, 2), jnp.swapaxes(bim_ref[...], 1, 2))
    bb = cmul((f_re, f_im), b_in)
    cc = (cre_ref[...], cim_ref[...])
    apow = [None, abar]
    for _ in range(tau - 1):
        apow.append(cmul(apow[-1], abar))
    at_ref[...] = cat(apow[tau])

    bw, cwt, kwt = [], [], []
    y0 = cat(bb)
    for i in range(tau):
        j = tau - 1 - i
        bw.append(bb if j == 0 else cmul(apow[j], bb))
        ca = cmul(cc, apow[i + 1])
        cwt.append((jnp.swapaxes(ca[0], 1, 2), jnp.swapaxes(-ca[1], 1, 2)))
        x = cc if i == 0 else cmul(cc, apow[i])
        x = jnp.concatenate([x[0], -x[1]], axis=-1)
        kwt.append(jnp.einsum("gqn,gpn->gqp", y0, x, precision=lax.Precision.HIGHEST,
                              preferred_element_type=F32))

    bmat_ref[...] = jnp.zeros_like(bmat_ref)
    cmat_ref[...] = jnp.zeros_like(cmat_ref)
    kmat_ref[...] = jnp.zeros_like(kmat_ref)
    half = S5_SC // 2
    for g in range(are_ref.shape[0]):
        blk, gl = divmod(g, S5_GROUPS)
        for i in range(tau):
            rows_i = slice(i * S5_CH + gl * p, i * S5_CH + (gl + 1) * p)
            for part in range(2):
                st = slice(part * half + gl * n, part * half + (gl + 1) * n)
                bmat_ref[blk, rows_i, st] = bw[i][part][g].astype(BF16)
                cmat_ref[blk, st, rows_i] = cwt[i][part][g].astype(BF16)
            for ip in range(i + 1):
                rows_ip = slice(ip * S5_CH + gl * p, ip * S5_CH + (gl + 1) * p)
                kmat_ref[blk, rows_ip, rows_i] = kwt[i - ip][g].astype(BF16)


def _s5prep(a_re, a_im, log_dt, b_re, b_im, c_re, c_im, d_skip, bsz):
    g, n = a_re.shape
    at, bmat, cmat, kmat = pl.pallas_call(
        _s5prep_kernel,
        out_shape=(jax.ShapeDtypeStruct((g, 1, 2 * n), F32),
                   jax.ShapeDtypeStruct((S5_NBLK, S5_UC, S5_SC), BF16),
                   jax.ShapeDtypeStruct((S5_NBLK, S5_SC, S5_UC), BF16),
                   jax.ShapeDtypeStruct((S5_NBLK, S5_UC, S5_UC), BF16)),
        compiler_params=pltpu.CompilerParams(vmem_limit_bytes=VMEM_LIMIT_BYTES),
        name="s5prep",
    )(a_re.reshape(g, 1, n), a_im.reshape(g, 1, n), log_dt.reshape(g, 1, 1),
      b_re, b_im, c_re, c_im)
    half = S5_SC // 2
    atv = jnp.concatenate([at[:, 0, :n].reshape(S5_NBLK, half),
                           at[:, 0, n:].reshape(S5_NBLK, half)], axis=-1)
    atv = jnp.broadcast_to(atv.reshape(S5_NBLK, 1, S5_SC), (S5_NBLK, bsz, S5_SC))
    d = jnp.broadcast_to(d_skip.reshape(S5_NBLK, 1, S5_CH), (S5_NBLK, S5_TAU, S5_CH))
    return bmat, cmat, kmat, atv, d.reshape(S5_NBLK, 1, S5_UC)


def _s5_kernel(x_ref, bw_ref, cw_ref, kw_ref, a_ref, d_ref, y_ref,
               xf_ref, u_ref, w_ref, s_ref, st_ref, *, bsz, n_sub):
    @pl.when(pl.program_id(1) == 0)
    def _():
        st_ref[...] = jnp.zeros_like(st_ref)

    half = S5_SC // 2
    a_re = a_ref[0, :, :half]
    a_im = a_ref[0, :, half:]
    s_re = st_ref[:, :half]
    s_im = st_ref[:, half:]
    d2 = d_ref[0]
    sub = u_ref.shape[0] // n_sub
    ksub = sub // bsz

    def slot(ref, c, i):
        return ref.at[:, pl.ds(c * ksub * S5_TAU + i, ksub, stride=S5_TAU), :]

    def project_in(c):
        t0 = c * ksub * S5_TAU
        xf_ref[:, t0:t0 + ksub * S5_TAU, :] = x_ref[:, t0:t0 + ksub * S5_TAU, :].astype(F32)
        for i in range(S5_TAU):
            x = pltpu.einshape("bkc->kbc", slot(xf_ref, c, i)[...]).reshape(sub, S5_CH)
            u_ref[c * sub:(c + 1) * sub, i * S5_CH:(i + 1) * S5_CH] = x.astype(BF16)
        w_ref[c * sub:(c + 1) * sub, :] = jnp.dot(u_ref[c * sub:(c + 1) * sub, :], bw_ref[0],
                                                  preferred_element_type=F32)

    project_in(0)
    for c in range(n_sub):
        r0 = c * sub
        if c + 1 < n_sub:
            project_in(c + 1)
        u = u_ref[r0:r0 + sub, :]
        for k in range(0, ksub, 2):
            before = []
            for kk in range(2):
                r = r0 + (k + kk) * bsz
                before.append(jnp.concatenate([s_re, s_im], axis=1))
                w = w_ref[r:r + bsz, :]
                s_re, s_im = (a_re * s_re - a_im * s_im + w[:, :half],
                              a_re * s_im + a_im * s_re + w[:, half:])
            s_ref[r0 + k * bsz:r0 + (k + 2) * bsz, :] = (
                jnp.concatenate(before, axis=0).astype(BF16))
        y = (jnp.dot(s_ref[r0:r0 + sub, :], cw_ref[0], preferred_element_type=F32)
             + jnp.dot(u, kw_ref[0], preferred_element_type=F32)
             + d2 * u.astype(F32))
        y = jax.nn.gelu(y)
        for i in range(S5_TAU):
            slot(y_ref, c, i)[...] = pltpu.einshape(
                "kbc->bkc", y[:, i * S5_CH:(i + 1) * S5_CH].reshape(ksub, bsz, S5_CH))
    st_ref[:, :half] = s_re
    st_ref[:, half:] = s_im


def _s5(proj, bmat, cmat, kmat, atv, d2):
    assert S5_CH == 128, "a block's channels must be one lane tile"
    bsz, seq_len, _ = proj.shape
    tpc = S5_ROWS // bsz * S5_TAU
    return pl.pallas_call(
        functools.partial(_s5_kernel, bsz=bsz, n_sub=S5_ROWS // S5_SUB_ROWS),
        out_shape=jax.ShapeDtypeStruct((bsz, seq_len, SSM_WIDTH), F32),
        grid=(S5_NBLK, seq_len // tpc),
        in_specs=[
            pl.BlockSpec((bsz, tpc, S5_CH), lambda g, c: (0, c, g)),
            pl.BlockSpec((1, S5_UC, S5_SC), lambda g, c: (g, 0, 0)),
            pl.BlockSpec((1, S5_SC, S5_UC), lambda g, c: (g, 0, 0)),
            pl.BlockSpec((1, S5_UC, S5_UC), lambda g, c: (g, 0, 0)),
            pl.BlockSpec((1, bsz, S5_SC), lambda g, c: (g, 0, 0)),
            pl.BlockSpec((1, 1, S5_UC), lambda g, c: (g, 0, 0))],
        out_specs=pl.BlockSpec((bsz, tpc, S5_CH), lambda g, c: (0, c, g)),
        scratch_shapes=[pltpu.VMEM((bsz, tpc, S5_CH), F32),
                        pltpu.VMEM((S5_ROWS, S5_UC), BF16),
                        pltpu.VMEM((S5_ROWS, S5_SC), F32),
                        pltpu.VMEM((S5_ROWS, S5_SC), BF16),
                        pltpu.VMEM((bsz, S5_SC), F32)],
        compiler_params=_cparams(("parallel", "arbitrary")),
        name="s5",
    )(proj, bmat, cmat, kmat, atv, d2)


_NT = (((1,), (1,)), ((), ()))


def _moba_scores(own, slot0, state, q_ref, k_ref, bias_ref, km_ref, s_ref):
    bs = MOBA_BLOCK
    q = q_ref[own * bs:(own + 1) * bs, :]

    ranks = []
    if own > 0:
        gate = lax.dot_general(km_ref[...].astype(BF16), q, _NT, preferred_element_type=F32)
        blk = lax.broadcasted_iota(jnp.int32, gate.shape, 0)
        gm = jnp.where(blk < own, gate, NEG_INF)
        for n in range(own):
            g_n = gm[n:n + 1, :]
            ge = jnp.where(gm >= g_n, 1.0, 0.0)
            gt = jnp.where(gm > g_n, 1.0, 0.0)
            ranks.append(jnp.sum(jnp.where(blk < n, ge, gt), axis=0, keepdims=True))

    c1 = HEAD_DIM ** -0.5 * LOG2E
    m = None
    for idx, j in enumerate([own] + list(range(own))):
        k_j = k_ref[j * bs:(j + 1) * bs, :]
        raw = lax.dot_general(k_j, q, _NT, preferred_element_type=F32)
        if j == own:
            key = lax.broadcasted_iota(jnp.int32, (bs, bs), 0)
            qry = lax.broadcasted_iota(jnp.int32, (bs, bs), 1)
            s = jnp.where(key <= qry, raw * c1 + bias_ref[0], NEG_INF)
        elif own - j == 1:
            s = jnp.where(ranks[j] < float(MOBA_TOPK), raw * c1 + bias_ref[1], NEG_INF)
        else:
            row = jnp.where(ranks[j] < float(MOBA_TOPK), bias_ref[2, 0:1, :], NEG_INF)
            s = raw * c1 + row
        s_ref[slot0 + idx] = s
        m_j = jnp.max(s, axis=0, keepdims=True)
        m = m_j if m is None else jnp.maximum(m, m_j)
        state["m"] = m
        yield


def _moba_values(own, slot0, state, vt_ref, o_ref, s_ref):
    bs = MOBA_BLOCK
    m = state["m"]
    acc = None
    for idx, j in enumerate([own] + list(range(own))):
        p = jnp.exp2((s_ref[slot0 + idx] - m).astype(BF16))
        a_j = jnp.dot(vt_ref[:, j * bs:(j + 1) * bs], p, preferred_element_type=F32)
        acc = a_j if acc is None else acc + a_j
        if idx == own:
            o = acc[:HEAD_DIM, :] / acc[HEAD_DIM:HEAD_DIM + 1, :]
            o_ref[own * bs:(own + 1) * bs, :] = o.T.astype(o_ref.dtype)
        yield


def _moba_kernel(*refs, n_blk, n_side):
    q_ref, k_ref, v_ref, bias_ref = refs[:4]
    side_in = refs[4:4 + n_side]
    o_ref = refs[4 + n_side]
    side_out = refs[5 + n_side:5 + 2 * n_side]
    km_ref, vt_ref, s_ref = refs[-3:]
    bs = MOBA_BLOCK
    heads = bias_ref.shape[0]
    _cast_sides(side_in, side_out)

    def head(ref, hh):
        return ref.at[:, hh * HEAD_DIM:(hh + 1) * HEAD_DIM]

    for hh in range(heads):
        vt_ref[hh, HEAD_DIM:, :] = jnp.ones((vt_ref.shape[1] - HEAD_DIM, vt_ref.shape[2]), BF16)
        for n in range(n_blk):
            kb = head(k_ref, hh)[n * bs:(n + 1) * bs, :].astype(F32)
            km_ref[hh, n:n + 1, :] = jnp.mean(kb, axis=0, keepdims=True)
            vt_ref[hh, :HEAD_DIM, n * bs:(n + 1) * bs] = (
                head(v_ref, hh)[n * bs:(n + 1) * bs, :].astype(F32).T.astype(BF16))

    per_head = n_blk * (n_blk + 1) // 2
    pending = iter(())
    for own in reversed(range(n_blk)):
        for hh in range(heads):
            slot0 = hh * per_head + own * (own + 1) // 2
            state = {}
            scores = _moba_scores(own, slot0, state, head(q_ref, hh), head(k_ref, hh),
                                  bias_ref.at[hh], km_ref.at[hh], s_ref)
            for _ in itertools.zip_longest(scores, pending):
                pass
            pending = _moba_values(own, slot0, state, vt_ref.at[hh], head(o_ref, hh), s_ref)
    for _ in pending:
        pass


def _moba(proj, bias_tiles, bsz, seq_len, side_casts=(), heads=2):
    n_blk = seq_len // MOBA_BLOCK
    hw = heads * HEAD_DIM
    q_off = SSM_WIDTH // hw
    k_off = q_off + ATTN_WIDTH // hw
    v_off = k_off + ATTN_WIDTH // hw
    n_slots = heads * n_blk * (n_blk + 1) // 2
    hg = ATTN_HEADS // heads
    side_specs, side_shapes = _side_casts(side_casts, bsz * hg, lambda b, h: b * hg + h)
    res = pl.pallas_call(
        functools.partial(_moba_kernel, n_blk=n_blk, n_side=len(side_casts)),
        out_shape=tuple([jax.ShapeDtypeStruct((bsz * seq_len, ATTN_WIDTH), BF16)] + side_shapes),
        grid=(bsz, hg),
        in_specs=[pl.BlockSpec((seq_len, hw), lambda b, h: (b, q_off + h)),
                  pl.BlockSpec((seq_len, hw), lambda b, h: (b, k_off + h)),
                  pl.BlockSpec((seq_len, hw), lambda b, h: (b, v_off + h)),
                  pl.BlockSpec((heads, 3, MOBA_BLOCK, MOBA_BLOCK), lambda b, h: (h, 0, 0, 0))]
        + side_specs,
        out_specs=tuple([pl.BlockSpec((seq_len, hw), lambda b, h: (b, h))] + side_specs),
        scratch_shapes=[pltpu.VMEM((heads, n_blk, HEAD_DIM), F32),
                        pltpu.VMEM((heads, HEAD_DIM + 16, seq_len), BF16),
                        pltpu.VMEM((n_slots, MOBA_BLOCK, MOBA_BLOCK), F32)],
        compiler_params=_cparams(("arbitrary", "arbitrary")),
        name="moba",
    )(proj, proj, proj, bias_tiles, *side_casts)
    return res if len(res) > 1 else res[0]


def _merge_kernel(y_ref, ya_ref, wg_ref, bg_ref, ws_ref, wa_ref, ga_ref, gb_ref, o_ref):
    y = y_ref[...]
    z = jnp.dot(y.astype(BF16), wg_ref[...], preferred_element_type=F32) + bg_ref[...]
    y_ssm = (y * jax.nn.sigmoid(z)).astype(BF16)
    a = jnp.dot(y_ssm, ws_ref[...], preferred_element_type=F32)
    b = jnp.dot(ya_ref[...], wa_ref[...], preferred_element_type=F32)
    o_ref[...] = (jax.nn.sigmoid(ga_ref[...].astype(F32)) * a
                  + jax.nn.sigmoid(gb_ref[...].astype(F32)) * b).astype(o_ref.dtype)


def _merge(y, y_att, w_glu, b_glu, w_ps, w_pa, proj, tm=512):
    t, k = y.shape
    n = w_ps.shape[1]
    ga_off = (SSM_WIDTH + 3 * ATTN_WIDTH) // n
    gb_off = ga_off + 1
    return pl.pallas_call(
        _merge_kernel,
        out_shape=jax.ShapeDtypeStruct((t, n), BF16),
        grid=(t // tm,),
        in_specs=[pl.BlockSpec((tm, k), lambda i: (i, 0)),
                  pl.BlockSpec((tm, k), lambda i: (i, 0)),
                  pl.BlockSpec((k, k), lambda i: (0, 0), pipeline_mode=_RESIDENT),
                  pl.BlockSpec((1, k), lambda i: (0, 0)),
                  pl.BlockSpec((k, n), lambda i: (0, 0), pipeline_mode=_RESIDENT),
                  pl.BlockSpec((k, n), lambda i: (0, 0), pipeline_mode=_RESIDENT),
                  pl.BlockSpec((tm, n), lambda i: (i, ga_off)),
                  pl.BlockSpec((tm, n), lambda i: (i, gb_off))],
        out_specs=pl.BlockSpec((tm, n), lambda i: (i, 0)),
        compiler_params=_cparams(("parallel",)),
        name="merge",
    )(y, y_att, w_glu, b_glu.reshape(1, k), w_ps, w_pa, proj, proj)


def _resid_mm_kernel(a_ref, w_ref, x_ref, g_ref, o_ref):
    acc = jnp.dot(a_ref[...], w_ref[...], preferred_element_type=F32)
    o_ref[...] = x_ref[...] + g_ref[0] * acc


def _resid_mm(a, w, x2d, mod4, g_idx, seq_len, tm=1024, tn=2048):
    t, k = a.shape
    n = w.shape[1]
    per_b = seq_len // tm
    return pl.pallas_call(
        _resid_mm_kernel,
        out_shape=jax.ShapeDtypeStruct((t, n), F32),
        grid=(t // tm, n // tn),
        in_specs=[pl.BlockSpec((tm, k), lambda i, j: (i, 0)),
                  pl.BlockSpec((k, tn), lambda i, j: (0, j), pipeline_mode=_RESIDENT),
                  pl.BlockSpec((tm, tn), lambda i, j: (i, j)),
                  pl.BlockSpec((1, 1, tn), lambda i, j: ((i // per_b) * N_MOD + g_idx, 0, j))],
        out_specs=pl.BlockSpec((tm, tn), lambda i, j: (i, j)),
        compiler_params=_cparams(("parallel", "parallel")),
        name="resid_mm",
    )(a, w, x2d, mod4)


def _ff2_kernel(h_ref, w_ref, x_hbm, g_ref, gf_ref, o_ref, x_ref, x_sem, *, rows):
    i = pl.program_id(0)
    k = pl.program_id(1)
    last = pl.num_programs(1) - 1
    tm = o_ref.shape[0]

    def x_copy():
        return pltpu.make_async_copy(x_hbm.at[pl.ds(pl.multiple_of(i * tm, tm), tm), :],
                                     x_ref, x_sem)

    @pl.when(k == 0)
    def _():
        x_copy().start()
        o_ref[...] = jnp.dot(h_ref[...], w_ref[...], preferred_element_type=F32)

    @pl.when((k != 0) & (k != last))
    def _():
        o_ref[...] += jnp.dot(h_ref[...], w_ref[...], preferred_element_type=F32)

    @pl.when(k == last)
    def _():
        x_copy().wait()
        for r in range(0, tm, rows):
            acc = o_ref[r:r + rows, :] + jnp.dot(h_ref[r:r + rows, :], w_ref[...],
                                                 preferred_element_type=F32)
            xo = x_ref[r:r + rows, :] + g_ref[0] * acc
            ms = jnp.mean(xo * xo, axis=-1, keepdims=True)
            o_ref[r:r + rows, :] = xo * lax.rsqrt(ms + EPS) * gf_ref[...]


def _ff2(hid, w, x2d, mod4, g_idx, gf, seq_len, tm=1024, tk=2048):
    t, kdim = hid.shape
    n = w.shape[1]
    per_b = seq_len // tm
    assert kdim // tk >= 2, "the residual copy is started and waited in different steps"
    return pl.pallas_call(
        functools.partial(_ff2_kernel, rows=256),
        out_shape=jax.ShapeDtypeStruct((t, n), F32),
        grid=(t // tm, kdim // tk),
        in_specs=[pl.BlockSpec((tm, tk), lambda i, k: (i, k)),
                  pl.BlockSpec((tk, n), lambda i, k: (k, 0)),
                  pl.BlockSpec(memory_space=pl.ANY),
                  pl.BlockSpec((1, 1, n), lambda i, k: ((i // per_b) * N_MOD + g_idx, 0, 0)),
                  pl.BlockSpec((1, n), lambda i, k: (0, 0))],
        out_specs=pl.BlockSpec((tm, n), lambda i, k: (i, 0)),
        scratch_shapes=[pltpu.VMEM((tm, n), F32), pltpu.SemaphoreType.DMA(())],
        compiler_params=_cparams(("arbitrary", "arbitrary")),
        name="ff2_final",
    )(hid, w, x2d, mod4, gf.reshape(1, n))


def kernel(x, c, rel_bias, w_ada, b_ada, norm_mix_g, w_in, ssm_a_re, ssm_a_im, ssm_log_dt,
           ssm_b_re, ssm_b_im, ssm_c_re, ssm_c_im, ssm_d, w_glu, b_glu, w_proj_ssm,
           w_proj_attn, w_out, norm_mlp_g, w_ff1, w_ff2, norm_final_g):
    bsz, seq_len, d = x.shape
    depth = w_in.shape[0]
    assert depth == 1, "the final rms_norm is fused into the single layer's ff2 kernel"
    t = bsz * seq_len
    x2d = x.reshape(t, d)

    for l in range(depth):
        mod, bias_tiles, w_in_b = _mod(c, w_ada[l], b_ada[l], rel_bias, w_in[l])
        mod4 = mod.reshape(bsz * N_MOD, 1, d)

        proj, w_glu_b, w_ps_b, w_pa_b, w_out_b = _norm_mm(
            x2d, norm_mix_g[l], mod4, 1, 0, w_in_b, seq_len, act="none",
            side_casts=(w_glu[l], w_proj_ssm[l], w_proj_attn[l], w_out[l]))

        bmat, cmat, kmat, atv, d2 = _s5prep(ssm_a_re[l], ssm_a_im[l], ssm_log_dt[l],
                                            ssm_b_re[l], ssm_b_im[l], ssm_c_re[l], ssm_c_im[l],
                                            ssm_d[l], bsz)
        y = _s5(proj.reshape(bsz, seq_len, IN_WIDTH), bmat, cmat, kmat, atv, d2)

        y_att, w_ff1_b, w_ff2_b = _moba(proj, bias_tiles, bsz, seq_len,
                                        side_casts=(w_ff1[l], w_ff2[l]))

        merged = _merge(y.reshape(t, SSM_WIDTH), y_att, w_glu_b, b_glu[l], w_ps_b, w_pa_b, proj)
        x2d = _resid_mm(merged, w_out_b, x2d, mod4, 2, seq_len)

        hid = _norm_mm(x2d, norm_mlp_g[l], mod4, 4, 3, w_ff1_b, seq_len, act="relu2")
        x2d = _ff2(hid, w_ff2_b, x2d, mod4, 5, norm_final_g, seq_len)
    return x2d.reshape(bsz, seq_len, d)
```

```python
import functools
import itertools
import math

import jax
import jax.numpy as jnp
from jax import lax
from jax.experimental import pallas as pl
from jax.experimental.pallas import tpu as pltpu

F32 = jnp.float32
BF16 = jnp.bfloat16

D_MODEL = 2048
SSM_WIDTH = 1024
SSM_GROUP = 16
SSM_GROUPS = 64
SSM_STATE = 64
ATTN_HEADS = 8
HEAD_DIM = 128
ATTN_WIDTH = 1024
MOBA_BLOCK = 256
MOBA_TOPK = 3
REL_BUCKETS = 32
REL_MAX_DIST = 128
N_MOD = 6
EPS = 1e-6
NEG_INF = -1e30
LOG2E = math.log2(math.e)
IN_WIDTH = SSM_WIDTH + 3 * ATTN_WIDTH + 2 * D_MODEL

VMEM_LIMIT_BYTES = 56 * 1024 * 1024
LANES = 128
BF16_ROW_TILE = 16

S5_TAU = 2
S5_GROUPS = 8
S5_CH = S5_GROUPS * SSM_GROUP
S5_UC = S5_TAU * S5_CH
S5_SC = 2 * S5_GROUPS * SSM_STATE
S5_NBLK = SSM_GROUPS // S5_GROUPS
S5_ROWS = 4096
S5_SUB_ROWS = 256


_RESIDENT = pl.Buffered(1)


def _cparams(sem):
    return pltpu.CompilerParams(dimension_semantics=sem,
                                vmem_limit_bytes=VMEM_LIMIT_BYTES)


def _mod_kernel(c_ref, w_ref, b_ref, relb_ref, bucket_ref, win_ref, o_ref, bias_ref, winb_ref):
    c = c_ref[...]
    ca = (c * jax.nn.sigmoid(c)).astype(BF16)
    o_ref[...] = jnp.dot(ca, w_ref[...].astype(BF16),
                         preferred_element_type=F32) + b_ref[...]
    winb_ref[...] = win_ref[...].astype(winb_ref.dtype)
    h = pl.program_id(0)

    @pl.when(h < ATTN_HEADS)
    def _():
        for d in range(bucket_ref.shape[0]):
            bucket = bucket_ref[d]
            out = jnp.zeros(bucket.shape, F32)
            for b in range(REL_BUCKETS):
                out = jnp.where(bucket == b, relb_ref[b * ATTN_HEADS + h], out)
            bias_ref[0, d] = out * LOG2E


def _t5_bucket(rel):
    n = jnp.maximum(rel, 0)
    max_exact = REL_BUCKETS // 2
    nf = jnp.maximum(n, 1).astype(F32)
    large = max_exact + (jnp.log(nf / max_exact) / math.log(REL_MAX_DIST / max_exact)
                         * (REL_BUCKETS - max_exact)).astype(jnp.int32)
    large = jnp.minimum(large, REL_BUCKETS - 1)
    return jnp.where(n < max_exact, n, large)


def _mod(c, w_ada, b_ada, rel_bias, w_in, steps=16):
    bsz, d = c.shape
    n = w_ada.shape[1]
    tn = n // steps
    assert steps >= ATTN_HEADS and tn % LANES == 0
    i = jnp.arange(MOBA_BLOCK)
    rel = (jnp.arange(3)[:, None, None] * MOBA_BLOCK + i[None, None, :] - i[None, :, None])
    bucket = _t5_bucket(rel)
    rows, cols = w_in.shape
    w_spec = pl.BlockSpec((rows // steps, cols), lambda j: (j, 0))
    return pl.pallas_call(
        _mod_kernel,
        out_shape=(jax.ShapeDtypeStruct((bsz, n), F32),
                   jax.ShapeDtypeStruct((ATTN_HEADS, 3, MOBA_BLOCK, MOBA_BLOCK), F32),
                   jax.ShapeDtypeStruct((rows, cols), BF16)),
        grid=(steps,),
        in_specs=[pl.BlockSpec((bsz, d), lambda j: (0, 0)),
                  pl.BlockSpec((d, tn), lambda j: (0, j)),
                  pl.BlockSpec((1, tn), lambda j: (0, j)),
                  pl.BlockSpec(memory_space=pltpu.SMEM),
                  pl.BlockSpec((3, MOBA_BLOCK, MOBA_BLOCK), lambda j: (0, 0, 0)),
                  w_spec],
        out_specs=(pl.BlockSpec((bsz, tn), lambda j: (0, j)),
                   pl.BlockSpec((1, 3, MOBA_BLOCK, MOBA_BLOCK),
                                lambda j: (jnp.minimum(j, ATTN_HEADS - 1), 0, 0, 0)),
                   w_spec),
        compiler_params=_cparams(("arbitrary",)),
        name="mod",
    )(c, w_ada, b_ada.reshape(1, n), rel_bias.astype(F32).reshape(-1), bucket, w_in)


def _side_casts(arrays, steps, step_of):
    specs, shapes = [], []
    for a in arrays:
        r, c = a.shape
        rb = max(BF16_ROW_TILE, r // steps)
        nblk = r // rb
        assert r % rb == 0 and steps % nblk == 0
        specs.append(pl.BlockSpec(
            (rb, c), lambda *ids, nblk=nblk: ((step_of(*ids) * nblk) // steps, 0)))
        shapes.append(jax.ShapeDtypeStruct((r, c), BF16))
    return specs, shapes


def _cast_sides(side_in, side_out):
    for src, dst in zip(side_in, side_out):
        dst[...] = src[...].astype(dst.dtype)


def _norm_mm_kernel(*refs, act, rows, n_side):
    x_ref, g_ref, sc_ref, sh_ref, w_ref = refs[:5]
    side_in = refs[5:5 + n_side]
    o_ref = refs[5 + n_side]
    side_out = refs[6 + n_side:6 + 2 * n_side]
    h_ref = refs[-1]

    def mm(h):
        acc = jnp.dot(h, w_ref[...], preferred_element_type=F32)
        if act == "relu2":
            acc = jnp.square(jnp.maximum(acc, 0.0))
        return acc.astype(o_ref.dtype)

    @pl.when(pl.program_id(1) == 0)
    def _():
        g = g_ref[...]
        sc = 1.0 + sc_ref[0]
        sh = sh_ref[0]
        tm = x_ref.shape[0]
        for r in range(0, tm, rows):
            x = x_ref[r:r + rows, :]
            ms = jnp.mean(x * x, axis=-1, keepdims=True)
            y = x * lax.rsqrt(ms + EPS) * g
            h = (y * sc + sh).astype(BF16)
            h_ref[r:r + rows, :] = h
            o_ref[r:r + rows, :] = mm(h)

    @pl.when(pl.program_id(1) != 0)
    def _():
        o_ref[...] = mm(h_ref[...])

    _cast_sides(side_in, side_out)


def _norm_mm(x2d, g, mod4, sc_idx, sh_idx, w, seq_len, *, act, side_casts=(),
             tm=1024, tn=2048):
    t, d = x2d.shape
    n = w.shape[1]
    per_b = seq_len // tm
    nj = n // tn
    side_specs, side_shapes = _side_casts(side_casts, (t // tm) * nj, lambda i, j: i * nj + j)
    res = pl.pallas_call(
        functools.partial(_norm_mm_kernel, act=act, rows=256, n_side=len(side_casts)),
        out_shape=tuple([jax.ShapeDtypeStruct((t, n), BF16)] + side_shapes),
        grid=(t // tm, nj),
        in_specs=[pl.BlockSpec((tm, d), lambda i, j: (i, 0)),
                  pl.BlockSpec((1, d), lambda i, j: (0, 0)),
                  pl.BlockSpec((1, 1, d), lambda i, j: ((i // per_b) * N_MOD + sc_idx, 0, 0)),
                  pl.BlockSpec((1, 1, d), lambda i, j: ((i // per_b) * N_MOD + sh_idx, 0, 0)),
                  pl.BlockSpec((d, tn), lambda i, j: (0, j))] + side_specs,
        out_specs=tuple([pl.BlockSpec((tm, tn), lambda i, j: (i, j))] + side_specs),
        scratch_shapes=[pltpu.VMEM((tm, d), BF16)],
        compiler_params=_cparams(("arbitrary", "arbitrary")),
        name="norm_mm_" + act,
    )(x2d, g.reshape(1, d), mod4, mod4, w, *side_casts)
    return res if len(res) > 1 else res[0]


def _s5prep_kernel(are_ref, aim_ref, ldt_ref, bre_ref, bim_ref, cre_ref, cim_ref,
                   at_ref, bmat_ref, cmat_ref, kmat_ref):
    tau = S5_TAU
    p, n = SSM_GROUP, SSM_STATE
    a_re = are_ref[...]
    a_im = aim_ref[...]
    dt = jnp.exp(ldt_ref[...])
    mag = jnp.exp(dt * a_re)
    abar_re = mag * jnp.cos(dt * a_im)
    abar_im = mag * jnp.sin(dt * a_im)
    den = a_re * a_re + a_im * a_im
    p_re = abar_re - 1.0
    f_re = (p_re * a_re + abar_im * a_im) / den
    f_im = (abar_im * a_re - p_re * a_im) / den

    def cmul(x, y):
        return x[0] * y[0] - x[1] * y[1], x[0] * y[1] + x[1] * y[0]

    def cat(x):
        return jnp.concatenate([x[0], x[1]], axis=-1)

    abar = (abar_re, abar_im)
    b_in = (jnp.swapaxes(bre_ref[...], 1, 2), jnp.swapaxes(bim_ref[...], 1, 2))
    bb = cmul((f_re, f_im), b_in)
    cc = (cre_ref[...], cim_ref[...])
    apow = [None, abar]
    for _ in range(tau - 1):
        apow.append(cmul(apow[-1], abar))
    at_ref[...] = cat(apow[tau])

    bw, cwt, kwt = [], [], []
    y0 = cat(bb)
    for i in range(tau):
        j = tau - 1 - i
        bw.append(bb if j == 0 else cmul(apow[j], bb))
        ca = cmul(cc, apow[i + 1])
        cwt.append((jnp.swapaxes(ca[0], 1, 2), jnp.swapaxes(-ca[1], 1, 2)))
        x = cc if i == 0 else cmul(cc, apow[i])
        x = jnp.concatenate([x[0], -x[1]], axis=-1)
        kwt.append(jnp.einsum("gqn,gpn->gqp", y0, x, precision=lax.Precision.HIGHEST,
                              preferred_element_type=F32))

    bmat_ref[...] = jnp.zeros_like(bmat_ref)
    cmat_ref[...] = jnp.zeros_like(cmat_ref)
    kmat_ref[...] = jnp.zeros_like(kmat_ref)
    half = S5_SC // 2
    for g in range(are_ref.shape[0]):
        blk, gl = divmod(g, S5_GROUPS)
        for i in range(tau):
            rows_i = slice(i * S5_CH + gl * p, i * S5_CH + (gl + 1) * p)
            for part in range(2):
                st = slice(part * half + gl * n, part * half + (gl + 1) * n)
                bmat_ref[blk, rows_i, st] = bw[i][part][g].astype(BF16)
                cmat_ref[blk, st, rows_i] = cwt[i][part][g].astype(BF16)
            for ip in range(i + 1):
                rows_ip = slice(ip * S5_CH + gl * p, ip * S5_CH + (gl + 1) * p)
                kmat_ref[blk, rows_ip, rows_i] = kwt[i - ip][g].astype(BF16)


def _s5prep(a_re, a_im, log_dt, b_re, b_im, c_re, c_im, d_skip, bsz):
    g, n = a_re.shape
    at, bmat, cmat, kmat = pl.pallas_call(
        _s5prep_kernel,
        out_shape=(jax.ShapeDtypeStruct((g, 1, 2 * n), F32),
                   jax.ShapeDtypeStruct((S5_NBLK, S5_UC, S5_SC), BF16),
                   jax.ShapeDtypeStruct((S5_NBLK, S5_SC, S5_UC), BF16),
                   jax.ShapeDtypeStruct((S5_NBLK, S5_UC, S5_UC), BF16)),
        compiler_params=pltpu.CompilerParams(vmem_limit_bytes=VMEM_LIMIT_BYTES),
        name="s5prep",
    )(a_re.reshape(g, 1, n), a_im.reshape(g, 1, n), log_dt.reshape(g, 1, 1),
      b_re, b_im, c_re, c_im)
    half = S5_SC // 2
    atv = jnp.concatenate([at[:, 0, :n].reshape(S5_NBLK, half),
                           at[:, 0, n:].reshape(S5_NBLK, half)], axis=-1)
    atv = jnp.broadcast_to(atv.reshape(S5_NBLK, 1, S5_SC), (S5_NBLK, bsz, S5_SC))
    d = jnp.broadcast_to(d_skip.reshape(S5_NBLK, 1, S5_CH), (S5_NBLK, S5_TAU, S5_CH))
    return bmat, cmat, kmat, atv, d.reshape(S5_NBLK, 1, S5_UC)


def _s5_kernel(x_ref, bw_ref, cw_ref, kw_ref, a_ref, d_ref, y_ref,
               xf_ref, u_ref, w_ref, s_ref, st_ref, *, bsz, n_sub):
    @pl.when(pl.program_id(1) == 0)
    def _():
        st_ref[...] = jnp.zeros_like(st_ref)

    half = S5_SC // 2
    a_re = a_ref[0, :, :half]
    a_im = a_ref[0, :, half:]
    s_re = st_ref[:, :half]
    s_im = st_ref[:, half:]
    d2 = d_ref[0]
    sub = u_ref.shape[0] // n_sub
    ksub = sub // bsz

    def slot(ref, c, i):
        return ref.at[:, pl.ds(c * ksub * S5_TAU + i, ksub, stride=S5_TAU), :]

    def project_in(c):
        t0 = c * ksub * S5_TAU
        xf_ref[:, t0:t0 + ksub * S5_TAU, :] = x_ref[:, t0:t0 + ksub * S5_TAU, :].astype(F32)
        for i in range(S5_TAU):
            x = pltpu.einshape("bkc->kbc", slot(xf_ref, c, i)[...]).reshape(sub, S5_CH)
            u_ref[c * sub:(c + 1) * sub, i * S5_CH:(i + 1) * S5_CH] = x.astype(BF16)
        w_ref[c * sub:(c + 1) * sub, :] = jnp.dot(u_ref[c * sub:(c + 1) * sub, :], bw_ref[0],
                                                  preferred_element_type=F32)

    project_in(0)
    for c in range(n_sub):
        r0 = c * sub
        if c + 1 < n_sub:
            project_in(c + 1)
        u = u_ref[r0:r0 + sub, :]
        for k in range(0, ksub, 2):
            before = []
            for kk in range(2):
                r = r0 + (k + kk) * bsz
                before.append(jnp.concatenate([s_re, s_im], axis=1))
                w = w_ref[r:r + bsz, :]
                s_re, s_im = (a_re * s_re - a_im * s_im + w[:, :half],
                              a_re * s_im + a_im * s_re + w[:, half:])
            s_ref[r0 + k * bsz:r0 + (k + 2) * bsz, :] = (
                jnp.concatenate(before, axis=0).astype(BF16))
        y = (jnp.dot(s_ref[r0:r0 + sub, :], cw_ref[0], preferred_element_type=F32)
             + jnp.dot(u, kw_ref[0], preferred_element_type=F32)
             + d2 * u.astype(F32))
        y = jax.nn.gelu(y)
        for i in range(S5_TAU):
            slot(y_ref, c, i)[...] = pltpu.einshape(
                "kbc->bkc", y[:, i * S5_CH:(i + 1) * S5_CH].reshape(ksub, bsz, S5_CH))
    st_ref[:, :half] = s_re
    st_ref[:, half:] = s_im


def _s5(proj, bmat, cmat, kmat, atv, d2):
    assert S5_CH == 128, "a block's channels must be one lane tile"
    bsz, seq_len, _ = proj.shape
    tpc = S5_ROWS // bsz * S5_TAU
    return pl.pallas_call(
        functools.partial(_s5_kernel, bsz=bsz, n_sub=S5_ROWS // S5_SUB_ROWS),
        out_shape=jax.ShapeDtypeStruct((bsz, seq_len, SSM_WIDTH), F32),
        grid=(S5_NBLK, seq_len // tpc),
        in_specs=[
            pl.BlockSpec((bsz, tpc, S5_CH), lambda g, c: (0, c, g)),
            pl.BlockSpec((1, S5_UC, S5_SC), lambda g, c: (g, 0, 0)),
            pl.BlockSpec((1, S5_SC, S5_UC), lambda g, c: (g, 0, 0)),
            pl.BlockSpec((1, S5_UC, S5_UC), lambda g, c: (g, 0, 0)),
            pl.BlockSpec((1, bsz, S5_SC), lambda g, c: (g, 0, 0)),
            pl.BlockSpec((1, 1, S5_UC), lambda g, c: (g, 0, 0))],
        out_specs=pl.BlockSpec((bsz, tpc, S5_CH), lambda g, c: (0, c, g)),
        scratch_shapes=[pltpu.VMEM((bsz, tpc, S5_CH), F32),
                        pltpu.VMEM((S5_ROWS, S5_UC), BF16),
                        pltpu.VMEM((S5_ROWS, S5_SC), F32),
                        pltpu.VMEM((S5_ROWS, S5_SC), BF16),
                        pltpu.VMEM((bsz, S5_SC), F32)],
        compiler_params=_cparams(("parallel", "arbitrary")),
        name="s5",
    )(proj, bmat, cmat, kmat, atv, d2)


_NT = (((1,), (1,)), ((), ()))


def _moba_scores(own, slot0, state, q_ref, k_ref, bias_ref, km_ref, s_ref):
    bs = MOBA_BLOCK
    q = q_ref[own * bs:(own + 1) * bs, :]

    ranks = []
    if own > 0:
        gate = lax.dot_general(km_ref[...].astype(BF16), q, _NT, preferred_element_type=F32)
        blk = lax.broadcasted_iota(jnp.int32, gate.shape, 0)
        gm = jnp.where(blk < own, gate, NEG_INF)
        for n in range(own):
            g_n = gm[n:n + 1, :]
            ge = jnp.where(gm >= g_n, 1.0, 0.0)
            gt = jnp.where(gm > g_n, 1.0, 0.0)
            ranks.append(jnp.sum(jnp.where(blk < n, ge, gt), axis=0, keepdims=True))

    c1 = HEAD_DIM ** -0.5 * LOG2E
    m = None
    for idx, j in enumerate([own] + list(range(own))):
        k_j = k_ref[j * bs:(j + 1) * bs, :]
        raw = lax.dot_general(k_j, q, _NT, preferred_element_type=F32)
        if j == own:
            key = lax.broadcasted_iota(jnp.int32, (bs, bs), 0)
            qry = lax.broadcasted_iota(jnp.int32, (bs, bs), 1)
            s = jnp.where(key <= qry, raw * c1 + bias_ref[0], NEG_INF)
        elif own - j == 1:
            s = jnp.where(ranks[j] < float(MOBA_TOPK), raw * c1 + bias_ref[1], NEG_INF)
        else:
            row = jnp.where(ranks[j] < float(MOBA_TOPK), bias_ref[2, 0:1, :], NEG_INF)
            s = raw * c1 + row
        s_ref[slot0 + idx] = s
        m_j = jnp.max(s, axis=0, keepdims=True)
        m = m_j if m is None else jnp.maximum(m, m_j)
        state["m"] = m
        yield


def _moba_probs(own, slot0, state, s_ref, p_ref):
    m = state["m"]
    for idx in range(own + 1):
        p_ref[slot0 + idx] = jnp.exp2((s_ref[slot0 + idx] - m).astype(BF16))
        yield


def _moba_values(own, slot0, vt_ref, o_ref, p_ref):
    bs = MOBA_BLOCK
    acc = None
    for idx, j in enumerate([own] + list(range(own))):
        a_j = jnp.dot(vt_ref[:, j * bs:(j + 1) * bs], p_ref[slot0 + idx],
                      preferred_element_type=F32)
        acc = a_j if acc is None else acc + a_j
        if idx == own:
            o = acc[:HEAD_DIM, :] / acc[HEAD_DIM:HEAD_DIM + 1, :]
            o_ref[own * bs:(own + 1) * bs, :] = o.T.astype(o_ref.dtype)
        yield


def _moba_kernel(*refs, n_blk, n_side):
    q_ref, k_ref, v_ref, bias_ref = refs[:4]
    side_in = refs[4:4 + n_side]
    o_ref = refs[4 + n_side]
    side_out = refs[5 + n_side:5 + 2 * n_side]
    km_ref, vt_ref, s_ref, p_ref = refs[-4:]
    bs = MOBA_BLOCK
    heads = bias_ref.shape[0]
    _cast_sides(side_in, side_out)

    def head(ref, hh):
        return ref.at[:, hh * HEAD_DIM:(hh + 1) * HEAD_DIM]

    for hh in range(heads):
        vt_ref[hh, HEAD_DIM:, :] = jnp.ones((vt_ref.shape[1] - HEAD_DIM, vt_ref.shape[2]), BF16)
        for n in range(n_blk):
            kb = head(k_ref, hh)[n * bs:(n + 1) * bs, :].astype(F32)
            km_ref[hh, n:n + 1, :] = jnp.mean(kb, axis=0, keepdims=True)
            vt_ref[hh, :HEAD_DIM, n * bs:(n + 1) * bs] = (
                head(v_ref, hh)[n * bs:(n + 1) * bs, :].astype(F32).T.astype(BF16))

    per_head = n_blk * (n_blk + 1) // 2
    probs, values, next_values = iter(()), iter(()), iter(())
    for own in reversed(range(n_blk)):
        for hh in range(heads):
            slot0 = hh * per_head + own * (own + 1) // 2
            state = {}
            scores = _moba_scores(own, slot0, state, head(q_ref, hh), head(k_ref, hh),
                                  bias_ref.at[hh], km_ref.at[hh], s_ref)
            for _ in itertools.zip_longest(scores, probs, values):
                pass
            values = next_values
            probs = _moba_probs(own, slot0, state, s_ref, p_ref)
            next_values = _moba_values(own, slot0, vt_ref.at[hh], head(o_ref, hh), p_ref)
    for _ in itertools.zip_longest(probs, values):
        pass
    for _ in next_values:
        pass


def _moba(proj, bias_tiles, bsz, seq_len, side_casts=(), heads=2):
    n_blk = seq_len // MOBA_BLOCK
    hw = heads * HEAD_DIM
    q_off = SSM_WIDTH // hw
    k_off = q_off + ATTN_WIDTH // hw
    v_off = k_off + ATTN_WIDTH // hw
    n_slots = heads * n_blk * (n_blk + 1) // 2
    hg = ATTN_HEADS // heads
    side_specs, side_shapes = _side_casts(side_casts, bsz * hg, lambda b, h: b * hg + h)
    res = pl.pallas_call(
        functools.partial(_moba_kernel, n_blk=n_blk, n_side=len(side_casts)),
        out_shape=tuple([jax.ShapeDtypeStruct((bsz * seq_len, ATTN_WIDTH), BF16)] + side_shapes),
        grid=(bsz, hg),
        in_specs=[pl.BlockSpec((seq_len, hw), lambda b, h: (b, q_off + h)),
                  pl.BlockSpec((seq_len, hw), lambda b, h: (b, k_off + h)),
                  pl.BlockSpec((seq_len, hw), lambda b, h: (b, v_off + h)),
                  pl.BlockSpec((heads, 3, MOBA_BLOCK, MOBA_BLOCK), lambda b, h: (h, 0, 0, 0))]
        + side_specs,
        out_specs=tuple([pl.BlockSpec((seq_len, hw), lambda b, h: (b, h))] + side_specs),
        scratch_shapes=[pltpu.VMEM((heads, n_blk, HEAD_DIM), F32),
                        pltpu.VMEM((heads, HEAD_DIM + 16, seq_len), BF16),
                        pltpu.VMEM((n_slots, MOBA_BLOCK, MOBA_BLOCK), F32),
                        pltpu.VMEM((n_slots, MOBA_BLOCK, MOBA_BLOCK), BF16)],
        compiler_params=_cparams(("arbitrary", "arbitrary")),
        name="moba",
    )(proj, proj, proj, bias_tiles, *side_casts)
    return res if len(res) > 1 else res[0]


def _merge_kernel(y_ref, ya_ref, wg_ref, bg_ref, ws_ref, wa_ref, ga_ref, gb_ref, o_ref, *, rows):
    for r in range(0, o_ref.shape[0], rows):
        y = y_ref[r:r + rows, :]
        z = jnp.dot(y.astype(BF16), wg_ref[...], preferred_element_type=F32) + bg_ref[...]
        y_ssm = (y * jax.nn.sigmoid(z)).astype(BF16)
        a = jnp.dot(y_ssm, ws_ref[...], preferred_element_type=F32)
        b = jnp.dot(ya_ref[r:r + rows, :], wa_ref[...], preferred_element_type=F32)
        o_ref[r:r + rows, :] = (jax.nn.sigmoid(ga_ref[r:r + rows, :].astype(F32)) * a
                                + jax.nn.sigmoid(gb_ref[r:r + rows, :].astype(F32)) * b
                                ).astype(o_ref.dtype)


def _merge(y, y_att, w_glu, b_glu, w_ps, w_pa, proj, tm=1024):
    t, k = y.shape
    n = w_ps.shape[1]
    ga_off = (SSM_WIDTH + 3 * ATTN_WIDTH) // n
    gb_off = ga_off + 1
    return pl.pallas_call(
        functools.partial(_merge_kernel, rows=256),
        out_shape=jax.ShapeDtypeStruct((t, n), BF16),
        grid=(t // tm,),
        in_specs=[pl.BlockSpec((tm, k), lambda i: (i, 0)),
                  pl.BlockSpec((tm, k), lambda i: (i, 0)),
                  pl.BlockSpec((k, k), lambda i: (0, 0), pipeline_mode=_RESIDENT),
                  pl.BlockSpec((1, k), lambda i: (0, 0)),
                  pl.BlockSpec((k, n), lambda i: (0, 0), pipeline_mode=_RESIDENT),
                  pl.BlockSpec((k, n), lambda i: (0, 0), pipeline_mode=_RESIDENT),
                  pl.BlockSpec((tm, n), lambda i: (i, ga_off)),
                  pl.BlockSpec((tm, n), lambda i: (i, gb_off))],
        out_specs=pl.BlockSpec((tm, n), lambda i: (i, 0)),
        compiler_params=_cparams(("parallel",)),
        name="merge",
    )(y, y_att, w_glu, b_glu.reshape(1, k), w_ps, w_pa, proj, proj)


def _resid_mm_kernel(a_ref, w_ref, x_ref, g_ref, o_ref):
    acc = jnp.dot(a_ref[...], w_ref[...], preferred_element_type=F32)
    o_ref[...] = x_ref[...] + g_ref[0] * acc


def _resid_mm(a, w, x2d, mod4, g_idx, seq_len, tm=1024, tn=2048):
    t, k = a.shape
    n = w.shape[1]
    per_b = seq_len // tm
    return pl.pallas_call(
        _resid_mm_kernel,
        out_shape=jax.ShapeDtypeStruct((t, n), F32),
        grid=(t // tm, n // tn),
        in_specs=[pl.BlockSpec((tm, k), lambda i, j: (i, 0)),
                  pl.BlockSpec((k, tn), lambda i, j: (0, j), pipeline_mode=_RESIDENT),
                  pl.BlockSpec((tm, tn), lambda i, j: (i, j)),
                  pl.BlockSpec((1, 1, tn), lambda i, j: ((i // per_b) * N_MOD + g_idx, 0, j))],
        out_specs=pl.BlockSpec((tm, tn), lambda i, j: (i, j)),
        compiler_params=_cparams(("parallel", "parallel")),
        name="resid_mm",
    )(a, w, x2d, mod4)


def _ff2_kernel(h_ref, w_ref, x_hbm, g_ref, gf_ref, o_ref, x_ref, x_sem, *, rows):
    i = pl.program_id(0)
    k = pl.program_id(1)
    last = pl.num_programs(1) - 1
    tm = o_ref.shape[0]

    def x_copy():
        return pltpu.make_async_copy(x_hbm.at[pl.ds(pl.multiple_of(i * tm, tm), tm), :],
                                     x_ref, x_sem)

    @pl.when(k == 0)
    def _():
        x_copy().start()
        o_ref[...] = jnp.dot(h_ref[...], w_ref[...], preferred_element_type=F32)

    @pl.when((k != 0) & (k != last))
    def _():
        o_ref[...] += jnp.dot(h_ref[...], w_ref[...], preferred_element_type=F32)

    @pl.when(k == last)
    def _():
        x_copy().wait()
        for r in range(0, tm, rows):
            acc = o_ref[r:r + rows, :] + jnp.dot(h_ref[r:r + rows, :], w_ref[...],
                                                 preferred_element_type=F32)
            xo = x_ref[r:r + rows, :] + g_ref[0] * acc
            ms = jnp.mean(xo * xo, axis=-1, keepdims=True)
            o_ref[r:r + rows, :] = xo * lax.rsqrt(ms + EPS) * gf_ref[...]


def _ff2(hid, w, x2d, mod4, g_idx, gf, seq_len, tm=1024, tk=2048):
    t, kdim = hid.shape
    n = w.shape[1]
    per_b = seq_len // tm
    assert kdim // tk >= 2, "the residual copy is started and waited in different steps"
    return pl.pallas_call(
        functools.partial(_ff2_kernel, rows=256),
        out_shape=jax.ShapeDtypeStruct((t, n), F32),
        grid=(t // tm, kdim // tk),
        in_specs=[pl.BlockSpec((tm, tk), lambda i, k: (i, k)),
                  pl.BlockSpec((tk, n), lambda i, k: (k, 0)),
                  pl.BlockSpec(memory_space=pl.ANY),
                  pl.BlockSpec((1, 1, n), lambda i, k: ((i // per_b) * N_MOD + g_idx, 0, 0)),
                  pl.BlockSpec((1, n), lambda i, k: (0, 0))],
        out_specs=pl.BlockSpec((tm, n), lambda i, k: (i, 0)),
        scratch_shapes=[pltpu.VMEM((tm, n), F32), pltpu.SemaphoreType.DMA(())],
        compiler_params=_cparams(("arbitrary", "arbitrary")),
        name="ff2_final",
    )(hid, w, x2d, mod4, gf.reshape(1, n))


def kernel(x, c, rel_bias, w_ada, b_ada, norm_mix_g, w_in, ssm_a_re, ssm_a_im, ssm_log_dt,
           ssm_b_re, ssm_b_im, ssm_c_re, ssm_c_im, ssm_d, w_glu, b_glu, w_proj_ssm,
           w_proj_attn, w_out, norm_mlp_g, w_ff1, w_ff2, norm_final_g):
    bsz, seq_len, d = x.shape
    depth = w_in.shape[0]
    assert depth == 1, "the final rms_norm is fused into the single layer's ff2 kernel"
    t = bsz * seq_len
    x2d = x.reshape(t, d)

    for l in range(depth):
        mod, bias_tiles, w_in_b = _mod(c, w_ada[l], b_ada[l], rel_bias, w_in[l])
        mod4 = mod.reshape(bsz * N_MOD, 1, d)

        proj, w_glu_b, w_ps_b, w_pa_b, w_out_b = _norm_mm(
            x2d, norm_mix_g[l], mod4, 1, 0, w_in_b, seq_len, act="none",
            side_casts=(w_glu[l], w_proj_ssm[l], w_proj_attn[l], w_out[l]))

        bmat, cmat, kmat, atv, d2 = _s5prep(ssm_a_re[l], ssm_a_im[l], ssm_log_dt[l],
                                            ssm_b_re[l], ssm_b_im[l], ssm_c_re[l], ssm_c_im[l],
                                            ssm_d[l], bsz)
        y = _s5(proj.reshape(bsz, seq_len, IN_WIDTH), bmat, cmat, kmat, atv, d2)

        y_att, w_ff1_b, w_ff2_b = _moba(proj, bias_tiles, bsz, seq_len,
                                        side_casts=(w_ff1[l], w_ff2[l]))

        merged = _merge(y.reshape(t, SSM_WIDTH), y_att, w_glu_b, b_glu[l], w_ps_b, w_pa_b, proj)
        x2d = _resid_mm(merged, w_out_b, x2d, mod4, 2, seq_len)

        hid = _norm_mm(x2d, norm_mlp_g[l], mod4, 4, 3, w_ff1_b, seq_len, act="relu2")
        x2d = _ff2(hid, w_ff2_b, x2d, mod4, 5, norm_final_g, seq_len)
    return x2d.reshape(bsz, seq_len, d)
```

```python
import functools
import itertools
import math

import jax
import jax.numpy as jnp
from jax import lax
from jax.experimental import pallas as pl
from jax.experimental.pallas import tpu as pltpu

F32 = jnp.float32
BF16 = jnp.bfloat16

D_MODEL = 2048
SSM_WIDTH = 1024
SSM_GROUP = 16
SSM_GROUPS = 64
SSM_STATE = 64
ATTN_HEADS = 8
HEAD_DIM = 128
ATTN_WIDTH = 1024
MOBA_BLOCK = 256
MOBA_TOPK = 3
REL_BUCKETS = 32
REL_MAX_DIST = 128
N_MOD = 6
EPS = 1e-6
NEG_INF = -1e30
LOG2E = math.log2(math.e)
IN_WIDTH = SSM_WIDTH + 3 * ATTN_WIDTH + 2 * D_MODEL

VMEM_LIMIT_BYTES = 56 * 1024 * 1024
LANES = 128
BF16_ROW_TILE = 16

S5_TAU = 2
S5_GROUPS = 8
S5_CH = S5_GROUPS * SSM_GROUP
S5_UC = S5_TAU * S5_CH
S5_SC = 2 * S5_GROUPS * SSM_STATE
S5_NBLK = SSM_GROUPS // S5_GROUPS
S5_ROWS = 4096
S5_SUB_ROWS = 256


_RESIDENT = pl.Buffered(1)


def _cparams(sem):
    return pltpu.CompilerParams(dimension_semantics=sem,
                                vmem_limit_bytes=VMEM_LIMIT_BYTES)


def _mod_kernel(c_ref, w_ref, b_ref, relb_ref, bucket_ref, win_ref, *rest):
    ssm_refs = rest[:7]
    o_ref, bias_ref, winb_ref = rest[7:10]
    s5_out_refs = rest[10:]
    c = c_ref[...]
    ca = (c * jax.nn.sigmoid(c)).astype(BF16)
    o_ref[...] = jnp.dot(ca, w_ref[...].astype(BF16),
                         preferred_element_type=F32) + b_ref[...]
    winb_ref[...] = win_ref[...].astype(winb_ref.dtype)
    h = pl.program_id(0)

    @pl.when(h == ATTN_HEADS)
    def _():
        _s5prep_body(*ssm_refs, *s5_out_refs)

    @pl.when(h < ATTN_HEADS)
    def _():
        for d in range(bucket_ref.shape[0]):
            bucket = bucket_ref[d]
            out = jnp.zeros(bucket.shape, F32)
            for b in range(REL_BUCKETS):
                out = jnp.where(bucket == b, relb_ref[b * ATTN_HEADS + h], out)
            bias_ref[0, d] = out * LOG2E


def _t5_bucket(rel):
    n = jnp.maximum(rel, 0)
    max_exact = REL_BUCKETS // 2
    nf = jnp.maximum(n, 1).astype(F32)
    large = max_exact + (jnp.log(nf / max_exact) / math.log(REL_MAX_DIST / max_exact)
                         * (REL_BUCKETS - max_exact)).astype(jnp.int32)
    large = jnp.minimum(large, REL_BUCKETS - 1)
    return jnp.where(n < max_exact, n, large)


def _mod(c, w_ada, b_ada, rel_bias, w_in, ssm, steps=16):
    bsz, d = c.shape
    n = w_ada.shape[1]
    tn = n // steps
    assert steps > ATTN_HEADS and tn % LANES == 0
    a_re, a_im, log_dt, b_re, b_im, c_re, c_im = ssm
    g, ns = a_re.shape
    ssm_in = (a_re.reshape(g, 1, ns), a_im.reshape(g, 1, ns), log_dt.reshape(g, 1, 1),
              b_re, b_im, c_re, c_im)
    s5_shapes = (jax.ShapeDtypeStruct((g, 1, 2 * ns), F32),
                 jax.ShapeDtypeStruct((S5_NBLK, S5_UC, S5_SC), BF16),
                 jax.ShapeDtypeStruct((S5_NBLK, S5_SC, S5_UC), BF16),
                 jax.ShapeDtypeStruct((S5_NBLK, S5_UC, S5_UC), BF16))

    def whole(a):
        return pl.BlockSpec(a.shape, lambda j: (0,) * len(a.shape))

    i = jnp.arange(MOBA_BLOCK)
    rel = (jnp.arange(3)[:, None, None] * MOBA_BLOCK + i[None, None, :] - i[None, :, None])
    bucket = _t5_bucket(rel)
    rows, cols = w_in.shape
    w_spec = pl.BlockSpec((rows // steps, cols), lambda j: (j, 0))
    return pl.pallas_call(
        _mod_kernel,
        out_shape=(jax.ShapeDtypeStruct((bsz, n), F32),
                   jax.ShapeDtypeStruct((ATTN_HEADS, 3, MOBA_BLOCK, MOBA_BLOCK), F32),
                   jax.ShapeDtypeStruct((rows, cols), BF16)) + s5_shapes,
        grid=(steps,),
        in_specs=[pl.BlockSpec((bsz, d), lambda j: (0, 0)),
                  pl.BlockSpec((d, tn), lambda j: (0, j)),
                  pl.BlockSpec((1, tn), lambda j: (0, j)),
                  pl.BlockSpec(memory_space=pltpu.SMEM),
                  pl.BlockSpec((3, MOBA_BLOCK, MOBA_BLOCK), lambda j: (0, 0, 0)),
                  w_spec] + [whole(a) for a in ssm_in],
        out_specs=(pl.BlockSpec((bsz, tn), lambda j: (0, j)),
                   pl.BlockSpec((1, 3, MOBA_BLOCK, MOBA_BLOCK),
                                lambda j: (jnp.minimum(j, ATTN_HEADS - 1), 0, 0, 0)),
                   w_spec) + tuple(whole(a) for a in s5_shapes),
        compiler_params=_cparams(("arbitrary",)),
        name="mod",
    )(c, w_ada, b_ada.reshape(1, n), rel_bias.astype(F32).reshape(-1), bucket, w_in, *ssm_in)


def _side_casts(arrays, steps, step_of):
    specs, shapes = [], []
    for a in arrays:
        r, c = a.shape
        rb = max(BF16_ROW_TILE, r // steps)
        nblk = r // rb
        assert r % rb == 0 and steps % nblk == 0
        specs.append(pl.BlockSpec(
            (rb, c), lambda *ids, nblk=nblk: ((step_of(*ids) * nblk) // steps, 0)))
        shapes.append(jax.ShapeDtypeStruct((r, c), BF16))
    return specs, shapes


def _cast_sides(side_in, side_out):
    for src, dst in zip(side_in, side_out):
        dst[...] = src[...].astype(dst.dtype)


def _norm_mm_kernel(*refs, act, rows, n_side):
    x_ref, g_ref, sc_ref, sh_ref, w_ref = refs[:5]
    side_in = refs[5:5 + n_side]
    o_ref = refs[5 + n_side]
    side_out = refs[6 + n_side:6 + 2 * n_side]
    h_ref = refs[-1]

    def mm(h):
        acc = jnp.dot(h, w_ref[...], preferred_element_type=F32)
        if act == "relu2":
            acc = jnp.square(jnp.maximum(acc, 0.0))
        return acc.astype(o_ref.dtype)

    @pl.when(pl.program_id(1) == 0)
    def _():
        g = g_ref[...]
        sc = 1.0 + sc_ref[0]
        sh = sh_ref[0]
        tm = x_ref.shape[0]
        for r in range(0, tm, rows):
            x = x_ref[r:r + rows, :]
            ms = jnp.mean(x * x, axis=-1, keepdims=True)
            y = x * lax.rsqrt(ms + EPS) * g
            h = (y * sc + sh).astype(BF16)
            h_ref[r:r + rows, :] = h
            o_ref[r:r + rows, :] = mm(h)

    @pl.when(pl.program_id(1) != 0)
    def _():
        o_ref[...] = mm(h_ref[...])

    _cast_sides(side_in, side_out)


def _norm_mm(x2d, g, mod4, sc_idx, sh_idx, w, seq_len, *, act, side_casts=(),
             tm=1024, tn=2048):
    t, d = x2d.shape
    n = w.shape[1]
    per_b = seq_len // tm
    nj = n // tn
    side_specs, side_shapes = _side_casts(side_casts, (t // tm) * nj, lambda i, j: i * nj + j)
    res = pl.pallas_call(
        functools.partial(_norm_mm_kernel, act=act, rows=256, n_side=len(side_casts)),
        out_shape=tuple([jax.ShapeDtypeStruct((t, n), BF16)] + side_shapes),
        grid=(t // tm, nj),
        in_specs=[pl.BlockSpec((tm, d), lambda i, j: (i, 0)),
                  pl.BlockSpec((1, d), lambda i, j: (0, 0)),
                  pl.BlockSpec((1, 1, d), lambda i, j: ((i // per_b) * N_MOD + sc_idx, 0, 0)),
                  pl.BlockSpec((1, 1, d), lambda i, j: ((i // per_b) * N_MOD + sh_idx, 0, 0)),
                  pl.BlockSpec((d, tn), lambda i, j: (0, j))] + side_specs,
        out_specs=tuple([pl.BlockSpec((tm, tn), lambda i, j: (i, j))] + side_specs),
        scratch_shapes=[pltpu.VMEM((tm, d), BF16)],
        compiler_params=_cparams(("arbitrary", "arbitrary")),
        name="norm_mm_" + act,
    )(x2d, g.reshape(1, d), mod4, mod4, w, *side_casts)
    return res if len(res) > 1 else res[0]


def _s5prep_body(are_ref, aim_ref, ldt_ref, bre_ref, bim_ref, cre_ref, cim_ref,
                 at_ref, bmat_ref, cmat_ref, kmat_ref):
    tau = S5_TAU
    p, n = SSM_GROUP, SSM_STATE
    a_re = are_ref[...]
    a_im = aim_ref[...]
    dt = jnp.exp(ldt_ref[...])
    mag = jnp.exp(dt * a_re)
    abar_re = mag * jnp.cos(dt * a_im)
    abar_im = mag * jnp.sin(dt * a_im)
    den = a_re * a_re + a_im * a_im
    p_re = abar_re - 1.0
    f_re = (p_re * a_re + abar_im * a_im) / den
    f_im = (abar_im * a_re - p_re * a_im) / den

    def cmul(x, y):
        return x[0] * y[0] - x[1] * y[1], x[0] * y[1] + x[1] * y[0]

    def cat(x):
        return jnp.concatenate([x[0], x[1]], axis=-1)

    abar = (abar_re, abar_im)
    b_in = (jnp.swapaxes(bre_ref[...], 1, 2), jnp.swapaxes(bim_ref[...], 1, 2))
    bb = cmul((f_re, f_im), b_in)
    cc = (cre_ref[...], cim_ref[...])
    apow = [None, abar]
    for _ in range(tau - 1):
        apow.append(cmul(apow[-1], abar))
    at_ref[...] = cat(apow[tau])

    bw, cwt, kwt = [], [], []
    y0 = cat(bb)
    for i in range(tau):
        j = tau - 1 - i
        bw.append(bb if j == 0 else cmul(apow[j], bb))
        ca = cmul(cc, apow[i + 1])
        cwt.append((jnp.swapaxes(ca[0], 1, 2), jnp.swapaxes(-ca[1], 1, 2)))
        x = cc if i == 0 else cmul(cc, apow[i])
        x = jnp.concatenate([x[0], -x[1]], axis=-1)
        kwt.append(jnp.einsum("gqn,gpn->gqp", y0, x, precision=lax.Precision.HIGHEST,
                              preferred_element_type=F32))

    bmat_ref[...] = jnp.zeros_like(bmat_ref)
    cmat_ref[...] = jnp.zeros_like(cmat_ref)
    kmat_ref[...] = jnp.zeros_like(kmat_ref)
    half = S5_SC // 2
    for g in range(are_ref.shape[0]):
        blk, gl = divmod(g, S5_GROUPS)
        for i in range(tau):
            rows_i = slice(i * S5_CH + gl * p, i * S5_CH + (gl + 1) * p)
            for part in range(2):
                st = slice(part * half + gl * n, part * half + (gl + 1) * n)
                bmat_ref[blk, rows_i, st] = bw[i][part][g].astype(BF16)
                cmat_ref[blk, st, rows_i] = cwt[i][part][g].astype(BF16)
            for ip in range(i + 1):
                rows_ip = slice(ip * S5_CH + gl * p, ip * S5_CH + (gl + 1) * p)
                kmat_ref[blk, rows_ip, rows_i] = kwt[i - ip][g].astype(BF16)


def _s5_vectors(at, d_skip, bsz):
    n = SSM_STATE
    half = S5_SC // 2
    atv = jnp.concatenate([at[:, 0, :n].reshape(S5_NBLK, half),
                           at[:, 0, n:].reshape(S5_NBLK, half)], axis=-1)
    atv = jnp.broadcast_to(atv.reshape(S5_NBLK, 1, S5_SC), (S5_NBLK, bsz, S5_SC))
    d = jnp.broadcast_to(d_skip.reshape(S5_NBLK, 1, S5_CH), (S5_NBLK, S5_TAU, S5_CH))
    return atv, d.reshape(S5_NBLK, 1, S5_UC)


def _s5_kernel(x_ref, bw_ref, cw_ref, kw_ref, a_ref, d_ref, y_ref,
               xf_ref, u_ref, w_ref, s_ref, st_ref, *, bsz, n_sub):
    @pl.when(pl.program_id(1) == 0)
    def _():
        st_ref[...] = jnp.zeros_like(st_ref)

    half = S5_SC // 2
    a_re = a_ref[0, :, :half]
    a_im = a_ref[0, :, half:]
    s_re = st_ref[:, :half]
    s_im = st_ref[:, half:]
    d2 = d_ref[0]
    sub = u_ref.shape[0] // n_sub
    ksub = sub // bsz

    def slot(ref, c, i):
        return ref.at[:, pl.ds(c * ksub * S5_TAU + i, ksub, stride=S5_TAU), :]

    def project_in(c):
        t0 = c * ksub * S5_TAU
        xf_ref[:, t0:t0 + ksub * S5_TAU, :] = x_ref[:, t0:t0 + ksub * S5_TAU, :].astype(F32)
        for i in range(S5_TAU):
            x = pltpu.einshape("bkc->kbc", slot(xf_ref, c, i)[...]).reshape(sub, S5_CH)
            u_ref[c * sub:(c + 1) * sub, i * S5_CH:(i + 1) * S5_CH] = x.astype(BF16)
        w_ref[c * sub:(c + 1) * sub, :] = jnp.dot(u_ref[c * sub:(c + 1) * sub, :], bw_ref[0],
                                                  preferred_element_type=F32)

    project_in(0)
    for c in range(n_sub):
        r0 = c * sub
        if c + 1 < n_sub:
            project_in(c + 1)
        u = u_ref[r0:r0 + sub, :]
        for k in range(0, ksub, 2):
            before = []
            for kk in range(2):
                r = r0 + (k + kk) * bsz
                before.append(jnp.concatenate([s_re, s_im], axis=1))
                w = w_ref[r:r + bsz, :]
                s_re, s_im = (a_re * s_re - a_im * s_im + w[:, :half],
                              a_re * s_im + a_im * s_re + w[:, half:])
            s_ref[r0 + k * bsz:r0 + (k + 2) * bsz, :] = (
                jnp.concatenate(before, axis=0).astype(BF16))
        y = (jnp.dot(s_ref[r0:r0 + sub, :], cw_ref[0], preferred_element_type=F32)
             + jnp.dot(u, kw_ref[0], preferred_element_type=F32)
             + d2 * u.astype(F32))
        y = jax.nn.gelu(y)
        for i in range(S5_TAU):
            slot(y_ref, c, i)[...] = pltpu.einshape(
                "kbc->bkc", y[:, i * S5_CH:(i + 1) * S5_CH].reshape(ksub, bsz, S5_CH))
    st_ref[:, :half] = s_re
    st_ref[:, half:] = s_im


def _s5(proj, bmat, cmat, kmat, atv, d2):
    assert S5_CH == 128, "a block's channels must be one lane tile"
    bsz, seq_len, _ = proj.shape
    tpc = S5_ROWS // bsz * S5_TAU
    return pl.pallas_call(
        functools.partial(_s5_kernel, bsz=bsz, n_sub=S5_ROWS // S5_SUB_ROWS),
        out_shape=jax.ShapeDtypeStruct((bsz, seq_len, SSM_WIDTH), F32),
        grid=(S5_NBLK, seq_len // tpc),
        in_specs=[
            pl.BlockSpec((bsz, tpc, S5_CH), lambda g, c: (0, c, g)),
            pl.BlockSpec((1, S5_UC, S5_SC), lambda g, c: (g, 0, 0)),
            pl.BlockSpec((1, S5_SC, S5_UC), lambda g, c: (g, 0, 0)),
            pl.BlockSpec((1, S5_UC, S5_UC), lambda g, c: (g, 0, 0)),
            pl.BlockSpec((1, bsz, S5_SC), lambda g, c: (g, 0, 0)),
            pl.BlockSpec((1, 1, S5_UC), lambda g, c: (g, 0, 0))],
        out_specs=pl.BlockSpec((bsz, tpc, S5_CH), lambda g, c: (0, c, g)),
        scratch_shapes=[pltpu.VMEM((bsz, tpc, S5_CH), F32),
                        pltpu.VMEM((S5_ROWS, S5_UC), BF16),
                        pltpu.VMEM((S5_ROWS, S5_SC), F32),
                        pltpu.VMEM((S5_ROWS, S5_SC), BF16),
                        pltpu.VMEM((bsz, S5_SC), F32)],
        compiler_params=_cparams(("parallel", "arbitrary")),
        name="s5",
    )(proj, bmat, cmat, kmat, atv, d2)


_NT = (((1,), (1,)), ((), ()))


def _moba_scores(own, slot0, state, q_ref, k_ref, bias_ref, km_ref, s_ref):
    bs = MOBA_BLOCK
    q = q_ref[own * bs:(own + 1) * bs, :]

    ranks = []
    if own > 0:
        gate = lax.dot_general(km_ref[...].astype(BF16), q, _NT, preferred_element_type=F32)
        blk = lax.broadcasted_iota(jnp.int32, gate.shape, 0)
        gm = jnp.where(blk < own, gate, NEG_INF)
        for n in range(own):
            g_n = gm[n:n + 1, :]
            ge = jnp.where(gm >= g_n, 1.0, 0.0)
            gt = jnp.where(gm > g_n, 1.0, 0.0)
            ranks.append(jnp.sum(jnp.where(blk < n, ge, gt), axis=0, keepdims=True))

    c1 = HEAD_DIM ** -0.5 * LOG2E
    m = None
    for idx, j in enumerate([own] + list(range(own))):
        k_j = k_ref[j * bs:(j + 1) * bs, :]
        raw = lax.dot_general(k_j, q, _NT, preferred_element_type=F32)
        if j == own:
            key = lax.broadcasted_iota(jnp.int32, (bs, bs), 0)
            qry = lax.broadcasted_iota(jnp.int32, (bs, bs), 1)
            s = jnp.where(key <= qry, raw * c1 + bias_ref[0], NEG_INF)
        elif own - j == 1:
            s = jnp.where(ranks[j] < float(MOBA_TOPK), raw * c1 + bias_ref[1], NEG_INF)
        else:
            row = jnp.where(ranks[j] < float(MOBA_TOPK), bias_ref[2, 0:1, :], NEG_INF)
            s = raw * c1 + row
        s_ref[slot0 + idx] = s
        m_j = jnp.max(s, axis=0, keepdims=True)
        m = m_j if m is None else jnp.maximum(m, m_j)
        state["m"] = m
        yield


def _moba_probs(own, slot0, state, s_ref, p_ref):
    m = state["m"]
    for idx in range(own + 1):
        p_ref[slot0 + idx] = jnp.exp2((s_ref[slot0 + idx] - m).astype(BF16))
        yield


def _moba_values(own, slot0, vt_ref, o_ref, p_ref):
    bs = MOBA_BLOCK
    acc = None
    for idx, j in enumerate([own] + list(range(own))):
        a_j = jnp.dot(vt_ref[:, j * bs:(j + 1) * bs], p_ref[slot0 + idx],
                      preferred_element_type=F32)
        acc = a_j if acc is None else acc + a_j
        if idx == own:
            o = acc[:HEAD_DIM, :] / acc[HEAD_DIM:HEAD_DIM + 1, :]
            o_ref[own * bs:(own + 1) * bs, :] = o.T.astype(o_ref.dtype)
        yield


def _moba_kernel(*refs, n_blk, n_side):
    q_ref, k_ref, v_ref, bias_ref = refs[:4]
    side_in = refs[4:4 + n_side]
    o_ref = refs[4 + n_side]
    side_out = refs[5 + n_side:5 + 2 * n_side]
    km_ref, vt_ref, s_ref, p_ref = refs[-4:]
    bs = MOBA_BLOCK
    heads = bias_ref.shape[0]
    _cast_sides(side_in, side_out)

    def head(ref, hh):
        return ref.at[:, hh * HEAD_DIM:(hh + 1) * HEAD_DIM]

    for hh in range(heads):
        vt_ref[hh, HEAD_DIM:, :] = jnp.ones((vt_ref.shape[1] - HEAD_DIM, vt_ref.shape[2]), BF16)
        for n in range(n_blk):
            kb = head(k_ref, hh)[n * bs:(n + 1) * bs, :].astype(F32)
            km_ref[hh, n:n + 1, :] = jnp.mean(kb, axis=0, keepdims=True)
            vt_ref[hh, :HEAD_DIM, n * bs:(n + 1) * bs] = (
                head(v_ref, hh)[n * bs:(n + 1) * bs, :].astype(F32).T.astype(BF16))

    per_head = n_blk * (n_blk + 1) // 2
    probs, values, next_values = iter(()), iter(()), iter(())
    for own in reversed(range(n_blk)):
        for hh in range(heads):
            slot0 = hh * per_head + own * (own + 1) // 2
            state = {}
            scores = _moba_scores(own, slot0, state, head(q_ref, hh), head(k_ref, hh),
                                  bias_ref.at[hh], km_ref.at[hh], s_ref)
            for _ in itertools.zip_longest(scores, probs, values):
                pass
            values = next_values
            probs = _moba_probs(own, slot0, state, s_ref, p_ref)
            next_values = _moba_values(own, slot0, vt_ref.at[hh], head(o_ref, hh), p_ref)
    for _ in itertools.zip_longest(probs, values):
        pass
    for _ in next_values:
        pass


def _moba(proj, bias_tiles, bsz, seq_len, side_casts=(), heads=2):
    n_blk = seq_len // MOBA_BLOCK
    hw = heads * HEAD_DIM
    q_off = SSM_WIDTH // hw
    k_off = q_off + ATTN_WIDTH // hw
    v_off = k_off + ATTN_WIDTH // hw
    n_slots = heads * n_blk * (n_blk + 1) // 2
    hg = ATTN_HEADS // heads
    side_specs, side_shapes = _side_casts(side_casts, bsz * hg, lambda b, h: b * hg + h)
    res = pl.pallas_call(
        functools.partial(_moba_kernel, n_blk=n_blk, n_side=len(side_casts)),
        out_shape=tuple([jax.ShapeDtypeStruct((bsz * seq_len, ATTN_WIDTH), BF16)] + side_shapes),
        grid=(bsz, hg),
        in_specs=[pl.BlockSpec((seq_len, hw), lambda b, h: (b, q_off + h)),
                  pl.BlockSpec((seq_len, hw), lambda b, h: (b, k_off + h)),
                  pl.BlockSpec((seq_len, hw), lambda b, h: (b, v_off + h)),
                  pl.BlockSpec((heads, 3, MOBA_BLOCK, MOBA_BLOCK), lambda b, h: (h, 0, 0, 0))]
        + side_specs,
        out_specs=tuple([pl.BlockSpec((seq_len, hw), lambda b, h: (b, h))] + side_specs),
        scratch_shapes=[pltpu.VMEM((heads, n_blk, HEAD_DIM), F32),
                        pltpu.VMEM((heads, HEAD_DIM + 16, seq_len), BF16),
                        pltpu.VMEM((n_slots, MOBA_BLOCK, MOBA_BLOCK), F32),
                        pltpu.VMEM((n_slots, MOBA_BLOCK, MOBA_BLOCK), BF16)],
        compiler_params=_cparams(("arbitrary", "arbitrary")),
        name="moba",
    )(proj, proj, proj, bias_tiles, *side_casts)
    return res if len(res) > 1 else res[0]


def _merge_kernel(y_ref, ya_ref, wg_ref, bg_ref, ws_ref, wa_ref, ga_ref, gb_ref, o_ref, *, rows):
    for r in range(0, o_ref.shape[0], rows):
        y = y_ref[r:r + rows, :]
        z = jnp.dot(y.astype(BF16), wg_ref[...], preferred_element_type=F32) + bg_ref[...]
        y_ssm = (y * jax.nn.sigmoid(z)).astype(BF16)
        a = jnp.dot(y_ssm, ws_ref[...], preferred_element_type=F32)
        b = jnp.dot(ya_ref[r:r + rows, :], wa_ref[...], preferred_element_type=F32)
        o_ref[r:r + rows, :] = (jax.nn.sigmoid(ga_ref[r:r + rows, :].astype(F32)) * a
                                + jax.nn.sigmoid(gb_ref[r:r + rows, :].astype(F32)) * b
                                ).astype(o_ref.dtype)


def _merge(y, y_att, w_glu, b_glu, w_ps, w_pa, proj, tm=1024):
    t, k = y.shape
    n = w_ps.shape[1]
    ga_off = (SSM_WIDTH + 3 * ATTN_WIDTH) // n
    gb_off = ga_off + 1
    return pl.pallas_call(
        functools.partial(_merge_kernel, rows=256),
        out_shape=jax.ShapeDtypeStruct((t, n), BF16),
        grid=(t // tm,),
        in_specs=[pl.BlockSpec((tm, k), lambda i: (i, 0)),
                  pl.BlockSpec((tm, k), lambda i: (i, 0)),
                  pl.BlockSpec((k, k), lambda i: (0, 0), pipeline_mode=_RESIDENT),
                  pl.BlockSpec((1, k), lambda i: (0, 0)),
                  pl.BlockSpec((k, n), lambda i: (0, 0), pipeline_mode=_RESIDENT),
                  pl.BlockSpec((k, n), lambda i: (0, 0), pipeline_mode=_RESIDENT),
                  pl.BlockSpec((tm, n), lambda i: (i, ga_off)),
                  pl.BlockSpec((tm, n), lambda i: (i, gb_off))],
        out_specs=pl.BlockSpec((tm, n), lambda i: (i, 0)),
        compiler_params=_cparams(("parallel",)),
        name="merge",
    )(y, y_att, w_glu, b_glu.reshape(1, k), w_ps, w_pa, proj, proj)


def _resid_mm_kernel(a_ref, w_ref, x_ref, g_ref, o_ref):
    acc = jnp.dot(a_ref[...], w_ref[...], preferred_element_type=F32)
    o_ref[...] = x_ref[...] + g_ref[0] * acc


def _resid_mm(a, w, x2d, mod4, g_idx, seq_len, tm=1024, tn=2048):
    t, k = a.shape
    n = w.shape[1]
    per_b = seq_len // tm
    return pl.pallas_call(
        _resid_mm_kernel,
        out_shape=jax.ShapeDtypeStruct((t, n), F32),
        grid=(t // tm, n // tn),
        in_specs=[pl.BlockSpec((tm, k), lambda i, j: (i, 0)),
                  pl.BlockSpec((k, tn), lambda i, j: (0, j), pipeline_mode=_RESIDENT),
                  pl.BlockSpec((tm, tn), lambda i, j: (i, j)),
                  pl.BlockSpec((1, 1, tn), lambda i, j: ((i // per_b) * N_MOD + g_idx, 0, j))],
        out_specs=pl.BlockSpec((tm, tn), lambda i, j: (i, j)),
        compiler_params=_cparams(("parallel", "parallel")),
        name="resid_mm",
    )(a, w, x2d, mod4)


def _ff2_kernel(h_ref, w_ref, x_hbm, g_ref, gf_ref, o_ref, x_ref, x_sem, *, rows):
    i = pl.program_id(0)
    k = pl.program_id(1)
    last = pl.num_programs(1) - 1
    tm = o_ref.shape[0]

    def x_copy():
        return pltpu.make_async_copy(x_hbm.at[pl.ds(pl.multiple_of(i * tm, tm), tm), :],
                                     x_ref, x_sem)

    @pl.when(k == 0)
    def _():
        x_copy().start()
        o_ref[...] = jnp.dot(h_ref[...], w_ref[...], preferred_element_type=F32)

    @pl.when((k != 0) & (k != last))
    def _():
        o_ref[...] += jnp.dot(h_ref[...], w_ref[...], preferred_element_type=F32)

    @pl.when(k == last)
    def _():
        x_copy().wait()
        for r in range(0, tm, rows):
            acc = o_ref[r:r + rows, :] + jnp.dot(h_ref[r:r + rows, :], w_ref[...],
                                                 preferred_element_type=F32)
            xo = x_ref[r:r + rows, :] + g_ref[0] * acc
            ms = jnp.mean(xo * xo, axis=-1, keepdims=True)
            o_ref[r:r + rows, :] = xo * lax.rsqrt(ms + EPS) * gf_ref[...]


def _ff2(hid, w, x2d, mod4, g_idx, gf, seq_len, tm=1024, tk=2048):
    t, kdim = hid.shape
    n = w.shape[1]
    per_b = seq_len // tm
    assert kdim // tk >= 2, "the residual copy is started and waited in different steps"
    return pl.pallas_call(
        functools.partial(_ff2_kernel, rows=256),
        out_shape=jax.ShapeDtypeStruct((t, n), F32),
        grid=(t // tm, kdim // tk),
        in_specs=[pl.BlockSpec((tm, tk), lambda i, k: (i, k)),
                  pl.BlockSpec((tk, n), lambda i, k: (k, 0)),
                  pl.BlockSpec(memory_space=pl.ANY),
                  pl.BlockSpec((1, 1, n), lambda i, k: ((i // per_b) * N_MOD + g_idx, 0, 0)),
                  pl.BlockSpec((1, n), lambda i, k: (0, 0))],
        out_specs=pl.BlockSpec((tm, n), lambda i, k: (i, 0)),
        scratch_shapes=[pltpu.VMEM((tm, n), F32), pltpu.SemaphoreType.DMA(())],
        compiler_params=_cparams(("arbitrary", "arbitrary")),
        name="ff2_final",
    )(hid, w, x2d, mod4, gf.reshape(1, n))


def kernel(x, c, rel_bias, w_ada, b_ada, norm_mix_g, w_in, ssm_a_re, ssm_a_im, ssm_log_dt,
           ssm_b_re, ssm_b_im, ssm_c_re, ssm_c_im, ssm_d, w_glu, b_glu, w_proj_ssm,
           w_proj_attn, w_out, norm_mlp_g, w_ff1, w_ff2, norm_final_g):
    bsz, seq_len, d = x.shape
    depth = w_in.shape[0]
    assert depth == 1, "the final rms_norm is fused into the single layer's ff2 kernel"
    t = bsz * seq_len
    x2d = x.reshape(t, d)

    for l in range(depth):
        ssm = (ssm_a_re[l], ssm_a_im[l], ssm_log_dt[l], ssm_b_re[l], ssm_b_im[l],
               ssm_c_re[l], ssm_c_im[l])
        mod, bias_tiles, w_in_b, at, bmat, cmat, kmat = _mod(c, w_ada[l], b_ada[l], rel_bias,
                                                             w_in[l], ssm)
        mod4 = mod.reshape(bsz * N_MOD, 1, d)

        proj, w_glu_b, w_ps_b, w_pa_b, w_out_b = _norm_mm(
            x2d, norm_mix_g[l], mod4, 1, 0, w_in_b, seq_len, act="none",
            side_casts=(w_glu[l], w_proj_ssm[l], w_proj_attn[l], w_out[l]))

        atv, d2 = _s5_vectors(at, ssm_d[l], bsz)
        y = _s5(proj.reshape(bsz, seq_len, IN_WIDTH), bmat, cmat, kmat, atv, d2)

        y_att, w_ff1_b, w_ff2_b = _moba(proj, bias_tiles, bsz, seq_len,
                                        side_casts=(w_ff1[l], w_ff2[l]))

        merged = _merge(y.reshape(t, SSM_WIDTH), y_att, w_glu_b, b_glu[l], w_ps_b, w_pa_b, proj)
        x2d = _resid_mm(merged, w_out_b, x2d, mod4, 2, seq_len)

        hid = _norm_mm(x2d, norm_mlp_g[l], mod4, 4, 3, w_ff1_b, seq_len, act="relu2")
        x2d = _ff2(hid, w_ff2_b, x2d, mod4, 5, norm_final_g, seq_len)
    return x2d.reshape(bsz, seq_len, d)
```
